```python
import math
import jax, jax.numpy as jnp
from jax import lax
import numpy as np

D_MODEL = 1024
BATCH = 8
SEQ = 2048
DEPTH = 2
DEC_BATCH = 128
DEC_SEQ = 4
PAST_LEN = 16384
PAGE_SIZE = 128

MIX_W = 512
HG_H = 4
HG_DK = 128
HG_DV = 128
ML_H = 4
ML_DK = 128
ML_DV = 128
RET_H = 4
RET_DK = 128
RET_DV = 128
N_BRANCH = 3
D_FF = 2816
CONV_W = 3
CHUNK = 64
EPS = 1e-6
ROPE_BASE = 10000.0
ML_F_BIAS = 3.0

HG_OFF = 0
HG_COLS = 4 * MIX_W
ML_OFF = HG_OFF + HG_COLS
ML_COLS = 4 * MIX_W + 2 * ML_H
RET_OFF = ML_OFF + ML_COLS
RET_COLS = 4 * MIX_W
GATE_OFF = RET_OFF + RET_COLS
N_IN = GATE_OFF + N_BRANCH * D_MODEL

kernel_name = 'hgrn2_mlstm_retention_gated_hybrid_step'


def _rmsnorm(x, w):
    xf = x.astype(jnp.float32)
    y = xf * lax.rsqrt(jnp.mean(xf * xf, axis=-1, keepdims=True) + EPS)
    return (y * w.astype(jnp.float32)).astype(x.dtype)


def _head_norm(o, w):
    b, t, h, d = o.shape
    y = o * lax.rsqrt(jnp.mean(o * o, axis=-1, keepdims=True) + EPS)
    y = y * w.astype(jnp.float32).reshape(h, d)
    return y.reshape(b, t, h * d)


def _heads(a, H):
    return a.reshape(a.shape[0], a.shape[1], H, -1)


def _chunk_len(T):
    return CHUNK if T % CHUNK == 0 else math.gcd(T, CHUNK)


def _to_chunks(a, L):
    b, t = a.shape[:2]
    return jnp.moveaxis(a.reshape((b, t // L, L) + a.shape[2:]), 1, 0)


def _from_chunks(a):
    n, b, L = a.shape[:3]
    return jnp.moveaxis(a, 0, 1).reshape((b, n * L) + a.shape[3:])


def _rotary(x, pos):
    d = x.shape[-1]
    inv = ROPE_BASE ** (-jnp.linspace(0.0, 1.0, d // 2, dtype=jnp.float32))
    ang = pos[:, None] * inv[None, :]
    cos = jnp.cos(ang)[None, :, None, :]
    sin = jnp.sin(ang)[None, :, None, :]
    x1, x2 = x[..., : d // 2], x[..., d // 2:]
    return jnp.concatenate([x1 * cos - x2 * sin, x1 * sin + x2 * cos], axis=-1)


def _hgrn2(q, logf, k, i, S0):
    L = _chunk_len(q.shape[1])
    causal = jnp.tril(jnp.ones((L, L), dtype=bool))

    def step(S, inp):
        qc, lfc, kc, ic = inp
        B = jnp.cumsum(lfc, axis=1)
        inter = jnp.einsum('bthk,bhkv->bthv', qc * jnp.exp(B), S)
        diff = B[:, :, None] - B[:, None, :]
        decay = jnp.exp(jnp.where(causal[None, :, :, None, None], diff, -jnp.inf))
        A = jnp.einsum('bthk,btshk,bshk->btsh', qc, decay, kc)
        intra = jnp.einsum('btsh,bshv->bthv', A, ic)
        BL = B[:, -1]
        S_new = jnp.exp(BL)[..., None] * S + jnp.einsum(
            'bshk,bshv->bhkv', kc * jnp.exp(BL[:, None] - B), ic)
        return S_new, inter + intra

    S, o = lax.scan(step, S0, tuple(_to_chunks(a, L) for a in (q, logf, k, i)))
    return _from_chunks(o), S


def _mlstm(q, k, v, ig, lf, C0, n0, m0):
    L = _chunk_len(q.shape[1])
    causal = jnp.tril(jnp.ones((L, L), dtype=bool))

    def step(carry, inp):
        C, n, m = carry
        qc, kc, vc, igc, lfc = inp
        bcum = jnp.cumsum(lfc, axis=1)
        logD = bcum[:, :, None] - bcum[:, None, :] + igc[:, None, :]
        logD = jnp.where(causal[None, :, :, None], logD, -jnp.inf)
        inter_log = bcum + m[:, None]
        m_t = jnp.maximum(inter_log, jnp.max(logD, axis=2))
        Dw = jnp.exp(logD - m_t[:, :, None])
        w_inter = jnp.exp(inter_log - m_t)
        Sm = jnp.einsum('bthk,bshk->btsh', qc, kc) * Dw
        num = w_inter[..., None] * jnp.einsum('bthk,bhkv->bthv', qc, C) + jnp.einsum('btsh,bshv->bthv', Sm, vc)
        nq = w_inter * jnp.einsum('bthk,bhk->bth', qc, n) + jnp.sum(Sm, axis=2)
        den = jnp.maximum(jnp.abs(nq), jnp.exp(-m_t))
        h = num / den[..., None]
        m_end = m_t[:, -1]
        bL = bcum[:, -1]
        w_prev = jnp.exp(bL + m - m_end)
        w_s = jnp.exp(bL[:, None] - bcum + igc - m_end[:, None])
        C_new = w_prev[..., None, None] * C + jnp.einsum('bsh,bshk,bshv->bhkv', w_s, kc, vc)
        n_new = w_prev[..., None] * n + jnp.einsum('bsh,bshk->bhk', w_s, kc)
        return (C_new, n_new, m_end), h

    (C, n, m), h = lax.scan(step, (C0, n0, m0), tuple(_to_chunks(a, L) for a in (q, k, v, ig, lf)))
    return _from_chunks(h), C, n, m


def _retention(q, k, v, lg, R0):
    L = _chunk_len(q.shape[1])
    causal = jnp.tril(jnp.ones((L, L), dtype=bool))
    pos = jnp.arange(L, dtype=jnp.float32)
    rel = pos[:, None] - pos[None, :]
    D = jnp.exp(jnp.where(causal[:, :, None], rel[:, :, None] * lg, -jnp.inf))
    w_in = jnp.exp((pos[:, None] + 1.0) * lg)
    w_st = jnp.exp((L - 1.0 - pos)[:, None] * lg)
    g_L = jnp.exp(L * lg)

    def step(R, inp):
        qc, kc, vc = inp
        inter = jnp.einsum('bthk,bhkv->bthv', qc, R) * w_in[None, :, :, None]
        A = jnp.einsum('bthk,bshk->btsh', qc, kc) * D[None]
        intra = jnp.einsum('btsh,bshv->bthv', A, vc)
        R_new = g_L[None, :, None, None] * R + jnp.einsum('bshk,bshv,sh->bhkv', kc, vc, w_st)
        return R_new, inter + intra

    R, o = lax.scan(step, R0, tuple(_to_chunks(a, L) for a in (q, k, v)))
    return _from_chunks(o), R


def _trunk(x, pos0, hg_S, ml_C, ml_n, ml_m, ret_R, conv_buf,
           norm1_w, w_in, mlstm_gate_b, branch_gate_b, hgrn_lb, hgrn_norm_w, mlstm_norm_w,
           ret_norm_w, w_branch, w_out, norm2_w, ffn_w_gate, ffn_w_up, ffn_conv_w, ffn_conv_b,
           ffn_w_down, final_norm_w):
    f32 = jnp.float32
    bsz, T, _ = x.shape
    pos = float(pos0) + jnp.arange(T, dtype=f32)
    lb_soft = jax.nn.softmax(hgrn_lb.astype(f32), axis=0)
    lb_all = jnp.cumsum(lb_soft, axis=0) - lb_soft[0:1]
    log_gamma = jnp.log(1.0 - jnp.power(2.0, -5.0 - jnp.arange(RET_H, dtype=f32)))
    o_hg, o_C, o_n, o_m, o_R, o_cv = [], [], [], [], [], []
    for l in range(DEPTH):
        h = _rmsnorm(x, norm1_w[l])
        z = jnp.matmul(h, w_in[l]).astype(f32)

        hq, hf, hi, hg = (z[..., HG_OFF + j * MIX_W: HG_OFF + (j + 1) * MIX_W] for j in range(4))
        lb = lb_all[l]
        f = lb + (1.0 - lb) * jax.nn.sigmoid(hf)
        oa, S_new = _hgrn2(_heads(jax.nn.silu(hq), HG_H), _heads(jnp.log(f), HG_H),
                           _heads(1.0 - f, HG_H), _heads(hi, HG_H), hg_S[l].astype(f32))
        y_hg = _head_norm(oa, hgrn_norm_w[l]) * jax.nn.silu(hg)

        mq, mk, mv, mo = (z[..., ML_OFF + j * MIX_W: ML_OFF + (j + 1) * MIX_W] for j in range(4))
        g0 = ML_OFF + 4 * MIX_W
        mgb = mlstm_gate_b[l].astype(f32)
        ig = z[..., g0: g0 + ML_H] + mgb[:ML_H]
        lf = jax.nn.log_sigmoid(z[..., g0 + ML_H: g0 + 2 * ML_H] + mgb[ML_H:])
        hb, C_new, n_new, m_new = _mlstm(_heads(mq, ML_H), _heads(mk, ML_H) * (ML_DK ** -0.5),
                                         _heads(mv, ML_H), ig, lf, ml_C[l].astype(f32),
                                         ml_n[l].astype(f32), ml_m[l].astype(f32))
        hb = _heads(jax.nn.sigmoid(mo), ML_H) * hb
        y_ml = _head_norm(hb, mlstm_norm_w[l])

        rq, rk, rv, rg = (z[..., RET_OFF + j * MIX_W: RET_OFF + (j + 1) * MIX_W] for j in range(4))
        qr = _rotary(_heads(rq, RET_H), pos)
        kr = _rotary(_heads(rk, RET_H), pos) * (RET_DK ** -0.5)
        oc, R_new = _retention(qr, kr, _heads(rv, RET_H), log_gamma, ret_R[l].astype(f32))
        y_ret = _head_norm(oc, ret_norm_w[l]) * jax.nn.silu(rg)

        bgb = branch_gate_b[l].astype(f32)
        merged = sum(
            jax.nn.sigmoid(z[..., GATE_OFF + j * D_MODEL: GATE_OFF + (j + 1) * D_MODEL]
                           + bgb[j * D_MODEL:(j + 1) * D_MODEL])
            * jnp.matmul(yb.astype(x.dtype), w_branch[l, j]).astype(f32)
            for j, yb in enumerate((y_hg, y_ml, y_ret)))
        x = x + jnp.matmul(merged.astype(x.dtype), w_out[l])

        h2 = _rmsnorm(x, norm2_w[l])
        a = jnp.matmul(h2, ffn_w_gate[l]).astype(f32)
        u = jnp.matmul(h2, ffn_w_up[l]).astype(f32)
        a_ext = jnp.concatenate([conv_buf[l].astype(f32), a], axis=1)
        cw = ffn_conv_w[l].astype(f32)
        c = sum(a_ext[:, j:j + T] * cw[j] for j in range(CONV_W)) + ffn_conv_b[l].astype(f32)
        x = x + jnp.matmul((jax.nn.silu(c) * u).astype(x.dtype), ffn_w_down[l])

        o_hg.append(S_new)
        o_C.append(C_new)
        o_n.append(n_new)
        o_m.append(m_new)
        o_R.append(R_new)
        o_cv.append(a_ext[:, T:])
    y = _rmsnorm(x, final_norm_w)
    return y, (jnp.stack(o_hg), jnp.stack(o_C), jnp.stack(o_n), jnp.stack(o_m),
               jnp.stack(o_R), jnp.stack(o_cv))


def setup_inputs(seed: int = 0) -> dict:
    key = jax.random.key(seed)
    ks = jax.random.split(key, 26)
    f32 = jnp.float32

    def nrm(k, shape, s):
        return s * jax.random.normal(k, shape, f32)

    D = D_MODEL
    return {
        'x_prompt': nrm(ks[0], (BATCH, SEQ, D), 1.0),
        'x_sample': nrm(ks[1], (DEC_BATCH, DEC_SEQ, D), 1.0),
        'state_hgrn': nrm(ks[2], (DEPTH, DEC_BATCH, HG_H, HG_DK, HG_DV), 0.5),
        'state_mlstm_C': nrm(ks[3], (DEPTH, DEC_BATCH, ML_H, ML_DK, ML_DV), 0.1),
        'state_mlstm_n': nrm(ks[4], (DEPTH, DEC_BATCH, ML_H, ML_DK), 0.1),
        'state_mlstm_m': nrm(ks[5], (DEPTH, DEC_BATCH, ML_H), 1.0),
        'state_ret': nrm(ks[6], (DEPTH, DEC_BATCH, RET_H, RET_DK, RET_DV), 1.0),
        'state_ffn_conv': nrm(ks[7], (DEPTH, DEC_BATCH, CONV_W - 1, D_FF), 1.0),
        'norm1_w': 1.0 + nrm(ks[8], (DEPTH, D), 0.02),
        'w_in': nrm(ks[9], (DEPTH, D, N_IN), D ** -0.5),
        'mlstm_gate_b': jnp.concatenate([nrm(ks[10], (DEPTH, ML_H), 0.1),
                                         ML_F_BIAS + nrm(ks[11], (DEPTH, ML_H), 0.1)], axis=1),
        'branch_gate_b': nrm(ks[12], (DEPTH, N_BRANCH * D), 0.1),
        'hgrn_lb': nrm(ks[13], (DEPTH, HG_H * HG_DK), 0.5),
        'hgrn_norm_w': 1.0 + nrm(ks[14], (DEPTH, HG_H * HG_DV), 0.02),
        'mlstm_norm_w': 1.0 + nrm(ks[15], (DEPTH, ML_H * ML_DV), 0.02),
        'ret_norm_w': 1.0 + nrm(ks[16], (DEPTH, RET_H * RET_DV), 0.02),
        'w_branch': nrm(ks[17], (DEPTH, N_BRANCH, MIX_W, D), MIX_W ** -0.5),
        'w_out': nrm(ks[18], (DEPTH, D, D), D ** -0.5),
        'norm2_w': 1.0 + nrm(ks[19], (DEPTH, D), 0.02),
        'ffn_w_gate': nrm(ks[20], (DEPTH, D, D_FF), D ** -0.5),
        'ffn_w_up': nrm(ks[21], (DEPTH, D, D_FF), D ** -0.5),
        'ffn_conv_w': nrm(ks[22], (DEPTH, CONV_W, D_FF), CONV_W ** -0.5),
        'ffn_conv_b': nrm(ks[23], (DEPTH, D_FF), 0.02),
        'ffn_w_down': nrm(ks[24], (DEPTH, D_FF, D), D_FF ** -0.5),
        'final_norm_w': 1.0 + nrm(ks[25], (D,), 0.02),
    }


def reference(x_prompt, x_sample, state_hgrn, state_mlstm_C, state_mlstm_n, state_mlstm_m,
              state_ret, state_ffn_conv, norm1_w, w_in, mlstm_gate_b, branch_gate_b, hgrn_lb,
              hgrn_norm_w, mlstm_norm_w, ret_norm_w, w_branch, w_out, norm2_w, ffn_w_gate,
              ffn_w_up, ffn_conv_w, ffn_conv_b, ffn_w_down, final_norm_w):
    f32 = jnp.float32
    bp = x_prompt.shape[0]
    z_hg = jnp.zeros((DEPTH, bp, HG_H, HG_DK, HG_DV), f32)
    z_C = jnp.zeros((DEPTH, bp, ML_H, ML_DK, ML_DV), f32)
    z_n = jnp.zeros((DEPTH, bp, ML_H, ML_DK), f32)
    z_m = jnp.zeros((DEPTH, bp, ML_H), f32)
    z_R = jnp.zeros((DEPTH, bp, RET_H, RET_DK, RET_DV), f32)
    z_cv = jnp.zeros((DEPTH, bp, CONV_W - 1, D_FF), f32)
    y_prompt, (hgrn_p, mlstm_C_p, mlstm_n_p, mlstm_m_p, ret_p, ffn_conv_p) = _trunk(
        x_prompt, 0, z_hg, z_C, z_n, z_m, z_R, z_cv,
        norm1_w, w_in, mlstm_gate_b, branch_gate_b, hgrn_lb, hgrn_norm_w, mlstm_norm_w,
        ret_norm_w, w_branch, w_out, norm2_w, ffn_w_gate, ffn_w_up, ffn_conv_w, ffn_conv_b,
        ffn_w_down, final_norm_w)
    y_sample, (hgrn_s, mlstm_C_s, mlstm_n_s, mlstm_m_s, ret_s, ffn_conv_s) = _trunk(
        x_sample, PAST_LEN, state_hgrn, state_mlstm_C, state_mlstm_n, state_mlstm_m,
        state_ret, state_ffn_conv,
        norm1_w, w_in, mlstm_gate_b, branch_gate_b, hgrn_lb, hgrn_norm_w, mlstm_norm_w,
        ret_norm_w, w_branch, w_out, norm2_w, ffn_w_gate, ffn_w_up, ffn_conv_w, ffn_conv_b,
        ffn_w_down, final_norm_w)
    return (y_prompt, y_sample, hgrn_p, hgrn_s, mlstm_C_p, mlstm_C_s, mlstm_n_p, mlstm_n_s,
            mlstm_m_p, mlstm_m_s, ret_p, ret_s, ffn_conv_p, ffn_conv_s)
```

```python
import functools
import math

import numpy as np
import jax
import jax.numpy as jnp
from jax import lax
from jax.experimental import pallas as pl
from jax.experimental.pallas import tpu as pltpu

F32 = jnp.float32
BF16 = jnp.bfloat16

D_MODEL = 1024
MIX_W = 512
N_HEAD = 4
D_HEAD = 128
N_BRANCH = 3
D_FF = 2816
CONV_W = 3
CHUNK = 64
EPS = 1e-6
ROPE_BASE = 10000.0
PAST_LEN = 16384

HG_OFF = 0
ML_OFF = 4 * MIX_W
ML_GATE_OFF = ML_OFF + 4 * MIX_W
RET_OFF = ML_GATE_OFF + 2 * N_HEAD
GATE_OFF = RET_OFF + 4 * MIX_W
N_MAIN = 3 * 4 * MIX_W + N_BRANCH * D_MODEL

LANES = 128
SUBLANES = 8
VMEM_LIMIT = 56 * 1024 * 1024

TOKEN_TILE = 512
PROJ_COL_TILE = 2304
FF_TILE = 256
HG_SUB = 16


def _resident(shape):
    return pl.BlockSpec(shape, lambda *_: (0,) * len(shape), pipeline_mode=pl.Buffered(1))


def _dot(a, b):
    return jnp.dot(a, b, preferred_element_type=F32)


def _dot_nt(a, b):
    return lax.dot_general(a, b, (((1,), (1,)), ((), ())), preferred_element_type=F32)


def _dot_tn(a, b):
    return lax.dot_general(a, b, (((0,), (0,)), ((), ())), preferred_element_type=F32)


def _split3(x):
    hi = x.astype(BF16)
    r = x - hi.astype(F32)
    mid = r.astype(BF16)
    lo = (r - mid.astype(F32)).astype(BF16)
    return hi, mid, lo


def _sigmoid(x):
    return 1.0 / (1.0 + jnp.exp(-x))


def _silu(x):
    return x * _sigmoid(x)


def _log_sigmoid(x):
    return jnp.minimum(x, 0.0) - jnp.log(1.0 + jnp.exp(-jnp.abs(x)))


def _rms(x, w):
    return x * lax.rsqrt(jnp.mean(x * x, axis=-1, keepdims=True) + EPS) * w


def _pad_rows(x, n):
    if x.shape[0] >= n:
        return x
    return jnp.concatenate([x, jnp.zeros((n - x.shape[0], x.shape[1]), x.dtype)], axis=0)


def _cumsum_rows(x):
    n = x.shape[0]
    row = lax.broadcasted_iota(jnp.int32, x.shape, 0)
    s = 1
    while s < n:
        x = x + jnp.where(row >= s, pltpu.roll(x, s, axis=0), 0.0)
        s *= 2
    return x


def _col_bcast(row):
    r = lax.broadcasted_iota(jnp.int32, (D_HEAD, D_HEAD), 0)
    c = lax.broadcasted_iota(jnp.int32, (D_HEAD, D_HEAD), 1)
    d = jnp.where(r == c, jnp.broadcast_to(row, (D_HEAD, D_HEAD)), 0.0)
    ones = jnp.ones((D_HEAD, D_HEAD), BF16)
    hi, mid, lo = _split3(d)
    return _dot(hi, ones) + _dot(mid, ones) + _dot(lo, ones)


def _row_bcast(col, width):
    n = col.shape[0]
    vals = _pad_rows(jnp.broadcast_to(col, (n, LANES)), width)
    lane = lax.broadcasted_iota(jnp.int32, (n, LANES), 1)
    pick = jnp.where(lane == 0, 1.0, 0.0).astype(BF16)
    hi, mid, lo = _split3(vals)
    return _dot_nt(pick, hi) + _dot_nt(pick, mid) + _dot_nt(pick, lo)


def _chunk_geometry(seq_len):
    if seq_len % CHUNK == 0:
        return CHUNK, CHUNK, CHUNK
    valid = math.gcd(seq_len, CHUNK)
    assert valid == seq_len and valid <= SUBLANES, "unsupported sequence length"
    return SUBLANES, valid, LANES


def _rope_kernel(inv_ref, cos_ref, sin_ref, *, pos0):
    shape = cos_ref.shape
    pos = lax.broadcasted_iota(jnp.int32, shape, 0).astype(F32) + pos0
    ang = pos * inv_ref[...]
    lane = lax.broadcasted_iota(jnp.int32, shape, 1)
    cos_ref[...] = jnp.cos(ang)
    sin_ref[...] = jnp.where(lane < D_HEAD // 2, -jnp.sin(ang), jnp.sin(ang))


def _rope_tables(rows, pos0):
    inv = np.float32(ROPE_BASE) ** (-np.linspace(0.0, 1.0, D_HEAD // 2, dtype=np.float32))
    inv2 = jnp.asarray(np.concatenate([inv, inv])[None, :], F32)
    return pl.pallas_call(
        functools.partial(_rope_kernel, pos0=float(pos0)),
        out_shape=(jax.ShapeDtypeStruct((rows, D_HEAD), F32),) * 2,
        name="rope_tables",
    )(inv2)


def _proj_in_kernel(x_ref, nw_ref, w_ref, wg_ref, z_ref, g_ref, h_scr):
    j = pl.program_id(1)

    @pl.when(j == 0)
    def _():
        hb = _rms(x_ref[...], nw_ref[...]).astype(BF16)
        h_scr[...] = hb
        g_ref[...] = _dot(hb, wg_ref[...])

    col = pl.multiple_of(j * PROJ_COL_TILE, PROJ_COL_TILE)
    z_ref[...] = _dot(h_scr[...], w_ref[:, pl.ds(col, PROJ_COL_TILE)])


def _proj_in(x, nw, w_main, w_gate):
    n = x.shape[0]
    tm = min(TOKEN_TILE, n)
    assert n % tm == 0
    return pl.pallas_call(
        _proj_in_kernel,
        grid=(n // tm, N_MAIN // PROJ_COL_TILE),
        in_specs=[
            pl.BlockSpec((tm, D_MODEL), lambda i, j: (i, 0)),
            pl.BlockSpec((1, D_MODEL), lambda i, j: (0, 0)),
            _resident((D_MODEL, N_MAIN)),
            _resident((D_MODEL, LANES)),
        ],
        out_specs=[
            pl.BlockSpec((tm, PROJ_COL_TILE), lambda i, j: (i, j)),
            pl.BlockSpec((tm, LANES), lambda i, j: (i, 0)),
        ],
        out_shape=[
            jax.ShapeDtypeStruct((n, N_MAIN), F32),
            jax.ShapeDtypeStruct((n, LANES), F32),
        ],
        scratch_shapes=[pltpu.VMEM((tm, D_MODEL), BF16)],
        compiler_params=pltpu.CompilerParams(
            dimension_semantics=("arbitrary", "arbitrary"), vmem_limit_bytes=VMEM_LIMIT),
        name="proj_in",
    )(x, nw, w_main, w_gate)


def _mixer_loop(kernel_chunk, z_ref, y_ref, pad_scr, *, seqs, rows, valid, n_chunks):
    def per_seq(b, carry):
        def per_chunk(c, carry2):
            if valid == rows:
                start = pl.multiple_of(c * rows, rows)
                zc = z_ref[b, pl.ds(start, rows), :]
            else:
                pad_scr[...] = jnp.zeros(pad_scr.shape, F32)
                pad_scr[0:valid, :] = z_ref[b]
                zc = pad_scr[...]
            y = kernel_chunk(b, c, zc)
            if valid == rows:
                y_ref[b, pl.ds(start, rows), :] = y
            else:
                y_ref[b] = y[0:valid, :]
            return carry2

        return lax.fori_loop(0, n_chunks, per_chunk, carry)

    lax.fori_loop(0, seqs, per_seq, 0)


def _init_state(t_idx, out_ref, in_ref):
    @pl.when(t_idx == 0)
    def _():
        if in_ref is None:
            out_ref[...] = jnp.zeros(out_ref.shape, F32)
        else:
            out_ref[...] = in_ref[...]


def _head(x, part, h):
    lo = part * MIX_W + h * D_HEAD
    return x[:, lo:lo + D_HEAD]


def _hgrn_lower_bound(lb_ref, layer):
    x = lb_ref[...]
    e = jnp.exp(x - jnp.max(x, axis=0, keepdims=True))
    soft = e / jnp.sum(e, axis=0, keepdims=True)
    cum = soft[0:1]
    for j in range(1, layer + 1):
        cum = cum + soft[j:j + 1]
    return cum - soft[0:1]


def _hgrn_kernel(*refs, layer, rows, valid, width, seqs, n_chunks, has_state):
    if has_state:
        z_ref, lb_ref, nw_ref, s0_ref, y_ref, s_ref, pad_scr = refs
    else:
        z_ref, lb_ref, nw_ref, y_ref, s_ref, pad_scr = refs
        s0_ref = None
    _init_state(pl.program_id(1), s_ref, s0_ref)
    lb = _hgrn_lower_bound(lb_ref, layer)
    nw = nw_ref[...]
    sub = min(HG_SUB, rows)
    n_sub = rows // sub
    ones = jnp.ones((D_HEAD, D_HEAD), BF16)
    row_id = lax.broadcasted_iota(jnp.int32, (rows, D_HEAD), 0)
    sub_row = lax.broadcasted_iota(jnp.int32, (sub, D_HEAD), 0)
    lane = lax.broadcasted_iota(jnp.int32, (sub, width), 1)

    def chunk(b, c, zc):
        ys = []
        for h in range(N_HEAD):
            hq, hf, hi, hg = (_head(zc, p, h) for p in range(4))
            lbh = lb[:, h * D_HEAD:(h + 1) * D_HEAD]
            q = _silu(hq)
            f = lbh + (1.0 - lbh) * _sigmoid(hf)
            lf = jnp.log(f)
            k = 1.0 - f
            if valid < rows:
                lf = jnp.where(row_id < valid, lf, 0.0)
                k = jnp.where(row_id < valid, k, 0.0)
            bcum = _cumsum_rows(lf)
            state = s_ref[b, h]
            inter = _dot((q * jnp.exp(bcum)).astype(BF16), state.astype(BF16))

            a_rows = []
            for i in range(n_sub):
                r0 = i * sub
                b_i = bcum[r0:r0 + sub]
                q_i = q[r0:r0 + sub]
                k_i = k[r0:r0 + sub]
                terms = []
                for s in range(sub):
                    d = jnp.where(sub_row >= s, b_i - b_i[s:s + 1], -jnp.inf)
                    terms.append(q_i * jnp.exp(d) * k_i[s:s + 1])
                sums = _dot(jnp.concatenate(terms, axis=0).astype(BF16), ones)
                a_i = jnp.zeros((sub, width), F32)
                for s in range(sub):
                    a_i = jnp.where(lane == r0 + s, sums[s * sub:(s + 1) * sub, 0:width], a_i)
                if i > 0:
                    b_ref = b_i[0:1]
                    q_dec = q_i * jnp.exp(b_i - b_ref)
                    k_dec = k * jnp.exp(jnp.minimum(b_ref - bcum, 0.0))
                    a_off = _dot_nt(q_dec.astype(BF16), k_dec.astype(BF16))
                    a_i = jnp.where(lane < r0, a_off, a_i)
                a_rows.append(a_i)
            a = a_rows[0] if n_sub == 1 else jnp.concatenate(a_rows, axis=0)
            o = inter + _dot(a.astype(BF16), _pad_rows(hi, width).astype(BF16))
            ys.append(_rms(o, nw[:, h * D_HEAD:(h + 1) * D_HEAD]) * _silu(hg))

            b_last = bcum[rows - 1:rows]
            k_w = k * jnp.exp(b_last - bcum)
            upd = _dot_tn(_pad_rows(k_w, width).astype(BF16), _pad_rows(hi, width).astype(BF16))
            s_ref[b, h] = _col_bcast(jnp.exp(b_last)) * state + upd
        return jnp.concatenate(ys, axis=1)

    _mixer_loop(chunk, z_ref, y_ref, pad_scr, seqs=seqs, rows=rows, valid=valid, n_chunks=n_chunks)


def _mlstm_kernel(*refs, rows, valid, width, seqs, n_chunks, has_state):
    if has_state:
        (z_ref, g_ref, gb_ref, nw_ref, c0_ref, n0_ref, m0_ref,
         y_ref, c_ref, n_ref, m_ref, pad_scr, gpad_scr) = refs
    else:
        z_ref, g_ref, gb_ref, nw_ref, y_ref, c_ref, n_ref, m_ref, pad_scr, gpad_scr = refs
        c0_ref = n0_ref = m0_ref = None
    t_idx = pl.program_id(1)
    _init_state(t_idx, c_ref, c0_ref)
    _init_state(t_idx, n_ref, n0_ref)
    _init_state(t_idx, m_ref, m0_ref)
    nw = nw_ref[...]
    gb = gb_ref[...]
    row_id = lax.broadcasted_iota(jnp.int32, (rows, LANES), 0)
    t_i = lax.broadcasted_iota(jnp.int32, (rows, width), 0)
    s_i = lax.broadcasted_iota(jnp.int32, (rows, width), 1)
    ok = s_i <= t_i
    if valid < rows:
        ok = ok & (s_i < valid)
    scale = D_HEAD ** -0.5

    def chunk(b, c, zc):
        if valid == rows:
            gc = g_ref[b, pl.ds(pl.multiple_of(c * rows, rows), rows), :]
        else:
            gpad_scr[...] = jnp.zeros(gpad_scr.shape, F32)
            gpad_scr[0:valid, :] = g_ref[b]
            gc = gpad_scr[...]
        gates = gc + gb
        lf_all = _log_sigmoid(gates)
        if valid < rows:
            lf_all = jnp.where(row_id < valid, lf_all, 0.0)
        bc_all = _cumsum_rows(lf_all)
        ys = []
        for h in range(N_HEAD):
            q, k, v, og = (_head(zc, p, h) for p in range(4))
            k = k * scale
            ig = gates[:, h:h + 1]
            bc = bc_all[:, N_HEAD + h:N_HEAD + h + 1]
            log_d = jnp.where(ok, bc + _row_bcast(ig - bc, width), -jnp.inf)
            m_prev = m_ref[b, h:h + 1, 0:1]
            inter_log = bc + m_prev
            m_t = jnp.maximum(inter_log, jnp.max(log_d, axis=1, keepdims=True))
            d_w = jnp.exp(log_d - m_t)
            w_inter = jnp.exp(inter_log - m_t)
            qb = q.astype(BF16)
            s_m = _dot_nt(qb, _pad_rows(k, width).astype(BF16)) * d_w
            c_state = c_ref[b, h]
            n_state = n_ref[b, h:h + 1, :]
            num = w_inter * _dot(qb, c_state.astype(BF16)) + _dot(
                s_m.astype(BF16), _pad_rows(v, width).astype(BF16))
            nq = w_inter * jnp.sum(q * n_state, axis=1, keepdims=True) + jnp.sum(
                s_m, axis=1, keepdims=True)
            den = jnp.maximum(jnp.abs(nq), jnp.exp(-m_t))
            hb = _sigmoid(og) * (num / den)
            ys.append(_rms(hb, nw[:, h * D_HEAD:(h + 1) * D_HEAD]))

            m_end = m_t[rows - 1:rows]
            b_last = bc[rows - 1:rows]
            w_prev = jnp.exp(b_last + m_prev - m_end)
            w_exp = b_last - bc + ig - m_end
            if valid < rows:
                w_exp = jnp.where(row_id[:, 0:1] < valid, w_exp, -jnp.inf)
            k_w = k * jnp.exp(w_exp)
            c_ref[b, h] = w_prev * c_state + _dot_tn(
                _pad_rows(k_w, width).astype(BF16), _pad_rows(v, width).astype(BF16))
            n_ref[b, h:h + 1, :] = w_prev * n_state + jnp.sum(k_w, axis=0, keepdims=True)
            m_ref[b, h:h + 1, :] = jnp.broadcast_to(m_end, (1, LANES))
        return jnp.concatenate(ys, axis=1)

    _mixer_loop(chunk, z_ref, y_ref, pad_scr, seqs=seqs, rows=rows, valid=valid, n_chunks=n_chunks)


def _ret_kernel(*refs, rows, valid, width, seqs, n_chunks, has_state):
    if has_state:
        z_ref, cos_ref, sin_ref, nw_ref, r0_ref, y_ref, r_ref, pad_scr = refs
    else:
        z_ref, cos_ref, sin_ref, nw_ref, y_ref, r_ref, pad_scr = refs
        r0_ref = None
    _init_state(pl.program_id(1), r_ref, r0_ref)
    nw = nw_ref[...]
    t_i = lax.broadcasted_iota(jnp.int32, (rows, width), 0)
    s_i = lax.broadcasted_iota(jnp.int32, (rows, width), 1)
    rel = (t_i - s_i).astype(F32)
    t_col = lax.broadcasted_iota(jnp.int32, (rows, 1), 0).astype(F32)
    scale = D_HEAD ** -0.5

    def rotate(x, cos, sin):
        return x * cos + pltpu.roll(x, D_HEAD // 2, axis=1) * sin

    def chunk(b, c, zc):
        start = pl.multiple_of(c * rows, rows)
        cos = cos_ref[pl.ds(start, rows), :]
        sin = sin_ref[pl.ds(start, rows), :]
        ys = []
        for h in range(N_HEAD):
            log_gamma = math.log(1.0 - 2.0 ** (-5.0 - h))
            q, k, v, rg = (_head(zc, p, h) for p in range(4))
            qr = rotate(q, cos, sin).astype(BF16)
            kr = rotate(k, cos, sin) * scale
            decay = jnp.where(s_i <= t_i, jnp.exp(rel * log_gamma), 0.0)
            state = r_ref[b, h]
            inter = _dot(qr, state.astype(BF16)) * jnp.exp((t_col + 1.0) * log_gamma)
            a = _dot_nt(qr, _pad_rows(kr, width).astype(BF16)) * decay
            vb = _pad_rows(v, width).astype(BF16)
            o = inter + _dot(a.astype(BF16), vb)
            ys.append(_rms(o, nw[:, h * D_HEAD:(h + 1) * D_HEAD]) * _silu(rg))
            k_w = kr * jnp.exp((valid - 1.0 - t_col) * log_gamma)
            r_ref[b, h] = math.exp(valid * log_gamma) * state + _dot_tn(
                _pad_rows(k_w, width).astype(BF16), vb)
        return jnp.concatenate(ys, axis=1)

    _mixer_loop(chunk, z_ref, y_ref, pad_scr, seqs=seqs, rows=rows, valid=valid, n_chunks=n_chunks)


def _mixer_tiling(batch, seq_len):
    rows, valid, width = _chunk_geometry(seq_len)
    if valid == rows:
        seqs, t_tile = 1, min(TOKEN_TILE, seq_len)
    else:
        seqs, t_tile = min(16, batch), seq_len
    assert batch % seqs == 0 and seq_len % t_tile == 0
    n_chunks = t_tile // valid
    return dict(rows=rows, valid=valid, width=width, seqs=seqs, n_chunks=n_chunks), t_tile


def _state_spec(seqs, tail):
    zeros = (0,) * len(tail)
    return pl.BlockSpec((seqs,) + tail, lambda bi, ti: (bi,) + zeros)


def _mixer_call(kernel, name, z3, col_block, extra_in, extra_specs, states, state_tails, scratch):
    batch, seq_len, _ = z3.shape
    geo, t_tile = _mixer_tiling(batch, seq_len)
    seqs = geo["seqs"]
    has_state = states is not None
    in_specs = [pl.BlockSpec((seqs, t_tile, 4 * MIX_W), lambda bi, ti: (bi, ti, col_block))] + extra_specs(
        seqs, t_tile)
    args = [z3] + list(extra_in)
    if has_state:
        in_specs += [_state_spec(seqs, tail) for tail in state_tails]
        args += list(states)
    out_specs = [pl.BlockSpec((seqs, t_tile, MIX_W), lambda bi, ti: (bi, ti, 0))] + [
        _state_spec(seqs, tail) for tail in state_tails]
    out_shape = [jax.ShapeDtypeStruct((batch, seq_len, MIX_W), F32)] + [
        jax.ShapeDtypeStruct((batch,) + tail, F32) for tail in state_tails]
    return pl.pallas_call(
        functools.partial(kernel, has_state=has_state, **geo),
        grid=(batch // seqs, seq_len // t_tile),
        in_specs=in_specs,
        out_specs=out_specs,
        out_shape=out_shape,
        scratch_shapes=scratch(geo["rows"]),
        compiler_params=pltpu.CompilerParams(
            dimension_semantics=("arbitrary", "arbitrary"), vmem_limit_bytes=VMEM_LIMIT),
        name=name,
    )(*args)


def _full2(shape):
    return pl.BlockSpec(shape, lambda bi, ti: (0, 0))


def _hgrn(z3, hgrn_lb, nw, layer, state):
    return _mixer_call(
        functools.partial(_hgrn_kernel, layer=layer), "hgrn2", z3, 0,
        [hgrn_lb, nw],
        lambda seqs, t_tile: [_full2(hgrn_lb.shape), _full2((1, MIX_W))],
        None if state is None else [state], [(N_HEAD, D_HEAD, D_HEAD)],
        lambda rows: [pltpu.VMEM((rows, 4 * MIX_W), F32)])


def _mlstm(z3, g3, gate_b, nw, states):
    return _mixer_call(
        _mlstm_kernel, "mlstm", z3, 1,
        [g3, gate_b, nw],
        lambda seqs, t_tile: [pl.BlockSpec((seqs, t_tile, LANES), lambda bi, ti: (bi, ti, 0)),
                              _full2((1, LANES)), _full2((1, MIX_W))],
        states, [(N_HEAD, D_HEAD, D_HEAD), (N_HEAD, D_HEAD), (N_HEAD, LANES)],
        lambda rows: [pltpu.VMEM((rows, 4 * MIX_W), F32), pltpu.VMEM((rows, LANES), F32)])


def _retention(z3, cos, sin, nw, state):
    def specs(seqs, t_tile):
        rows = cos.shape[0] if t_tile < SUBLANES else t_tile
        table = pl.BlockSpec((rows, D_HEAD), lambda bi, ti: (ti, 0))
        return [table, table, _full2((1, MIX_W))]

    return _mixer_call(
        _ret_kernel, "retention", z3, 2,
        [cos, sin, nw], specs,
        None if state is None else [state], [(N_HEAD, D_HEAD, D_HEAD)],
        lambda rows: [pltpu.VMEM((rows, 4 * MIX_W), F32)])


def _merge_kernel(x_ref, yh_ref, ym_ref, yr_ref, zg_ref, bg_ref, wb_ref, wo_ref, n2_ref, x1_ref, h2_ref):
    merged = None
    for j, y_ref in enumerate((yh_ref, ym_ref, yr_ref)):
        cols = slice(j * D_MODEL, (j + 1) * D_MODEL)
        gate = _sigmoid(zg_ref[:, cols] + bg_ref[:, cols])
        term = gate * _dot(y_ref[...].astype(BF16), wb_ref[j])
        merged = term if merged is None else merged + term
    x1 = x_ref[...] + _dot(merged.astype(BF16), wo_ref[...])
    x1_ref[...] = x1
    h2_ref[...] = _rms(x1, n2_ref[...]).astype(BF16)


def _merge(x, yh, ym, yr, z, gate_b, w_branch, w_out, norm2_w):
    n = x.shape[0]
    tm = min(TOKEN_TILE, n)
    gate_w = N_BRANCH * D_MODEL
    row = lambda width: pl.BlockSpec((tm, width), lambda i: (i, 0))
    return pl.pallas_call(
        _merge_kernel,
        grid=(n // tm,),
        in_specs=[
            row(D_MODEL), row(MIX_W), row(MIX_W), row(MIX_W),
            pl.BlockSpec((tm, gate_w), lambda i: (i, (N_MAIN - gate_w) // gate_w)),
            pl.BlockSpec((1, gate_w), lambda i: (0, 0)),
            _resident((N_BRANCH, MIX_W, D_MODEL)),
            _resident((D_MODEL, D_MODEL)),
            pl.BlockSpec((1, D_MODEL), lambda i: (0, 0)),
        ],
        out_specs=[row(D_MODEL), row(D_MODEL)],
        out_shape=[jax.ShapeDtypeStruct((n, D_MODEL), F32), jax.ShapeDtypeStruct((n, D_MODEL), BF16)],
        compiler_params=pltpu.CompilerParams(
            dimension_semantics=("arbitrary",), vmem_limit_bytes=VMEM_LIMIT),
        name="merge",
    )(x, yh, ym, yr, z, gate_b, w_branch, w_out, norm2_w)


def _ffn_kernel(*refs, seq_len, has_state, final_norm, tail_rows):
    refs = list(refs)
    x1_ref, h2_ref, wg_ref, wu_ref, wd_ref, cw_ref, cb_ref = refs[:7]
    refs = refs[7:]
    p1_ref = p2_ref = fw_ref = None
    if has_state:
        p1_ref, p2_ref = refs[:2]
        refs = refs[2:]
    if final_norm:
        fw_ref = refs[0]
        refs = refs[1:]
    out_ref, tail_ref, carry_scr, a_scr = refs
    tm = x1_ref.shape[0]
    i = pl.program_id(0)

    @pl.when(i == 0)
    def _():
        carry_scr[...] = jnp.zeros(carry_scr.shape, F32)

    t = (i * tm + lax.broadcasted_iota(jnp.int32, (tm, FF_TILE), 0)) & (seq_len - 1)
    h2 = h2_ref[...]
    acc = x1_ref[...]
    for j in range(D_FF // FF_TILE):
        cols = slice(j * FF_TILE, (j + 1) * FF_TILE)
        a = _dot(h2, wg_ref[:, cols])
        u = _dot(h2, wu_ref[:, cols])
        a_scr[0:SUBLANES, :] = carry_scr[:, cols]
        a_scr[SUBLANES:, :] = a
        carry_scr[:, cols] = a[tm - SUBLANES:, :]
        prev1 = jnp.where(t >= 1, a_scr[SUBLANES - 1:SUBLANES - 1 + tm, :], 0.0)
        prev2 = jnp.where(t >= 2, a_scr[SUBLANES - 2:SUBLANES - 2 + tm, :], 0.0)
        if has_state:
            prev1 = prev1 + p1_ref[:, cols]
            prev2 = prev2 + p2_ref[:, cols]
        conv = prev2 * cw_ref[0:1, cols] + prev1 * cw_ref[1:2, cols] + a * cw_ref[2:3, cols] + cb_ref[:, cols]
        acc = acc + _dot((_silu(conv) * u).astype(BF16), wd_ref[cols, :])
        if tail_rows == tm:
            tail_ref[:, cols] = a
        else:
            tail_ref[0, :, cols] = a[tm - tail_rows:, :]
    if final_norm:
        acc = _rms(acc, fw_ref[...])
    out_ref[...] = acc


def _ffn(x1, h2, wg, wu, wd, conv_w, conv_b, seq_len, conv_state, final_w):
    n = x1.shape[0]
    has_state = conv_state is not None
    tm = min(TOKEN_TILE // 2 if has_state else TOKEN_TILE, n)
    assert n % tm == 0 and seq_len & (seq_len - 1) == 0
    batch = n // seq_len
    row = lambda width: pl.BlockSpec((tm, width), lambda i: (i, 0))
    full = lambda shape: pl.BlockSpec(shape, lambda i: (0,) * len(shape))
    in_specs = [row(D_MODEL), row(D_MODEL), _resident((D_MODEL, D_FF)), _resident((D_MODEL, D_FF)),
                _resident((D_FF, D_MODEL)), full((CONV_W, D_FF)), full((1, D_FF))]
    args = [x1, h2, wg, wu, wd, conv_w, conv_b]
    if has_state:
        p1 = jnp.pad(conv_state[:, 1:2], ((0, 0), (0, seq_len - 1), (0, 0))).reshape(n, D_FF)
        p2 = jnp.pad(conv_state, ((0, 0), (0, seq_len - (CONV_W - 1)), (0, 0))).reshape(n, D_FF)
        in_specs += [row(D_FF), row(D_FF)]
        args += [p1, p2]
    if final_w is not None:
        in_specs.append(full((1, D_MODEL)))
        args.append(final_w)
    if seq_len % tm == 0:
        tail_rows = CONV_W - 1
        tail_spec = pl.BlockSpec((1, tail_rows, D_FF), lambda i: ((i * tm) // seq_len, 0, 0))
        tail_shape = jax.ShapeDtypeStruct((batch, tail_rows, D_FF), F32)
    else:
        assert tm % seq_len == 0 and seq_len >= CONV_W - 1
        tail_rows = tm
        tail_spec = row(D_FF)
        tail_shape = jax.ShapeDtypeStruct((n, D_FF), F32)
    out, tail = pl.pallas_call(
        functools.partial(_ffn_kernel, seq_len=seq_len, has_state=has_state,
                          final_norm=final_w is not None, tail_rows=tail_rows),
        grid=(n // tm,),
        in_specs=in_specs,
        out_specs=[row(D_MODEL), tail_spec],
        out_shape=[jax.ShapeDtypeStruct((n, D_MODEL), F32), tail_shape],
        scratch_shapes=[pltpu.VMEM((SUBLANES, D_FF), F32), pltpu.VMEM((tm + SUBLANES, FF_TILE), F32)],
        compiler_params=pltpu.CompilerParams(
            dimension_semantics=("arbitrary",), vmem_limit_bytes=VMEM_LIMIT),
        name="conv_ffn",
    )(*args)
    if tail_rows == tm:
        tail = tail.reshape(batch, seq_len, D_FF)[:, seq_len - (CONV_W - 1):, :]
    return out, tail


def _prep_weights(w_in, w_branch, w_out, ffn_w_gate, ffn_w_up, ffn_w_down):
    w_main = jnp.concatenate(
        [w_in[:, :, HG_OFF:ML_GATE_OFF], w_in[:, :, RET_OFF:]], axis=2).astype(BF16)
    w_gate = jnp.pad(w_in[:, :, ML_GATE_OFF:RET_OFF],
                     ((0, 0), (0, 0), (0, LANES - 2 * N_HEAD))).astype(BF16)
    return (w_main, w_gate, w_branch.astype(BF16), w_out.astype(BF16),
            ffn_w_gate.astype(BF16), ffn_w_up.astype(BF16), ffn_w_down.astype(BF16))


def _trunk(x3, pos0, states, params, prepped):
    (norm1_w, mlstm_gate_b, branch_gate_b, hgrn_lb, hgrn_norm_w, mlstm_norm_w, ret_norm_w,
     norm2_w, ffn_conv_w, ffn_conv_b, final_norm_w) = params
    w_main, w_gate, w_branch, w_out, w_ffg, w_ffu, w_ffd = prepped
    batch, seq_len, _ = x3.shape
    n = batch * seq_len
    depth = norm1_w.shape[0]
    rows, _, _ = _chunk_geometry(seq_len)
    cos, sin = _rope_tables(max(seq_len, rows), pos0)
    x = x3.reshape(n, D_MODEL)
    outs = [[] for _ in range(6)]
    for l in range(depth):
        z, g = _proj_in(x, norm1_w[l:l + 1], w_main[l], w_gate[l])
        z3 = z.reshape(batch, seq_len, N_MAIN)
        g3 = g.reshape(batch, seq_len, LANES)
        if states is None:
            st_h = st_m = st_r = st_cv = None
        else:
            hg_s, ml_c, ml_n, ml_m, ret_r, conv_buf = states
            st_h, st_r, st_cv = hg_s[l], ret_r[l], conv_buf[l]
            st_m = [ml_c[l], ml_n[l], jnp.broadcast_to(ml_m[l][:, :, None], (batch, N_HEAD, LANES))]
        y_hg, s_new = _hgrn(z3, hgrn_lb, hgrn_norm_w[l:l + 1], l, st_h)
        gate_b = jnp.pad(mlstm_gate_b[l:l + 1], ((0, 0), (0, LANES - 2 * N_HEAD)))
        y_ml, c_new, n_new, m_new = _mlstm(z3, g3, gate_b, mlstm_norm_w[l:l + 1], st_m)
        y_ret, r_new = _retention(z3, cos, sin, ret_norm_w[l:l + 1], st_r)
        x1, h2 = _merge(x, y_hg.reshape(n, MIX_W), y_ml.reshape(n, MIX_W), y_ret.reshape(n, MIX_W),
                        z, branch_gate_b[l:l + 1], w_branch[l], w_out[l], norm2_w[l:l + 1])
        x, conv_tail = _ffn(x1, h2, w_ffg[l], w_ffu[l], w_ffd[l], ffn_conv_w[l], ffn_conv_b[l:l + 1],
                            seq_len, st_cv, final_norm_w[None, :] if l == depth - 1 else None)
        for acc, val in zip(outs, (s_new, c_new, n_new, m_new[:, :, 0], r_new, conv_tail)):
            acc.append(val)
    return x.reshape(batch, seq_len, D_MODEL), tuple(jnp.stack(o) for o in outs)


def kernel(x_prompt, x_sample, state_hgrn, state_mlstm_C, state_mlstm_n, state_mlstm_m, state_ret,
           state_ffn_conv, norm1_w, w_in, mlstm_gate_b, branch_gate_b, hgrn_lb, hgrn_norm_w,
           mlstm_norm_w, ret_norm_w, w_branch, w_out, norm2_w, ffn_w_gate, ffn_w_up, ffn_conv_w,
           ffn_conv_b, ffn_w_down, final_norm_w):
    params = (norm1_w, mlstm_gate_b, branch_gate_b, hgrn_lb, hgrn_norm_w, mlstm_norm_w, ret_norm_w,
              norm2_w, ffn_conv_w, ffn_conv_b, final_norm_w)
    prepped = _prep_weights(w_in, w_branch, w_out, ffn_w_gate, ffn_w_up, ffn_w_down)
    y_p, (hg_p, c_p, n_p, m_p, r_p, cv_p) = _trunk(x_prompt, 0, None, params, prepped)
    y_s, (hg_s, c_s, n_s, m_s, r_s, cv_s) = _trunk(
        x_sample, PAST_LEN,
        (state_hgrn, state_mlstm_C, state_mlstm_n, state_mlstm_m, state_ret, state_ffn_conv),
        params, prepped)
    return (y_p, y_s, hg_p, hg_s, c_p, c_s, n_p, n_s, m_p, m_s, r_p, r_s, cv_p, cv_s)
```

```python
import functools
import math

import jax
import jax.numpy as jnp
from jax import lax
from jax.experimental import pallas as pl
from jax.experimental.pallas import tpu as pltpu

F32 = jnp.float32
BF16 = jnp.bfloat16

D_MODEL = 1024
MIX_W = 512
N_HEAD = 4
D_HEAD = 128
N_BRANCH = 3
D_FF = 2816
CONV_W = 3
CHUNK = 64
EPS = 1e-6
ROPE_BASE = 10000.0
PAST_LEN = 16384

HG_OFF = 0
ML_OFF = 4 * MIX_W
ML_GATE_OFF = ML_OFF + 4 * MIX_W
RET_OFF = ML_GATE_OFF + 2 * N_HEAD
GATE_OFF = RET_OFF + 4 * MIX_W
N_MAIN = 3 * 4 * MIX_W + N_BRANCH * D_MODEL

LANES = 128
SUBLANES = 8
VMEM_LIMIT = 56 * 1024 * 1024

TOKEN_TILE = 512
PROJ_COL_TILE = 2304
FF_TILE = 256
HG_SUB = 16
GROUP = 8
SEQ_BLOCK = 16


def _resident(shape):
    return pl.BlockSpec(shape, lambda *_: (0,) * len(shape), pipeline_mode=pl.Buffered(1))


def _dot(a, b):
    return jnp.dot(a, b, preferred_element_type=F32)


def _dot_tn(a, b):
    return lax.dot_general(a, b, (((0,), (0,)), ((), ())), preferred_element_type=F32)


def _bdot(a, b):
    return jnp.einsum('nlk,nkv->nlv', a, b, preferred_element_type=F32)


def _bdot_nt(a, b):
    return jnp.einsum('nqd,nkd->nqk', a, b, preferred_element_type=F32)


def _bdot_tn(a, b):
    return jnp.stack([_dot_tn(a[i], b[i]) for i in range(a.shape[0])])


def _split3(x):
    hi = x.astype(BF16)
    r = x - hi.astype(F32)
    mid = r.astype(BF16)
    lo = (r - mid.astype(F32)).astype(BF16)
    return hi, mid, lo


def _sigmoid(x):
    return 1.0 / (1.0 + jnp.exp(-x))


def _silu(x):
    return x * _sigmoid(x)


def _log_sigmoid(x):
    return jnp.minimum(x, 0.0) - jnp.log(1.0 + jnp.exp(-jnp.abs(x)))


def _rms(x, w):
    return x * lax.rsqrt(jnp.mean(x * x, axis=-1, keepdims=True) + EPS) * w


def _pad_rows(x, n):
    if x.shape[1] >= n:
        return x
    return jnp.concatenate([x, jnp.zeros((x.shape[0], n - x.shape[1], x.shape[2]), x.dtype)], axis=1)


def _seg_cumsum(x, seg):
    shape = x.shape
    flat = x.reshape(shape[0] * seg, shape[2])
    row = lax.broadcasted_iota(jnp.int32, flat.shape, 0) & (seg - 1)
    s = 1
    while s < seg:
        flat = flat + jnp.where(row >= s, pltpu.roll(flat, s, axis=0), 0.0)
        s *= 2
    return flat.reshape(shape)


def _col_bcast(rows):
    n = rows.shape[0]
    r = lax.broadcasted_iota(jnp.int32, (n, D_HEAD, D_HEAD), 1)
    c = lax.broadcasted_iota(jnp.int32, (n, D_HEAD, D_HEAD), 2)
    d = jnp.where(r == c, jnp.broadcast_to(rows, (n, D_HEAD, D_HEAD)), 0.0).reshape(n * D_HEAD, D_HEAD)
    ones = jnp.ones((D_HEAD, D_HEAD), BF16)
    hi, mid, lo = _split3(d)
    return (_dot(hi, ones) + _dot(mid, ones) + _dot(lo, ones)).reshape(n, D_HEAD, D_HEAD)


def _through_vmem(x):
    def body(ref):
        ref[...] = x
        return ref[...]
    return pl.run_scoped(body, pltpu.VMEM(x.shape, x.dtype))


def _row_bcast(col, width):
    n, rows, _ = col.shape
    vals = _through_vmem(_pad_rows(col, width))
    lane = lax.broadcasted_iota(jnp.int32, (n, rows, LANES), 2)
    pick = _through_vmem(jnp.where(lane == 0, 1.0, 0.0).astype(BF16))
    hi, mid, lo = _split3(vals)
    return _bdot_nt(pick, hi) + _bdot_nt(pick, mid) + _bdot_nt(pick, lo)


def _chunk_geometry(seq_len):
    if seq_len % CHUNK == 0:
        return CHUNK, CHUNK, CHUNK
    valid = math.gcd(seq_len, CHUNK)
    assert valid == seq_len and valid <= SUBLANES, "unsupported sequence length"
    return SUBLANES, valid, LANES


def _rope_kernel(inv_ref, cos_ref, sin_ref, *, pos0):
    shape = cos_ref.shape
    pos = lax.broadcasted_iota(jnp.int32, shape, 0).astype(F32) + pos0
    ang = pos * inv_ref[...]
    lane = lax.broadcasted_iota(jnp.int32, shape, 1)
    cos_ref[...] = jnp.cos(ang)
    sin_ref[...] = jnp.where(lane < D_HEAD // 2, -jnp.sin(ang), jnp.sin(ang))


def _rope_tables(rows, pos0):
    inv = ROPE_BASE ** (-jnp.linspace(0.0, 1.0, D_HEAD // 2, dtype=F32))
    inv2 = jnp.concatenate([inv, inv])[None, :]
    return pl.pallas_call(
        functools.partial(_rope_kernel, pos0=float(pos0)),
        out_shape=(jax.ShapeDtypeStruct((rows, D_HEAD), F32),) * 2,
        name="rope_tables",
    )(inv2)


def _proj_in_kernel(x_ref, nw_ref, w_ref, wg_ref, z_ref, g_ref, h_scr):
    j = pl.program_id(1)

    @pl.when(j == 0)
    def _():
        hb = _rms(x_ref[...], nw_ref[...]).astype(BF16)
        h_scr[...] = hb
        g_ref[...] = _dot(hb, wg_ref[...])

    col = pl.multiple_of(j * PROJ_COL_TILE, PROJ_COL_TILE)
    z_ref[...] = _dot(h_scr[...], w_ref[:, pl.ds(col, PROJ_COL_TILE)])


def _proj_in(x, nw, w_main, w_gate):
    n = x.shape[0]
    tm = min(TOKEN_TILE, n)
    assert n % tm == 0
    return pl.pallas_call(
        _proj_in_kernel,
        grid=(n // tm, N_MAIN // PROJ_COL_TILE),
        in_specs=[
            pl.BlockSpec((tm, D_MODEL), lambda i, j: (i, 0)),
            pl.BlockSpec((1, D_MODEL), lambda i, j: (0, 0)),
            _resident((D_MODEL, N_MAIN)),
            _resident((D_MODEL, LANES)),
        ],
        out_specs=[
            pl.BlockSpec((tm, PROJ_COL_TILE), lambda i, j: (i, j)),
            pl.BlockSpec((tm, LANES), lambda i, j: (i, 0)),
        ],
        out_shape=[
            jax.ShapeDtypeStruct((n, N_MAIN), F32),
            jax.ShapeDtypeStruct((n, LANES), F32),
        ],
        scratch_shapes=[pltpu.VMEM((tm, D_MODEL), BF16)],
        compiler_params=pltpu.CompilerParams(
            dimension_semantics=("arbitrary", "arbitrary"), vmem_limit_bytes=VMEM_LIMIT),
        name="proj_in",
    )(x, nw, w_main, w_gate)


class _Geo:
    def __init__(self, batch, seq_len):
        self.rows, self.valid, self.width = _chunk_geometry(seq_len)
        if self.valid == self.rows:
            self.seqs, self.chunks = 1, seq_len // self.rows
            self.group = min(GROUP, self.chunks)
        else:
            self.seqs, self.chunks = min(SEQ_BLOCK, batch), 1
            self.group = self.seqs
        assert batch % self.seqs == 0 and self.chunks % min(self.group, self.chunks) == 0
        self.batch = batch
        self.problems = self.seqs * self.chunks
        self.n_groups = self.problems // self.group

    def load(self, ref, pad_ref, g):
        if self.chunks > 1:
            return ref[0, pl.ds(g * self.group, self.group)]
        x = ref[:, 0]
        if self.valid < self.rows:
            pad_ref[...] = jnp.zeros(pad_ref.shape, F32)
            pad_ref[:, 0:self.valid, :] = x
            x = pad_ref[...]
        return x

    def store(self, ref, g, y):
        if self.chunks > 1:
            ref[0, pl.ds(g * self.group, self.group)] = y
        else:
            ref[:, 0] = y[:, 0:self.valid, :]

    def rows_of(self, g):
        return pl.ds(g * self.group, self.group)

    def row_mask(self, shape):
        if self.valid == self.rows:
            return None
        return lax.broadcasted_iota(jnp.int32, shape, 1) < self.valid

    def for_groups(self, body):
        if self.n_groups == 1:
            body(0)
        else:
            lax.fori_loop(0, self.n_groups, lambda g, c: (body(g), c)[1], 0)

    def scan_chunks(self, step, init):
        if self.chunks == 1:
            return step(pl.ds(0, self.seqs), init)
        return lax.fori_loop(0, self.chunks, lambda c, carry: step(pl.ds(c * self.seqs, self.seqs), carry), init)


def _split_refs(refs, counts):
    out, k = [], 0
    for c in counts:
        out.append(refs[k:k + c])
        k += c
    return out


def _hgrn_lower_bound(lb_ref, layer):
    x = lb_ref[...]
    e = jnp.exp(x - jnp.max(x, axis=0, keepdims=True))
    soft = e / jnp.sum(e, axis=0, keepdims=True)
    cum = soft[0:1]
    for j in range(1, layer + 1):
        cum = cum + soft[j:j + 1]
    return cum - soft[0:1]


def _hgrn_intra(q, k, bcum, geo):
    n = q.shape[0]
    sub = min(HG_SUB, geo.rows)
    ones = jnp.ones((D_HEAD, D_HEAD), BF16)
    sub_row = lax.broadcasted_iota(jnp.int32, (n, sub, D_HEAD), 1)
    lane = lax.broadcasted_iota(jnp.int32, (n, sub, geo.width), 2)
    blocks = []
    for i in range(geo.rows // sub):
        r0 = i * sub
        b_i, q_i, k_i = (a[:, r0:r0 + sub] for a in (bcum, q, k))
        terms = []
        for s in range(sub):
            d = jnp.where(sub_row >= s, b_i - b_i[:, s:s + 1], -jnp.inf)
            terms.append(q_i * jnp.exp(d) * k_i[:, s:s + 1])
        stacked = jnp.concatenate(terms, axis=1).reshape(n * sub * sub, D_HEAD)
        sums = _dot(stacked.astype(BF16), ones).reshape(n, sub * sub, D_HEAD)
        a_i = jnp.zeros((n, sub, geo.width), F32)
        for s in range(sub):
            a_i = jnp.where(lane == r0 + s, sums[:, s * sub:(s + 1) * sub, 0:geo.width], a_i)
        if i > 0:
            b_ref = b_i[:, 0:1]
            q_dec = q_i * jnp.exp(b_i - b_ref)
            k_dec = k * jnp.exp(jnp.minimum(b_ref - bcum, 0.0))
            a_i = jnp.where(lane < r0, _bdot_nt(q_dec.astype(BF16), k_dec.astype(BF16)), a_i)
        blocks.append(a_i)
    return blocks[0] if len(blocks) == 1 else jnp.concatenate(blocks, axis=1)


def _hgrn_kernel(*refs, layer, geo, has_state):
    parts, (lb_ref, nw_ref), s0, _, (y_ref, s_ref), (qt_scr, o_scr, u_scr, dc_scr), pads = _split_refs(
        refs, (4, 2, int(has_state), 1, 2, 4, len(refs)))
    lb = _hgrn_lower_bound(lb_ref, layer)
    rows, width = geo.rows, geo.width

    def phase_a(g):
        hq, hf, hi = (geo.load(parts[p], pads[p] if pads else None, g) for p in range(3))
        q = _silu(hq)
        f = lb + (1.0 - lb) * _sigmoid(hf)
        lf = jnp.log(f)
        k = 1.0 - f
        mask = geo.row_mask(lf.shape)
        if mask is not None:
            lf = jnp.where(mask, lf, 0.0)
            k = jnp.where(mask, k, 0.0)
        bcum = _seg_cumsum(lf, rows)
        sl = geo.rows_of(g)
        qt_scr[sl] = (q * jnp.exp(bcum)).astype(BF16)
        a = _hgrn_intra(q, k, bcum, geo)
        vb = _pad_rows(hi, width).astype(BF16)
        o_scr[sl] = _bdot(a.astype(BF16), vb)
        b_last = bcum[:, rows - 1:rows]
        k_w = k * jnp.exp(b_last - bcum)
        u_scr[sl] = _bdot_tn(_pad_rows(k_w, width).astype(BF16), vb)
        dc_scr[sl] = _col_bcast(jnp.exp(b_last))

    geo.for_groups(phase_a)

    def step(sl, state):
        o_scr[sl] = o_scr[sl] + _bdot(qt_scr[sl], state.astype(BF16))
        return dc_scr[sl] * state + u_scr[sl]

    init = s0[0][...] if has_state else jnp.zeros((geo.seqs, D_HEAD, D_HEAD), F32)
    s_ref[...] = geo.scan_chunks(step, init)

    def epilogue(g):
        hg = geo.load(parts[3], pads[3] if pads else None, g)
        geo.store(y_ref, g, _rms(o_scr[geo.rows_of(g)], nw_ref[...]) * _silu(hg))

    geo.for_groups(epilogue)


def _mlstm_kernel(*refs, geo, has_state):
    (parts, (g_ref, gb_ref, nw_ref), st0, _, (y_ref, c_ref, n_ref, m_ref),
     (q_scr, v_scr, u_scr, nu_scr, bc_scr, ml_scr, rs_scr, bl_scr, mu_scr), pads) = _split_refs(
        refs, (4, 3, 3 * int(has_state), 3, 4, 9, len(refs)))
    rows, width = geo.rows, geo.width
    head = pl.program_id(1)
    scale = D_HEAD ** -0.5
    t_i = lax.broadcasted_iota(jnp.int32, (1, rows, width), 1)
    s_i = lax.broadcasted_iota(jnp.int32, (1, rows, width), 2)
    ok = s_i <= t_i
    if geo.valid < rows:
        ok = ok & (s_i < geo.valid)

    def phase_a(g):
        q, k, v = (geo.load(parts[p], pads[p] if pads else None, g) for p in range(3))
        gates = geo.load(g_ref, pads[4] if pads else None, g) + gb_ref[...]
        lf_all = _log_sigmoid(gates)
        mask = geo.row_mask(gates.shape)
        if mask is not None:
            lf_all = jnp.where(mask, lf_all, 0.0)
        bc_all = _seg_cumsum(lf_all, rows)
        lane = lax.broadcasted_iota(jnp.int32, gates.shape, 2)
        wide = lambda x: jnp.broadcast_to(x, gates.shape)
        ig = wide(jnp.sum(jnp.where(lane == head, gates, 0.0), axis=2, keepdims=True))
        bc_col = jnp.sum(jnp.where(lane == head + N_HEAD, bc_all, 0.0), axis=2, keepdims=True)
        bc = wide(bc_col)
        log_d = jnp.where(ok, bc_col + _row_bcast(ig - bc, width), -jnp.inf)
        m_col = jnp.max(log_d, axis=2, keepdims=True)
        m_loc = wide(m_col)
        k = k * scale
        qb = q.astype(BF16)
        vb = _pad_rows(v, width).astype(BF16)
        s_m = _bdot_nt(qb, _pad_rows(k, width).astype(BF16)) * jnp.exp(log_d - m_col)
        sl = geo.rows_of(g)
        q_scr[sl] = qb
        v_scr[sl] = _bdot(s_m.astype(BF16), vb)
        rs_scr[sl] = wide(jnp.sum(s_m, axis=2, keepdims=True))
        bc_scr[sl] = bc
        ml_scr[sl] = m_loc
        b_last = bc[:, rows - 1:rows]
        m_upd = m_loc[:, rows - 1:rows]
        w_exp = b_last - bc + ig - m_upd
        if mask is not None:
            w_exp = jnp.where(mask, w_exp, -jnp.inf)
        k_w = k * jnp.exp(w_exp)
        u_scr[sl] = _bdot_tn(_pad_rows(k_w, width).astype(BF16), vb)
        nu_scr[sl] = jnp.sum(k_w, axis=1, keepdims=True)
        bl_scr[sl] = b_last
        mu_scr[sl] = m_upd

    geo.for_groups(phase_a)

    def step(sl, c_state):
        n_state, m_prev = n_ref[...], m_ref[...]
        inter_log = bc_scr[sl] + m_prev
        m_t = jnp.maximum(inter_log, ml_scr[sl])
        w_inter = jnp.exp(inter_log - m_t)
        w_loc = jnp.exp(ml_scr[sl] - m_t)
        qb = q_scr[sl]
        num = w_inter * _bdot(qb, c_state.astype(BF16)) + w_loc * v_scr[sl]
        q_n = jnp.broadcast_to(jnp.sum(qb.astype(F32) * n_state, axis=2, keepdims=True), m_t.shape)
        nq = w_inter * q_n + w_loc * rs_scr[sl]
        v_scr[sl] = num / jnp.maximum(jnp.abs(nq), jnp.exp(-m_t))
        m_new = m_t[:, rows - 1:rows]
        w_prev = jnp.exp(bl_scr[sl] + m_prev - m_new)
        w_upd = jnp.exp(mu_scr[sl] - m_new)
        n_ref[...] = w_prev * n_state + w_upd * nu_scr[sl]
        m_ref[...] = m_new
        return w_prev * c_state + w_upd * u_scr[sl]

    if has_state:
        c_init = st0[0][...]
        n_ref[...] = st0[1][...]
        m_ref[...] = st0[2][...]
    else:
        c_init = jnp.zeros((geo.seqs, D_HEAD, D_HEAD), F32)
        n_ref[...] = jnp.zeros(n_ref.shape, F32)
        m_ref[...] = jnp.zeros(m_ref.shape, F32)
    c_ref[...] = geo.scan_chunks(step, c_init)

    def epilogue(g):
        og = geo.load(parts[3], pads[3] if pads else None, g)
        geo.store(y_ref, g, _rms(_sigmoid(og) * v_scr[geo.rows_of(g)], nw_ref[...]))

    geo.for_groups(epilogue)


def _ret_kernel(*refs, geo, has_state):
    (parts, (cos_ref, sin_ref, nw_ref), r0, _, (y_ref, r_ref), (q_scr, o_scr, u_scr), pads) = _split_refs(
        refs, (4, 3, int(has_state), 1, 2, 3, len(refs)))
    rows, width, valid = geo.rows, geo.width, geo.valid
    head = pl.program_id(1)
    log_gamma = jnp.zeros((1, 1, 1), F32)
    for h in range(N_HEAD):
        log_gamma = jnp.where(head == h, math.log(1.0 - 2.0 ** (-5.0 - h)), log_gamma)
    scale = D_HEAD ** -0.5
    t_i = lax.broadcasted_iota(jnp.int32, (1, rows, width), 1)
    s_i = lax.broadcasted_iota(jnp.int32, (1, rows, width), 2)
    decay = jnp.where(s_i <= t_i, jnp.exp((t_i - s_i).astype(F32) * log_gamma), 0.0)
    t_col = lax.broadcasted_iota(jnp.int32, (1, rows, D_HEAD), 1).astype(F32)
    w_in = jnp.exp((t_col + 1.0) * log_gamma)
    w_st = jnp.exp((valid - 1.0 - t_col) * log_gamma)
    g_all = jnp.exp(valid * log_gamma)

    def rotate(x, cos, sin):
        flat = x.reshape(x.shape[0] * rows, D_HEAD)
        return x * cos + pltpu.roll(flat, D_HEAD // 2, axis=1).reshape(x.shape) * sin

    def phase_a(g):
        q, k, v = (geo.load(parts[p], pads[p] if pads else None, g) for p in range(3))
        if geo.chunks > 1:
            cos, sin = cos_ref[geo.rows_of(g)], sin_ref[geo.rows_of(g)]
        else:
            cos, sin = cos_ref[...], sin_ref[...]
        qr = rotate(q, cos, sin).astype(BF16)
        kr = rotate(k, cos, sin) * scale
        vb = _pad_rows(v, width).astype(BF16)
        a = _bdot_nt(qr, _pad_rows(kr, width).astype(BF16)) * decay
        sl = geo.rows_of(g)
        q_scr[sl] = qr
        o_scr[sl] = _bdot(a.astype(BF16), vb)
        u_scr[sl] = _bdot_tn(_pad_rows(kr * w_st, width).astype(BF16), vb)

    geo.for_groups(phase_a)

    def step(sl, state):
        o_scr[sl] = o_scr[sl] + _bdot(q_scr[sl], state.astype(BF16)) * w_in
        return g_all * state + u_scr[sl]

    init = r0[0][...] if has_state else jnp.zeros((geo.seqs, D_HEAD, D_HEAD), F32)
    r_ref[...] = geo.scan_chunks(step, init)

    def epilogue(g):
        rg = geo.load(parts[3], pads[3] if pads else None, g)
        geo.store(y_ref, g, _rms(o_scr[geo.rows_of(g)], nw_ref[...]) * _silu(rg))

    geo.for_groups(epilogue)


def _mixer_call(kernel, name, geo, layer, z4, mixer, extra, states_in, states_prev, scratch):
    tv = z4.shape[2]
    seqs = geo.seqs

    def part_spec(p):
        col = (mixer * 4 + p) * N_HEAD
        return pl.BlockSpec((seqs, geo.chunks, tv, D_HEAD), lambda bi, j: (bi, 0, 0, col + j))

    def state_spec(arr):
        tail = arr.shape[3:]
        zeros = (0,) * len(tail)
        return pl.BlockSpec((None, seqs, None) + tail, lambda bi, j: (layer, bi, j) + zeros)

    has_state = states_in is not None
    args = [z4] * 4 + [a for a, _ in extra]
    in_specs = [part_spec(p) for p in range(4)] + [s for _, s in extra]
    if has_state:
        args += list(states_in)
        in_specs += [state_spec(a) for a in states_in]
    n_in = len(args)
    args += list(states_prev)
    in_specs += [pl.BlockSpec(memory_space=pl.ANY) for _ in states_prev]
    out_specs = [pl.BlockSpec((seqs, geo.chunks, tv, D_HEAD), lambda bi, j: (bi, 0, 0, j))] + [
        state_spec(a) for a in states_prev]
    out_shape = [jax.ShapeDtypeStruct(z4.shape[:3] + (MIX_W,), F32)] + [
        jax.ShapeDtypeStruct(a.shape, F32) for a in states_prev]
    scratch_shapes = list(scratch)
    if geo.valid < geo.rows:
        n_pads = 4 + (1 if name == "mlstm" else 0)
        scratch_shapes += [pltpu.VMEM((seqs, geo.rows, D_HEAD), F32)] * n_pads
    return pl.pallas_call(
        functools.partial(kernel, geo=geo, has_state=has_state),
        grid=(geo.batch // seqs, N_HEAD),
        in_specs=in_specs,
        out_specs=out_specs,
        out_shape=out_shape,
        scratch_shapes=scratch_shapes,
        input_output_aliases={n_in + i: 1 + i for i in range(len(states_prev))},
        compiler_params=pltpu.CompilerParams(
            dimension_semantics=("arbitrary", "arbitrary"), vmem_limit_bytes=VMEM_LIMIT),
        name=name,
    )(*args)


def _head_block(rows):
    return pl.BlockSpec((rows, D_HEAD), lambda bi, j: (0, j))


def _hgrn(geo, layer, z4, hgrn_lb, nw, state_in, state_prev):
    n = geo.problems
    scratch = [pltpu.VMEM((n, geo.rows, D_HEAD), BF16), pltpu.VMEM((n, geo.rows, D_HEAD), F32),
               pltpu.VMEM((n, D_HEAD, D_HEAD), F32), pltpu.VMEM((n, D_HEAD, D_HEAD), F32)]
    return _mixer_call(
        functools.partial(_hgrn_kernel, layer=layer), "hgrn2", geo, layer, z4, 0,
        [(hgrn_lb, _head_block(hgrn_lb.shape[0])), (nw, _head_block(1))],
        None if state_in is None else [state_in], [state_prev], scratch)


def _mlstm(geo, layer, z4, g4, gate_b, nw, states_in, states_prev):
    n = geo.problems
    col = lambda r: pltpu.VMEM((n, r, LANES), F32)
    scratch = [pltpu.VMEM((n, geo.rows, D_HEAD), BF16), pltpu.VMEM((n, geo.rows, D_HEAD), F32),
               pltpu.VMEM((n, D_HEAD, D_HEAD), F32), pltpu.VMEM((n, 1, D_HEAD), F32),
               col(geo.rows), col(geo.rows), col(geo.rows), col(1), col(1)]
    g_spec = pl.BlockSpec((geo.seqs, geo.chunks, g4.shape[2], LANES), lambda bi, j: (bi, 0, 0, 0))
    return _mixer_call(
        _mlstm_kernel, "mlstm", geo, layer, z4, 1,
        [(g4, g_spec), (gate_b, pl.BlockSpec((1, LANES), lambda bi, j: (0, 0))), (nw, _head_block(1))],
        states_in, states_prev, scratch)


def _retention(geo, layer, z4, cos, sin, nw, state_in, state_prev):
    n = geo.problems
    scratch = [pltpu.VMEM((n, geo.rows, D_HEAD), BF16), pltpu.VMEM((n, geo.rows, D_HEAD), F32),
               pltpu.VMEM((n, D_HEAD, D_HEAD), F32)]
    table = pl.BlockSpec(cos.shape, lambda bi, j: (0, 0, 0))
    return _mixer_call(
        _ret_kernel, "retention", geo, layer, z4, 2,
        [(cos, table), (sin, table), (nw, _head_block(1))],
        None if state_in is None else [state_in], [state_prev], scratch)


def _merge_kernel(x_ref, yh_ref, ym_ref, yr_ref, zg_ref, bg_ref, wb_ref, wo_ref, n2_ref, x1_ref, h2_ref):
    merged = None
    for j, y_ref in enumerate((yh_ref, ym_ref, yr_ref)):
        cols = slice(j * D_MODEL, (j + 1) * D_MODEL)
        gate = _sigmoid(zg_ref[:, cols] + bg_ref[:, cols])
        term = gate * _dot(y_ref[...].astype(BF16), wb_ref[j])
        merged = term if merged is None else merged + term
    x1 = x_ref[...] + _dot(merged.astype(BF16), wo_ref[...])
    x1_ref[...] = x1
    h2_ref[...] = _rms(x1, n2_ref[...]).astype(BF16)


def _merge(x, yh, ym, yr, z, gate_b, w_branch, w_out, norm2_w):
    n = x.shape[0]
    tm = min(TOKEN_TILE, n)
    gate_w = N_BRANCH * D_MODEL
    row = lambda width: pl.BlockSpec((tm, width), lambda i: (i, 0))
    return pl.pallas_call(
        _merge_kernel,
        grid=(n // tm,),
        in_specs=[
            row(D_MODEL), row(MIX_W), row(MIX_W), row(MIX_W),
            pl.BlockSpec((tm, gate_w), lambda i: (i, (N_MAIN - gate_w) // gate_w)),
            pl.BlockSpec((1, gate_w), lambda i: (0, 0)),
            _resident((N_BRANCH, MIX_W, D_MODEL)),
            _resident((D_MODEL, D_MODEL)),
            pl.BlockSpec((1, D_MODEL), lambda i: (0, 0)),
        ],
        out_specs=[row(D_MODEL), row(D_MODEL)],
        out_shape=[jax.ShapeDtypeStruct((n, D_MODEL), F32), jax.ShapeDtypeStruct((n, D_MODEL), BF16)],
        compiler_params=pltpu.CompilerParams(
            dimension_semantics=("arbitrary",), vmem_limit_bytes=VMEM_LIMIT),
        name="merge",
    )(x, yh, ym, yr, z, gate_b, w_branch, w_out, norm2_w)


def _ffn_kernel(*refs, seq_len, has_state, final_norm, tail_rows):
    refs = list(refs)
    x1_ref, h2_ref, wg_ref, wu_ref, wd_ref, cw_ref, cb_ref = refs[:7]
    refs = refs[7:]
    p1_ref = p2_ref = fw_ref = None
    if has_state:
        p1_ref, p2_ref = refs[:2]
        refs = refs[2:]
    if final_norm:
        fw_ref = refs[0]
        refs = refs[1:]
    out_ref, tail_ref, carry_scr, a_scr = refs
    tm = x1_ref.shape[0]
    i = pl.program_id(0)

    @pl.when(i == 0)
    def _():
        carry_scr[...] = jnp.zeros(carry_scr.shape, F32)

    t = (i * tm + lax.broadcasted_iota(jnp.int32, (tm, FF_TILE), 0)) & (seq_len - 1)
    h2 = h2_ref[...]
    acc = x1_ref[...]
    for j in range(D_FF // FF_TILE):
        cols = slice(j * FF_TILE, (j + 1) * FF_TILE)
        a = _dot(h2, wg_ref[:, cols])
        u = _dot(h2, wu_ref[:, cols])
        a_scr[0:SUBLANES, :] = carry_scr[:, cols]
        a_scr[SUBLANES:, :] = a
        carry_scr[:, cols] = a[tm - SUBLANES:, :]
        prev1 = jnp.where(t >= 1, a_scr[SUBLANES - 1:SUBLANES - 1 + tm, :], 0.0)
        prev2 = jnp.where(t >= 2, a_scr[SUBLANES - 2:SUBLANES - 2 + tm, :], 0.0)
        if has_state:
            prev1 = prev1 + p1_ref[:, cols]
            prev2 = prev2 + p2_ref[:, cols]
        conv = prev2 * cw_ref[0:1, cols] + prev1 * cw_ref[1:2, cols] + a * cw_ref[2:3, cols] + cb_ref[:, cols]
        acc = acc + _dot((_silu(conv) * u).astype(BF16), wd_ref[cols, :])
        if tail_rows == tm:
            tail_ref[:, cols] = a
        else:
            tail_ref[0, :, cols] = a[tm - tail_rows:, :]
    if final_norm:
        acc = _rms(acc, fw_ref[...])
    out_ref[...] = acc


def _ffn(x1, h2, wg, wu, wd, conv_w, conv_b, seq_len, conv_state, final_w):
    n = x1.shape[0]
    has_state = conv_state is not None
    tm = min(TOKEN_TILE // 2 if has_state else TOKEN_TILE, n)
    assert n % tm == 0 and seq_len & (seq_len - 1) == 0
    batch = n // seq_len
    row = lambda width: pl.BlockSpec((tm, width), lambda i: (i, 0))
    full = lambda shape: pl.BlockSpec(shape, lambda i: (0,) * len(shape))
    in_specs = [row(D_MODEL), row(D_MODEL), _resident((D_MODEL, D_FF)), _resident((D_MODEL, D_FF)),
                _resident((D_FF, D_MODEL)), full((CONV_W, D_FF)), full((1, D_FF))]
    args = [x1, h2, wg, wu, wd, conv_w, conv_b]
    if has_state:
        p1 = jnp.pad(conv_state[:, 1:2], ((0, 0), (0, seq_len - 1), (0, 0))).reshape(n, D_FF)
        p2 = jnp.pad(conv_state, ((0, 0), (0, seq_len - (CONV_W - 1)), (0, 0))).reshape(n, D_FF)
        in_specs += [row(D_FF), row(D_FF)]
        args += [p1, p2]
    if final_w is not None:
        in_specs.append(full((1, D_MODEL)))
        args.append(final_w)
    if seq_len % tm == 0:
        tail_rows = CONV_W - 1
        tail_spec = pl.BlockSpec((1, tail_rows, D_FF), lambda i: ((i * tm) // seq_len, 0, 0))
        tail_shape = jax.ShapeDtypeStruct((batch, tail_rows, D_FF), F32)
    else:
        assert tm % seq_len == 0 and seq_len >= CONV_W - 1
        tail_rows = tm
        tail_spec = row(D_FF)
        tail_shape = jax.ShapeDtypeStruct((n, D_FF), F32)
    out, tail = pl.pallas_call(
        functools.partial(_ffn_kernel, seq_len=seq_len, has_state=has_state,
                          final_norm=final_w is not None, tail_rows=tail_rows),
        grid=(n // tm,),
        in_specs=in_specs,
        out_specs=[row(D_MODEL), tail_spec],
        out_shape=[jax.ShapeDtypeStruct((n, D_MODEL), F32), tail_shape],
        scratch_shapes=[pltpu.VMEM((SUBLANES, D_FF), F32), pltpu.VMEM((tm + SUBLANES, FF_TILE), F32)],
        compiler_params=pltpu.CompilerParams(
            dimension_semantics=("arbitrary",), vmem_limit_bytes=VMEM_LIMIT),
        name="conv_ffn",
    )(*args)
    if tail_rows == tm:
        tail = tail.reshape(batch, seq_len, D_FF)[:, seq_len - (CONV_W - 1):, :]
    return out, tail


def _prep_weights(w_in, w_branch, w_out, ffn_w_gate, ffn_w_up, ffn_w_down):
    w_main = jnp.concatenate(
        [w_in[:, :, HG_OFF:ML_GATE_OFF], w_in[:, :, RET_OFF:]], axis=2).astype(BF16)
    w_gate = jnp.pad(w_in[:, :, ML_GATE_OFF:RET_OFF],
                     ((0, 0), (0, 0), (0, LANES - 2 * N_HEAD))).astype(BF16)
    return (w_main, w_gate, w_branch.astype(BF16), w_out.astype(BF16),
            ffn_w_gate.astype(BF16), ffn_w_up.astype(BF16), ffn_w_down.astype(BF16))


def _trunk(x3, pos0, states, params, prepped):
    (norm1_w, mlstm_gate_b, branch_gate_b, hgrn_lb, hgrn_norm_w, mlstm_norm_w, ret_norm_w,
     norm2_w, ffn_conv_w, ffn_conv_b, final_norm_w) = params
    w_main, w_gate, w_branch, w_out, w_ffg, w_ffu, w_ffd = prepped
    batch, seq_len, _ = x3.shape
    n = batch * seq_len
    depth = norm1_w.shape[0]
    geo = _Geo(batch, seq_len)
    tv = geo.valid
    cos, sin = (t.reshape(geo.chunks, geo.rows, D_HEAD) for t in _rope_tables(geo.chunks * geo.rows, pos0))

    mat = (depth, batch, N_HEAD, D_HEAD, D_HEAD)
    vec = (depth, batch, N_HEAD, 1, D_HEAD)
    if states is None:
        hg_in = ml_in = ret_in = conv_in = None
    else:
        hg_in, ml_c, ml_n, ml_m, ret_in, conv_in = states
        ml_in = [ml_c, ml_n.reshape(vec), jnp.broadcast_to(ml_m[..., None, None], vec)]
    hg_out, c_out, r_out = (jnp.zeros(mat, F32) for _ in range(3))
    n_out, m_out = (jnp.zeros(vec, F32) for _ in range(2))

    x = x3.reshape(n, D_MODEL)
    conv_tails = []
    for l in range(depth):
        z, g = _proj_in(x, norm1_w[l:l + 1], w_main[l], w_gate[l])
        z4 = z.reshape(batch, geo.chunks, tv, N_MAIN)
        g4 = g.reshape(batch, geo.chunks, tv, LANES)
        y_hg, hg_out = _hgrn(geo, l, z4, hgrn_lb, hgrn_norm_w[l:l + 1], hg_in, hg_out)
        gate_b = jnp.pad(mlstm_gate_b[l:l + 1], ((0, 0), (0, LANES - 2 * N_HEAD)))
        y_ml, c_out, n_out, m_out = _mlstm(geo, l, z4, g4, gate_b, mlstm_norm_w[l:l + 1], ml_in,
                                           [c_out, n_out, m_out])
        y_ret, r_out = _retention(geo, l, z4, cos, sin, ret_norm_w[l:l + 1], ret_in, r_out)
        x1, h2 = _merge(x, y_hg.reshape(n, MIX_W), y_ml.reshape(n, MIX_W), y_ret.reshape(n, MIX_W),
                        z, branch_gate_b[l:l + 1], w_branch[l], w_out[l], norm2_w[l:l + 1])
        x, conv_tail = _ffn(x1, h2, w_ffg[l], w_ffu[l], w_ffd[l], ffn_conv_w[l], ffn_conv_b[l:l + 1],
                            seq_len, None if conv_in is None else conv_in[l],
                            final_norm_w[None, :] if l == depth - 1 else None)
        conv_tails.append(conv_tail)
    return x.reshape(batch, seq_len, D_MODEL), (
        hg_out, c_out, n_out.reshape(depth, batch, N_HEAD, D_HEAD), m_out[:, :, :, 0, 0], r_out,
        jnp.stack(conv_tails))


def kernel(x_prompt, x_sample, state_hgrn, state_mlstm_C, state_mlstm_n, state_mlstm_m, state_ret,
           state_ffn_conv, norm1_w, w_in, mlstm_gate_b, branch_gate_b, hgrn_lb, hgrn_norm_w,
           mlstm_norm_w, ret_norm_w, w_branch, w_out, norm2_w, ffn_w_gate, ffn_w_up, ffn_conv_w,
           ffn_conv_b, ffn_w_down, final_norm_w):
    params = (norm1_w, mlstm_gate_b, branch_gate_b, hgrn_lb, hgrn_norm_w, mlstm_norm_w, ret_norm_w,
              norm2_w, ffn_conv_w, ffn_conv_b, final_norm_w)
    prepped = _prep_weights(w_in, w_branch, w_out, ffn_w_gate, ffn_w_up, ffn_w_down)
    y_p, (hg_p, c_p, n_p, m_p, r_p, cv_p) = _trunk(x_prompt, 0, None, params, prepped)
    y_s, (hg_s, c_s, n_s, m_s, r_s, cv_s) = _trunk(
        x_sample, PAST_LEN,
        (state_hgrn, state_mlstm_C, state_mlstm_n, state_mlstm_m, state_ret, state_ffn_conv),
        params, prepped)
    return (y_p, y_s, hg_p, hg_s, c_p, c_s, n_p, n_s, m_p, m_s, r_p, r_s, cv_p, cv_s)
```

```python
import functools
import math

import jax
import jax.numpy as jnp
from jax import lax
from jax.experimental import pallas as pl
from jax.experimental.pallas import tpu as pltpu

F32 = jnp.float32
BF16 = jnp.bfloat16

D_MODEL = 1024
MIX_W = 512
N_HEAD = 4
D_HEAD = 128
N_BRANCH = 3
D_FF = 2816
CONV_W = 3
CHUNK = 64
EPS = 1e-6
ROPE_BASE = 10000.0
PAST_LEN = 16384

HG_OFF = 0
ML_OFF = 4 * MIX_W
ML_GATE_OFF = ML_OFF + 4 * MIX_W
RET_OFF = ML_GATE_OFF + 2 * N_HEAD
GATE_OFF = RET_OFF + 4 * MIX_W
N_MAIN = 3 * 4 * MIX_W + N_BRANCH * D_MODEL

LANES = 128
SUBLANES = 8
VMEM_LIMIT = 56 * 1024 * 1024

TOKEN_TILE = 512
PROJ_COL_TILE = 2304
FF_TILE = 256
HG_SUB = 16
GROUP = 8
SEQ_BLOCK = 16
SCAN_UNROLL = 8


def _resident(shape):
    return pl.BlockSpec(shape, lambda *_: (0,) * len(shape), pipeline_mode=pl.Buffered(1))


def _dot(a, b):
    return jnp.dot(a, b, preferred_element_type=F32)


def _dot_tn(a, b):
    return lax.dot_general(a, b, (((0,), (0,)), ((), ())), preferred_element_type=F32)


def _bdot(a, b):
    return jnp.einsum('nlk,nkv->nlv', a, b, preferred_element_type=F32)


def _bdot_nt(a, b):
    return jnp.einsum('nqd,nkd->nqk', a, b, preferred_element_type=F32)


def _bdot_tn(a, b):
    return jnp.stack([_dot_tn(a[i], b[i]) for i in range(a.shape[0])])


def _split3(x):
    hi = x.astype(BF16)
    r = x - hi.astype(F32)
    mid = r.astype(BF16)
    lo = (r - mid.astype(F32)).astype(BF16)
    return hi, mid, lo


def _sigmoid(x):
    return 1.0 / (1.0 + jnp.exp(-x))


def _silu(x):
    return x * _sigmoid(x)


def _log_sigmoid(x):
    return jnp.minimum(x, 0.0) - jnp.log(1.0 + jnp.exp(-jnp.abs(x)))


def _rms(x, w):
    return x * lax.rsqrt(jnp.mean(x * x, axis=-1, keepdims=True) + EPS) * w


def _pad_rows(x, n):
    if x.shape[1] >= n:
        return x
    return jnp.concatenate([x, jnp.zeros((x.shape[0], n - x.shape[1], x.shape[2]), x.dtype)], axis=1)


def _seg_cumsum(x, seg):
    shape = x.shape
    flat = x.reshape(shape[0] * seg, shape[2])
    row = lax.broadcasted_iota(jnp.int32, flat.shape, 0) & (seg - 1)
    s = 1
    while s < seg:
        flat = flat + jnp.where(row >= s, pltpu.roll(flat, s, axis=0), 0.0)
        s *= 2
    return flat.reshape(shape)


def _col_bcast(rows):
    n = rows.shape[0]
    r = lax.broadcasted_iota(jnp.int32, (n, D_HEAD, D_HEAD), 1)
    c = lax.broadcasted_iota(jnp.int32, (n, D_HEAD, D_HEAD), 2)
    d = jnp.where(r == c, jnp.broadcast_to(rows, (n, D_HEAD, D_HEAD)), 0.0).reshape(n * D_HEAD, D_HEAD)
    ones = jnp.ones((D_HEAD, D_HEAD), BF16)
    hi, mid, lo = _split3(d)
    return (_dot(hi, ones) + _dot(mid, ones) + _dot(lo, ones)).reshape(n, D_HEAD, D_HEAD)


def _through_vmem(x):
    def body(ref):
        ref[...] = x
        return ref[...]
    return pl.run_scoped(body, pltpu.VMEM(x.shape, x.dtype))


def _row_bcast(col, width):
    n, rows, _ = col.shape
    vals = _through_vmem(_pad_rows(col, width))
    lane = lax.broadcasted_iota(jnp.int32, (n, rows, LANES), 2)
    pick = _through_vmem(jnp.where(lane == 0, 1.0, 0.0).astype(BF16))
    hi, mid, lo = _split3(vals)
    return _bdot_nt(pick, hi) + _bdot_nt(pick, mid) + _bdot_nt(pick, lo)


def _chunk_geometry(seq_len):
    if seq_len % CHUNK == 0:
        return CHUNK, CHUNK, CHUNK
    valid = math.gcd(seq_len, CHUNK)
    assert valid == seq_len and valid <= SUBLANES, "unsupported sequence length"
    return SUBLANES, valid, LANES


def _rope_kernel(inv_ref, cos_ref, sin_ref, *, pos0):
    shape = cos_ref.shape
    pos = lax.broadcasted_iota(jnp.int32, shape, 0).astype(F32) + pos0
    ang = pos * inv_ref[...]
    lane = lax.broadcasted_iota(jnp.int32, shape, 1)
    cos_ref[...] = jnp.cos(ang)
    sin_ref[...] = jnp.where(lane < D_HEAD // 2, -jnp.sin(ang), jnp.sin(ang))


def _rope_tables(rows, pos0):
    inv = ROPE_BASE ** (-jnp.linspace(0.0, 1.0, D_HEAD // 2, dtype=F32))
    inv2 = jnp.concatenate([inv, inv])[None, :]
    return pl.pallas_call(
        functools.partial(_rope_kernel, pos0=float(pos0)),
        out_shape=(jax.ShapeDtypeStruct((rows, D_HEAD), F32),) * 2,
        name="rope_tables",
    )(inv2)


def _proj_in_kernel(x_ref, nw_ref, w_ref, wg_ref, z_ref, g_ref, h_scr):
    j = pl.program_id(1)

    @pl.when(j == 0)
    def _():
        hb = _rms(x_ref[...], nw_ref[...]).astype(BF16)
        h_scr[...] = hb
        g_ref[...] = _dot(hb, wg_ref[...])

    col = pl.multiple_of(j * PROJ_COL_TILE, PROJ_COL_TILE)
    z_ref[...] = _dot(h_scr[...], w_ref[:, pl.ds(col, PROJ_COL_TILE)])


def _proj_in(x, nw, w_main, w_gate):
    n = x.shape[0]
    tm = min(TOKEN_TILE, n)
    assert n % tm == 0
    return pl.pallas_call(
        _proj_in_kernel,
        grid=(n // tm, N_MAIN // PROJ_COL_TILE),
        in_specs=[
            pl.BlockSpec((tm, D_MODEL), lambda i, j: (i, 0)),
            pl.BlockSpec((1, D_MODEL), lambda i, j: (0, 0)),
            _resident((D_MODEL, N_MAIN)),
            _resident((D_MODEL, LANES)),
        ],
        out_specs=[
            pl.BlockSpec((tm, PROJ_COL_TILE), lambda i, j: (i, j)),
            pl.BlockSpec((tm, LANES), lambda i, j: (i, 0)),
        ],
        out_shape=[
            jax.ShapeDtypeStruct((n, N_MAIN), F32),
            jax.ShapeDtypeStruct((n, LANES), F32),
        ],
        scratch_shapes=[pltpu.VMEM((tm, D_MODEL), BF16)],
        compiler_params=pltpu.CompilerParams(
            dimension_semantics=("arbitrary", "arbitrary"), vmem_limit_bytes=VMEM_LIMIT),
        name="proj_in",
    )(x, nw, w_main, w_gate)


class _Geo:
    def __init__(self, batch, seq_len):
        self.rows, self.valid, self.width = _chunk_geometry(seq_len)
        if self.valid == self.rows:
            self.seqs, self.chunks = 1, seq_len // self.rows
            self.group = min(GROUP, self.chunks)
        else:
            self.seqs, self.chunks = min(SEQ_BLOCK, batch), 1
            self.group = self.seqs
        assert batch % self.seqs == 0 and self.chunks % min(self.group, self.chunks) == 0
        self.batch = batch
        self.problems = self.seqs * self.chunks
        self.n_groups = self.problems // self.group

    def load(self, ref, pad_ref, g):
        if self.chunks > 1:
            return ref[0, pl.ds(g * self.group, self.group)]
        x = ref[:, 0]
        if self.valid < self.rows:
            pad_ref[...] = jnp.zeros(pad_ref.shape, F32)
            pad_ref[:, 0:self.valid, :] = x
            x = pad_ref[...]
        return x

    def store(self, ref, g, y):
        if self.chunks > 1:
            ref[0, pl.ds(g * self.group, self.group)] = y
        else:
            ref[:, 0] = y[:, 0:self.valid, :]

    def rows_of(self, g):
        return pl.ds(g * self.group, self.group)

    def row_mask(self, shape):
        if self.valid == self.rows:
            return None
        return lax.broadcasted_iota(jnp.int32, shape, 1) < self.valid

    def for_groups(self, body):
        if self.n_groups == 1:
            body(0)
        else:
            lax.fori_loop(0, self.n_groups, lambda g, c: (body(g), c)[1], 0)

    def scan_chunks(self, step, init):
        if self.chunks == 1:
            return step(pl.ds(0, self.seqs), init)
        return lax.fori_loop(0, self.chunks, lambda c, carry: step(pl.ds(c * self.seqs, self.seqs), carry), init,
                             unroll=min(SCAN_UNROLL, self.chunks))


def _split_refs(refs, counts):
    out, k = [], 0
    for c in counts:
        out.append(refs[k:k + c])
        k += c
    return out


def _hgrn_lower_bound(lb_ref, layer):
    x = lb_ref[...]
    e = jnp.exp(x - jnp.max(x, axis=0, keepdims=True))
    soft = e / jnp.sum(e, axis=0, keepdims=True)
    cum = soft[0:1]
    for j in range(1, layer + 1):
        cum = cum + soft[j:j + 1]
    return cum - soft[0:1]


def _hgrn_intra(q, k, bcum, geo):
    n = q.shape[0]
    sub = min(HG_SUB, geo.rows)
    ones = jnp.ones((D_HEAD, D_HEAD), BF16)
    sub_row = lax.broadcasted_iota(jnp.int32, (n, sub, D_HEAD), 1)
    lane = lax.broadcasted_iota(jnp.int32, (n, sub, geo.width), 2)
    blocks = []
    for i in range(geo.rows // sub):
        r0 = i * sub
        b_i, q_i, k_i = (a[:, r0:r0 + sub] for a in (bcum, q, k))
        terms = []
        for s in range(sub):
            d = jnp.where(sub_row >= s, b_i - b_i[:, s:s + 1], -jnp.inf)
            terms.append(q_i * jnp.exp(d) * k_i[:, s:s + 1])
        stacked = jnp.concatenate(terms, axis=1).reshape(n * sub * sub, D_HEAD)
        sums = _dot(stacked.astype(BF16), ones).reshape(n, sub * sub, D_HEAD)
        a_i = jnp.zeros((n, sub, geo.width), F32)
        for s in range(sub):
            a_i = jnp.where(lane == r0 + s, sums[:, s * sub:(s + 1) * sub, 0:geo.width], a_i)
        if i > 0:
            b_ref = b_i[:, 0:1]
            q_dec = q_i * jnp.exp(b_i - b_ref)
            k_dec = _pad_rows(k[:, 0:r0] * jnp.exp(b_ref - bcum[:, 0:r0]), geo.width)
            a_i = jnp.where(lane < r0, _bdot_nt(q_dec.astype(BF16), k_dec.astype(BF16)), a_i)
        blocks.append(a_i)
    return blocks[0] if len(blocks) == 1 else jnp.concatenate(blocks, axis=1)


def _hgrn_kernel(*refs, layer, geo, has_state):
    parts, (lb_ref, nw_ref), s0, _, (y_ref, s_ref), (qt_scr, o_scr, u_scr, dc_scr), pads = _split_refs(
        refs, (4, 2, int(has_state), 1, 2, 4, len(refs)))
    lb = _hgrn_lower_bound(lb_ref, layer)
    rows, width = geo.rows, geo.width

    def phase_a(g):
        hq, hf, hi = (geo.load(parts[p], pads[p] if pads else None, g) for p in range(3))
        q = _silu(hq)
        f = lb + (1.0 - lb) * _sigmoid(hf)
        lf = jnp.log(f)
        k = 1.0 - f
        mask = geo.row_mask(lf.shape)
        if mask is not None:
            lf = jnp.where(mask, lf, 0.0)
            k = jnp.where(mask, k, 0.0)
        bcum = _seg_cumsum(lf, rows)
        sl = geo.rows_of(g)
        qt_scr[sl] = (q * jnp.exp(bcum)).astype(BF16)
        a = _hgrn_intra(q, k, bcum, geo)
        vb = _pad_rows(hi, width).astype(BF16)
        o_scr[sl] = _bdot(a.astype(BF16), vb)
        b_last = bcum[:, rows - 1:rows]
        k_w = k * jnp.exp(b_last - bcum)
        u_scr[sl] = _bdot_tn(_pad_rows(k_w, width).astype(BF16), vb)
        dc_scr[sl] = _col_bcast(jnp.exp(b_last))

    geo.for_groups(phase_a)

    def step(sl, state):
        o_scr[sl] = o_scr[sl] + _bdot(qt_scr[sl], state.astype(BF16))
        return dc_scr[sl] * state + u_scr[sl]

    init = s0[0][...] if has_state else jnp.zeros((geo.seqs, D_HEAD, D_HEAD), F32)
    s_ref[...] = geo.scan_chunks(step, init)

    def epilogue(g):
        hg = geo.load(parts[3], pads[3] if pads else None, g)
        geo.store(y_ref, g, _rms(o_scr[geo.rows_of(g)], nw_ref[...]) * _silu(hg))

    geo.for_groups(epilogue)


def _mlstm_kernel(*refs, geo, has_state):
    (parts, (g_ref, gb_ref, nw_ref), st0, _, (y_ref, c_ref, n_ref, m_ref),
     (q_scr, v_scr, u_scr, nu_scr, bc_scr, ml_scr, rs_scr, bl_scr, mu_scr), pads) = _split_refs(
        refs, (4, 3, 3 * int(has_state), 3, 4, 9, len(refs)))
    rows, width = geo.rows, geo.width
    head = pl.program_id(1)
    scale = D_HEAD ** -0.5
    t_i = lax.broadcasted_iota(jnp.int32, (1, rows, width), 1)
    s_i = lax.broadcasted_iota(jnp.int32, (1, rows, width), 2)
    ok = s_i <= t_i
    if geo.valid < rows:
        ok = ok & (s_i < geo.valid)

    def phase_a(g):
        q, k, v = (geo.load(parts[p], pads[p] if pads else None, g) for p in range(3))
        gates = geo.load(g_ref, pads[4] if pads else None, g) + gb_ref[...]
        lf_all = _log_sigmoid(gates)
        mask = geo.row_mask(gates.shape)
        if mask is not None:
            lf_all = jnp.where(mask, lf_all, 0.0)
        bc_all = _seg_cumsum(lf_all, rows)
        lane = lax.broadcasted_iota(jnp.int32, gates.shape, 2)
        wide = lambda x: jnp.broadcast_to(x, gates.shape)
        ig = wide(jnp.sum(jnp.where(lane == head, gates, 0.0), axis=2, keepdims=True))
        bc_col = jnp.sum(jnp.where(lane == head + N_HEAD, bc_all, 0.0), axis=2, keepdims=True)
        bc = wide(bc_col)
        log_d = jnp.where(ok, bc_col + _row_bcast(ig - bc, width), -jnp.inf)
        m_col = jnp.max(log_d, axis=2, keepdims=True)
        m_loc = wide(m_col)
        k = k * scale
        qb = q.astype(BF16)
        vb = _pad_rows(v, width).astype(BF16)
        s_m = _bdot_nt(qb, _pad_rows(k, width).astype(BF16)) * jnp.exp(log_d - m_col)
        sl = geo.rows_of(g)
        q_scr[sl] = qb
        v_scr[sl] = _bdot(s_m.astype(BF16), vb)
        rs_scr[sl] = wide(jnp.sum(s_m, axis=2, keepdims=True))
        bc_scr[sl] = bc
        ml_scr[sl] = m_loc
        b_last = bc[:, rows - 1:rows]
        m_upd = m_loc[:, rows - 1:rows]
        w_exp = b_last - bc + ig - m_upd
        if mask is not None:
            w_exp = jnp.where(mask, w_exp, -jnp.inf)
        k_w = k * jnp.exp(w_exp)
        u_scr[sl] = _bdot_tn(_pad_rows(k_w, width).astype(BF16), vb)
        nu_scr[sl] = jnp.sum(k_w, axis=1, keepdims=True)
        bl_scr[sl] = b_last
        mu_scr[sl] = m_upd

    geo.for_groups(phase_a)

    def step(sl, c_state):
        n_state, m_prev = n_ref[...], m_ref[...]
        inter_log = bc_scr[sl] + m_prev
        m_t = jnp.maximum(inter_log, ml_scr[sl])
        w_inter = jnp.exp(inter_log - m_t)
        w_loc = jnp.exp(ml_scr[sl] - m_t)
        qb = q_scr[sl]
        num = w_inter * _bdot(qb, c_state.astype(BF16)) + w_loc * v_scr[sl]
        q_n = jnp.broadcast_to(jnp.sum(qb.astype(F32) * n_state, axis=2, keepdims=True), m_t.shape)
        nq = w_inter * q_n + w_loc * rs_scr[sl]
        v_scr[sl] = num / jnp.maximum(jnp.abs(nq), jnp.exp(-m_t))
        m_new = m_t[:, rows - 1:rows]
        w_prev = jnp.exp(bl_scr[sl] + m_prev - m_new)
        w_upd = jnp.exp(mu_scr[sl] - m_new)
        n_ref[...] = w_prev * n_state + w_upd * nu_scr[sl]
        m_ref[...] = m_new
        return w_prev * c_state + w_upd * u_scr[sl]

    if has_state:
        c_init = st0[0][...]
        n_ref[...] = st0[1][...]
        m_ref[...] = st0[2][...]
    else:
        c_init = jnp.zeros((geo.seqs, D_HEAD, D_HEAD), F32)
        n_ref[...] = jnp.zeros(n_ref.shape, F32)
        m_ref[...] = jnp.zeros(m_ref.shape, F32)
    c_ref[...] = geo.scan_chunks(step, c_init)

    def epilogue(g):
        og = geo.load(parts[3], pads[3] if pads else None, g)
        geo.store(y_ref, g, _rms(_sigmoid(og) * v_scr[geo.rows_of(g)], nw_ref[...]))

    geo.for_groups(epilogue)


def _ret_kernel(*refs, geo, has_state):
    (parts, (cos_ref, sin_ref, nw_ref), r0, _, (y_ref, r_ref), (q_scr, o_scr, u_scr), pads) = _split_refs(
        refs, (4, 3, int(has_state), 1, 2, 3, len(refs)))
    rows, width, valid = geo.rows, geo.width, geo.valid
    head = pl.program_id(1)
    log_gamma = jnp.zeros((1, 1, 1), F32)
    for h in range(N_HEAD):
        log_gamma = jnp.where(head == h, math.log(1.0 - 2.0 ** (-5.0 - h)), log_gamma)
    scale = D_HEAD ** -0.5
    t_i = lax.broadcasted_iota(jnp.int32, (1, rows, width), 1)
    s_i = lax.broadcasted_iota(jnp.int32, (1, rows, width), 2)
    decay = jnp.where(s_i <= t_i, jnp.exp((t_i - s_i).astype(F32) * log_gamma), 0.0)
    t_col = lax.broadcasted_iota(jnp.int32, (1, rows, D_HEAD), 1).astype(F32)
    w_in = jnp.exp((t_col + 1.0) * log_gamma)
    w_st = jnp.exp((valid - 1.0 - t_col) * log_gamma)
    g_all = jnp.exp(valid * log_gamma)

    def rotate(x, cos, sin):
        flat = x.reshape(x.shape[0] * rows, D_HEAD)
        return x * cos + pltpu.roll(flat, D_HEAD // 2, axis=1).reshape(x.shape) * sin

    def phase_a(g):
        q, k, v = (geo.load(parts[p], pads[p] if pads else None, g) for p in range(3))
        if geo.chunks > 1:
            cos, sin = cos_ref[geo.rows_of(g)], sin_ref[geo.rows_of(g)]
        else:
            cos, sin = cos_ref[...], sin_ref[...]
        qr = rotate(q, cos, sin).astype(BF16)
        kr = rotate(k, cos, sin) * scale
        vb = _pad_rows(v, width).astype(BF16)
        a = _bdot_nt(qr, _pad_rows(kr, width).astype(BF16)) * decay
        sl = geo.rows_of(g)
        q_scr[sl] = qr
        o_scr[sl] = _bdot(a.astype(BF16), vb)
        u_scr[sl] = _bdot_tn(_pad_rows(kr * w_st, width).astype(BF16), vb)

    geo.for_groups(phase_a)

    def step(sl, state):
        o_scr[sl] = o_scr[sl] + _bdot(q_scr[sl], state.astype(BF16)) * w_in
        return g_all * state + u_scr[sl]

    init = r0[0][...] if has_state else jnp.zeros((geo.seqs, D_HEAD, D_HEAD), F32)
    r_ref[...] = geo.scan_chunks(step, init)

    def epilogue(g):
        rg = geo.load(parts[3], pads[3] if pads else None, g)
        geo.store(y_ref, g, _rms(o_scr[geo.rows_of(g)], nw_ref[...]) * _silu(rg))

    geo.for_groups(epilogue)


def _mixer_call(kernel, name, geo, layer, z4, mixer, extra, states_in, states_prev, scratch):
    tv = z4.shape[2]
    seqs = geo.seqs

    def part_spec(p):
        col = (mixer * 4 + p) * N_HEAD
        return pl.BlockSpec((seqs, geo.chunks, tv, D_HEAD), lambda bi, j: (bi, 0, 0, col + j))

    def state_spec(arr):
        tail = arr.shape[3:]
        zeros = (0,) * len(tail)
        return pl.BlockSpec((None, seqs, None) + tail, lambda bi, j: (layer, bi, j) + zeros)

    has_state = states_in is not None
    args = [z4] * 4 + [a for a, _ in extra]
    in_specs = [part_spec(p) for p in range(4)] + [s for _, s in extra]
    if has_state:
        args += list(states_in)
        in_specs += [state_spec(a) for a in states_in]
    n_in = len(args)
    args += list(states_prev)
    in_specs += [pl.BlockSpec(memory_space=pl.ANY) for _ in states_prev]
    out_specs = [pl.BlockSpec((seqs, geo.chunks, tv, D_HEAD), lambda bi, j: (bi, 0, 0, j))] + [
        state_spec(a) for a in states_prev]
    out_shape = [jax.ShapeDtypeStruct(z4.shape[:3] + (MIX_W,), F32)] + [
        jax.ShapeDtypeStruct(a.shape, F32) for a in states_prev]
    scratch_shapes = list(scratch)
    if geo.valid < geo.rows:
        n_pads = 4 + (1 if name == "mlstm" else 0)
        scratch_shapes += [pltpu.VMEM((seqs, geo.rows, D_HEAD), F32)] * n_pads
    return pl.pallas_call(
        functools.partial(kernel, geo=geo, has_state=has_state),
        grid=(geo.batch // seqs, N_HEAD),
        in_specs=in_specs,
        out_specs=out_specs,
        out_shape=out_shape,
        scratch_shapes=scratch_shapes,
        input_output_aliases={n_in + i: 1 + i for i in range(len(states_prev))},
        compiler_params=pltpu.CompilerParams(
            dimension_semantics=("arbitrary", "arbitrary"), vmem_limit_bytes=VMEM_LIMIT),
        name=name,
    )(*args)


def _head_block(rows):
    return pl.BlockSpec((rows, D_HEAD), lambda bi, j: (0, j))


def _hgrn(geo, layer, z4, hgrn_lb, nw, state_in, state_prev):
    n = geo.problems
    scratch = [pltpu.VMEM((n, geo.rows, D_HEAD), BF16), pltpu.VMEM((n, geo.rows, D_HEAD), F32),
               pltpu.VMEM((n, D_HEAD, D_HEAD), F32), pltpu.VMEM((n, D_HEAD, D_HEAD), F32)]
    return _mixer_call(
        functools.partial(_hgrn_kernel, layer=layer), "hgrn2", geo, layer, z4, 0,
        [(hgrn_lb, _head_block(hgrn_lb.shape[0])), (nw, _head_block(1))],
        None if state_in is None else [state_in], [state_prev], scratch)


def _mlstm(geo, layer, z4, g4, gate_b, nw, states_in, states_prev):
    n = geo.problems
    col = lambda r: pltpu.VMEM((n, r, LANES), F32)
    scratch = [pltpu.VMEM((n, geo.rows, D_HEAD), BF16), pltpu.VMEM((n, geo.rows, D_HEAD), F32),
               pltpu.VMEM((n, D_HEAD, D_HEAD), F32), pltpu.VMEM((n, 1, D_HEAD), F32),
               col(geo.rows), col(geo.rows), col(geo.rows), col(1), col(1)]
    g_spec = pl.BlockSpec((geo.seqs, geo.chunks, g4.shape[2], LANES), lambda bi, j: (bi, 0, 0, 0))
    return _mixer_call(
        _mlstm_kernel, "mlstm", geo, layer, z4, 1,
        [(g4, g_spec), (gate_b, pl.BlockSpec((1, LANES), lambda bi, j: (0, 0))), (nw, _head_block(1))],
        states_in, states_prev, scratch)


def _retention(geo, layer, z4, cos, sin, nw, state_in, state_prev):
    n = geo.problems
    scratch = [pltpu.VMEM((n, geo.rows, D_HEAD), BF16), pltpu.VMEM((n, geo.rows, D_HEAD), F32),
               pltpu.VMEM((n, D_HEAD, D_HEAD), F32)]
    table = pl.BlockSpec(cos.shape, lambda bi, j: (0, 0, 0))
    return _mixer_call(
        _ret_kernel, "retention", geo, layer, z4, 2,
        [(cos, table), (sin, table), (nw, _head_block(1))],
        None if state_in is None else [state_in], [state_prev], scratch)


def _merge_kernel(x_ref, yh_ref, ym_ref, yr_ref, zg_ref, bg_ref, wb_ref, wo_ref, n2_ref, x1_ref, h2_ref):
    merged = None
    for j, y_ref in enumerate((yh_ref, ym_ref, yr_ref)):
        cols = slice(j * D_MODEL, (j + 1) * D_MODEL)
        gate = _sigmoid(zg_ref[:, cols] + bg_ref[:, cols])
        term = gate * _dot(y_ref[...].astype(BF16), wb_ref[j])
        merged = term if merged is None else merged + term
    x1 = x_ref[...] + _dot(merged.astype(BF16), wo_ref[...])
    x1_ref[...] = x1
    h2_ref[...] = _rms(x1, n2_ref[...]).astype(BF16)


def _merge(x, yh, ym, yr, z, gate_b, w_branch, w_out, norm2_w):
    n = x.shape[0]
    tm = min(TOKEN_TILE, n)
    gate_w = N_BRANCH * D_MODEL
    row = lambda width: pl.BlockSpec((tm, width), lambda i: (i, 0))
    return pl.pallas_call(
        _merge_kernel,
        grid=(n // tm,),
        in_specs=[
            row(D_MODEL), row(MIX_W), row(MIX_W), row(MIX_W),
            pl.BlockSpec((tm, gate_w), lambda i: (i, (N_MAIN - gate_w) // gate_w)),
            pl.BlockSpec((1, gate_w), lambda i: (0, 0)),
            _resident((N_BRANCH, MIX_W, D_MODEL)),
            _resident((D_MODEL, D_MODEL)),
            pl.BlockSpec((1, D_MODEL), lambda i: (0, 0)),
        ],
        out_specs=[row(D_MODEL), row(D_MODEL)],
        out_shape=[jax.ShapeDtypeStruct((n, D_MODEL), F32), jax.ShapeDtypeStruct((n, D_MODEL), BF16)],
        compiler_params=pltpu.CompilerParams(
            dimension_semantics=("arbitrary",), vmem_limit_bytes=VMEM_LIMIT),
        name="merge",
    )(x, yh, ym, yr, z, gate_b, w_branch, w_out, norm2_w)


def _ffn_kernel(*refs, seq_len, has_state, final_norm, tail_rows):
    refs = list(refs)
    x1_ref, h2_ref, wg_ref, wu_ref, wd_ref, cw_ref, cb_ref = refs[:7]
    refs = refs[7:]
    p1_ref = p2_ref = fw_ref = None
    if has_state:
        p1_ref, p2_ref = refs[:2]
        refs = refs[2:]
    if final_norm:
        fw_ref = refs[0]
        refs = refs[1:]
    out_ref, tail_ref, carry_scr, a_scr = refs
    tm = x1_ref.shape[0]
    i = pl.program_id(0)

    @pl.when(i == 0)
    def _():
        carry_scr[...] = jnp.zeros(carry_scr.shape, F32)

    t = (i * tm + lax.broadcasted_iota(jnp.int32, (tm, FF_TILE), 0)) & (seq_len - 1)
    h2 = h2_ref[...]
    acc = x1_ref[...]
    for j in range(D_FF // FF_TILE):
        cols = slice(j * FF_TILE, (j + 1) * FF_TILE)
        a = _dot(h2, wg_ref[:, cols])
        u = _dot(h2, wu_ref[:, cols])
        a_scr[0:SUBLANES, :] = carry_scr[:, cols]
        a_scr[SUBLANES:, :] = a
        carry_scr[:, cols] = a[tm - SUBLANES:, :]
        prev1 = jnp.where(t >= 1, a_scr[SUBLANES - 1:SUBLANES - 1 + tm, :], 0.0)
        prev2 = jnp.where(t >= 2, a_scr[SUBLANES - 2:SUBLANES - 2 + tm, :], 0.0)
        if has_state:
            prev1 = prev1 + p1_ref[:, cols]
            prev2 = prev2 + p2_ref[:, cols]
        conv = prev2 * cw_ref[0:1, cols] + prev1 * cw_ref[1:2, cols] + a * cw_ref[2:3, cols] + cb_ref[:, cols]
        acc = acc + _dot((_silu(conv) * u).astype(BF16), wd_ref[cols, :])
        if tail_rows == tm:
            tail_ref[:, cols] = a
        else:
            tail_ref[0, :, cols] = a[tm - tail_rows:, :]
    if final_norm:
        acc = _rms(acc, fw_ref[...])
    out_ref[...] = acc


def _ffn(x1, h2, wg, wu, wd, conv_w, conv_b, seq_len, conv_state, final_w):
    n = x1.shape[0]
    has_state = conv_state is not None
    tm = min(TOKEN_TILE // 2 if has_state else TOKEN_TILE, n)
    assert n % tm == 0 and seq_len & (seq_len - 1) == 0
    batch = n // seq_len
    row = lambda width: pl.BlockSpec((tm, width), lambda i: (i, 0))
    full = lambda shape: pl.BlockSpec(shape, lambda i: (0,) * len(shape))
    in_specs = [row(D_MODEL), row(D_MODEL), _resident((D_MODEL, D_FF)), _resident((D_MODEL, D_FF)),
                _resident((D_FF, D_MODEL)), full((CONV_W, D_FF)), full((1, D_FF))]
    args = [x1, h2, wg, wu, wd, conv_w, conv_b]
    if has_state:
        p1 = jnp.pad(conv_state[:, 1:2], ((0, 0), (0, seq_len - 1), (0, 0))).reshape(n, D_FF)
        p2 = jnp.pad(conv_state, ((0, 0), (0, seq_len - (CONV_W - 1)), (0, 0))).reshape(n, D_FF)
        in_specs += [row(D_FF), row(D_FF)]
        args += [p1, p2]
    if final_w is not None:
        in_specs.append(full((1, D_MODEL)))
        args.append(final_w)
    if seq_len % tm == 0:
        tail_rows = CONV_W - 1
        tail_spec = pl.BlockSpec((1, tail_rows, D_FF), lambda i: ((i * tm) // seq_len, 0, 0))
        tail_shape = jax.ShapeDtypeStruct((batch, tail_rows, D_FF), F32)
    else:
        assert tm % seq_len == 0 and seq_len >= CONV_W - 1
        tail_rows = tm
        tail_spec = row(D_FF)
        tail_shape = jax.ShapeDtypeStruct((n, D_FF), F32)
    out, tail = pl.pallas_call(
        functools.partial(_ffn_kernel, seq_len=seq_len, has_state=has_state,
                          final_norm=final_w is not None, tail_rows=tail_rows),
        grid=(n // tm,),
        in_specs=in_specs,
        out_specs=[row(D_MODEL), tail_spec],
        out_shape=[jax.ShapeDtypeStruct((n, D_MODEL), F32), tail_shape],
        scratch_shapes=[pltpu.VMEM((SUBLANES, D_FF), F32), pltpu.VMEM((tm + SUBLANES, FF_TILE), F32)],
        compiler_params=pltpu.CompilerParams(
            dimension_semantics=("arbitrary",), vmem_limit_bytes=VMEM_LIMIT),
        name="conv_ffn",
    )(*args)
    if tail_rows == tm:
        tail = tail.reshape(batch, seq_len, D_FF)[:, seq_len - (CONV_W - 1):, :]
    return out, tail


def _prep_weights(w_in, w_branch, w_out, ffn_w_gate, ffn_w_up, ffn_w_down):
    w_in = w_in.astype(BF16)
    w_main = jnp.concatenate([w_in[:, :, HG_OFF:ML_GATE_OFF], w_in[:, :, RET_OFF:]], axis=2)
    w_gate = jnp.pad(w_in[:, :, ML_GATE_OFF:RET_OFF], ((0, 0), (0, 0), (0, LANES - 2 * N_HEAD)))
    return (w_main, w_gate, w_branch.astype(BF16), w_out.astype(BF16),
            ffn_w_gate.astype(BF16), ffn_w_up.astype(BF16), ffn_w_down.astype(BF16))


def _trunk(x3, pos0, states, params, prepped):
    (norm1_w, mlstm_gate_b, branch_gate_b, hgrn_lb, hgrn_norm_w, mlstm_norm_w, ret_norm_w,
     norm2_w, ffn_conv_w, ffn_conv_b, final_norm_w) = params
    w_main, w_gate, w_branch, w_out, w_ffg, w_ffu, w_ffd = prepped
    batch, seq_len, _ = x3.shape
    n = batch * seq_len
    depth = norm1_w.shape[0]
    geo = _Geo(batch, seq_len)
    tv = geo.valid
    cos, sin = (t.reshape(geo.chunks, geo.rows, D_HEAD) for t in _rope_tables(geo.chunks * geo.rows, pos0))

    mat = (depth, batch, N_HEAD, D_HEAD, D_HEAD)
    vec = (depth, batch, N_HEAD, 1, D_HEAD)
    if states is None:
        hg_in = ml_in = ret_in = conv_in = None
    else:
        hg_in, ml_c, ml_n, ml_m, ret_in, conv_in = states
        ml_in = [ml_c, ml_n.reshape(vec), jnp.broadcast_to(ml_m[..., None, None], vec)]
    hg_out, c_out, r_out = (jnp.zeros(mat, F32) for _ in range(3))
    n_out, m_out = (jnp.zeros(vec, F32) for _ in range(2))

    x = x3.reshape(n, D_MODEL)
    conv_tails = []
    for l in range(depth):
        z, g = _proj_in(x, norm1_w[l:l + 1], w_main[l], w_gate[l])
        z4 = z.reshape(batch, geo.chunks, tv, N_MAIN)
        g4 = g.reshape(batch, geo.chunks, tv, LANES)
        y_hg, hg_out = _hgrn(geo, l, z4, hgrn_lb, hgrn_norm_w[l:l + 1], hg_in, hg_out)
        gate_b = jnp.pad(mlstm_gate_b[l:l + 1], ((0, 0), (0, LANES - 2 * N_HEAD)))
        y_ml, c_out, n_out, m_out = _mlstm(geo, l, z4, g4, gate_b, mlstm_norm_w[l:l + 1], ml_in,
                                           [c_out, n_out, m_out])
        y_ret, r_out = _retention(geo, l, z4, cos, sin, ret_norm_w[l:l + 1], ret_in, r_out)
        x1, h2 = _merge(x, y_hg.reshape(n, MIX_W), y_ml.reshape(n, MIX_W), y_ret.reshape(n, MIX_W),
                        z, branch_gate_b[l:l + 1], w_branch[l], w_out[l], norm2_w[l:l + 1])
        x, conv_tail = _ffn(x1, h2, w_ffg[l], w_ffu[l], w_ffd[l], ffn_conv_w[l], ffn_conv_b[l:l + 1],
                            seq_len, None if conv_in is None else conv_in[l],
                            final_norm_w[None, :] if l == depth - 1 else None)
        conv_tails.append(conv_tail)
    return x.reshape(batch, seq_len, D_MODEL), (
        hg_out, c_out, n_out.reshape(depth, batch, N_HEAD, D_HEAD), m_out[:, :, :, 0, 0], r_out,
        jnp.stack(conv_tails))


def kernel(x_prompt, x_sample, state_hgrn, state_mlstm_C, state_mlstm_n, state_mlstm_m, state_ret,
           state_ffn_conv, norm1_w, w_in, mlstm_gate_b, branch_gate_b, hgrn_lb, hgrn_norm_w,
           mlstm_norm_w, ret_norm_w, w_branch, w_out, norm2_w, ffn_w_gate, ffn_w_up, ffn_conv_w,
           ffn_conv_b, ffn_w_down, final_norm_w):
    params = (norm1_w, mlstm_gate_b, branch_gate_b, hgrn_lb, hgrn_norm_w, mlstm_norm_w, ret_norm_w,
              norm2_w, ffn_conv_w, ffn_conv_b, final_norm_w)
    prepped = _prep_weights(w_in, w_branch, w_out, ffn_w_gate, ffn_w_up, ffn_w_down)
    y_p, (hg_p, c_p, n_p, m_p, r_p, cv_p) = _trunk(x_prompt, 0, None, params, prepped)
    y_s, (hg_s, c_s, n_s, m_s, r_s, cv_s) = _trunk(
        x_sample, PAST_LEN,
        (state_hgrn, state_mlstm_C, state_mlstm_n, state_mlstm_m, state_ret, state_ffn_conv),
        params, prepped)
    return (y_p, y_s, hg_p, hg_s, c_p, c_s, n_p, n_s, m_p, m_s, r_p, r_s, cv_p, cv_s)
```

```python
import functools
import math

import numpy as np
import jax
import jax.numpy as jnp
from jax import lax
from jax.experimental import pallas as pl
from jax.experimental.pallas import tpu as pltpu

F32 = jnp.float32
BF16 = jnp.bfloat16

D_MODEL = 1024
MIX_W = 512
N_HEAD = 4
D_HEAD = 128
N_BRANCH = 3
D_FF = 2816
CONV_W = 3
CHUNK = 64
EPS = 1e-6
ROPE_BASE = 10000.0
PAST_LEN = 16384

HG_OFF = 0
ML_OFF = 4 * MIX_W
ML_GATE_OFF = ML_OFF + 4 * MIX_W
RET_OFF = ML_GATE_OFF + 2 * N_HEAD
GATE_OFF = RET_OFF + 4 * MIX_W
N_MAIN = 3 * 4 * MIX_W
LOG2E = 1.4426950408889634

LANES = 128
SUBLANES = 8
VMEM_LIMIT = 56 * 1024 * 1024

TOKEN_TILE = 512
PROJ_TOKEN_TILE = 1024
PROJ_COL_TILE = 2048
FF_TILE = 2816
HG_SUB = 16
GROUP = 8
SEQ_BLOCK = 16
SCAN_UNROLL = 8


def _resident(shape):
    return pl.BlockSpec(shape, lambda *_: (0,) * len(shape), pipeline_mode=pl.Buffered(1))


def _dot(a, b):
    return jnp.dot(a, b, preferred_element_type=F32)


def _dot_tn(a, b):
    return lax.dot_general(a, b, (((0,), (0,)), ((), ())), preferred_element_type=F32)


def _bdot(a, b):
    return jnp.einsum('nlk,nkv->nlv', a, b, preferred_element_type=F32)


def _bdot_nt(a, b):
    return jnp.einsum('nqd,nkd->nqk', a, b, preferred_element_type=F32)


def _bdot_tn(a, b):
    return jnp.stack([_dot_tn(a[i], b[i]) for i in range(a.shape[0])])


def _split3(x):
    hi = x.astype(BF16)
    r = x - hi.astype(F32)
    mid = r.astype(BF16)
    lo = (r - mid.astype(F32)).astype(BF16)
    return hi, mid, lo


def _sigmoid(x):
    return 1.0 / (1.0 + jnp.exp(-x))


def _silu(x):
    return x * _sigmoid(x)


def _log_sigmoid(x):
    return jnp.minimum(x, 0.0) - jnp.log(1.0 + jnp.exp(-jnp.abs(x)))


def _rms(x, w):
    return x * lax.rsqrt(jnp.mean(x * x, axis=-1, keepdims=True) + EPS) * w


def _pad_rows(x, n):
    if x.shape[1] >= n:
        return x
    return jnp.concatenate([x, jnp.zeros((x.shape[0], n - x.shape[1], x.shape[2]), x.dtype)], axis=1)


def _seg_cumsum(x, seg):
    shape = x.shape
    flat = x.reshape(shape[0] * seg, shape[2])
    row = lax.broadcasted_iota(jnp.int32, flat.shape, 0) & (seg - 1)
    s = 1
    while s < seg:
        flat = flat + jnp.where(row >= s, pltpu.roll(flat, s, axis=0), 0.0)
        s *= 2
    return flat.reshape(shape)


def _col_bcast(rows):
    n = rows.shape[0]
    r = lax.broadcasted_iota(jnp.int32, (n, D_HEAD, D_HEAD), 1)
    c = lax.broadcasted_iota(jnp.int32, (n, D_HEAD, D_HEAD), 2)
    d = jnp.where(r == c, jnp.broadcast_to(rows, (n, D_HEAD, D_HEAD)), 0.0).reshape(n * D_HEAD, D_HEAD)
    ones = jnp.ones((D_HEAD, D_HEAD), BF16)
    hi, mid, lo = _split3(d)
    return (_dot(hi, ones) + _dot(mid, ones) + _dot(lo, ones)).reshape(n, D_HEAD, D_HEAD)


def _through_vmem(x):
    def body(ref):
        ref[...] = x
        return ref[...]
    return pl.run_scoped(body, pltpu.VMEM(x.shape, x.dtype))


def _row_bcast(col, width):
    n, rows, _ = col.shape
    vals = _through_vmem(_pad_rows(col, width))
    lane = lax.broadcasted_iota(jnp.int32, (n, rows, LANES), 2)
    pick = _through_vmem(jnp.where(lane == 0, 1.0, 0.0).astype(BF16))
    hi, mid, lo = _split3(vals)
    return _bdot_nt(pick, hi) + _bdot_nt(pick, mid) + _bdot_nt(pick, lo)


def _chunk_geometry(seq_len):
    if seq_len % CHUNK == 0:
        return CHUNK, CHUNK, CHUNK
    valid = math.gcd(seq_len, CHUNK)
    assert valid == seq_len and valid <= SUBLANES, "unsupported sequence length"
    return SUBLANES, valid, LANES


def _rope_kernel(inv_ref, cos_ref, sin_ref, *, pos0):
    shape = cos_ref.shape
    pos = lax.broadcasted_iota(jnp.int32, shape, 0).astype(F32) + pos0
    ang = pos * inv_ref[...]
    lane = lax.broadcasted_iota(jnp.int32, shape, 1)
    cos_ref[...] = jnp.cos(ang)
    sin_ref[...] = jnp.where(lane < D_HEAD // 2, -jnp.sin(ang), jnp.sin(ang))


def _rope_tables(rows, pos0):
    inv = ROPE_BASE ** (-jnp.linspace(0.0, 1.0, D_HEAD // 2, dtype=F32))
    inv2 = jnp.concatenate([inv, inv])[None, :]
    return pl.pallas_call(
        functools.partial(_rope_kernel, pos0=float(pos0)),
        out_shape=(jax.ShapeDtypeStruct((rows, D_HEAD), F32),) * 2,
        name="rope_tables",
    )(inv2)


def _proj_in_kernel(x_ref, nw_ref, w_ref, wg_ref, z_ref, g_ref, h_scr):
    j = pl.program_id(1)

    @pl.when(j == 0)
    def _():
        hb = _rms(x_ref[...], nw_ref[...]).astype(BF16)
        h_scr[...] = hb
        g_ref[...] = _dot(hb, wg_ref[...])

    col = pl.multiple_of(j * PROJ_COL_TILE, PROJ_COL_TILE)
    z_ref[...] = _dot(h_scr[...], w_ref[:, pl.ds(col, PROJ_COL_TILE)])


def _proj_in(x, nw, w_main, w_gate):
    n = x.shape[0]
    tm = min(PROJ_TOKEN_TILE, n)
    assert n % tm == 0
    return pl.pallas_call(
        _proj_in_kernel,
        grid=(n // tm, N_MAIN // PROJ_COL_TILE),
        in_specs=[
            pl.BlockSpec((tm, D_MODEL), lambda i, j: (i, 0)),
            pl.BlockSpec((1, D_MODEL), lambda i, j: (0, 0)),
            _resident((D_MODEL, N_MAIN)),
            _resident((D_MODEL, LANES)),
        ],
        out_specs=[
            pl.BlockSpec((tm, PROJ_COL_TILE), lambda i, j: (i, j)),
            pl.BlockSpec((tm, LANES), lambda i, j: (i, 0)),
        ],
        out_shape=[
            jax.ShapeDtypeStruct((n, N_MAIN), F32),
            jax.ShapeDtypeStruct((n, LANES), F32),
        ],
        scratch_shapes=[pltpu.VMEM((tm, D_MODEL), BF16)],
        compiler_params=pltpu.CompilerParams(
            dimension_semantics=("arbitrary", "arbitrary"), vmem_limit_bytes=VMEM_LIMIT),
        name="proj_in",
    )(x, nw, w_main, w_gate)


class _Geo:
    def __init__(self, batch, seq_len):
        self.rows, self.valid, self.width = _chunk_geometry(seq_len)
        if self.valid == self.rows:
            self.seqs, self.chunks = 1, seq_len // self.rows
            self.group = min(GROUP, self.chunks)
        else:
            self.seqs, self.chunks = min(SEQ_BLOCK, batch), 1
            self.group = self.seqs
        assert batch % self.seqs == 0 and self.chunks % min(self.group, self.chunks) == 0
        self.batch = batch
        self.problems = self.seqs * self.chunks
        self.n_groups = self.problems // self.group

    def load(self, ref, pad_ref, g):
        if self.chunks > 1:
            return ref[0, pl.ds(g * self.group, self.group)]
        x = ref[:, 0]
        if self.valid < self.rows:
            pad_ref[...] = jnp.zeros(pad_ref.shape, F32)
            pad_ref[:, 0:self.valid, :] = x
            x = pad_ref[...]
        return x

    def store(self, ref, g, y):
        y = y.astype(ref.dtype)
        if self.chunks > 1:
            ref[0, pl.ds(g * self.group, self.group)] = y
        else:
            ref[:, 0] = y[:, 0:self.valid, :]

    def rows_of(self, g):
        return pl.ds(g * self.group, self.group)

    def row_mask(self, shape):
        if self.valid == self.rows:
            return None
        return lax.broadcasted_iota(jnp.int32, shape, 1) < self.valid

    def for_groups(self, body):
        if self.n_groups == 1:
            body(0)
        else:
            lax.fori_loop(0, self.n_groups, lambda g, c: (body(g), c)[1], 0)

    def scan_chunks(self, step, init):
        if self.chunks == 1:
            return step(pl.ds(0, self.seqs), init)
        return lax.fori_loop(0, self.chunks, lambda c, carry: step(pl.ds(c * self.seqs, self.seqs), carry), init,
                             unroll=min(SCAN_UNROLL, self.chunks))


def _split_refs(refs, counts):
    out, k = [], 0
    for c in counts:
        out.append(refs[k:k + c])
        k += c
    return out


def _hgrn_lower_bound(lb_ref, layer):
    x = lb_ref[...]
    e = jnp.exp(x - jnp.max(x, axis=0, keepdims=True))
    soft = e / jnp.sum(e, axis=0, keepdims=True)
    cum = soft[0:1]
    for j in range(1, layer + 1):
        cum = cum + soft[j:j + 1]
    return cum - soft[0:1]


def _hgrn_placement(geo):
    sub = min(HG_SUB, geo.rows)
    r = np.arange(sub * D_HEAD)[:, None] // D_HEAD
    c = np.arange(geo.width)[None, :]
    return jnp.asarray(np.stack([r + i * sub == c for i in range(geo.rows // sub)]), BF16)


def _hgrn_intra(q, k, b2, place_ref, geo):
    n = q.shape[0]
    sub = min(HG_SUB, geo.rows)
    sub_row = lax.broadcasted_iota(jnp.int32, (n, sub, geo.width), 1)
    lane = lax.broadcasted_iota(jnp.int32, (n, sub, geo.width), 2)
    blocks = []
    for i in range(geo.rows // sub):
        r0 = i * sub
        b_i, q_i, k_i = (a[:, r0:r0 + sub] for a in (b2, q, k))
        terms = []
        for s in range(sub):
            lo = s // SUBLANES * SUBLANES
            e = jnp.exp2(jnp.minimum(b_i[:, lo:] - b_i[:, s:s + 1], 0.0))
            term = q_i[:, lo:] * e * k_i[:, s:s + 1]
            if lo:
                term = jnp.concatenate([jnp.zeros((n, lo, D_HEAD), F32), term], axis=1)
            terms.append(term)
        cat = jnp.concatenate(terms, axis=2).reshape(n * sub, sub * D_HEAD)
        a_i = _dot(cat.astype(BF16), place_ref[i]).reshape(n, sub, geo.width)
        a_i = jnp.where(sub_row >= lane - r0, a_i, 0.0)
        if i > 0:
            b_ref = b_i[:, 0:1]
            q_dec = q_i * jnp.exp2(b_i - b_ref)
            k_dec = _pad_rows(k[:, 0:r0] * jnp.exp2(b_ref - b2[:, 0:r0]), geo.width)
            a_i = jnp.where(lane < r0, _bdot_nt(q_dec.astype(BF16), k_dec.astype(BF16)), a_i)
        blocks.append(a_i)
    return blocks[0] if len(blocks) == 1 else jnp.concatenate(blocks, axis=1)


def _hgrn_kernel(*refs, layer, geo, has_state):
    parts, (lb_ref, nw_ref, place), s0, _, (y_ref, s_ref), (qt_scr, o_scr, u_scr, dc_scr), pads = _split_refs(
        refs, (4, 3, int(has_state), 1, 2, 4, len(refs)))
    lb = _hgrn_lower_bound(lb_ref, layer)
    rows, width = geo.rows, geo.width

    def phase_a(g):
        hq, hf, hi = (geo.load(parts[p], pads[p] if pads else None, g) for p in range(3))
        q = _silu(hq)
        f = lb + (1.0 - lb) * _sigmoid(hf)
        lf = jnp.log(f)
        k = 1.0 - f
        mask = geo.row_mask(lf.shape)
        if mask is not None:
            lf = jnp.where(mask, lf, 0.0)
            k = jnp.where(mask, k, 0.0)
        b2 = _seg_cumsum(lf, rows) * LOG2E
        sl = geo.rows_of(g)
        qt_scr[sl] = (q * jnp.exp2(b2)).astype(BF16)
        a = _hgrn_intra(q, k, b2, place, geo)
        vb = _pad_rows(hi, width).astype(BF16)
        o_scr[sl] = _bdot(a.astype(BF16), vb)
        b_last = b2[:, rows - 1:rows]
        k_w = k * jnp.exp2(b_last - b2)
        u_scr[sl] = _bdot_tn(_pad_rows(k_w, width).astype(BF16), vb)
        dc_scr[sl] = _col_bcast(jnp.exp2(b_last))

    geo.for_groups(phase_a)

    def step(sl, state):
        o_scr[sl] = o_scr[sl] + _bdot(qt_scr[sl], state.astype(BF16))
        return dc_scr[sl] * state + u_scr[sl]

    init = s0[0][...] if has_state else jnp.zeros((geo.seqs, D_HEAD, D_HEAD), F32)
    s_ref[...] = geo.scan_chunks(step, init)

    def epilogue(g):
        hg = geo.load(parts[3], pads[3] if pads else None, g)
        geo.store(y_ref, g, _rms(o_scr[geo.rows_of(g)], nw_ref[...]) * _silu(hg))

    geo.for_groups(epilogue)


def _mlstm_kernel(*refs, geo, has_state):
    (parts, (g_ref, gb_ref, nw_ref), st0, _, (y_ref, c_ref, n_ref, m_ref),
     (q_scr, v_scr, u_scr, nu_scr, bc_scr, ml_scr, rs_scr, bl_scr, mu_scr), pads) = _split_refs(
        refs, (4, 3, 3 * int(has_state), 3, 4, 9, len(refs)))
    rows, width = geo.rows, geo.width
    head = pl.program_id(1)
    scale = D_HEAD ** -0.5
    t_i = lax.broadcasted_iota(jnp.int32, (1, rows, width), 1)
    s_i = lax.broadcasted_iota(jnp.int32, (1, rows, width), 2)
    ok = s_i <= t_i
    if geo.valid < rows:
        ok = ok & (s_i < geo.valid)

    def phase_a(g):
        q, k, v = (geo.load(parts[p], pads[p] if pads else None, g) for p in range(3))
        gates = geo.load(g_ref, pads[4] if pads else None, g) + gb_ref[...]
        lf_all = _log_sigmoid(gates)
        mask = geo.row_mask(gates.shape)
        if mask is not None:
            lf_all = jnp.where(mask, lf_all, 0.0)
        bc_all = _seg_cumsum(lf_all, rows)
        lane = lax.broadcasted_iota(jnp.int32, gates.shape, 2)
        wide = lambda x: jnp.broadcast_to(x, gates.shape)
        ig = wide(jnp.sum(jnp.where(lane == head, gates, 0.0), axis=2, keepdims=True))
        bc_col = jnp.sum(jnp.where(lane == head + N_HEAD, bc_all, 0.0), axis=2, keepdims=True)
        bc = wide(bc_col)
        log_d = jnp.where(ok, bc_col + _row_bcast(ig - bc, width), -jnp.inf)
        m_col = jnp.max(log_d, axis=2, keepdims=True)
        m_loc = wide(m_col)
        k = k * scale
        qb = q.astype(BF16)
        vb = _pad_rows(v, width).astype(BF16)
        s_m = _bdot_nt(qb, _pad_rows(k, width).astype(BF16)) * jnp.exp(log_d - m_col)
        sl = geo.rows_of(g)
        q_scr[sl] = qb
        v_scr[sl] = _bdot(s_m.astype(BF16), vb)
        rs_scr[sl] = wide(jnp.sum(s_m, axis=2, keepdims=True))
        bc_scr[sl] = bc
        ml_scr[sl] = m_loc
        b_last = bc[:, rows - 1:rows]
        m_upd = m_loc[:, rows - 1:rows]
        w_exp = b_last - bc + ig - m_upd
        if mask is not None:
            w_exp = jnp.where(mask, w_exp, -jnp.inf)
        k_w = k * jnp.exp(w_exp)
        u_scr[sl] = _bdot_tn(_pad_rows(k_w, width).astype(BF16), vb)
        nu_scr[sl] = jnp.sum(k_w, axis=1, keepdims=True)
        bl_scr[sl] = b_last
        mu_scr[sl] = m_upd

    geo.for_groups(phase_a)

    def step(sl, c_state):
        n_state, m_prev = n_ref[...], m_ref[...]
        inter_log = bc_scr[sl] + m_prev
        m_t = jnp.maximum(inter_log, ml_scr[sl])
        w_inter = jnp.exp(inter_log - m_t)
        w_loc = jnp.exp(ml_scr[sl] - m_t)
        qb = q_scr[sl]
        num = w_inter * _bdot(qb, c_state.astype(BF16)) + w_loc * v_scr[sl]
        q_n = jnp.broadcast_to(jnp.sum(qb.astype(F32) * n_state, axis=2, keepdims=True), m_t.shape)
        nq = w_inter * q_n + w_loc * rs_scr[sl]
        v_scr[sl] = num / jnp.maximum(jnp.abs(nq), jnp.exp(-m_t))
        m_new = m_t[:, rows - 1:rows]
        w_prev = jnp.exp(bl_scr[sl] + m_prev - m_new)
        w_upd = jnp.exp(mu_scr[sl] - m_new)
        n_ref[...] = w_prev * n_state + w_upd * nu_scr[sl]
        m_ref[...] = m_new
        return w_prev * c_state + w_upd * u_scr[sl]

    if has_state:
        c_init = st0[0][...]
        n_ref[...] = st0[1][...]
        m_ref[...] = st0[2][...]
    else:
        c_init = jnp.zeros((geo.seqs, D_HEAD, D_HEAD), F32)
        n_ref[...] = jnp.zeros(n_ref.shape, F32)
        m_ref[...] = jnp.zeros(m_ref.shape, F32)
    c_ref[...] = geo.scan_chunks(step, c_init)

    def epilogue(g):
        og = geo.load(parts[3], pads[3] if pads else None, g)
        geo.store(y_ref, g, _rms(_sigmoid(og) * v_scr[geo.rows_of(g)], nw_ref[...]))

    geo.for_groups(epilogue)


def _ret_kernel(*refs, geo, has_state):
    (parts, (cos_ref, sin_ref, nw_ref), r0, _, (y_ref, r_ref), (q_scr, o_scr, u_scr), pads) = _split_refs(
        refs, (4, 3, int(has_state), 1, 2, 3, len(refs)))
    rows, width, valid = geo.rows, geo.width, geo.valid
    head = pl.program_id(1)
    log_gamma = jnp.zeros((1, 1, 1), F32)
    for h in range(N_HEAD):
        log_gamma = jnp.where(head == h, math.log(1.0 - 2.0 ** (-5.0 - h)), log_gamma)
    scale = D_HEAD ** -0.5
    t_i = lax.broadcasted_iota(jnp.int32, (1, rows, width), 1)
    s_i = lax.broadcasted_iota(jnp.int32, (1, rows, width), 2)
    decay = jnp.where(s_i <= t_i, jnp.exp((t_i - s_i).astype(F32) * log_gamma), 0.0)
    t_col = lax.broadcasted_iota(jnp.int32, (1, rows, D_HEAD), 1).astype(F32)
    w_in = jnp.exp((t_col + 1.0) * log_gamma)
    w_st = jnp.exp((valid - 1.0 - t_col) * log_gamma)
    g_all = jnp.exp(valid * log_gamma)

    def rotate(x, cos, sin):
        flat = x.reshape(x.shape[0] * rows, D_HEAD)
        return x * cos + pltpu.roll(flat, D_HEAD // 2, axis=1).reshape(x.shape) * sin

    def phase_a(g):
        q, k, v = (geo.load(parts[p], pads[p] if pads else None, g) for p in range(3))
        if geo.chunks > 1:
            cos, sin = cos_ref[geo.rows_of(g)], sin_ref[geo.rows_of(g)]
        else:
            cos, sin = cos_ref[...], sin_ref[...]
        qr = rotate(q, cos, sin).astype(BF16)
        kr = rotate(k, cos, sin) * scale
        vb = _pad_rows(v, width).astype(BF16)
        a = _bdot_nt(qr, _pad_rows(kr, width).astype(BF16)) * decay
        sl = geo.rows_of(g)
        q_scr[sl] = qr
        o_scr[sl] = _bdot(a.astype(BF16), vb)
        u_scr[sl] = _bdot_tn(_pad_rows(kr * w_st, width).astype(BF16), vb)

    geo.for_groups(phase_a)

    def step(sl, state):
        o_scr[sl] = o_scr[sl] + _bdot(q_scr[sl], state.astype(BF16)) * w_in
        return g_all * state + u_scr[sl]

    init = r0[0][...] if has_state else jnp.zeros((geo.seqs, D_HEAD, D_HEAD), F32)
    r_ref[...] = geo.scan_chunks(step, init)

    def epilogue(g):
        rg = geo.load(parts[3], pads[3] if pads else None, g)
        geo.store(y_ref, g, _rms(o_scr[geo.rows_of(g)], nw_ref[...]) * _silu(rg))

    geo.for_groups(epilogue)


def _mixer_call(kernel, name, geo, layer, z4, mixer, extra, states_in, states_prev, scratch):
    tv = z4.shape[2]
    seqs = geo.seqs

    def part_spec(p):
        col = (mixer * 4 + p) * N_HEAD
        return pl.BlockSpec((seqs, geo.chunks, tv, D_HEAD), lambda bi, j: (bi, 0, 0, col + j))

    def state_spec(arr):
        tail = arr.shape[3:]
        zeros = (0,) * len(tail)
        return pl.BlockSpec((None, seqs, None) + tail, lambda bi, j: (layer, bi, j) + zeros)

    has_state = states_in is not None
    args = [z4] * 4 + [a for a, _ in extra]
    in_specs = [part_spec(p) for p in range(4)] + [s for _, s in extra]
    if has_state:
        args += list(states_in)
        in_specs += [state_spec(a) for a in states_in]
    n_in = len(args)
    args += list(states_prev)
    in_specs += [pl.BlockSpec(memory_space=pl.ANY) for _ in states_prev]
    out_specs = [pl.BlockSpec((seqs, geo.chunks, tv, D_HEAD), lambda bi, j: (bi, 0, 0, j))] + [
        state_spec(a) for a in states_prev]
    out_shape = [jax.ShapeDtypeStruct(z4.shape[:3] + (MIX_W,), BF16)] + [
        jax.ShapeDtypeStruct(a.shape, F32) for a in states_prev]
    scratch_shapes = list(scratch)
    if geo.valid < geo.rows:
        n_pads = 4 + (1 if name == "mlstm" else 0)
        scratch_shapes += [pltpu.VMEM((seqs, geo.rows, D_HEAD), F32)] * n_pads
    return pl.pallas_call(
        functools.partial(kernel, geo=geo, has_state=has_state),
        grid=(geo.batch // seqs, N_HEAD),
        in_specs=in_specs,
        out_specs=out_specs,
        out_shape=out_shape,
        scratch_shapes=scratch_shapes,
        input_output_aliases={n_in + i: 1 + i for i in range(len(states_prev))},
        compiler_params=pltpu.CompilerParams(
            dimension_semantics=("arbitrary", "arbitrary"), vmem_limit_bytes=VMEM_LIMIT),
        name=name,
    )(*args)


def _head_block(rows):
    return pl.BlockSpec((rows, D_HEAD), lambda bi, j: (0, j))


def _hgrn(geo, layer, z4, hgrn_lb, nw, state_in, state_prev):
    n = geo.problems
    scratch = [pltpu.VMEM((n, geo.rows, D_HEAD), BF16), pltpu.VMEM((n, geo.rows, D_HEAD), F32),
               pltpu.VMEM((n, D_HEAD, D_HEAD), F32), pltpu.VMEM((n, D_HEAD, D_HEAD), F32)]
    place = _hgrn_placement(geo)
    return _mixer_call(
        functools.partial(_hgrn_kernel, layer=layer), "hgrn2", geo, layer, z4, 0,
        [(hgrn_lb, _head_block(hgrn_lb.shape[0])), (nw, _head_block(1)), (place, _resident(place.shape))],
        None if state_in is None else [state_in], [state_prev], scratch)


def _mlstm(geo, layer, z4, g4, gate_b, nw, states_in, states_prev):
    n = geo.problems
    col = lambda r: pltpu.VMEM((n, r, LANES), F32)
    scratch = [pltpu.VMEM((n, geo.rows, D_HEAD), BF16), pltpu.VMEM((n, geo.rows, D_HEAD), F32),
               pltpu.VMEM((n, D_HEAD, D_HEAD), F32), pltpu.VMEM((n, 1, D_HEAD), F32),
               col(geo.rows), col(geo.rows), col(geo.rows), col(1), col(1)]
    g_spec = pl.BlockSpec((geo.seqs, geo.chunks, g4.shape[2], LANES), lambda bi, j: (bi, 0, 0, 0))
    return _mixer_call(
        _mlstm_kernel, "mlstm", geo, layer, z4, 1,
        [(g4, g_spec), (gate_b, pl.BlockSpec((1, LANES), lambda bi, j: (0, 0))), (nw, _head_block(1))],
        states_in, states_prev, scratch)


def _retention(geo, layer, z4, cos, sin, nw, state_in, state_prev):
    n = geo.problems
    scratch = [pltpu.VMEM((n, geo.rows, D_HEAD), BF16), pltpu.VMEM((n, geo.rows, D_HEAD), F32),
               pltpu.VMEM((n, D_HEAD, D_HEAD), F32)]
    table = pl.BlockSpec(cos.shape, lambda bi, j: (0, 0, 0))
    return _mixer_call(
        _ret_kernel, "retention", geo, layer, z4, 2,
        [(cos, table), (sin, table), (nw, _head_block(1))],
        None if state_in is None else [state_in], [state_prev], scratch)


def _merge_kernel(x_ref, n1_ref, yh_ref, ym_ref, yr_ref, wg_ref, bg_ref, wb_ref, wo_ref, n2_ref, x1_ref, h2_ref):
    x = x_ref[...]
    hb = _rms(x, n1_ref[...]).astype(BF16)
    merged = None
    for j, y_ref in enumerate((yh_ref, ym_ref, yr_ref)):
        cols = slice(j * D_MODEL, (j + 1) * D_MODEL)
        gate = _sigmoid(_dot(hb, wg_ref[:, cols]) + bg_ref[:, cols])
        term = gate * _dot(y_ref[...], wb_ref[j])
        merged = term if merged is None else merged + term
    x1 = x + _dot(merged.astype(BF16), wo_ref[...])
    x1_ref[...] = x1
    h2_ref[...] = _rms(x1, n2_ref[...]).astype(BF16)


def _merge(x, norm1_w, yh, ym, yr, w_bgate, gate_b, w_branch, w_out, norm2_w):
    n = x.shape[0]
    tm = min(TOKEN_TILE, n)
    gate_w = N_BRANCH * D_MODEL
    row = lambda width: pl.BlockSpec((tm, width), lambda i: (i, 0))
    vec = lambda width: pl.BlockSpec((1, width), lambda i: (0, 0))
    return pl.pallas_call(
        _merge_kernel,
        grid=(n // tm,),
        in_specs=[
            row(D_MODEL), vec(D_MODEL), row(MIX_W), row(MIX_W), row(MIX_W),
            _resident((D_MODEL, gate_w)), vec(gate_w),
            _resident((N_BRANCH, MIX_W, D_MODEL)),
            _resident((D_MODEL, D_MODEL)),
            vec(D_MODEL),
        ],
        out_specs=[row(D_MODEL), row(D_MODEL)],
        out_shape=[jax.ShapeDtypeStruct((n, D_MODEL), F32), jax.ShapeDtypeStruct((n, D_MODEL), BF16)],
        compiler_params=pltpu.CompilerParams(
            dimension_semantics=("arbitrary",), vmem_limit_bytes=VMEM_LIMIT),
        name="merge",
    )(x, norm1_w, yh, ym, yr, w_bgate, gate_b, w_branch, w_out, norm2_w)


def _ffn_kernel(*refs, seq_len, has_state, final_norm, tail_rows):
    refs = list(refs)
    x1_ref, h2_ref, wg_ref, wu_ref, wd_ref, cw_ref, cb_ref = refs[:7]
    refs = refs[7:]
    p1_ref = p2_ref = fw_ref = None
    if has_state:
        p1_ref, p2_ref = refs[:2]
        refs = refs[2:]
    if final_norm:
        fw_ref = refs[0]
        refs = refs[1:]
    out_ref, tail_ref, carry_scr, a_scr = refs
    tm = x1_ref.shape[0]
    i = pl.program_id(0)

    @pl.when(i == 0)
    def _():
        carry_scr[...] = jnp.zeros(carry_scr.shape, F32)

    t = (i * tm + lax.broadcasted_iota(jnp.int32, (tm, FF_TILE), 0)) & (seq_len - 1)
    h2 = h2_ref[...]
    acc = x1_ref[...]
    for j in range(D_FF // FF_TILE):
        cols = slice(j * FF_TILE, (j + 1) * FF_TILE)
        a = _dot(h2, wg_ref[:, cols])
        u = _dot(h2, wu_ref[:, cols])
        a_scr[0:SUBLANES, :] = carry_scr[:, cols]
        a_scr[SUBLANES:, :] = a
        carry_scr[:, cols] = a[tm - SUBLANES:, :]
        prev1 = jnp.where(t >= 1, a_scr[SUBLANES - 1:SUBLANES - 1 + tm, :], 0.0)
        prev2 = jnp.where(t >= 2, a_scr[SUBLANES - 2:SUBLANES - 2 + tm, :], 0.0)
        if has_state:
            prev1 = prev1 + p1_ref[:, cols]
            prev2 = prev2 + p2_ref[:, cols]
        conv = prev2 * cw_ref[0:1, cols] + prev1 * cw_ref[1:2, cols] + a * cw_ref[2:3, cols] + cb_ref[:, cols]
        acc = acc + _dot((_silu(conv) * u).astype(BF16), wd_ref[cols, :])
        if tail_rows == tm:
            tail_ref[:, cols] = a
        else:
            tail_ref[0, :, cols] = a[tm - tail_rows:, :]
    if final_norm:
        acc = _rms(acc, fw_ref[...])
    out_ref[...] = acc


def _ffn(x1, h2, wg, wu, wd, conv_w, conv_b, seq_len, conv_state, final_w):
    n = x1.shape[0]
    has_state = conv_state is not None
    tm = min(TOKEN_TILE // 2 if has_state else TOKEN_TILE, n)
    assert n % tm == 0 and seq_len & (seq_len - 1) == 0
    batch = n // seq_len
    row = lambda width: pl.BlockSpec((tm, width), lambda i: (i, 0))
    full = lambda shape: pl.BlockSpec(shape, lambda i: (0,) * len(shape))
    in_specs = [row(D_MODEL), row(D_MODEL), _resident((D_MODEL, D_FF)), _resident((D_MODEL, D_FF)),
                _resident((D_FF, D_MODEL)), full((CONV_W, D_FF)), full((1, D_FF))]
    args = [x1, h2, wg, wu, wd, conv_w, conv_b]
    if has_state:
        p1 = jnp.pad(conv_state[:, 1:2], ((0, 0), (0, seq_len - 1), (0, 0))).reshape(n, D_FF)
        p2 = jnp.pad(conv_state, ((0, 0), (0, seq_len - (CONV_W - 1)), (0, 0))).reshape(n, D_FF)
        in_specs += [row(D_FF), row(D_FF)]
        args += [p1, p2]
    if final_w is not None:
        in_specs.append(full((1, D_MODEL)))
        args.append(final_w)
    if seq_len % tm == 0:
        tail_rows = CONV_W - 1
        tail_spec = pl.BlockSpec((1, tail_rows, D_FF), lambda i: ((i * tm) // seq_len, 0, 0))
        tail_shape = jax.ShapeDtypeStruct((batch, tail_rows, D_FF), F32)
    else:
        assert tm % seq_len == 0 and seq_len >= CONV_W - 1
        tail_rows = tm
        tail_spec = row(D_FF)
        tail_shape = jax.ShapeDtypeStruct((n, D_FF), F32)
    out, tail = pl.pallas_call(
        functools.partial(_ffn_kernel, seq_len=seq_len, has_state=has_state,
                          final_norm=final_w is not None, tail_rows=tail_rows),
        grid=(n // tm,),
        in_specs=in_specs,
        out_specs=[row(D_MODEL), tail_spec],
        out_shape=[jax.ShapeDtypeStruct((n, D_MODEL), F32), tail_shape],
        scratch_shapes=[pltpu.VMEM((SUBLANES, D_FF), F32), pltpu.VMEM((tm + SUBLANES, FF_TILE), F32)],
        compiler_params=pltpu.CompilerParams(
            dimension_semantics=("arbitrary",), vmem_limit_bytes=VMEM_LIMIT),
        name="conv_ffn",
    )(*args)
    if tail_rows == tm:
        tail = tail.reshape(batch, seq_len, D_FF)[:, seq_len - (CONV_W - 1):, :]
    return out, tail


def _prep_weights(w_in, w_branch, w_out, ffn_w_gate, ffn_w_up, ffn_w_down):
    w_in = w_in.astype(BF16)
    w_main = jnp.concatenate([w_in[:, :, HG_OFF:ML_GATE_OFF], w_in[:, :, RET_OFF:GATE_OFF]], axis=2)
    w_gate = jnp.pad(w_in[:, :, ML_GATE_OFF:RET_OFF], ((0, 0), (0, 0), (0, LANES - 2 * N_HEAD)))
    return (w_main, w_gate, w_in[:, :, GATE_OFF:], w_branch.astype(BF16), w_out.astype(BF16),
            ffn_w_gate.astype(BF16), ffn_w_up.astype(BF16), ffn_w_down.astype(BF16))


def _trunk(x3, pos0, states, params, prepped):
    (norm1_w, mlstm_gate_b, branch_gate_b, hgrn_lb, hgrn_norm_w, mlstm_norm_w, ret_norm_w,
     norm2_w, ffn_conv_w, ffn_conv_b, final_norm_w) = params
    w_main, w_gate, w_bgate, w_branch, w_out, w_ffg, w_ffu, w_ffd = prepped
    batch, seq_len, _ = x3.shape
    n = batch * seq_len
    depth = norm1_w.shape[0]
    geo = _Geo(batch, seq_len)
    tv = geo.valid
    cos, sin = (t.reshape(geo.chunks, geo.rows, D_HEAD) for t in _rope_tables(geo.chunks * geo.rows, pos0))

    mat = (depth, batch, N_HEAD, D_HEAD, D_HEAD)
    vec = (depth, batch, N_HEAD, 1, D_HEAD)
    if states is None:
        hg_in = ml_in = ret_in = conv_in = None
    else:
        hg_in, ml_c, ml_n, ml_m, ret_in, conv_in = states
        ml_in = [ml_c, ml_n.reshape(vec), jnp.broadcast_to(ml_m[..., None, None], vec)]
    hg_out, c_out, r_out = (jnp.zeros(mat, F32) for _ in range(3))
    n_out, m_out = (jnp.zeros(vec, F32) for _ in range(2))

    x = x3.reshape(n, D_MODEL)
    conv_tails = []
    for l in range(depth):
        z, g = _proj_in(x, norm1_w[l:l + 1], w_main[l], w_gate[l])
        z4 = z.reshape(batch, geo.chunks, tv, N_MAIN)
        g4 = g.reshape(batch, geo.chunks, tv, LANES)
        y_hg, hg_out = _hgrn(geo, l, z4, hgrn_lb, hgrn_norm_w[l:l + 1], hg_in, hg_out)
        gate_b = jnp.pad(mlstm_gate_b[l:l + 1], ((0, 0), (0, LANES - 2 * N_HEAD)))
        y_ml, c_out, n_out, m_out = _mlstm(geo, l, z4, g4, gate_b, mlstm_norm_w[l:l + 1], ml_in,
                                           [c_out, n_out, m_out])
        y_ret, r_out = _retention(geo, l, z4, cos, sin, ret_norm_w[l:l + 1], ret_in, r_out)
        x1, h2 = _merge(x, norm1_w[l:l + 1], y_hg.reshape(n, MIX_W), y_ml.reshape(n, MIX_W),
                        y_ret.reshape(n, MIX_W), w_bgate[l], branch_gate_b[l:l + 1], w_branch[l], w_out[l],
                        norm2_w[l:l + 1])
        x, conv_tail = _ffn(x1, h2, w_ffg[l], w_ffu[l], w_ffd[l], ffn_conv_w[l], ffn_conv_b[l:l + 1],
                            seq_len, None if conv_in is None else conv_in[l],
                            final_norm_w[None, :] if l == depth - 1 else None)
        conv_tails.append(conv_tail)
    return x.reshape(batch, seq_len, D_MODEL), (
        hg_out, c_out, n_out.reshape(depth, batch, N_HEAD, D_HEAD), m_out[:, :, :, 0, 0], r_out,
        jnp.stack(conv_tails))


def kernel(x_prompt, x_sample, state_hgrn, state_mlstm_C, state_mlstm_n, state_mlstm_m, state_ret,
           state_ffn_conv, norm1_w, w_in, mlstm_gate_b, branch_gate_b, hgrn_lb, hgrn_norm_w,
           mlstm_norm_w, ret_norm_w, w_branch, w_out, norm2_w, ffn_w_gate, ffn_w_up, ffn_conv_w,
           ffn_conv_b, ffn_w_down, final_norm_w):
    params = (norm1_w, mlstm_gate_b, branch_gate_b, hgrn_lb, hgrn_norm_w, mlstm_norm_w, ret_norm_w,
              norm2_w, ffn_conv_w, ffn_conv_b, final_norm_w)
    prepped = _prep_weights(w_in, w_branch, w_out, ffn_w_gate, ffn_w_up, ffn_w_down)
    y_p, (hg_p, c_p, n_p, m_p, r_p, cv_p) = _trunk(x_prompt, 0, None, params, prepped)
    y_s, (hg_s, c_s, n_s, m_s, r_s, cv_s) = _trunk(
        x_sample, PAST_LEN,
        (state_hgrn, state_mlstm_C, state_mlstm_n, state_mlstm_m, state_ret, state_ffn_conv),
        params, prepped)
    return (y_p, y_s, hg_p, hg_s, c_p, c_s, n_p, n_s, m_p, m_s, r_p, r_s, cv_p, cv_s)
```

```python
import functools
import math

import numpy as np
import jax
import jax.numpy as jnp
from jax import lax
from jax.experimental import pallas as pl
from jax.experimental.pallas import tpu as pltpu

F32 = jnp.float32
BF16 = jnp.bfloat16

D_MODEL = 1024
MIX_W = 512
N_HEAD = 4
D_HEAD = 128
N_BRANCH = 3
D_FF = 2816
CONV_W = 3
CHUNK = 64
EPS = 1e-6
ROPE_BASE = 10000.0
PAST_LEN = 16384

HG_OFF = 0
ML_OFF = 4 * MIX_W
ML_GATE_OFF = ML_OFF + 4 * MIX_W
RET_OFF = ML_GATE_OFF + 2 * N_HEAD
GATE_OFF = RET_OFF + 4 * MIX_W
LOG2E = 1.4426950408889634

LANES = 128
SUBLANES = 8
VMEM_LIMIT = 56 * 1024 * 1024

TOKEN_TILE = 512
NORM_TOKEN_TILE = 1024
FF_TILE = 2816
HG_SUB = 16
GROUP = 8
SEQ_BLOCK = 64
SCAN_UNROLL = 8


def _resident(shape):
    return pl.BlockSpec(shape, lambda *_: (0,) * len(shape), pipeline_mode=pl.Buffered(1))


def _dot(a, b):
    return jnp.dot(a, b, preferred_element_type=F32)


def _dot_tn(a, b):
    return lax.dot_general(a, b, (((0,), (0,)), ((), ())), preferred_element_type=F32)


def _bdot(a, b):
    return jnp.einsum('nlk,nkv->nlv', a, b, preferred_element_type=F32)


def _bdot_nt(a, b):
    return jnp.einsum('nqd,nkd->nqk', a, b, preferred_element_type=F32)


def _bdot_tn(a, b):
    return jnp.stack([_dot_tn(a[i], b[i]) for i in range(a.shape[0])])


def _split3(x):
    hi = x.astype(BF16)
    r = x - hi.astype(F32)
    mid = r.astype(BF16)
    lo = (r - mid.astype(F32)).astype(BF16)
    return hi, mid, lo


def _sigmoid(x):
    return 1.0 / (1.0 + jnp.exp(-x))


def _silu(x):
    return x * _sigmoid(x)


def _log_sigmoid(x):
    return jnp.minimum(x, 0.0) - jnp.log(1.0 + jnp.exp(-jnp.abs(x)))


def _rms(x, w):
    return x * lax.rsqrt(jnp.mean(x * x, axis=-1, keepdims=True) + EPS) * w


def _pad_rows(x, n):
    if x.shape[1] >= n:
        return x
    return jnp.concatenate([x, jnp.zeros((x.shape[0], n - x.shape[1], x.shape[2]), x.dtype)], axis=1)


def _seg_cumsum(x, seg):
    shape = x.shape
    flat = x.reshape(shape[0] * seg, shape[2])
    row = lax.broadcasted_iota(jnp.int32, flat.shape, 0) & (seg - 1)
    s = 1
    while s < seg:
        flat = flat + jnp.where(row >= s, pltpu.roll(flat, s, axis=0), 0.0)
        s *= 2
    return flat.reshape(shape)


def _col_bcast(rows):
    n = rows.shape[0]
    r = lax.broadcasted_iota(jnp.int32, (n, D_HEAD, D_HEAD), 1)
    c = lax.broadcasted_iota(jnp.int32, (n, D_HEAD, D_HEAD), 2)
    d = jnp.where(r == c, jnp.broadcast_to(rows, (n, D_HEAD, D_HEAD)), 0.0).reshape(n * D_HEAD, D_HEAD)
    ones = jnp.ones((D_HEAD, D_HEAD), BF16)
    hi, mid, lo = _split3(d)
    return (_dot(hi, ones) + _dot(mid, ones) + _dot(lo, ones)).reshape(n, D_HEAD, D_HEAD)


def _through_vmem(x):
    def body(ref):
        ref[...] = x
        return ref[...]
    return pl.run_scoped(body, pltpu.VMEM(x.shape, x.dtype))


def _row_bcast(col, width):
    n, rows, _ = col.shape
    vals = _through_vmem(_pad_rows(col, width))
    lane = lax.broadcasted_iota(jnp.int32, (n, rows, LANES), 2)
    pick = _through_vmem(jnp.where(lane == 0, 1.0, 0.0).astype(BF16))
    hi, mid, lo = _split3(vals)
    return _bdot_nt(pick, hi) + _bdot_nt(pick, mid) + _bdot_nt(pick, lo)


def _chunk_geometry(seq_len):
    if seq_len % CHUNK == 0:
        return CHUNK, CHUNK, CHUNK
    valid = math.gcd(seq_len, CHUNK)
    assert valid == seq_len and valid <= SUBLANES, "unsupported sequence length"
    return SUBLANES, valid, LANES


def _rope_kernel(inv_ref, cos_ref, sin_ref, *, pos0):
    shape = cos_ref.shape
    pos = lax.broadcasted_iota(jnp.int32, shape, 0).astype(F32) + pos0
    ang = pos * inv_ref[...]
    lane = lax.broadcasted_iota(jnp.int32, shape, 1)
    cos_ref[...] = jnp.cos(ang)
    sin_ref[...] = jnp.where(lane < D_HEAD // 2, -jnp.sin(ang), jnp.sin(ang))


def _rope_tables(rows, pos0):
    inv = ROPE_BASE ** (-jnp.linspace(0.0, 1.0, D_HEAD // 2, dtype=F32))
    inv2 = jnp.concatenate([inv, inv])[None, :]
    return pl.pallas_call(
        functools.partial(_rope_kernel, pos0=float(pos0)),
        out_shape=(jax.ShapeDtypeStruct((rows, D_HEAD), F32),) * 2,
        name="rope_tables",
    )(inv2)


def _norm_kernel(x_ref, w_ref, h_ref):
    h_ref[...] = _rms(x_ref[...], w_ref[...]).astype(BF16)


def _norm(x, w):
    n = x.shape[0]
    tm = min(NORM_TOKEN_TILE, n)
    assert n % tm == 0
    return pl.pallas_call(
        _norm_kernel,
        grid=(n // tm,),
        in_specs=[pl.BlockSpec((tm, D_MODEL), lambda i: (i, 0)), pl.BlockSpec((1, D_MODEL), lambda i: (0, 0))],
        out_specs=pl.BlockSpec((tm, D_MODEL), lambda i: (i, 0)),
        out_shape=jax.ShapeDtypeStruct((n, D_MODEL), BF16),
        compiler_params=pltpu.CompilerParams(dimension_semantics=("arbitrary",), vmem_limit_bytes=VMEM_LIMIT),
        name="norm_in",
    )(x, w)


class _Geo:
    def __init__(self, batch, seq_len):
        self.rows, self.valid, self.width = _chunk_geometry(seq_len)
        if self.valid == self.rows:
            self.seqs, self.chunks = 1, seq_len // self.rows
            self.group = min(GROUP, self.chunks)
        else:
            self.seqs, self.chunks = min(SEQ_BLOCK, batch), 1
            self.group = self.seqs
        assert batch % self.seqs == 0 and self.chunks % min(self.group, self.chunks) == 0
        self.batch = batch
        self.problems = self.seqs * self.chunks
        self.n_groups = self.problems // self.group
        self.group_tokens = self.group * self.valid
        self.tokens = self.problems * self.valid
        self.padded = self.valid < self.rows

    def project(self, h_ref, w, g):
        return _dot(h_ref[pl.ds(g * self.group_tokens, self.group_tokens), :], w)

    def split(self, z, relay):
        n_parts = z.shape[1] // D_HEAD
        cols = lambda p: slice(p * D_HEAD, (p + 1) * D_HEAD)
        if not self.padded:
            return [z[:, cols(p)].reshape(self.group, self.rows, D_HEAD) for p in range(n_parts)]
        out = []
        for p in range(n_parts):
            flat_ref, pad_ref = relay[2 * p], relay[2 * p + 1]
            flat_ref[...] = z[:, cols(p)]
            pad_ref[...] = jnp.zeros(pad_ref.shape, F32)
            for t in range(self.valid):
                pad_ref[pl.ds(t, self.seqs, stride=self.rows), :] = flat_ref[
                    pl.ds(t, self.seqs, stride=self.valid), :]
            out.append(pad_ref[...].reshape(self.seqs, self.rows, D_HEAD))
        return out

    def store(self, y_ref, g, y, relay):
        if not self.padded:
            n = self.group * self.rows
            y_ref[pl.ds(g * n, n), :] = y.reshape(n, D_HEAD).astype(y_ref.dtype)
            return
        pad = relay[1]
        pad[...] = y.reshape(self.seqs * self.rows, D_HEAD)
        for t in range(self.valid):
            y_ref[pl.ds(t, self.seqs, stride=self.valid), :] = pad[pl.ds(t, self.seqs, stride=self.rows), :].astype(
                y_ref.dtype)

    def rows_of(self, g):
        return pl.ds(g * self.group, self.group)

    def row_mask(self, shape):
        if not self.padded:
            return None
        return lax.broadcasted_iota(jnp.int32, shape, 1) < self.valid

    def for_groups(self, body):
        if self.n_groups == 1:
            body(0)
        else:
            lax.fori_loop(0, self.n_groups, lambda g, c: (body(g), c)[1], 0)

    def for_groups_static(self, body):
        for g in range(self.n_groups):
            body(g)

    def scan_chunks(self, step, init):
        if self.chunks == 1:
            return step(pl.ds(0, self.seqs), init)
        return lax.fori_loop(0, self.chunks, lambda c, carry: step(pl.ds(c * self.seqs, self.seqs), carry), init,
                             unroll=min(SCAN_UNROLL, self.chunks))


def _split_refs(refs, counts):
    out, k = [], 0
    for c in counts:
        out.append(refs[k:k + c])
        k += c
    return out


def _hgrn_lower_bound(lb_ref, layer):
    x = lb_ref[...]
    e = jnp.exp(x - jnp.max(x, axis=0, keepdims=True))
    soft = e / jnp.sum(e, axis=0, keepdims=True)
    cum = soft[0:1]
    for j in range(1, layer + 1):
        cum = cum + soft[j:j + 1]
    return cum - soft[0:1]


def _hgrn_placement(geo):
    sub = min(HG_SUB, geo.rows)
    r = np.arange(sub * D_HEAD)[:, None] // D_HEAD
    c = np.arange(geo.width)[None, :]
    return jnp.asarray(np.stack([r + i * sub == c for i in range(geo.rows // sub)]), BF16)


def _hgrn_intra(q, k, b2, place_ref, geo):
    n = q.shape[0]
    sub = min(HG_SUB, geo.rows)
    sub_row = lax.broadcasted_iota(jnp.int32, (n, sub, geo.width), 1)
    lane = lax.broadcasted_iota(jnp.int32, (n, sub, geo.width), 2)
    blocks = []
    for i in range(geo.rows // sub):
        r0 = i * sub
        b_i, q_i, k_i = (a[:, r0:r0 + sub] for a in (b2, q, k))
        terms = []
        for s in range(sub):
            lo = s // SUBLANES * SUBLANES
            e = jnp.exp2(jnp.minimum(b_i[:, lo:] - b_i[:, s:s + 1], 0.0))
            term = q_i[:, lo:] * e * k_i[:, s:s + 1]
            if lo:
                term = jnp.concatenate([jnp.zeros((n, lo, D_HEAD), F32), term], axis=1)
            terms.append(term)
        cat = jnp.concatenate(terms, axis=2).reshape(n * sub, sub * D_HEAD)
        a_i = _dot(cat.astype(BF16), place_ref[i]).reshape(n, sub, geo.width)
        a_i = jnp.where(sub_row >= lane - r0, a_i, 0.0)
        if i > 0:
            b_ref = b_i[:, 0:1]
            q_dec = q_i * jnp.exp2(b_i - b_ref)
            k_dec = _pad_rows(k[:, 0:r0] * jnp.exp2(b_ref - b2[:, 0:r0]), geo.width)
            a_i = jnp.where(lane < r0, _bdot_nt(q_dec.astype(BF16), k_dec.astype(BF16)), a_i)
        blocks.append(a_i)
    return blocks[0] if len(blocks) == 1 else jnp.concatenate(blocks, axis=1)


def _hgrn_kernel(*refs, layer, geo, has_state):
    ((h_ref, w_ref, lb_ref, nw_ref, place), s0, _, (y_ref, s_ref),
     (qt_scr, o_scr, u_scr, dc_scr, gate_scr), relay) = _split_refs(
        refs, (5, int(has_state), 1, 2, 5, len(refs)))
    lb = _hgrn_lower_bound(lb_ref, layer)
    rows, width = geo.rows, geo.width
    w = w_ref[...]

    def phase_a(g):
        hq, hf, hi, hg = geo.split(geo.project(h_ref, w, g), relay)
        q = _silu(hq)
        f = lb + (1.0 - lb) * _sigmoid(hf)
        lf = jnp.log(f)
        k = 1.0 - f
        mask = geo.row_mask(lf.shape)
        if mask is not None:
            lf = jnp.where(mask, lf, 0.0)
            k = jnp.where(mask, k, 0.0)
        b2 = _seg_cumsum(lf, rows) * LOG2E
        sl = geo.rows_of(g)
        gate_scr[sl] = hg
        qt_scr[sl] = (q * jnp.exp2(b2)).astype(BF16)
        a = _hgrn_intra(q, k, b2, place, geo)
        vb = _pad_rows(hi, width).astype(BF16)
        o_scr[sl] = _bdot(a.astype(BF16), vb)
        b_last = b2[:, rows - 1:rows]
        k_w = k * jnp.exp2(b_last - b2)
        u_scr[sl] = _bdot_tn(_pad_rows(k_w, width).astype(BF16), vb)
        dc_scr[sl] = _col_bcast(jnp.exp2(b_last))

    geo.for_groups_static(phase_a)

    def step(sl, state):
        o_scr[sl] = o_scr[sl] + _bdot(qt_scr[sl], state.astype(BF16))
        return dc_scr[sl] * state + u_scr[sl]

    init = s0[0][...] if has_state else jnp.zeros((geo.seqs, D_HEAD, D_HEAD), F32)
    s_ref[...] = geo.scan_chunks(step, init)

    def epilogue(g):
        sl = geo.rows_of(g)
        geo.store(y_ref, g, _rms(o_scr[sl], nw_ref[...]) * _silu(gate_scr[sl]), relay)

    geo.for_groups(epilogue)


def _mlstm_kernel(*refs, geo, has_state):
    ((h_ref, w_ref, gb_ref, nw_ref), st0, _, (y_ref, c_ref, n_ref, m_ref),
     (q_scr, v_scr, u_scr, nu_scr, bc_scr, ml_scr, rs_scr, bl_scr, mu_scr, gate_scr), relay) = _split_refs(
        refs, (4, 3 * int(has_state), 3, 4, 10, len(refs)))
    rows, width = geo.rows, geo.width
    head = pl.program_id(1)
    scale = D_HEAD ** -0.5
    t_i = lax.broadcasted_iota(jnp.int32, (1, rows, width), 1)
    s_i = lax.broadcasted_iota(jnp.int32, (1, rows, width), 2)
    ok = s_i <= t_i
    if geo.padded:
        ok = ok & (s_i < geo.valid)
    w = w_ref[...]

    def phase_a(g):
        q, k, v, og, gates = geo.split(geo.project(h_ref, w, g), relay)
        gates = gates + gb_ref[...]
        lf_all = _log_sigmoid(gates)
        mask = geo.row_mask(gates.shape)
        if mask is not None:
            lf_all = jnp.where(mask, lf_all, 0.0)
        bc_all = _seg_cumsum(lf_all, rows)
        lane = lax.broadcasted_iota(jnp.int32, gates.shape, 2)
        wide = lambda x: jnp.broadcast_to(x, gates.shape)
        ig = wide(jnp.sum(jnp.where(lane == head, gates, 0.0), axis=2, keepdims=True))
        bc_col = jnp.sum(jnp.where(lane == head + N_HEAD, bc_all, 0.0), axis=2, keepdims=True)
        bc = wide(bc_col)
        log_d = jnp.where(ok, bc_col + _row_bcast(ig - bc, width), -jnp.inf)
        m_col = jnp.max(log_d, axis=2, keepdims=True)
        m_loc = wide(m_col)
        k = k * scale
        qb = q.astype(BF16)
        vb = _pad_rows(v, width).astype(BF16)
        s_m = _bdot_nt(qb, _pad_rows(k, width).astype(BF16)) * jnp.exp(log_d - m_col)
        sl = geo.rows_of(g)
        gate_scr[sl] = og
        q_scr[sl] = qb
        v_scr[sl] = _bdot(s_m.astype(BF16), vb)
        rs_scr[sl] = wide(jnp.sum(s_m, axis=2, keepdims=True))
        bc_scr[sl] = bc
        ml_scr[sl] = m_loc
        b_last = bc[:, rows - 1:rows]
        m_upd = m_loc[:, rows - 1:rows]
        w_exp = b_last - bc + ig - m_upd
        if mask is not None:
            w_exp = jnp.where(mask, w_exp, -jnp.inf)
        k_w = k * jnp.exp(w_exp)
        u_scr[sl] = _bdot_tn(_pad_rows(k_w, width).astype(BF16), vb)
        nu_scr[sl] = jnp.sum(k_w, axis=1, keepdims=True)
        bl_scr[sl] = b_last
        mu_scr[sl] = m_upd

    geo.for_groups_static(phase_a)

    def step(sl, c_state):
        n_state, m_prev = n_ref[...], m_ref[...]
        inter_log = bc_scr[sl] + m_prev
        m_t = jnp.maximum(inter_log, ml_scr[sl])
        w_inter = jnp.exp(inter_log - m_t)
        w_loc = jnp.exp(ml_scr[sl] - m_t)
        qb = q_scr[sl]
        num = w_inter * _bdot(qb, c_state.astype(BF16)) + w_loc * v_scr[sl]
        q_n = jnp.broadcast_to(jnp.sum(qb.astype(F32) * n_state, axis=2, keepdims=True), m_t.shape)
        nq = w_inter * q_n + w_loc * rs_scr[sl]
        v_scr[sl] = num / jnp.maximum(jnp.abs(nq), jnp.exp(-m_t))
        m_new = m_t[:, rows - 1:rows]
        w_prev = jnp.exp(bl_scr[sl] + m_prev - m_new)
        w_upd = jnp.exp(mu_scr[sl] - m_new)
        n_ref[...] = w_prev * n_state + w_upd * nu_scr[sl]
        m_ref[...] = m_new
        return w_prev * c_state + w_upd * u_scr[sl]

    if has_state:
        c_init = st0[0][...]
        n_ref[...] = st0[1][...]
        m_ref[...] = st0[2][...]
    else:
        c_init = jnp.zeros((geo.seqs, D_HEAD, D_HEAD), F32)
        n_ref[...] = jnp.zeros(n_ref.shape, F32)
        m_ref[...] = jnp.zeros(m_ref.shape, F32)
    c_ref[...] = geo.scan_chunks(step, c_init)

    def epilogue(g):
        sl = geo.rows_of(g)
        geo.store(y_ref, g, _rms(_sigmoid(gate_scr[sl]) * v_scr[sl], nw_ref[...]), relay)

    geo.for_groups(epilogue)


def _ret_kernel(*refs, geo, has_state):
    ((h_ref, w_ref, cos_ref, sin_ref, nw_ref), r0, _, (y_ref, r_ref),
     (q_scr, o_scr, u_scr, gate_scr), relay) = _split_refs(
        refs, (5, int(has_state), 1, 2, 4, len(refs)))
    rows, width, valid = geo.rows, geo.width, geo.valid
    head = pl.program_id(1)
    log_gamma = jnp.zeros((1, 1, 1), F32)
    for h in range(N_HEAD):
        log_gamma = jnp.where(head == h, math.log(1.0 - 2.0 ** (-5.0 - h)), log_gamma)
    scale = D_HEAD ** -0.5
    t_i = lax.broadcasted_iota(jnp.int32, (1, rows, width), 1)
    s_i = lax.broadcasted_iota(jnp.int32, (1, rows, width), 2)
    decay = jnp.where(s_i <= t_i, jnp.exp((t_i - s_i).astype(F32) * log_gamma), 0.0)
    t_col = lax.broadcasted_iota(jnp.int32, (1, rows, D_HEAD), 1).astype(F32)
    w_in = jnp.exp((t_col + 1.0) * log_gamma)
    w_st = jnp.exp((valid - 1.0 - t_col) * log_gamma)
    g_all = jnp.exp(valid * log_gamma)
    w = w_ref[...]

    def rotate(x, cos, sin):
        flat = x.reshape(x.shape[0] * rows, D_HEAD)
        return x * cos + pltpu.roll(flat, D_HEAD // 2, axis=1).reshape(x.shape) * sin

    def phase_a(g):
        q, k, v, rg = geo.split(geo.project(h_ref, w, g), relay)
        if geo.chunks > 1:
            cos, sin = cos_ref[geo.rows_of(g)], sin_ref[geo.rows_of(g)]
        else:
            cos, sin = cos_ref[...], sin_ref[...]
        qr = rotate(q, cos, sin).astype(BF16)
        kr = rotate(k, cos, sin) * scale
        vb = _pad_rows(v, width).astype(BF16)
        a = _bdot_nt(qr, _pad_rows(kr, width).astype(BF16)) * decay
        sl = geo.rows_of(g)
        gate_scr[sl] = rg
        q_scr[sl] = qr
        o_scr[sl] = _bdot(a.astype(BF16), vb)
        u_scr[sl] = _bdot_tn(_pad_rows(kr * w_st, width).astype(BF16), vb)

    geo.for_groups_static(phase_a)

    def step(sl, state):
        o_scr[sl] = o_scr[sl] + _bdot(q_scr[sl], state.astype(BF16)) * w_in
        return g_all * state + u_scr[sl]

    init = r0[0][...] if has_state else jnp.zeros((geo.seqs, D_HEAD, D_HEAD), F32)
    r_ref[...] = geo.scan_chunks(step, init)

    def epilogue(g):
        sl = geo.rows_of(g)
        geo.store(y_ref, g, _rms(o_scr[sl], nw_ref[...]) * _silu(gate_scr[sl]), relay)

    geo.for_groups(epilogue)


def _mixer_call(kernel, name, geo, layer, h, w_mix, extra, states_in, states_prev, scratch):
    seqs, n_cols = geo.seqs, w_mix.shape[-1]

    def state_spec(arr):
        tail = arr.shape[3:]
        zeros = (0,) * len(tail)
        return pl.BlockSpec((None, seqs, None) + tail, lambda bi, j: (layer, bi, j) + zeros)

    has_state = states_in is not None
    args = [h, w_mix] + [a for a, _ in extra]
    in_specs = [pl.BlockSpec((geo.tokens, D_MODEL), lambda bi, j: (bi, 0)),
                pl.BlockSpec((None, None, D_MODEL, n_cols), lambda bi, j: (layer, j, 0, 0))] + [s for _, s in extra]
    if has_state:
        args += list(states_in)
        in_specs += [state_spec(a) for a in states_in]
    n_in = len(args)
    args += list(states_prev)
    in_specs += [pl.BlockSpec(memory_space=pl.ANY) for _ in states_prev]
    out_specs = [pl.BlockSpec((geo.tokens, D_HEAD), lambda bi, j: (bi, j))] + [state_spec(a) for a in states_prev]
    out_shape = [jax.ShapeDtypeStruct((h.shape[0], MIX_W), F32 if geo.padded else BF16)] + [
        jax.ShapeDtypeStruct(a.shape, F32) for a in states_prev]
    scratch_shapes = list(scratch) + [pltpu.VMEM((geo.problems, geo.rows, D_HEAD), F32)]
    if geo.padded:
        scratch_shapes += [pltpu.VMEM((geo.group_tokens, D_HEAD), F32),
                           pltpu.VMEM((seqs * geo.rows, D_HEAD), F32)] * (n_cols // D_HEAD)
    return pl.pallas_call(
        functools.partial(kernel, geo=geo, has_state=has_state),
        grid=(geo.batch // seqs, N_HEAD),
        in_specs=in_specs,
        out_specs=out_specs,
        out_shape=out_shape,
        scratch_shapes=scratch_shapes,
        input_output_aliases={n_in + i: 1 + i for i in range(len(states_prev))},
        compiler_params=pltpu.CompilerParams(
            dimension_semantics=("arbitrary", "arbitrary"), vmem_limit_bytes=VMEM_LIMIT),
        name=name,
    )(*args)


def _head_block(rows):
    return pl.BlockSpec((rows, D_HEAD), lambda bi, j: (0, j))


def _hgrn(geo, layer, h, w_mix, hgrn_lb, nw, state_in, state_prev):
    n = geo.problems
    scratch = [pltpu.VMEM((n, geo.rows, D_HEAD), BF16), pltpu.VMEM((n, geo.rows, D_HEAD), F32),
               pltpu.VMEM((n, D_HEAD, D_HEAD), F32), pltpu.VMEM((n, D_HEAD, D_HEAD), F32)]
    place = _hgrn_placement(geo)
    return _mixer_call(
        functools.partial(_hgrn_kernel, layer=layer), "hgrn2", geo, layer, h, w_mix,
        [(hgrn_lb, _head_block(hgrn_lb.shape[0])), (nw, _head_block(1)), (place, _resident(place.shape))],
        None if state_in is None else [state_in], [state_prev], scratch)


def _mlstm(geo, layer, h, w_mix, gate_b, nw, states_in, states_prev):
    n = geo.problems
    col = lambda r: pltpu.VMEM((n, r, LANES), F32)
    scratch = [pltpu.VMEM((n, geo.rows, D_HEAD), BF16), pltpu.VMEM((n, geo.rows, D_HEAD), F32),
               pltpu.VMEM((n, D_HEAD, D_HEAD), F32), pltpu.VMEM((n, 1, D_HEAD), F32),
               col(geo.rows), col(geo.rows), col(geo.rows), col(1), col(1)]
    return _mixer_call(
        _mlstm_kernel, "mlstm", geo, layer, h, w_mix,
        [(gate_b, pl.BlockSpec((1, LANES), lambda bi, j: (0, 0))), (nw, _head_block(1))],
        states_in, states_prev, scratch)


def _retention(geo, layer, h, w_mix, cos, sin, nw, state_in, state_prev):
    n = geo.problems
    scratch = [pltpu.VMEM((n, geo.rows, D_HEAD), BF16), pltpu.VMEM((n, geo.rows, D_HEAD), F32),
               pltpu.VMEM((n, D_HEAD, D_HEAD), F32)]
    table = pl.BlockSpec(cos.shape, lambda bi, j: (0, 0, 0))
    return _mixer_call(
        _ret_kernel, "retention", geo, layer, h, w_mix,
        [(cos, table), (sin, table), (nw, _head_block(1))],
        None if state_in is None else [state_in], [state_prev], scratch)


def _merge_kernel(x_ref, h_ref, yh_ref, ym_ref, yr_ref, wg_ref, bg_ref, wb_ref, wo_ref, n2_ref, x1_ref, h2_ref):
    hb = h_ref[...]
    merged = None
    for j, y_ref in enumerate((yh_ref, ym_ref, yr_ref)):
        cols = slice(j * D_MODEL, (j + 1) * D_MODEL)
        gate = _sigmoid(_dot(hb, wg_ref[:, cols]) + bg_ref[:, cols])
        term = gate * _dot(y_ref[...].astype(BF16), wb_ref[j])
        merged = term if merged is None else merged + term
    x1 = x_ref[...] + _dot(merged.astype(BF16), wo_ref[...])
    x1_ref[...] = x1
    h2_ref[...] = _rms(x1, n2_ref[...]).astype(BF16)


def _merge(x, h, yh, ym, yr, w_bgate, gate_b, w_branch, w_out, norm2_w):
    n = x.shape[0]
    tm = min(TOKEN_TILE, n)
    gate_w = N_BRANCH * D_MODEL
    row = lambda width: pl.BlockSpec((tm, width), lambda i: (i, 0))
    vec = lambda width: pl.BlockSpec((1, width), lambda i: (0, 0))
    return pl.pallas_call(
        _merge_kernel,
        grid=(n // tm,),
        in_specs=[
            row(D_MODEL), row(D_MODEL), row(MIX_W), row(MIX_W), row(MIX_W),
            _resident((D_MODEL, gate_w)), vec(gate_w),
            _resident((N_BRANCH, MIX_W, D_MODEL)),
            _resident((D_MODEL, D_MODEL)),
            vec(D_MODEL),
        ],
        out_specs=[row(D_MODEL), row(D_MODEL)],
        out_shape=[jax.ShapeDtypeStruct((n, D_MODEL), F32), jax.ShapeDtypeStruct((n, D_MODEL), BF16)],
        compiler_params=pltpu.CompilerParams(
            dimension_semantics=("arbitrary",), vmem_limit_bytes=VMEM_LIMIT),
        name="merge",
    )(x, h, yh, ym, yr, w_bgate, gate_b, w_branch, w_out, norm2_w)


def _ffn_kernel(*refs, seq_len, has_state, final_norm, tail_rows):
    refs = list(refs)
    x1_ref, h2_ref, wg_ref, wu_ref, wd_ref, cw_ref, cb_ref, nw_ref = refs[:8]
    refs = refs[8:]
    p1_ref = p2_ref = None
    if has_state:
        p1_ref, p2_ref = refs[:2]
        refs = refs[2:]
    if final_norm:
        out_ref, tail_ref, carry_scr, a_scr = refs
    else:
        out_ref, hn_ref, tail_ref, carry_scr, a_scr = refs
    tm = x1_ref.shape[0]
    i = pl.program_id(0)

    @pl.when(i == 0)
    def _():
        carry_scr[...] = jnp.zeros(carry_scr.shape, F32)

    t = (i * tm + lax.broadcasted_iota(jnp.int32, (tm, FF_TILE), 0)) & (seq_len - 1)
    h2 = h2_ref[...]
    acc = x1_ref[...]
    for j in range(D_FF // FF_TILE):
        cols = slice(j * FF_TILE, (j + 1) * FF_TILE)
        a = _dot(h2, wg_ref[:, cols])
        u = _dot(h2, wu_ref[:, cols])
        a_scr[0:SUBLANES, :] = carry_scr[:, cols]
        a_scr[SUBLANES:, :] = a
        carry_scr[:, cols] = a[tm - SUBLANES:, :]
        prev1 = jnp.where(t >= 1, a_scr[SUBLANES - 1:SUBLANES - 1 + tm, :], 0.0)
        prev2 = jnp.where(t >= 2, a_scr[SUBLANES - 2:SUBLANES - 2 + tm, :], 0.0)
        if has_state:
            prev1 = prev1 + p1_ref[:, cols]
            prev2 = prev2 + p2_ref[:, cols]
        conv = prev2 * cw_ref[0:1, cols] + prev1 * cw_ref[1:2, cols] + a * cw_ref[2:3, cols] + cb_ref[:, cols]
        acc = acc + _dot((_silu(conv) * u).astype(BF16), wd_ref[cols, :])
        if tail_rows == tm:
            tail_ref[:, cols] = a
        else:
            tail_ref[0, :, cols] = a[tm - tail_rows:, :]
    if final_norm:
        out_ref[...] = _rms(acc, nw_ref[...])
    else:
        out_ref[...] = acc
        hn_ref[...] = _rms(acc, nw_ref[...]).astype(BF16)


def _ffn(x1, h2, wg, wu, wd, conv_w, conv_b, seq_len, conv_state, norm_w, final_norm):
    n = x1.shape[0]
    has_state = conv_state is not None
    tm = min(TOKEN_TILE // 2 if has_state else TOKEN_TILE, n)
    assert n % tm == 0 and seq_len & (seq_len - 1) == 0
    batch = n // seq_len
    row = lambda width: pl.BlockSpec((tm, width), lambda i: (i, 0))
    full = lambda shape: pl.BlockSpec(shape, lambda i: (0,) * len(shape))
    in_specs = [row(D_MODEL), row(D_MODEL), _resident((D_MODEL, D_FF)), _resident((D_MODEL, D_FF)),
                _resident((D_FF, D_MODEL)), full((CONV_W, D_FF)), full((1, D_FF)), full((1, D_MODEL))]
    args = [x1, h2, wg, wu, wd, conv_w, conv_b, norm_w]
    if has_state:
        p1 = jnp.pad(conv_state[:, 1:2], ((0, 0), (0, seq_len - 1), (0, 0))).reshape(n, D_FF)
        p2 = jnp.pad(conv_state, ((0, 0), (0, seq_len - (CONV_W - 1)), (0, 0))).reshape(n, D_FF)
        in_specs += [row(D_FF), row(D_FF)]
        args += [p1, p2]
    if seq_len % tm == 0:
        tail_rows = CONV_W - 1
        tail_spec = pl.BlockSpec((1, tail_rows, D_FF), lambda i: ((i * tm) // seq_len, 0, 0))
        tail_shape = jax.ShapeDtypeStruct((batch, tail_rows, D_FF), F32)
    else:
        assert tm % seq_len == 0 and seq_len >= CONV_W - 1
        tail_rows = tm
        tail_spec = row(D_FF)
        tail_shape = jax.ShapeDtypeStruct((n, D_FF), F32)
    out_specs, out_shape = [row(D_MODEL)], [jax.ShapeDtypeStruct((n, D_MODEL), F32)]
    if not final_norm:
        out_specs.append(row(D_MODEL))
        out_shape.append(jax.ShapeDtypeStruct((n, D_MODEL), BF16))
    *outs, tail = pl.pallas_call(
        functools.partial(_ffn_kernel, seq_len=seq_len, has_state=has_state,
                          final_norm=final_norm, tail_rows=tail_rows),
        grid=(n // tm,),
        in_specs=in_specs,
        out_specs=out_specs + [tail_spec],
        out_shape=out_shape + [tail_shape],
        scratch_shapes=[pltpu.VMEM((SUBLANES, D_FF), F32), pltpu.VMEM((tm + SUBLANES, FF_TILE), F32)],
        compiler_params=pltpu.CompilerParams(
            dimension_semantics=("arbitrary",), vmem_limit_bytes=VMEM_LIMIT),
        name="conv_ffn",
    )(*args)
    if tail_rows == tm:
        tail = tail.reshape(batch, seq_len, D_FF)[:, seq_len - (CONV_W - 1):, :]
    return outs, tail


def _per_head(w, off, extra=None):
    depth = w.shape[0]
    blk = w[:, :, off:off + 4 * MIX_W].reshape(depth, D_MODEL, 4, N_HEAD, D_HEAD)
    blk = blk.transpose(0, 3, 1, 2, 4).reshape(depth, N_HEAD, D_MODEL, 4 * D_HEAD)
    if extra is not None:
        blk = jnp.concatenate([blk, jnp.broadcast_to(extra[:, None], (depth, N_HEAD) + extra.shape[1:])], axis=3)
    return blk


def _prep_weights(w_in, w_branch, w_out, ffn_w_gate, ffn_w_up, ffn_w_down):
    w_in = w_in.astype(BF16)
    ml_gates = jnp.pad(w_in[:, :, ML_GATE_OFF:RET_OFF], ((0, 0), (0, 0), (0, LANES - 2 * N_HEAD)))
    return (_per_head(w_in, HG_OFF), _per_head(w_in, ML_OFF, ml_gates), _per_head(w_in, RET_OFF),
            w_in[:, :, GATE_OFF:], w_branch.astype(BF16), w_out.astype(BF16),
            ffn_w_gate.astype(BF16), ffn_w_up.astype(BF16), ffn_w_down.astype(BF16))


def _trunk(x3, pos0, states, params, prepped):
    (norm1_w, mlstm_gate_b, branch_gate_b, hgrn_lb, hgrn_norm_w, mlstm_norm_w, ret_norm_w,
     norm2_w, ffn_conv_w, ffn_conv_b, final_norm_w) = params
    w_hg, w_ml, w_ret, w_bgate, w_branch, w_out, w_ffg, w_ffu, w_ffd = prepped
    batch, seq_len, _ = x3.shape
    n = batch * seq_len
    depth = norm1_w.shape[0]
    geo = _Geo(batch, seq_len)
    cos, sin = (t.reshape(geo.chunks, geo.rows, D_HEAD) for t in _rope_tables(geo.chunks * geo.rows, pos0))

    mat = (depth, batch, N_HEAD, D_HEAD, D_HEAD)
    vec = (depth, batch, N_HEAD, 1, D_HEAD)
    if states is None:
        hg_in = ml_in = ret_in = conv_in = None
    else:
        hg_in, ml_c, ml_n, ml_m, ret_in, conv_in = states
        ml_in = [ml_c, ml_n.reshape(vec), jnp.broadcast_to(ml_m[..., None, None], vec)]
    hg_out, c_out, r_out = (jnp.zeros(mat, F32) for _ in range(3))
    n_out, m_out = (jnp.zeros(vec, F32) for _ in range(2))

    x = x3.reshape(n, D_MODEL)
    h = _norm(x, norm1_w[0:1])
    conv_tails = []
    for l in range(depth):
        last = l == depth - 1
        y_hg, hg_out = _hgrn(geo, l, h, w_hg, hgrn_lb, hgrn_norm_w[l:l + 1], hg_in, hg_out)
        gate_b = jnp.pad(mlstm_gate_b[l:l + 1], ((0, 0), (0, LANES - 2 * N_HEAD)))
        y_ml, c_out, n_out, m_out = _mlstm(geo, l, h, w_ml, gate_b, mlstm_norm_w[l:l + 1], ml_in,
                                           [c_out, n_out, m_out])
        y_ret, r_out = _retention(geo, l, h, w_ret, cos, sin, ret_norm_w[l:l + 1], ret_in, r_out)
        x1, h2 = _merge(x, h, y_hg, y_ml, y_ret, w_bgate[l], branch_gate_b[l:l + 1], w_branch[l], w_out[l],
                        norm2_w[l:l + 1])
        outs, conv_tail = _ffn(x1, h2, w_ffg[l], w_ffu[l], w_ffd[l], ffn_conv_w[l], ffn_conv_b[l:l + 1],
                               seq_len, None if conv_in is None else conv_in[l],
                               final_norm_w[None, :] if last else norm1_w[l + 1:l + 2], last)
        x = outs[0]
        h = None if last else outs[1]
        conv_tails.append(conv_tail)
    return x.reshape(batch, seq_len, D_MODEL), (
        hg_out, c_out, n_out.reshape(depth, batch, N_HEAD, D_HEAD), m_out[:, :, :, 0, 0], r_out,
        jnp.stack(conv_tails))


def kernel(x_prompt, x_sample, state_hgrn, state_mlstm_C, state_mlstm_n, state_mlstm_m, state_ret,
           state_ffn_conv, norm1_w, w_in, mlstm_gate_b, branch_gate_b, hgrn_lb, hgrn_norm_w,
           mlstm_norm_w, ret_norm_w, w_branch, w_out, norm2_w, ffn_w_gate, ffn_w_up, ffn_conv_w,
           ffn_conv_b, ffn_w_down, final_norm_w):
    params = (norm1_w, mlstm_gate_b, branch_gate_b, hgrn_lb, hgrn_norm_w, mlstm_norm_w, ret_norm_w,
              norm2_w, ffn_conv_w, ffn_conv_b, final_norm_w)
    prepped = _prep_weights(w_in, w_branch, w_out, ffn_w_gate, ffn_w_up, ffn_w_down)
    y_p, (hg_p, c_p, n_p, m_p, r_p, cv_p) = _trunk(x_prompt, 0, None, params, prepped)
    y_s, (hg_s, c_s, n_s, m_s, r_s, cv_s) = _trunk(
        x_sample, PAST_LEN,
        (state_hgrn, state_mlstm_C, state_mlstm_n, state_mlstm_m, state_ret, state_ffn_conv),
        params, prepped)
    return (y_p, y_s, hg_p, hg_s, c_p, c_s, n_p, n_s, m_p, m_s, r_p, r_s, cv_p, cv_s)
```

```python
import functools
import math

import numpy as np
import jax
import jax.numpy as jnp
from jax import lax
from jax.experimental import pallas as pl
from jax.experimental.pallas import tpu as pltpu

F32 = jnp.float32
BF16 = jnp.bfloat16

D_MODEL = 1024
MIX_W = 512
N_HEAD = 4
D_HEAD = 128
N_BRANCH = 3
D_FF = 2816
CONV_W = 3
CHUNK = 64
EPS = 1e-6
ROPE_BASE = 10000.0
PAST_LEN = 16384

HG_OFF = 0
ML_OFF = 4 * MIX_W
ML_GATE_OFF = ML_OFF + 4 * MIX_W
RET_OFF = ML_GATE_OFF + 2 * N_HEAD
GATE_OFF = RET_OFF + 4 * MIX_W
LOG2E = 1.4426950408889634

LANES = 128
SUBLANES = 8
VMEM_LIMIT = 56 * 1024 * 1024

TOKEN_TILE = 512
NORM_TOKEN_TILE = 1024
FF_TILE = 2816
HG_SUB = 16
GROUP = 8
SEQ_BLOCK = 64
SCAN_UNROLL = 8


def _resident(shape):
    return pl.BlockSpec(shape, lambda *_: (0,) * len(shape), pipeline_mode=pl.Buffered(1))


def _dot(a, b):
    return jnp.dot(a, b, preferred_element_type=F32)


def _dot_tn(a, b):
    return lax.dot_general(a, b, (((0,), (0,)), ((), ())), preferred_element_type=F32)


def _bdot(a, b):
    return jnp.einsum('nlk,nkv->nlv', a, b, preferred_element_type=F32)


def _bdot_nt(a, b):
    return jnp.einsum('nqd,nkd->nqk', a, b, preferred_element_type=F32)


def _bdot_tn(a, b):
    return jnp.stack([_dot_tn(a[i], b[i]) for i in range(a.shape[0])])


def _split3(x):
    hi = x.astype(BF16)
    r = x - hi.astype(F32)
    mid = r.astype(BF16)
    lo = (r - mid.astype(F32)).astype(BF16)
    return hi, mid, lo


def _sigmoid(x):
    return 1.0 / (1.0 + jnp.exp(-x))


def _silu(x):
    return x * _sigmoid(x)


def _log_sigmoid(x):
    return jnp.minimum(x, 0.0) - jnp.log(1.0 + jnp.exp(-jnp.abs(x)))


def _rms(x, w):
    return x * lax.rsqrt(jnp.mean(x * x, axis=-1, keepdims=True) + EPS) * w


def _pad_rows(x, n):
    if x.shape[1] >= n:
        return x
    return jnp.concatenate([x, jnp.zeros((x.shape[0], n - x.shape[1], x.shape[2]), x.dtype)], axis=1)


def _seg_cumsum(x, seg):
    shape = x.shape
    flat = x.reshape(shape[0] * seg, shape[2])
    row = lax.broadcasted_iota(jnp.int32, flat.shape, 0) & (seg - 1)
    s = 1
    while s < seg:
        flat = flat + jnp.where(row >= s, pltpu.roll(flat, s, axis=0), 0.0)
        s *= 2
    return flat.reshape(shape)


def _col_bcast(rows):
    n = rows.shape[0]
    r = lax.broadcasted_iota(jnp.int32, (n, D_HEAD, D_HEAD), 1)
    c = lax.broadcasted_iota(jnp.int32, (n, D_HEAD, D_HEAD), 2)
    d = jnp.where(r == c, jnp.broadcast_to(rows, (n, D_HEAD, D_HEAD)), 0.0).reshape(n * D_HEAD, D_HEAD)
    ones = jnp.ones((D_HEAD, D_HEAD), BF16)
    hi, mid, lo = _split3(d)
    return (_dot(hi, ones) + _dot(mid, ones) + _dot(lo, ones)).reshape(n, D_HEAD, D_HEAD)


def _through_vmem(x):
    def body(ref):
        ref[...] = x
        return ref[...]
    return pl.run_scoped(body, pltpu.VMEM(x.shape, x.dtype))


def _row_bcast(col, width):
    n, rows, _ = col.shape
    vals = _through_vmem(_pad_rows(col, width))
    lane = lax.broadcasted_iota(jnp.int32, (n, rows, LANES), 2)
    pick = _through_vmem(jnp.where(lane == 0, 1.0, 0.0).astype(BF16))
    hi, mid, lo = _split3(vals)
    return _bdot_nt(pick, hi) + _bdot_nt(pick, mid) + _bdot_nt(pick, lo)


def _chunk_geometry(seq_len):
    if seq_len % CHUNK == 0:
        return CHUNK, CHUNK, CHUNK
    valid = math.gcd(seq_len, CHUNK)
    assert valid == seq_len and valid <= SUBLANES, "unsupported sequence length"
    return SUBLANES, valid, LANES


def _rope_kernel(inv_ref, cos_ref, sin_ref, *, pos0):
    shape = cos_ref.shape
    pos = lax.broadcasted_iota(jnp.int32, shape, 0).astype(F32) + pos0
    ang = pos * inv_ref[...]
    lane = lax.broadcasted_iota(jnp.int32, shape, 1)
    cos_ref[...] = jnp.cos(ang)
    sin_ref[...] = jnp.where(lane < D_HEAD // 2, -jnp.sin(ang), jnp.sin(ang))


def _rope_tables(rows, pos0):
    inv = ROPE_BASE ** (-jnp.linspace(0.0, 1.0, D_HEAD // 2, dtype=F32))
    inv2 = jnp.concatenate([inv, inv])[None, :]
    return pl.pallas_call(
        functools.partial(_rope_kernel, pos0=float(pos0)),
        out_shape=(jax.ShapeDtypeStruct((rows, D_HEAD), F32),) * 2,
        name="rope_tables",
    )(inv2)


def _norm_kernel(x_ref, w_ref, h_ref):
    h_ref[...] = _rms(x_ref[...], w_ref[...]).astype(BF16)


def _norm(x, w):
    n = x.shape[0]
    tm = min(NORM_TOKEN_TILE, n)
    assert n % tm == 0
    return pl.pallas_call(
        _norm_kernel,
        grid=(n // tm,),
        in_specs=[pl.BlockSpec((tm, D_MODEL), lambda i: (i, 0)), pl.BlockSpec((1, D_MODEL), lambda i: (0, 0))],
        out_specs=pl.BlockSpec((tm, D_MODEL), lambda i: (i, 0)),
        out_shape=jax.ShapeDtypeStruct((n, D_MODEL), BF16),
        compiler_params=pltpu.CompilerParams(dimension_semantics=("arbitrary",), vmem_limit_bytes=VMEM_LIMIT),
        name="norm_in",
    )(x, w)


class _Geo:
    def __init__(self, batch, seq_len):
        self.rows, self.valid, self.width = _chunk_geometry(seq_len)
        if self.valid == self.rows:
            self.seqs, self.chunks = 1, seq_len // self.rows
            self.group = min(GROUP, self.chunks)
        else:
            self.seqs, self.chunks = min(SEQ_BLOCK, batch), 1
            self.group = self.seqs
        assert batch % self.seqs == 0 and self.chunks % min(self.group, self.chunks) == 0
        self.batch = batch
        self.problems = self.seqs * self.chunks
        self.n_groups = self.problems // self.group
        self.group_tokens = self.group * self.valid
        self.tokens = self.problems * self.valid
        self.padded = self.valid < self.rows

    def project(self, h_ref, w, g):
        return _dot(h_ref[pl.ds(g * self.group_tokens, self.group_tokens), :], w)

    def split(self, z, relay):
        n_parts = z.shape[1] // D_HEAD
        cols = lambda p: slice(p * D_HEAD, (p + 1) * D_HEAD)
        if not self.padded:
            return [z[:, cols(p)].reshape(self.group, self.rows, D_HEAD) for p in range(n_parts)]
        out = []
        for p in range(n_parts):
            flat_ref, pad_ref = relay[2 * p], relay[2 * p + 1]
            flat_ref[...] = z[:, cols(p)]
            pad_ref[...] = jnp.zeros(pad_ref.shape, F32)
            for t in range(self.valid):
                pad_ref[pl.ds(t, self.seqs, stride=self.rows), :] = flat_ref[
                    pl.ds(t, self.seqs, stride=self.valid), :]
            out.append(pad_ref[...].reshape(self.seqs, self.rows, D_HEAD))
        return out

    def store(self, y_ref, g, y, relay):
        if not self.padded:
            n = self.group * self.rows
            y_ref[pl.ds(g * n, n), :] = y.reshape(n, D_HEAD).astype(y_ref.dtype)
            return
        pad = relay[1]
        pad[...] = y.reshape(self.seqs * self.rows, D_HEAD)
        for t in range(self.valid):
            y_ref[pl.ds(t, self.seqs, stride=self.valid), :] = pad[pl.ds(t, self.seqs, stride=self.rows), :].astype(
                y_ref.dtype)

    def rows_of(self, g):
        return pl.ds(g * self.group, self.group)

    def row_mask(self, shape):
        if not self.padded:
            return None
        return lax.broadcasted_iota(jnp.int32, shape, 1) < self.valid

    def for_groups(self, body):
        if self.n_groups == 1:
            body(0)
        else:
            lax.fori_loop(0, self.n_groups, lambda g, c: (body(g), c)[1], 0)

    def for_groups_static(self, body):
        for g in range(self.n_groups):
            body(g)

    def scan_chunks(self, step, init):
        if self.chunks == 1:
            return step(pl.ds(0, self.seqs), init)
        return lax.fori_loop(0, self.chunks, lambda c, carry: step(pl.ds(c * self.seqs, self.seqs), carry), init,
                             unroll=min(SCAN_UNROLL, self.chunks))


def _join_columns(w_refs):
    return jnp.concatenate([r[...] for r in w_refs], axis=1)


def _transpose_minor(x):
    return jnp.stack([x[i].T for i in range(x.shape[0])])


def _split_refs(refs, counts):
    out, k = [], 0
    for c in counts:
        out.append(refs[k:k + c])
        k += c
    return out


def _hgrn_lower_bound(lb_ref, layer):
    x = lb_ref[...]
    e = jnp.exp(x - jnp.max(x, axis=0, keepdims=True))
    soft = e / jnp.sum(e, axis=0, keepdims=True)
    cum = soft[0:1]
    for j in range(1, layer + 1):
        cum = cum + soft[j:j + 1]
    return cum - soft[0:1]


def _hgrn_placement(geo):
    sub = min(HG_SUB, geo.rows)
    r = np.arange(sub * D_HEAD)[:, None] // D_HEAD
    c = np.arange(geo.width)[None, :]
    return jnp.asarray(np.stack([r + i * sub == c for i in range(geo.rows // sub)]), BF16)


def _hgrn_intra(q, k, b2, place_ref, geo):
    n = q.shape[0]
    sub = min(HG_SUB, geo.rows)
    sub_row = lax.broadcasted_iota(jnp.int32, (n, sub, geo.width), 1)
    lane = lax.broadcasted_iota(jnp.int32, (n, sub, geo.width), 2)
    blocks = []
    for i in range(geo.rows // sub):
        r0 = i * sub
        b_i, q_i, k_i = (a[:, r0:r0 + sub] for a in (b2, q, k))
        terms = []
        for s in range(sub):
            lo = s // SUBLANES * SUBLANES
            e = jnp.exp2(jnp.minimum(b_i[:, lo:] - b_i[:, s:s + 1], 0.0))
            term = q_i[:, lo:] * e * k_i[:, s:s + 1]
            if lo:
                term = jnp.concatenate([jnp.zeros((n, lo, D_HEAD), F32), term], axis=1)
            terms.append(term)
        cat = jnp.concatenate(terms, axis=2).reshape(n * sub, sub * D_HEAD)
        a_i = _dot(cat.astype(BF16), place_ref[i]).reshape(n, sub, geo.width)
        a_i = jnp.where(sub_row >= lane - r0, a_i, 0.0)
        if i > 0:
            b_ref = b_i[:, 0:1]
            q_dec = q_i * jnp.exp2(b_i - b_ref)
            k_dec = _pad_rows(k[:, 0:r0] * jnp.exp2(b_ref - b2[:, 0:r0]), geo.width)
            a_i = jnp.where(lane < r0, _bdot_nt(q_dec.astype(BF16), k_dec.astype(BF16)), a_i)
        blocks.append(a_i)
    return blocks[0] if len(blocks) == 1 else jnp.concatenate(blocks, axis=1)


def _hgrn_kernel(*refs, layer, geo, has_state):
    ((h_ref,), w_refs, (lb_ref, nw_ref, place), s0, _, (y_ref, s_ref),
     (qt_scr, o_scr, u_scr, dc_scr, gate_scr), relay) = _split_refs(
        refs, (1, 4, 3, int(has_state), 1, 2, 5, len(refs)))
    lb = _hgrn_lower_bound(lb_ref, layer)
    rows, width = geo.rows, geo.width
    w = _join_columns(w_refs)

    def phase_a(g):
        hq, hf, hi, hg = geo.split(geo.project(h_ref, w, g), relay)
        q = _silu(hq)
        f = lb + (1.0 - lb) * _sigmoid(hf)
        lf = jnp.log(f)
        k = 1.0 - f
        mask = geo.row_mask(lf.shape)
        if mask is not None:
            lf = jnp.where(mask, lf, 0.0)
            k = jnp.where(mask, k, 0.0)
        b2 = _seg_cumsum(lf, rows) * LOG2E
        sl = geo.rows_of(g)
        gate_scr[sl] = hg
        qt_scr[sl] = (q * jnp.exp2(b2)).astype(BF16)
        a = _hgrn_intra(q, k, b2, place, geo)
        vb = _pad_rows(hi, width).astype(BF16)
        o_scr[sl] = _bdot(a.astype(BF16), vb)
        b_last = b2[:, rows - 1:rows]
        k_w = k * jnp.exp2(b_last - b2)
        u_scr[sl] = _bdot_tn(vb, _pad_rows(k_w, width).astype(BF16))
        dc_scr[sl] = jnp.exp2(b_last)

    geo.for_groups_static(phase_a)

    def step(sl, state_t):
        o_scr[sl] = o_scr[sl] + _bdot_nt(qt_scr[sl], state_t.astype(BF16))
        return dc_scr[sl] * state_t + u_scr[sl]

    init = _transpose_minor(s0[0][...]) if has_state else jnp.zeros((geo.seqs, D_HEAD, D_HEAD), F32)
    s_ref[...] = _transpose_minor(geo.scan_chunks(step, init))

    def epilogue(g):
        sl = geo.rows_of(g)
        geo.store(y_ref, g, _rms(o_scr[sl], nw_ref[...]) * _silu(gate_scr[sl]), relay)

    geo.for_groups(epilogue)


def _mlstm_kernel(*refs, geo, has_state):
    ((h_ref,), w_refs, (gb_ref, nw_ref), st0, _, (y_ref, c_ref, n_ref, m_ref),
     (q_scr, v_scr, u_scr, nu_scr, bc_scr, ml_scr, rs_scr, bl_scr, mu_scr, gate_scr), relay) = _split_refs(
        refs, (1, 5, 2, 3 * int(has_state), 3, 4, 10, len(refs)))
    rows, width = geo.rows, geo.width
    head = pl.program_id(1)
    scale = D_HEAD ** -0.5
    t_i = lax.broadcasted_iota(jnp.int32, (1, rows, width), 1)
    s_i = lax.broadcasted_iota(jnp.int32, (1, rows, width), 2)
    ok = s_i <= t_i
    if geo.padded:
        ok = ok & (s_i < geo.valid)
    w = _join_columns(w_refs)

    def phase_a(g):
        q, k, v, og, gates = geo.split(geo.project(h_ref, w, g), relay)
        gates = gates + gb_ref[...]
        lf_all = _log_sigmoid(gates)
        mask = geo.row_mask(gates.shape)
        if mask is not None:
            lf_all = jnp.where(mask, lf_all, 0.0)
        bc_all = _seg_cumsum(lf_all, rows)
        lane = lax.broadcasted_iota(jnp.int32, gates.shape, 2)
        wide = lambda x: jnp.broadcast_to(x, gates.shape)
        ig = wide(jnp.sum(jnp.where(lane == head, gates, 0.0), axis=2, keepdims=True))
        bc_col = jnp.sum(jnp.where(lane == head + N_HEAD, bc_all, 0.0), axis=2, keepdims=True)
        bc = wide(bc_col)
        log_d = jnp.where(ok, bc_col + _row_bcast(ig - bc, width), -jnp.inf)
        m_col = jnp.max(log_d, axis=2, keepdims=True)
        m_loc = wide(m_col)
        k = k * scale
        qb = q.astype(BF16)
        vb = _pad_rows(v, width).astype(BF16)
        s_m = _bdot_nt(qb, _pad_rows(k, width).astype(BF16)) * jnp.exp(log_d - m_col)
        sl = geo.rows_of(g)
        gate_scr[sl] = og
        q_scr[sl] = qb
        v_scr[sl] = _bdot(s_m.astype(BF16), vb)
        rs_scr[sl] = wide(jnp.sum(s_m, axis=2, keepdims=True))
        bc_scr[sl] = bc
        ml_scr[sl] = m_loc
        b_last = bc[:, rows - 1:rows]
        m_upd = m_loc[:, rows - 1:rows]
        w_exp = b_last - bc + ig - m_upd
        if mask is not None:
            w_exp = jnp.where(mask, w_exp, -jnp.inf)
        k_w = k * jnp.exp(w_exp)
        u_scr[sl] = _bdot_tn(_pad_rows(k_w, width).astype(BF16), vb)
        nu_scr[sl] = jnp.sum(k_w, axis=1, keepdims=True)
        bl_scr[sl] = b_last
        mu_scr[sl] = m_upd

    geo.for_groups_static(phase_a)

    def step(sl, c_state):
        n_state, m_prev = n_ref[...], m_ref[...]
        inter_log = bc_scr[sl] + m_prev
        m_t = jnp.maximum(inter_log, ml_scr[sl])
        w_inter = jnp.exp(inter_log - m_t)
        w_loc = jnp.exp(ml_scr[sl] - m_t)
        qb = q_scr[sl]
        num = w_inter * _bdot(qb, c_state.astype(BF16)) + w_loc * v_scr[sl]
        q_n = jnp.broadcast_to(jnp.sum(qb.astype(F32) * n_state, axis=2, keepdims=True), m_t.shape)
        nq = w_inter * q_n + w_loc * rs_scr[sl]
        v_scr[sl] = num / jnp.maximum(jnp.abs(nq), jnp.exp(-m_t))
        m_new = m_t[:, rows - 1:rows]
        w_prev = jnp.exp(bl_scr[sl] + m_prev - m_new)
        w_upd = jnp.exp(mu_scr[sl] - m_new)
        n_ref[...] = w_prev * n_state + w_upd * nu_scr[sl]
        m_ref[...] = m_new
        return w_prev * c_state + w_upd * u_scr[sl]

    if has_state:
        c_init = st0[0][...]
        n_ref[...] = st0[1][...]
        m_ref[...] = st0[2][...]
    else:
        c_init = jnp.zeros((geo.seqs, D_HEAD, D_HEAD), F32)
        n_ref[...] = jnp.zeros(n_ref.shape, F32)
        m_ref[...] = jnp.zeros(m_ref.shape, F32)
    c_ref[...] = geo.scan_chunks(step, c_init)

    def epilogue(g):
        sl = geo.rows_of(g)
        geo.store(y_ref, g, _rms(_sigmoid(gate_scr[sl]) * v_scr[sl], nw_ref[...]), relay)

    geo.for_groups(epilogue)


def _ret_kernel(*refs, geo, has_state):
    ((h_ref,), w_refs, (cos_ref, sin_ref, nw_ref), r0, _, (y_ref, r_ref),
     (q_scr, o_scr, u_scr, gate_scr), relay) = _split_refs(
        refs, (1, 4, 3, int(has_state), 1, 2, 4, len(refs)))
    rows, width, valid = geo.rows, geo.width, geo.valid
    head = pl.program_id(1)
    log_gamma = jnp.zeros((1, 1, 1), F32)
    for h in range(N_HEAD):
        log_gamma = jnp.where(head == h, math.log(1.0 - 2.0 ** (-5.0 - h)), log_gamma)
    scale = D_HEAD ** -0.5
    t_i = lax.broadcasted_iota(jnp.int32, (1, rows, width), 1)
    s_i = lax.broadcasted_iota(jnp.int32, (1, rows, width), 2)
    decay = jnp.where(s_i <= t_i, jnp.exp((t_i - s_i).astype(F32) * log_gamma), 0.0)
    t_col = lax.broadcasted_iota(jnp.int32, (1, rows, D_HEAD), 1).astype(F32)
    w_in = jnp.exp((t_col + 1.0) * log_gamma)
    w_st = jnp.exp((valid - 1.0 - t_col) * log_gamma)
    g_all = jnp.exp(valid * log_gamma)
    w = _join_columns(w_refs)

    def rotate(x, cos, sin):
        flat = x.reshape(x.shape[0] * rows, D_HEAD)
        return x * cos + pltpu.roll(flat, D_HEAD // 2, axis=1).reshape(x.shape) * sin

    def phase_a(g):
        q, k, v, rg = geo.split(geo.project(h_ref, w, g), relay)
        if geo.chunks > 1:
            cos, sin = cos_ref[geo.rows_of(g)], sin_ref[geo.rows_of(g)]
        else:
            cos, sin = cos_ref[...], sin_ref[...]
        qr = rotate(q, cos, sin).astype(BF16)
        kr = rotate(k, cos, sin) * scale
        vb = _pad_rows(v, width).astype(BF16)
        a = _bdot_nt(qr, _pad_rows(kr, width).astype(BF16)) * decay
        sl = geo.rows_of(g)
        gate_scr[sl] = rg
        q_scr[sl] = qr
        o_scr[sl] = _bdot(a.astype(BF16), vb)
        u_scr[sl] = _bdot_tn(_pad_rows(kr * w_st, width).astype(BF16), vb)

    geo.for_groups_static(phase_a)

    def step(sl, state):
        o_scr[sl] = o_scr[sl] + _bdot(q_scr[sl], state.astype(BF16)) * w_in
        return g_all * state + u_scr[sl]

    init = r0[0][...] if has_state else jnp.zeros((geo.seqs, D_HEAD, D_HEAD), F32)
    r_ref[...] = geo.scan_chunks(step, init)

    def epilogue(g):
        sl = geo.rows_of(g)
        geo.store(y_ref, g, _rms(o_scr[sl], nw_ref[...]) * _silu(gate_scr[sl]), relay)

    geo.for_groups(epilogue)


def _mixer_call(kernel, name, geo, layer, h, weights, extra, states_in, states_prev, scratch):
    seqs, n_cols = geo.seqs, len(weights) * D_HEAD

    def weight_spec(first, per_head):
        return pl.BlockSpec((None, D_MODEL, D_HEAD), lambda bi, j: (layer, 0, first + per_head * j))

    def state_spec(arr):
        tail = arr.shape[3:]
        zeros = (0,) * len(tail)
        return pl.BlockSpec((None, seqs, None) + tail, lambda bi, j: (layer, bi, j) + zeros)

    has_state = states_in is not None
    args = [h] + [a for a, _, _ in weights] + [a for a, _ in extra]
    in_specs = [pl.BlockSpec((geo.tokens, D_MODEL), lambda bi, j: (bi, 0))] + [
        weight_spec(first, per_head) for _, first, per_head in weights] + [s for _, s in extra]
    if has_state:
        args += list(states_in)
        in_specs += [state_spec(a) for a in states_in]
    n_in = len(args)
    args += list(states_prev)
    in_specs += [pl.BlockSpec(memory_space=pl.ANY) for _ in states_prev]
    out_specs = [pl.BlockSpec((geo.tokens, D_HEAD), lambda bi, j: (bi, j))] + [state_spec(a) for a in states_prev]
    out_shape = [jax.ShapeDtypeStruct((h.shape[0], MIX_W), F32 if geo.padded else BF16)] + [
        jax.ShapeDtypeStruct(a.shape, F32) for a in states_prev]
    scratch_shapes = list(scratch) + [pltpu.VMEM((geo.problems, geo.rows, D_HEAD), F32)]
    if geo.padded:
        scratch_shapes += [pltpu.VMEM((geo.group_tokens, D_HEAD), F32),
                           pltpu.VMEM((seqs * geo.rows, D_HEAD), F32)] * (n_cols // D_HEAD)
    return pl.pallas_call(
        functools.partial(kernel, geo=geo, has_state=has_state),
        grid=(geo.batch // seqs, N_HEAD),
        in_specs=in_specs,
        out_specs=out_specs,
        out_shape=out_shape,
        scratch_shapes=scratch_shapes,
        input_output_aliases={n_in + i: 1 + i for i in range(len(states_prev))},
        compiler_params=pltpu.CompilerParams(
            dimension_semantics=("arbitrary", "arbitrary"), vmem_limit_bytes=VMEM_LIMIT),
        name=name,
    )(*args)


def _head_block(rows):
    return pl.BlockSpec((rows, D_HEAD), lambda bi, j: (0, j))


def _part_blocks(w_main, mixer):
    return [(w_main, (mixer * 4 + p) * N_HEAD, 1) for p in range(4)]


def _hgrn(geo, layer, h, w_main, hgrn_lb, nw, state_in, state_prev):
    n = geo.problems
    scratch = [pltpu.VMEM((n, geo.rows, D_HEAD), BF16), pltpu.VMEM((n, geo.rows, D_HEAD), F32),
               pltpu.VMEM((n, D_HEAD, D_HEAD), F32), pltpu.VMEM((n, 1, D_HEAD), F32)]
    place = _hgrn_placement(geo)
    return _mixer_call(
        functools.partial(_hgrn_kernel, layer=layer), "hgrn2", geo, layer, h, _part_blocks(w_main, 0),
        [(hgrn_lb, _head_block(hgrn_lb.shape[0])), (nw, _head_block(1)), (place, _resident(place.shape))],
        None if state_in is None else [state_in], [state_prev], scratch)


def _mlstm(geo, layer, h, w_main, w_gate, gate_b, nw, states_in, states_prev):
    n = geo.problems
    col = lambda r: pltpu.VMEM((n, r, LANES), F32)
    scratch = [pltpu.VMEM((n, geo.rows, D_HEAD), BF16), pltpu.VMEM((n, geo.rows, D_HEAD), F32),
               pltpu.VMEM((n, D_HEAD, D_HEAD), F32), pltpu.VMEM((n, 1, D_HEAD), F32),
               col(geo.rows), col(geo.rows), col(geo.rows), col(1), col(1)]
    return _mixer_call(
        _mlstm_kernel, "mlstm", geo, layer, h, _part_blocks(w_main, 1) + [(w_gate, 0, 0)],
        [(gate_b, pl.BlockSpec((1, LANES), lambda bi, j: (0, 0))), (nw, _head_block(1))],
        states_in, states_prev, scratch)


def _retention(geo, layer, h, w_main, cos, sin, nw, state_in, state_prev):
    n = geo.problems
    scratch = [pltpu.VMEM((n, geo.rows, D_HEAD), BF16), pltpu.VMEM((n, geo.rows, D_HEAD), F32),
               pltpu.VMEM((n, D_HEAD, D_HEAD), F32)]
    table = pl.BlockSpec(cos.shape, lambda bi, j: (0, 0, 0))
    return _mixer_call(
        _ret_kernel, "retention", geo, layer, h, _part_blocks(w_main, 2),
        [(cos, table), (sin, table), (nw, _head_block(1))],
        None if state_in is None else [state_in], [state_prev], scratch)


def _merge_kernel(x_ref, h_ref, yh_ref, ym_ref, yr_ref, wg_ref, bg_ref, wb_ref, wo_ref, n2_ref, x1_ref, h2_ref):
    hb = h_ref[...]
    merged = None
    for j, y_ref in enumerate((yh_ref, ym_ref, yr_ref)):
        cols = slice(j * D_MODEL, (j + 1) * D_MODEL)
        gate = _sigmoid(_dot(hb, wg_ref[:, cols]) + bg_ref[:, cols])
        term = gate * _dot(y_ref[...].astype(BF16), wb_ref[j])
        merged = term if merged is None else merged + term
    x1 = x_ref[...] + _dot(merged.astype(BF16), wo_ref[...])
    x1_ref[...] = x1
    h2_ref[...] = _rms(x1, n2_ref[...]).astype(BF16)


def _merge(x, h, yh, ym, yr, w_bgate, gate_b, w_branch, w_out, norm2_w):
    n = x.shape[0]
    tm = min(TOKEN_TILE, n)
    gate_w = N_BRANCH * D_MODEL
    row = lambda width: pl.BlockSpec((tm, width), lambda i: (i, 0))
    vec = lambda width: pl.BlockSpec((1, width), lambda i: (0, 0))
    return pl.pallas_call(
        _merge_kernel,
        grid=(n // tm,),
        in_specs=[
            row(D_MODEL), row(D_MODEL), row(MIX_W), row(MIX_W), row(MIX_W),
            _resident((D_MODEL, gate_w)), vec(gate_w),
            _resident((N_BRANCH, MIX_W, D_MODEL)),
            _resident((D_MODEL, D_MODEL)),
            vec(D_MODEL),
        ],
        out_specs=[row(D_MODEL), row(D_MODEL)],
        out_shape=[jax.ShapeDtypeStruct((n, D_MODEL), F32), jax.ShapeDtypeStruct((n, D_MODEL), BF16)],
        compiler_params=pltpu.CompilerParams(
            dimension_semantics=("arbitrary",), vmem_limit_bytes=VMEM_LIMIT),
        name="merge",
    )(x, h, yh, ym, yr, w_bgate, gate_b, w_branch, w_out, norm2_w)


def _ffn_kernel(*refs, seq_len, has_state, final_norm, tail_rows):
    refs = list(refs)
    x1_ref, h2_ref, wg_ref, wu_ref, wd_ref, cw_ref, cb_ref, nw_ref = refs[:8]
    refs = refs[8:]
    p1_ref = p2_ref = None
    if has_state:
        p1_ref, p2_ref = refs[:2]
        refs = refs[2:]
    if final_norm:
        out_ref, tail_ref, carry_scr, a_scr = refs
    else:
        out_ref, hn_ref, tail_ref, carry_scr, a_scr = refs
    tm = x1_ref.shape[0]
    i = pl.program_id(0)

    @pl.when(i == 0)
    def _():
        carry_scr[...] = jnp.zeros(carry_scr.shape, F32)

    t = (i * tm + lax.broadcasted_iota(jnp.int32, (tm, FF_TILE), 0)) & (seq_len - 1)
    h2 = h2_ref[...]
    acc = x1_ref[...]
    for j in range(D_FF // FF_TILE):
        cols = slice(j * FF_TILE, (j + 1) * FF_TILE)
        a = _dot(h2, wg_ref[:, cols])
        u = _dot(h2, wu_ref[:, cols])
        a_scr[0:SUBLANES, :] = carry_scr[:, cols]
        a_scr[SUBLANES:, :] = a
        carry_scr[:, cols] = a[tm - SUBLANES:, :]
        prev1 = jnp.where(t >= 1, a_scr[SUBLANES - 1:SUBLANES - 1 + tm, :], 0.0)
        prev2 = jnp.where(t >= 2, a_scr[SUBLANES - 2:SUBLANES - 2 + tm, :], 0.0)
        if has_state:
            prev1 = prev1 + p1_ref[:, cols]
            prev2 = prev2 + p2_ref[:, cols]
        conv = prev2 * cw_ref[0:1, cols] + prev1 * cw_ref[1:2, cols] + a * cw_ref[2:3, cols] + cb_ref[:, cols]
        acc = acc + _dot((_silu(conv) * u).astype(BF16), wd_ref[cols, :])
        if tail_rows == tm:
            tail_ref[:, cols] = a
        else:
            tail_ref[0, :, cols] = a[tm - tail_rows:, :]
    if final_norm:
        out_ref[...] = _rms(acc, nw_ref[...])
    else:
        out_ref[...] = acc
        hn_ref[...] = _rms(acc, nw_ref[...]).astype(BF16)


def _ffn(x1, h2, wg, wu, wd, conv_w, conv_b, seq_len, conv_state, norm_w, final_norm):
    n = x1.shape[0]
    has_state = conv_state is not None
    tm = min(TOKEN_TILE // 2 if has_state else TOKEN_TILE, n)
    assert n % tm == 0 and seq_len & (seq_len - 1) == 0
    batch = n // seq_len
    row = lambda width: pl.BlockSpec((tm, width), lambda i: (i, 0))
    full = lambda shape: pl.BlockSpec(shape, lambda i: (0,) * len(shape))
    in_specs = [row(D_MODEL), row(D_MODEL), _resident((D_MODEL, D_FF)), _resident((D_MODEL, D_FF)),
                _resident((D_FF, D_MODEL)), full((CONV_W, D_FF)), full((1, D_FF)), full((1, D_MODEL))]
    args = [x1, h2, wg, wu, wd, conv_w, conv_b, norm_w]
    if has_state:
        p1 = jnp.pad(conv_state[:, 1:2], ((0, 0), (0, seq_len - 1), (0, 0))).reshape(n, D_FF)
        p2 = jnp.pad(conv_state, ((0, 0), (0, seq_len - (CONV_W - 1)), (0, 0))).reshape(n, D_FF)
        in_specs += [row(D_FF), row(D_FF)]
        args += [p1, p2]
    if seq_len % tm == 0:
        tail_rows = CONV_W - 1
        tail_spec = pl.BlockSpec((1, tail_rows, D_FF), lambda i: ((i * tm) // seq_len, 0, 0))
        tail_shape = jax.ShapeDtypeStruct((batch, tail_rows, D_FF), F32)
    else:
        assert tm % seq_len == 0 and seq_len >= CONV_W - 1
        tail_rows = tm
        tail_spec = row(D_FF)
        tail_shape = jax.ShapeDtypeStruct((n, D_FF), F32)
    out_specs, out_shape = [row(D_MODEL)], [jax.ShapeDtypeStruct((n, D_MODEL), F32)]
    if not final_norm:
        out_specs.append(row(D_MODEL))
        out_shape.append(jax.ShapeDtypeStruct((n, D_MODEL), BF16))
    *outs, tail = pl.pallas_call(
        functools.partial(_ffn_kernel, seq_len=seq_len, has_state=has_state,
                          final_norm=final_norm, tail_rows=tail_rows),
        grid=(n // tm,),
        in_specs=in_specs,
        out_specs=out_specs + [tail_spec],
        out_shape=out_shape + [tail_shape],
        scratch_shapes=[pltpu.VMEM((SUBLANES, D_FF), F32), pltpu.VMEM((tm + SUBLANES, FF_TILE), F32)],
        compiler_params=pltpu.CompilerParams(
            dimension_semantics=("arbitrary",), vmem_limit_bytes=VMEM_LIMIT),
        name="conv_ffn",
    )(*args)
    if tail_rows == tm:
        tail = tail.reshape(batch, seq_len, D_FF)[:, seq_len - (CONV_W - 1):, :]
    return outs, tail


def _prep_weights(w_in, w_branch, w_out, ffn_w_gate, ffn_w_up, ffn_w_down):
    w_in = w_in.astype(BF16)
    w_main = jnp.concatenate([w_in[:, :, HG_OFF:ML_GATE_OFF], w_in[:, :, RET_OFF:GATE_OFF]], axis=2)
    w_gate = jnp.pad(w_in[:, :, ML_GATE_OFF:RET_OFF], ((0, 0), (0, 0), (0, LANES - 2 * N_HEAD)))
    return (w_main, w_gate,
            w_in[:, :, GATE_OFF:], w_branch.astype(BF16), w_out.astype(BF16),
            ffn_w_gate.astype(BF16), ffn_w_up.astype(BF16), ffn_w_down.astype(BF16))


def _trunk(x3, pos0, states, params, prepped):
    (norm1_w, mlstm_gate_b, branch_gate_b, hgrn_lb, hgrn_norm_w, mlstm_norm_w, ret_norm_w,
     norm2_w, ffn_conv_w, ffn_conv_b, final_norm_w) = params
    w_main, w_gate, w_bgate, w_branch, w_out, w_ffg, w_ffu, w_ffd = prepped
    batch, seq_len, _ = x3.shape
    n = batch * seq_len
    depth = norm1_w.shape[0]
    geo = _Geo(batch, seq_len)
    cos, sin = (t.reshape(geo.chunks, geo.rows, D_HEAD) for t in _rope_tables(geo.chunks * geo.rows, pos0))

    mat = (depth, batch, N_HEAD, D_HEAD, D_HEAD)
    vec = (depth, batch, N_HEAD, 1, D_HEAD)
    if states is None:
        hg_in = ml_in = ret_in = conv_in = None
    else:
        hg_in, ml_c, ml_n, ml_m, ret_in, conv_in = states
        ml_in = [ml_c, ml_n.reshape(vec), jnp.broadcast_to(ml_m[..., None, None], vec)]
    hg_out, c_out, r_out = (jnp.zeros(mat, F32) for _ in range(3))
    n_out, m_out = (jnp.zeros(vec, F32) for _ in range(2))

    x = x3.reshape(n, D_MODEL)
    h = _norm(x, norm1_w[0:1])
    conv_tails = []
    for l in range(depth):
        last = l == depth - 1
        y_hg, hg_out = _hgrn(geo, l, h, w_main, hgrn_lb, hgrn_norm_w[l:l + 1], hg_in, hg_out)
        gate_b = jnp.pad(mlstm_gate_b[l:l + 1], ((0, 0), (0, LANES - 2 * N_HEAD)))
        y_ml, c_out, n_out, m_out = _mlstm(geo, l, h, w_main, w_gate, gate_b, mlstm_norm_w[l:l + 1], ml_in,
                                           [c_out, n_out, m_out])
        y_ret, r_out = _retention(geo, l, h, w_main, cos, sin, ret_norm_w[l:l + 1], ret_in, r_out)
        x1, h2 = _merge(x, h, y_hg, y_ml, y_ret, w_bgate[l], branch_gate_b[l:l + 1], w_branch[l], w_out[l],
                        norm2_w[l:l + 1])
        outs, conv_tail = _ffn(x1, h2, w_ffg[l], w_ffu[l], w_ffd[l], ffn_conv_w[l], ffn_conv_b[l:l + 1],
                               seq_len, None if conv_in is None else conv_in[l],
                               final_norm_w[None, :] if last else norm1_w[l + 1:l + 2], last)
        x = outs[0]
        h = None if last else outs[1]
        conv_tails.append(conv_tail)
    return x.reshape(batch, seq_len, D_MODEL), (
        hg_out, c_out, n_out.reshape(depth, batch, N_HEAD, D_HEAD), m_out[:, :, :, 0, 0], r_out,
        jnp.stack(conv_tails))


def kernel(x_prompt, x_sample, state_hgrn, state_mlstm_C, state_mlstm_n, state_mlstm_m, state_ret,
           state_ffn_conv, norm1_w, w_in, mlstm_gate_b, branch_gate_b, hgrn_lb, hgrn_norm_w,
           mlstm_norm_w, ret_norm_w, w_branch, w_out, norm2_w, ffn_w_gate, ffn_w_up, ffn_conv_w,
           ffn_conv_b, ffn_w_down, final_norm_w):
    params = (norm1_w, mlstm_gate_b, branch_gate_b, hgrn_lb, hgrn_norm_w, mlstm_norm_w, ret_norm_w,
              norm2_w, ffn_conv_w, ffn_conv_b, final_norm_w)
    prepped = _prep_weights(w_in, w_branch, w_out, ffn_w_gate, ffn_w_up, ffn_w_down)
    y_p, (hg_p, c_p, n_p, m_p, r_p, cv_p) = _trunk(x_prompt, 0, None, params, prepped)
    y_s, (hg_s, c_s, n_s, m_s, r_s, cv_s) = _trunk(
        x_sample, PAST_LEN,
        (state_hgrn, state_mlstm_C, state_mlstm_n, state_mlstm_m, state_ret, state_ffn_conv),
        params, prepped)
    return (y_p, y_s, hg_p, hg_s, c_p, c_s, n_p, n_s, m_p, m_s, r_p, r_s, cv_p, cv_s)
```

```python
import functools
import math

import numpy as np
import jax
import jax.numpy as jnp
from jax import lax
from jax.experimental import pallas as pl
from jax.experimental.pallas import tpu as pltpu

F32 = jnp.float32
BF16 = jnp.bfloat16

D_MODEL = 1024
MIX_W = 512
N_HEAD = 4
D_HEAD = 128
N_BRANCH = 3
D_FF = 2816
CONV_W = 3
CHUNK = 64
EPS = 1e-6
ROPE_BASE = 10000.0
PAST_LEN = 16384

HG_OFF = 0
ML_OFF = 4 * MIX_W
ML_GATE_OFF = ML_OFF + 4 * MIX_W
RET_OFF = ML_GATE_OFF + 2 * N_HEAD
GATE_OFF = RET_OFF + 4 * MIX_W
LOG2E = 1.4426950408889634

LANES = 128
SUBLANES = 8
VMEM_LIMIT = 56 * 1024 * 1024

TOKEN_TILE = 512
NORM_TOKEN_TILE = 1024
FF_TILE = 2816
HG_SUB = 16
GROUP = 8
SEQ_BLOCK = 64
SCAN_UNROLL = 8


def _resident(shape):
    return pl.BlockSpec(shape, lambda *_: (0,) * len(shape), pipeline_mode=pl.Buffered(1))


def _dot(a, b):
    return jnp.dot(a, b, preferred_element_type=F32)


def _dot_tn(a, b):
    return lax.dot_general(a, b, (((0,), (0,)), ((), ())), preferred_element_type=F32)


def _bdot(a, b):
    return jnp.einsum('nlk,nkv->nlv', a, b, preferred_element_type=F32)


def _bdot_nt(a, b):
    return jnp.einsum('nqd,nkd->nqk', a, b, preferred_element_type=F32)


def _bdot_tn(a, b):
    return jnp.stack([_dot_tn(a[i], b[i]) for i in range(a.shape[0])])


def _split3(x):
    hi = x.astype(BF16)
    r = x - hi.astype(F32)
    mid = r.astype(BF16)
    lo = (r - mid.astype(F32)).astype(BF16)
    return hi, mid, lo


def _sigmoid(x):
    return 1.0 / (1.0 + jnp.exp(-x))


def _silu(x):
    return x * _sigmoid(x)


def _log_sigmoid(x):
    return jnp.minimum(x, 0.0) - jnp.log(1.0 + jnp.exp(-jnp.abs(x)))


def _rms(x, w):
    return x * lax.rsqrt(jnp.mean(x * x, axis=-1, keepdims=True) + EPS) * w


def _pad_rows(x, n):
    if x.shape[1] >= n:
        return x
    return jnp.concatenate([x, jnp.zeros((x.shape[0], n - x.shape[1], x.shape[2]), x.dtype)], axis=1)


def _seg_cumsum(x, seg):
    shape = x.shape
    flat = x.reshape(shape[0] * seg, shape[2])
    row = lax.broadcasted_iota(jnp.int32, flat.shape, 0) & (seg - 1)
    s = 1
    while s < seg:
        flat = flat + jnp.where(row >= s, pltpu.roll(flat, s, axis=0), 0.0)
        s *= 2
    return flat.reshape(shape)


def _col_bcast(rows):
    n = rows.shape[0]
    r = lax.broadcasted_iota(jnp.int32, (n, D_HEAD, D_HEAD), 1)
    c = lax.broadcasted_iota(jnp.int32, (n, D_HEAD, D_HEAD), 2)
    d = jnp.where(r == c, jnp.broadcast_to(rows, (n, D_HEAD, D_HEAD)), 0.0).reshape(n * D_HEAD, D_HEAD)
    ones = jnp.ones((D_HEAD, D_HEAD), BF16)
    hi, mid, lo = _split3(d)
    return (_dot(hi, ones) + _dot(mid, ones) + _dot(lo, ones)).reshape(n, D_HEAD, D_HEAD)


def _through_vmem(x):
    def body(ref):
        ref[...] = x
        return ref[...]
    return pl.run_scoped(body, pltpu.VMEM(x.shape, x.dtype))


def _row_bcast(col, width):
    n, rows, _ = col.shape
    vals = _through_vmem(_pad_rows(col, width))
    lane = lax.broadcasted_iota(jnp.int32, (n, rows, LANES), 2)
    pick = _through_vmem(jnp.where(lane == 0, 1.0, 0.0).astype(BF16))
    hi, mid, lo = _split3(vals)
    return _bdot_nt(pick, hi) + _bdot_nt(pick, mid) + _bdot_nt(pick, lo)


def _chunk_geometry(seq_len):
    if seq_len % CHUNK == 0:
        return CHUNK, CHUNK, CHUNK
    valid = math.gcd(seq_len, CHUNK)
    assert valid == seq_len and valid <= SUBLANES, "unsupported sequence length"
    return SUBLANES, valid, LANES


def _rope_kernel(inv_ref, cos_ref, sin_ref, *, pos0):
    shape = cos_ref.shape
    pos = lax.broadcasted_iota(jnp.int32, shape, 0).astype(F32) + pos0
    ang = pos * inv_ref[...]
    lane = lax.broadcasted_iota(jnp.int32, shape, 1)
    cos_ref[...] = jnp.cos(ang)
    sin_ref[...] = jnp.where(lane < D_HEAD // 2, -jnp.sin(ang), jnp.sin(ang))


def _rope_tables(rows, pos0):
    inv = ROPE_BASE ** (-jnp.linspace(0.0, 1.0, D_HEAD // 2, dtype=F32))
    inv2 = jnp.concatenate([inv, inv])[None, :]
    return pl.pallas_call(
        functools.partial(_rope_kernel, pos0=float(pos0)),
        out_shape=(jax.ShapeDtypeStruct((rows, D_HEAD), F32),) * 2,
        name="rope_tables",
    )(inv2)


def _norm_kernel(x_ref, w_ref, h_ref):
    h_ref[...] = _rms(x_ref[...], w_ref[...]).astype(BF16)


def _norm(x, w):
    n = x.shape[0]
    tm = min(NORM_TOKEN_TILE, n)
    assert n % tm == 0
    return pl.pallas_call(
        _norm_kernel,
        grid=(n // tm,),
        in_specs=[pl.BlockSpec((tm, D_MODEL), lambda i: (i, 0)), pl.BlockSpec((1, D_MODEL), lambda i: (0, 0))],
        out_specs=pl.BlockSpec((tm, D_MODEL), lambda i: (i, 0)),
        out_shape=jax.ShapeDtypeStruct((n, D_MODEL), BF16),
        compiler_params=pltpu.CompilerParams(dimension_semantics=("arbitrary",), vmem_limit_bytes=VMEM_LIMIT),
        name="norm_in",
    )(x, w)


class _Geo:
    def __init__(self, batch, seq_len):
        self.rows, self.valid, self.width = _chunk_geometry(seq_len)
        if self.valid == self.rows:
            self.seqs, self.chunks = 1, seq_len // self.rows
            self.group = min(GROUP, self.chunks)
        else:
            self.seqs, self.chunks = min(SEQ_BLOCK, batch), 1
            self.group = self.seqs
        assert batch % self.seqs == 0 and self.chunks % min(self.group, self.chunks) == 0
        self.batch = batch
        self.problems = self.seqs * self.chunks
        self.n_groups = self.problems // self.group
        self.group_tokens = self.group * self.valid
        self.tokens = self.problems * self.valid
        self.padded = self.valid < self.rows

    def project(self, h_ref, w, g):
        return _dot(h_ref[pl.ds(g * self.group_tokens, self.group_tokens), :], w)

    def split(self, z, relay):
        n_parts = z.shape[1] // D_HEAD
        cols = lambda p: slice(p * D_HEAD, (p + 1) * D_HEAD)
        if not self.padded:
            return [z[:, cols(p)].reshape(self.group, self.rows, D_HEAD) for p in range(n_parts)]
        out = []
        for p in range(n_parts):
            flat_ref, pad_ref = relay[2 * p], relay[2 * p + 1]
            flat_ref[...] = z[:, cols(p)]
            pad_ref[...] = jnp.zeros(pad_ref.shape, F32)
            for t in range(self.valid):
                pad_ref[pl.ds(t, self.seqs, stride=self.rows), :] = flat_ref[
                    pl.ds(t, self.seqs, stride=self.valid), :]
            out.append(pad_ref[...].reshape(self.seqs, self.rows, D_HEAD))
        return out

    def store(self, y_ref, g, y, relay):
        if not self.padded:
            n = self.group * self.rows
            y_ref[pl.ds(g * n, n), :] = y.reshape(n, D_HEAD).astype(y_ref.dtype)
            return
        pad = relay[1]
        pad[...] = y.reshape(self.seqs * self.rows, D_HEAD)
        for t in range(self.valid):
            y_ref[pl.ds(t, self.seqs, stride=self.valid), :] = pad[pl.ds(t, self.seqs, stride=self.rows), :].astype(
                y_ref.dtype)

    def rows_of(self, g):
        return pl.ds(g * self.group, self.group)

    def row_mask(self, shape):
        if not self.padded:
            return None
        return lax.broadcasted_iota(jnp.int32, shape, 1) < self.valid

    def for_groups(self, body):
        if self.n_groups == 1:
            body(0)
        else:
            lax.fori_loop(0, self.n_groups, lambda g, c: (body(g), c)[1], 0)

    def for_groups_static(self, body):
        for g in range(self.n_groups):
            body(g)

    def scan_chunks(self, step, init):
        if self.chunks == 1:
            return step(pl.ds(0, self.seqs), init)
        return lax.fori_loop(0, self.chunks, lambda c, carry: step(pl.ds(c * self.seqs, self.seqs), carry), init,
                             unroll=min(SCAN_UNROLL, self.chunks))


def _join_columns(w_refs):
    return jnp.concatenate([r[...] for r in w_refs], axis=1)


def _transpose_minor(x):
    return jnp.stack([x[i].T for i in range(x.shape[0])])


def _state_views(refs, kinds, layer, layered):
    head = pl.program_id(1)
    views = []
    for ref, kind in zip(refs, kinds):
        for l in (range(ref.shape[0]) if layered else [layer]):
            view = ref.at[l] if layered else ref
            if kind == 'vec':
                view = view.at[:, pl.ds(head, 1), :]
            if l == layer:
                views.append(view)
            else:
                view[...] = jnp.zeros(view.shape, F32)
    return views


def _split_refs(refs, counts):
    out, k = [], 0
    for c in counts:
        out.append(refs[k:k + c])
        k += c
    return out


def _hgrn_lower_bound(lb_ref, layer):
    x = lb_ref[...]
    e = jnp.exp(x - jnp.max(x, axis=0, keepdims=True))
    soft = e / jnp.sum(e, axis=0, keepdims=True)
    cum = soft[0:1]
    for j in range(1, layer + 1):
        cum = cum + soft[j:j + 1]
    return cum - soft[0:1]


def _hgrn_placement(geo):
    sub = min(HG_SUB, geo.rows)
    r = np.arange(sub * D_HEAD)[:, None] // D_HEAD
    c = np.arange(geo.width)[None, :]
    return jnp.asarray(np.stack([r + i * sub == c for i in range(geo.rows // sub)]), BF16)


def _hgrn_intra(q, k, b2, place_ref, geo):
    n = q.shape[0]
    sub = min(HG_SUB, geo.rows)
    sub_row = lax.broadcasted_iota(jnp.int32, (n, sub, geo.width), 1)
    lane = lax.broadcasted_iota(jnp.int32, (n, sub, geo.width), 2)
    blocks = []
    for i in range(geo.rows // sub):
        r0 = i * sub
        b_i, q_i, k_i = (a[:, r0:r0 + sub] for a in (b2, q, k))
        terms = []
        for s in range(sub):
            lo = s // SUBLANES * SUBLANES
            e = jnp.exp2(jnp.minimum(b_i[:, lo:] - b_i[:, s:s + 1], 0.0))
            term = q_i[:, lo:] * e * k_i[:, s:s + 1]
            if lo:
                term = jnp.concatenate([jnp.zeros((n, lo, D_HEAD), F32), term], axis=1)
            terms.append(term)
        cat = jnp.concatenate(terms, axis=2).reshape(n * sub, sub * D_HEAD)
        a_i = _dot(cat.astype(BF16), place_ref[i]).reshape(n, sub, geo.width)
        a_i = jnp.where(sub_row >= lane - r0, a_i, 0.0)
        if i > 0:
            b_ref = b_i[:, 0:1]
            q_dec = q_i * jnp.exp2(b_i - b_ref)
            k_dec = _pad_rows(k[:, 0:r0] * jnp.exp2(b_ref - b2[:, 0:r0]), geo.width)
            a_i = jnp.where(lane < r0, _bdot_nt(q_dec.astype(BF16), k_dec.astype(BF16)), a_i)
        blocks.append(a_i)
    return blocks[0] if len(blocks) == 1 else jnp.concatenate(blocks, axis=1)


def _hgrn_kernel(*refs, layer, geo, has_state, fresh):
    ((h_ref,), w_refs, (lb_ref, nw_ref, place), s0, _, (y_ref,), s_out,
     (qt_scr, o_scr, u_scr, dc_scr, gate_scr), relay) = _split_refs(
        refs, (1, 4, 3, int(has_state), int(not fresh), 1, 1, 5, len(refs)))
    (s_ref,) = _state_views(s_out, ['mat'], layer, fresh)
    lb = _hgrn_lower_bound(lb_ref, layer)
    rows, width = geo.rows, geo.width
    w = _join_columns(w_refs)

    def phase_a(g):
        hq, hf, hi, hg = geo.split(geo.project(h_ref, w, g), relay)
        q = _silu(hq)
        f = lb + (1.0 - lb) * _sigmoid(hf)
        lf = jnp.log(f)
        k = 1.0 - f
        mask = geo.row_mask(lf.shape)
        if mask is not None:
            lf = jnp.where(mask, lf, 0.0)
            k = jnp.where(mask, k, 0.0)
        b2 = _seg_cumsum(lf, rows) * LOG2E
        sl = geo.rows_of(g)
        gate_scr[sl] = hg
        qt_scr[sl] = (q * jnp.exp2(b2)).astype(BF16)
        a = _hgrn_intra(q, k, b2, place, geo)
        vb = _pad_rows(hi, width).astype(BF16)
        o_scr[sl] = _bdot(a.astype(BF16), vb)
        b_last = b2[:, rows - 1:rows]
        k_w = k * jnp.exp2(b_last - b2)
        u_scr[sl] = _bdot_tn(vb, _pad_rows(k_w, width).astype(BF16))
        dc_scr[sl] = jnp.exp2(b_last)

    geo.for_groups_static(phase_a)

    def step(sl, state_t):
        o_scr[sl] = o_scr[sl] + _bdot_nt(qt_scr[sl], state_t.astype(BF16))
        return dc_scr[sl] * state_t + u_scr[sl]

    init = _transpose_minor(s0[0][...]) if has_state else jnp.zeros((geo.seqs, D_HEAD, D_HEAD), F32)
    s_ref[...] = _transpose_minor(geo.scan_chunks(step, init))

    def epilogue(g):
        sl = geo.rows_of(g)
        geo.store(y_ref, g, _rms(o_scr[sl], nw_ref[...]) * _silu(gate_scr[sl]), relay)

    geo.for_groups(epilogue)


def _mlstm_kernel(*refs, layer, geo, has_state, fresh):
    ((h_ref,), w_refs, (gb_ref, nw_ref), st0, _, (y_ref,), st_out,
     (q_scr, v_scr, u_scr, nu_scr, bc_scr, ml_scr, rs_scr, bl_scr, mu_scr, gate_scr), relay) = _split_refs(
        refs, (1, 5, 2, 3 * int(has_state), 3 * int(not fresh), 1, 3, 10, len(refs)))
    kinds = ['mat', 'vec', 'vec']
    c_ref, n_ref, m_ref = _state_views(st_out, kinds, layer, fresh)
    st0 = _state_views(st0, kinds, layer, False) if has_state else st0
    rows, width = geo.rows, geo.width
    head = pl.program_id(1)
    scale = D_HEAD ** -0.5
    t_i = lax.broadcasted_iota(jnp.int32, (1, rows, width), 1)
    s_i = lax.broadcasted_iota(jnp.int32, (1, rows, width), 2)
    ok = s_i <= t_i
    if geo.padded:
        ok = ok & (s_i < geo.valid)
    w = _join_columns(w_refs)

    def phase_a(g):
        q, k, v, og, gates = geo.split(geo.project(h_ref, w, g), relay)
        gates = gates + gb_ref[...]
        lf_all = _log_sigmoid(gates)
        mask = geo.row_mask(gates.shape)
        if mask is not None:
            lf_all = jnp.where(mask, lf_all, 0.0)
        bc_all = _seg_cumsum(lf_all, rows)
        lane = lax.broadcasted_iota(jnp.int32, gates.shape, 2)
        wide = lambda x: jnp.broadcast_to(x, gates.shape)
        ig = wide(jnp.sum(jnp.where(lane == head, gates, 0.0), axis=2, keepdims=True))
        bc_col = jnp.sum(jnp.where(lane == head + N_HEAD, bc_all, 0.0), axis=2, keepdims=True)
        bc = wide(bc_col)
        log_d = jnp.where(ok, bc_col + _row_bcast(ig - bc, width), -jnp.inf)
        m_col = jnp.max(log_d, axis=2, keepdims=True)
        m_loc = wide(m_col)
        k = k * scale
        qb = q.astype(BF16)
        vb = _pad_rows(v, width).astype(BF16)
        s_m = _bdot_nt(qb, _pad_rows(k, width).astype(BF16)) * jnp.exp(log_d - m_col)
        sl = geo.rows_of(g)
        gate_scr[sl] = og
        q_scr[sl] = qb
        v_scr[sl] = _bdot(s_m.astype(BF16), vb)
        rs_scr[sl] = wide(jnp.sum(s_m, axis=2, keepdims=True))
        bc_scr[sl] = bc
        ml_scr[sl] = m_loc
        b_last = bc[:, rows - 1:rows]
        m_upd = m_loc[:, rows - 1:rows]
        w_exp = b_last - bc + ig - m_upd
        if mask is not None:
            w_exp = jnp.where(mask, w_exp, -jnp.inf)
        k_w = k * jnp.exp(w_exp)
        u_scr[sl] = _bdot_tn(_pad_rows(k_w, width).astype(BF16), vb)
        nu_scr[sl] = jnp.sum(k_w, axis=1, keepdims=True)
        bl_scr[sl] = b_last
        mu_scr[sl] = m_upd

    geo.for_groups_static(phase_a)

    def step(sl, c_state):
        n_state, m_prev = n_ref[...], m_ref[...]
        inter_log = bc_scr[sl] + m_prev
        m_t = jnp.maximum(inter_log, ml_scr[sl])
        w_inter = jnp.exp(inter_log - m_t)
        w_loc = jnp.exp(ml_scr[sl] - m_t)
        qb = q_scr[sl]
        num = w_inter * _bdot(qb, c_state.astype(BF16)) + w_loc * v_scr[sl]
        q_n = jnp.broadcast_to(jnp.sum(qb.astype(F32) * n_state, axis=2, keepdims=True), m_t.shape)
        nq = w_inter * q_n + w_loc * rs_scr[sl]
        v_scr[sl] = num / jnp.maximum(jnp.abs(nq), jnp.exp(-m_t))
        m_new = m_t[:, rows - 1:rows]
        w_prev = jnp.exp(bl_scr[sl] + m_prev - m_new)
        w_upd = jnp.exp(mu_scr[sl] - m_new)
        n_ref[...] = w_prev * n_state + w_upd * nu_scr[sl]
        m_ref[...] = m_new
        return w_prev * c_state + w_upd * u_scr[sl]

    if has_state:
        c_init = st0[0][...]
        n_ref[...] = st0[1][...]
        m_ref[...] = st0[2][...]
    else:
        c_init = jnp.zeros((geo.seqs, D_HEAD, D_HEAD), F32)
        n_ref[...] = jnp.zeros(n_ref.shape, F32)
        m_ref[...] = jnp.zeros(m_ref.shape, F32)
    c_ref[...] = geo.scan_chunks(step, c_init)

    def epilogue(g):
        sl = geo.rows_of(g)
        geo.store(y_ref, g, _rms(_sigmoid(gate_scr[sl]) * v_scr[sl], nw_ref[...]), relay)

    geo.for_groups(epilogue)


def _ret_kernel(*refs, layer, geo, has_state, fresh):
    ((h_ref,), w_refs, (cos_ref, sin_ref, nw_ref), r0, _, (y_ref,), r_out,
     (q_scr, o_scr, u_scr, gate_scr), relay) = _split_refs(
        refs, (1, 4, 3, int(has_state), int(not fresh), 1, 1, 4, len(refs)))
    (r_ref,) = _state_views(r_out, ['mat'], layer, fresh)
    rows, width, valid = geo.rows, geo.width, geo.valid
    head = pl.program_id(1)
    log_gamma = jnp.zeros((1, 1, 1), F32)
    for h in range(N_HEAD):
        log_gamma = jnp.where(head == h, math.log(1.0 - 2.0 ** (-5.0 - h)), log_gamma)
    scale = D_HEAD ** -0.5
    t_i = lax.broadcasted_iota(jnp.int32, (1, rows, width), 1)
    s_i = lax.broadcasted_iota(jnp.int32, (1, rows, width), 2)
    decay = jnp.where(s_i <= t_i, jnp.exp((t_i - s_i).astype(F32) * log_gamma), 0.0)
    t_col = lax.broadcasted_iota(jnp.int32, (1, rows, D_HEAD), 1).astype(F32)
    w_in = jnp.exp((t_col + 1.0) * log_gamma)
    w_st = jnp.exp((valid - 1.0 - t_col) * log_gamma)
    g_all = jnp.exp(valid * log_gamma)
    w = _join_columns(w_refs)

    def rotate(x, cos, sin):
        flat = x.reshape(x.shape[0] * rows, D_HEAD)
        return x * cos + pltpu.roll(flat, D_HEAD // 2, axis=1).reshape(x.shape) * sin

    def phase_a(g):
        q, k, v, rg = geo.split(geo.project(h_ref, w, g), relay)
        if geo.chunks > 1:
            cos, sin = cos_ref[geo.rows_of(g)], sin_ref[geo.rows_of(g)]
        else:
            cos, sin = cos_ref[...], sin_ref[...]
        qr = rotate(q, cos, sin).astype(BF16)
        kr = rotate(k, cos, sin) * scale
        vb = _pad_rows(v, width).astype(BF16)
        a = _bdot_nt(qr, _pad_rows(kr, width).astype(BF16)) * decay
        sl = geo.rows_of(g)
        gate_scr[sl] = rg
        q_scr[sl] = qr
        o_scr[sl] = _bdot(a.astype(BF16), vb)
        u_scr[sl] = _bdot_tn(_pad_rows(kr * w_st, width).astype(BF16), vb)

    geo.for_groups_static(phase_a)

    def step(sl, state):
        o_scr[sl] = o_scr[sl] + _bdot(q_scr[sl], state.astype(BF16)) * w_in
        return g_all * state + u_scr[sl]

    init = r0[0][...] if has_state else jnp.zeros((geo.seqs, D_HEAD, D_HEAD), F32)
    r_ref[...] = geo.scan_chunks(step, init)

    def epilogue(g):
        sl = geo.rows_of(g)
        geo.store(y_ref, g, _rms(o_scr[sl], nw_ref[...]) * _silu(gate_scr[sl]), relay)

    geo.for_groups(epilogue)


def _mixer_call(kernel, name, geo, layer, h, weights, extra, states_in, states_prev, scratch):
    seqs, n_cols = geo.seqs, len(weights) * D_HEAD
    fresh = isinstance(states_prev[0], jax.ShapeDtypeStruct)

    def weight_spec(first, per_head):
        return pl.BlockSpec((None, D_MODEL, D_HEAD), lambda bi, j: (layer, 0, first + per_head * j))

    def state_spec(arr, layered=False):
        lead, first = (arr.shape[0], 0) if layered else (None, layer)
        if len(arr.shape) == 5:
            return pl.BlockSpec((lead, seqs, None, D_HEAD, D_HEAD), lambda bi, j: (first, bi, j, 0, 0))
        return pl.BlockSpec((lead, seqs, N_HEAD, D_HEAD), lambda bi, j: (first, bi, 0, 0))

    has_state = states_in is not None
    args = [h] + [a for a, _, _ in weights] + [a for a, _ in extra]
    in_specs = [pl.BlockSpec((geo.tokens, D_MODEL), lambda bi, j: (bi, 0))] + [
        weight_spec(first, per_head) for _, first, per_head in weights] + [s for _, s in extra]
    if has_state:
        args += list(states_in)
        in_specs += [state_spec(a) for a in states_in]
    n_in = len(args)
    if not fresh:
        args += list(states_prev)
        in_specs += [pl.BlockSpec(memory_space=pl.ANY) for _ in states_prev]
    out_specs = [pl.BlockSpec((geo.tokens, D_HEAD), lambda bi, j: (bi, j))] + [
        state_spec(a, fresh) for a in states_prev]
    out_shape = [jax.ShapeDtypeStruct((h.shape[0], MIX_W), F32 if geo.padded else BF16)] + [
        jax.ShapeDtypeStruct(a.shape, F32) for a in states_prev]
    scratch_shapes = list(scratch) + [pltpu.VMEM((geo.problems, geo.rows, D_HEAD), F32)]
    if geo.padded:
        scratch_shapes += [pltpu.VMEM((geo.group_tokens, D_HEAD), F32),
                           pltpu.VMEM((seqs * geo.rows, D_HEAD), F32)] * (n_cols // D_HEAD)
    return pl.pallas_call(
        functools.partial(kernel, layer=layer, geo=geo, has_state=has_state, fresh=fresh),
        grid=(geo.batch // seqs, N_HEAD),
        in_specs=in_specs,
        out_specs=out_specs,
        out_shape=out_shape,
        scratch_shapes=scratch_shapes,
        input_output_aliases={} if fresh else {n_in + i: 1 + i for i in range(len(states_prev))},
        compiler_params=pltpu.CompilerParams(
            dimension_semantics=("arbitrary", "arbitrary"), vmem_limit_bytes=VMEM_LIMIT),
        name=name,
    )(*args)


def _head_block(rows):
    return pl.BlockSpec((rows, D_HEAD), lambda bi, j: (0, j))


def _part_blocks(w_main, mixer):
    return [(w_main, (mixer * 4 + p) * N_HEAD, 1) for p in range(4)]


def _hgrn(geo, layer, h, w_main, hgrn_lb, nw, state_in, state_prev):
    n = geo.problems
    scratch = [pltpu.VMEM((n, geo.rows, D_HEAD), BF16), pltpu.VMEM((n, geo.rows, D_HEAD), F32),
               pltpu.VMEM((n, D_HEAD, D_HEAD), F32), pltpu.VMEM((n, 1, D_HEAD), F32)]
    place = _hgrn_placement(geo)
    return _mixer_call(
        _hgrn_kernel, "hgrn2", geo, layer, h, _part_blocks(w_main, 0),
        [(hgrn_lb, _head_block(hgrn_lb.shape[0])), (nw, _head_block(1)), (place, _resident(place.shape))],
        None if state_in is None else [state_in], [state_prev], scratch)


def _mlstm(geo, layer, h, w_main, w_gate, gate_b, nw, states_in, states_prev):
    n = geo.problems
    col = lambda r: pltpu.VMEM((n, r, LANES), F32)
    scratch = [pltpu.VMEM((n, geo.rows, D_HEAD), BF16), pltpu.VMEM((n, geo.rows, D_HEAD), F32),
               pltpu.VMEM((n, D_HEAD, D_HEAD), F32), pltpu.VMEM((n, 1, D_HEAD), F32),
               col(geo.rows), col(geo.rows), col(geo.rows), col(1), col(1)]
    return _mixer_call(
        _mlstm_kernel, "mlstm", geo, layer, h, _part_blocks(w_main, 1) + [(w_gate, 0, 0)],
        [(gate_b, pl.BlockSpec((1, LANES), lambda bi, j: (0, 0))), (nw, _head_block(1))],
        states_in, states_prev, scratch)


def _retention(geo, layer, h, w_main, cos, sin, nw, state_in, state_prev):
    n = geo.problems
    scratch = [pltpu.VMEM((n, geo.rows, D_HEAD), BF16), pltpu.VMEM((n, geo.rows, D_HEAD), F32),
               pltpu.VMEM((n, D_HEAD, D_HEAD), F32)]
    table = pl.BlockSpec(cos.shape, lambda bi, j: (0, 0, 0))
    return _mixer_call(
        _ret_kernel, "retention", geo, layer, h, _part_blocks(w_main, 2),
        [(cos, table), (sin, table), (nw, _head_block(1))],
        None if state_in is None else [state_in], [state_prev], scratch)


def _merge_kernel(x_ref, h_ref, yh_ref, ym_ref, yr_ref, wg_ref, bg_ref, wb_ref, wo_ref, n2_ref, x1_ref, h2_ref):
    hb = h_ref[...]
    merged = None
    for j, y_ref in enumerate((yh_ref, ym_ref, yr_ref)):
        cols = slice(j * D_MODEL, (j + 1) * D_MODEL)
        gate = _sigmoid(_dot(hb, wg_ref[:, cols]) + bg_ref[:, cols])
        term = gate * _dot(y_ref[...].astype(BF16), wb_ref[j])
        merged = term if merged is None else merged + term
    x1 = x_ref[...] + _dot(merged.astype(BF16), wo_ref[...])
    x1_ref[...] = x1
    h2_ref[...] = _rms(x1, n2_ref[...]).astype(BF16)


def _merge(x, h, yh, ym, yr, w_bgate, gate_b, w_branch, w_out, norm2_w):
    n = x.shape[0]
    tm = min(TOKEN_TILE, n)
    gate_w = N_BRANCH * D_MODEL
    row = lambda width: pl.BlockSpec((tm, width), lambda i: (i, 0))
    vec = lambda width: pl.BlockSpec((1, width), lambda i: (0, 0))
    return pl.pallas_call(
        _merge_kernel,
        grid=(n // tm,),
        in_specs=[
            row(D_MODEL), row(D_MODEL), row(MIX_W), row(MIX_W), row(MIX_W),
            _resident((D_MODEL, gate_w)), vec(gate_w),
            _resident((N_BRANCH, MIX_W, D_MODEL)),
            _resident((D_MODEL, D_MODEL)),
            vec(D_MODEL),
        ],
        out_specs=[row(D_MODEL), row(D_MODEL)],
        out_shape=[jax.ShapeDtypeStruct((n, D_MODEL), F32), jax.ShapeDtypeStruct((n, D_MODEL), BF16)],
        compiler_params=pltpu.CompilerParams(
            dimension_semantics=("arbitrary",), vmem_limit_bytes=VMEM_LIMIT),
        name="merge",
    )(x, h, yh, ym, yr, w_bgate, gate_b, w_branch, w_out, norm2_w)


def _ffn_kernel(*refs, seq_len, has_state, final_norm, tail_rows):
    refs = list(refs)
    x1_ref, h2_ref, wg_ref, wu_ref, wd_ref, cw_ref, cb_ref, nw_ref = refs[:8]
    refs = refs[8:]
    p1_ref = p2_ref = None
    if has_state:
        p1_ref, p2_ref = refs[:2]
        refs = refs[2:]
    if final_norm:
        out_ref, tail_ref, carry_scr, a_scr = refs
    else:
        out_ref, hn_ref, tail_ref, carry_scr, a_scr = refs
    tm = x1_ref.shape[0]
    i = pl.program_id(0)

    @pl.when(i == 0)
    def _():
        carry_scr[...] = jnp.zeros(carry_scr.shape, F32)

    t = (i * tm + lax.broadcasted_iota(jnp.int32, (tm, FF_TILE), 0)) & (seq_len - 1)
    h2 = h2_ref[...]
    acc = x1_ref[...]
    for j in range(D_FF // FF_TILE):
        cols = slice(j * FF_TILE, (j + 1) * FF_TILE)
        a = _dot(h2, wg_ref[:, cols])
        u = _dot(h2, wu_ref[:, cols])
        a_scr[0:SUBLANES, :] = carry_scr[:, cols]
        a_scr[SUBLANES:, :] = a
        carry_scr[:, cols] = a[tm - SUBLANES:, :]
        prev1 = jnp.where(t >= 1, a_scr[SUBLANES - 1:SUBLANES - 1 + tm, :], 0.0)
        prev2 = jnp.where(t >= 2, a_scr[SUBLANES - 2:SUBLANES - 2 + tm, :], 0.0)
        if has_state:
            prev1 = prev1 + p1_ref[:, cols]
            prev2 = prev2 + p2_ref[:, cols]
        conv = prev2 * cw_ref[0:1, cols] + prev1 * cw_ref[1:2, cols] + a * cw_ref[2:3, cols] + cb_ref[:, cols]
        acc = acc + _dot((_silu(conv) * u).astype(BF16), wd_ref[cols, :])
        if tail_rows == tm:
            tail_ref[:, cols] = a
        else:
            tail_ref[0, :, cols] = a[tm - tail_rows:, :]
    if final_norm:
        out_ref[...] = _rms(acc, nw_ref[...])
    else:
        out_ref[...] = acc
        hn_ref[...] = _rms(acc, nw_ref[...]).astype(BF16)


def _ffn(x1, h2, wg, wu, wd, conv_w, conv_b, seq_len, conv_state, norm_w, final_norm):
    n = x1.shape[0]
    has_state = conv_state is not None
    tm = min(TOKEN_TILE // 2 if has_state else TOKEN_TILE, n)
    assert n % tm == 0 and seq_len & (seq_len - 1) == 0
    batch = n // seq_len
    row = lambda width: pl.BlockSpec((tm, width), lambda i: (i, 0))
    full = lambda shape: pl.BlockSpec(shape, lambda i: (0,) * len(shape))
    in_specs = [row(D_MODEL), row(D_MODEL), _resident((D_MODEL, D_FF)), _resident((D_MODEL, D_FF)),
                _resident((D_FF, D_MODEL)), full((CONV_W, D_FF)), full((1, D_FF)), full((1, D_MODEL))]
    args = [x1, h2, wg, wu, wd, conv_w, conv_b, norm_w]
    if has_state:
        p1 = jnp.pad(conv_state[:, 1:2], ((0, 0), (0, seq_len - 1), (0, 0))).reshape(n, D_FF)
        p2 = jnp.pad(conv_state, ((0, 0), (0, seq_len - (CONV_W - 1)), (0, 0))).reshape(n, D_FF)
        in_specs += [row(D_FF), row(D_FF)]
        args += [p1, p2]
    if seq_len % tm == 0:
        tail_rows = CONV_W - 1
        tail_spec = pl.BlockSpec((1, tail_rows, D_FF), lambda i: ((i * tm) // seq_len, 0, 0))
        tail_shape = jax.ShapeDtypeStruct((batch, tail_rows, D_FF), F32)
    else:
        assert tm % seq_len == 0 and seq_len >= CONV_W - 1
        tail_rows = tm
        tail_spec = row(D_FF)
        tail_shape = jax.ShapeDtypeStruct((n, D_FF), F32)
    out_specs, out_shape = [row(D_MODEL)], [jax.ShapeDtypeStruct((n, D_MODEL), F32)]
    if not final_norm:
        out_specs.append(row(D_MODEL))
        out_shape.append(jax.ShapeDtypeStruct((n, D_MODEL), BF16))
    *outs, tail = pl.pallas_call(
        functools.partial(_ffn_kernel, seq_len=seq_len, has_state=has_state,
                          final_norm=final_norm, tail_rows=tail_rows),
        grid=(n // tm,),
        in_specs=in_specs,
        out_specs=out_specs + [tail_spec],
        out_shape=out_shape + [tail_shape],
        scratch_shapes=[pltpu.VMEM((SUBLANES, D_FF), F32), pltpu.VMEM((tm + SUBLANES, FF_TILE), F32)],
        compiler_params=pltpu.CompilerParams(
            dimension_semantics=("arbitrary",), vmem_limit_bytes=VMEM_LIMIT),
        name="conv_ffn",
    )(*args)
    if tail_rows == tm:
        tail = tail.reshape(batch, seq_len, D_FF)[:, seq_len - (CONV_W - 1):, :]
    return outs, tail


def _prep_weights(w_in, w_branch, w_out, ffn_w_gate, ffn_w_up, ffn_w_down):
    w_in = w_in.astype(BF16)
    w_main = jnp.concatenate([w_in[:, :, HG_OFF:ML_GATE_OFF], w_in[:, :, RET_OFF:GATE_OFF]], axis=2)
    w_gate = jnp.pad(w_in[:, :, ML_GATE_OFF:RET_OFF], ((0, 0), (0, 0), (0, LANES - 2 * N_HEAD)))
    return (w_main, w_gate,
            w_in[:, :, GATE_OFF:], w_branch.astype(BF16), w_out.astype(BF16),
            ffn_w_gate.astype(BF16), ffn_w_up.astype(BF16), ffn_w_down.astype(BF16))


def _trunk(x3, pos0, states, params, prepped):
    (norm1_w, mlstm_gate_b, branch_gate_b, hgrn_lb, hgrn_norm_w, mlstm_norm_w, ret_norm_w,
     norm2_w, ffn_conv_w, ffn_conv_b, final_norm_w) = params
    w_main, w_gate, w_bgate, w_branch, w_out, w_ffg, w_ffu, w_ffd = prepped
    batch, seq_len, _ = x3.shape
    n = batch * seq_len
    depth = norm1_w.shape[0]
    geo = _Geo(batch, seq_len)
    cos, sin = (t.reshape(geo.chunks, geo.rows, D_HEAD) for t in _rope_tables(geo.chunks * geo.rows, pos0))

    mat = jax.ShapeDtypeStruct((depth, batch, N_HEAD, D_HEAD, D_HEAD), F32)
    vec = jax.ShapeDtypeStruct((depth, batch, N_HEAD, D_HEAD), F32)
    if states is None:
        hg_in = ml_in = ret_in = conv_in = None
    else:
        hg_in, ml_c, ml_n, ml_m, ret_in, conv_in = states
        ml_in = [ml_c, ml_n, jnp.broadcast_to(ml_m[..., None], vec.shape)]
    hg_out, c_out, r_out, n_out, m_out = mat, mat, mat, vec, vec

    x = x3.reshape(n, D_MODEL)
    h = _norm(x, norm1_w[0:1])
    conv_tails = []
    for l in range(depth):
        last = l == depth - 1
        y_hg, hg_out = _hgrn(geo, l, h, w_main, hgrn_lb, hgrn_norm_w[l:l + 1], hg_in, hg_out)
        gate_b = jnp.pad(mlstm_gate_b[l:l + 1], ((0, 0), (0, LANES - 2 * N_HEAD)))
        y_ml, c_out, n_out, m_out = _mlstm(geo, l, h, w_main, w_gate, gate_b, mlstm_norm_w[l:l + 1], ml_in,
                                           [c_out, n_out, m_out])
        y_ret, r_out = _retention(geo, l, h, w_main, cos, sin, ret_norm_w[l:l + 1], ret_in, r_out)
        x1, h2 = _merge(x, h, y_hg, y_ml, y_ret, w_bgate[l], branch_gate_b[l:l + 1], w_branch[l], w_out[l],
                        norm2_w[l:l + 1])
        outs, conv_tail = _ffn(x1, h2, w_ffg[l], w_ffu[l], w_ffd[l], ffn_conv_w[l], ffn_conv_b[l:l + 1],
                               seq_len, None if conv_in is None else conv_in[l],
                               final_norm_w[None, :] if last else norm1_w[l + 1:l + 2], last)
        x = outs[0]
        h = None if last else outs[1]
        conv_tails.append(conv_tail)
    return x.reshape(batch, seq_len, D_MODEL), (
        hg_out, c_out, n_out, m_out[:, :, :, 0], r_out,
        jnp.stack(conv_tails))


def kernel(x_prompt, x_sample, state_hgrn, state_mlstm_C, state_mlstm_n, state_mlstm_m, state_ret,
           state_ffn_conv, norm1_w, w_in, mlstm_gate_b, branch_gate_b, hgrn_lb, hgrn_norm_w,
           mlstm_norm_w, ret_norm_w, w_branch, w_out, norm2_w, ffn_w_gate, ffn_w_up, ffn_conv_w,
           ffn_conv_b, ffn_w_down, final_norm_w):
    params = (norm1_w, mlstm_gate_b, branch_gate_b, hgrn_lb, hgrn_norm_w, mlstm_norm_w, ret_norm_w,
              norm2_w, ffn_conv_w, ffn_conv_b, final_norm_w)
    prepped = _prep_weights(w_in, w_branch, w_out, ffn_w_gate, ffn_w_up, ffn_w_down)
    y_p, (hg_p, c_p, n_p, m_p, r_p, cv_p) = _trunk(x_prompt, 0, None, params, prepped)
    y_s, (hg_s, c_s, n_s, m_s, r_s, cv_s) = _trunk(
        x_sample, PAST_LEN,
        (state_hgrn, state_mlstm_C, state_mlstm_n, state_mlstm_m, state_ret, state_ffn_conv),
        params, prepped)
    return (y_p, y_s, hg_p, hg_s, c_p, c_s, n_p, n_s, m_p, m_s, r_p, r_s, cv_p, cv_s)
```

```python
import functools
import math

import numpy as np
import jax
import jax.numpy as jnp
from jax import lax
from jax.experimental import pallas as pl
from jax.experimental.pallas import tpu as pltpu

F32 = jnp.float32
BF16 = jnp.bfloat16

D_MODEL = 1024
MIX_W = 512
N_HEAD = 4
D_HEAD = 128
N_BRANCH = 3
D_FF = 2816
CONV_W = 3
CHUNK = 64
EPS = 1e-6
ROPE_BASE = 10000.0
PAST_LEN = 16384

HG_OFF = 0
ML_OFF = 4 * MIX_W
ML_GATE_OFF = ML_OFF + 4 * MIX_W
RET_OFF = ML_GATE_OFF + 2 * N_HEAD
GATE_OFF = RET_OFF + 4 * MIX_W
LOG2E = 1.4426950408889634

LANES = 128
SUBLANES = 8
VMEM_LIMIT = 56 * 1024 * 1024

TOKEN_TILE = 512
NORM_TOKEN_TILE = 1024
REGROUP_ROW_TILE = 256
FF_TILE = 2816
HG_SUB = 16
GROUP = 8
SEQ_BLOCK = 64
SCAN_UNROLL = 8


def _resident(shape):
    return pl.BlockSpec(shape, lambda *_: (0,) * len(shape), pipeline_mode=pl.Buffered(1))


def _dot(a, b):
    return jnp.dot(a, b, preferred_element_type=F32)


def _dot_tn(a, b):
    return lax.dot_general(a, b, (((0,), (0,)), ((), ())), preferred_element_type=F32)


def _bdot(a, b):
    return jnp.einsum('nlk,nkv->nlv', a, b, preferred_element_type=F32)


def _bdot_nt(a, b):
    return jnp.einsum('nqd,nkd->nqk', a, b, preferred_element_type=F32)


def _bdot_tn(a, b):
    return jnp.stack([_dot_tn(a[i], b[i]) for i in range(a.shape[0])])


def _split3(x):
    hi = x.astype(BF16)
    r = x - hi.astype(F32)
    mid = r.astype(BF16)
    lo = (r - mid.astype(F32)).astype(BF16)
    return hi, mid, lo


def _sigmoid(x):
    return 1.0 / (1.0 + jnp.exp(-x))


def _silu(x):
    return x * _sigmoid(x)


def _log_sigmoid(x):
    return jnp.minimum(x, 0.0) - jnp.log(1.0 + jnp.exp(-jnp.abs(x)))


def _rms(x, w):
    return x * lax.rsqrt(jnp.mean(x * x, axis=-1, keepdims=True) + EPS) * w


def _pad_rows(x, n):
    if x.shape[1] >= n:
        return x
    return jnp.concatenate([x, jnp.zeros((x.shape[0], n - x.shape[1], x.shape[2]), x.dtype)], axis=1)


def _seg_cumsum(x, seg):
    shape = x.shape
    flat = x.reshape(shape[0] * seg, shape[2])
    row = lax.broadcasted_iota(jnp.int32, flat.shape, 0) & (seg - 1)
    s = 1
    while s < seg:
        flat = flat + jnp.where(row >= s, pltpu.roll(flat, s, axis=0), 0.0)
        s *= 2
    return flat.reshape(shape)


def _col_bcast(rows):
    n = rows.shape[0]
    r = lax.broadcasted_iota(jnp.int32, (n, D_HEAD, D_HEAD), 1)
    c = lax.broadcasted_iota(jnp.int32, (n, D_HEAD, D_HEAD), 2)
    d = jnp.where(r == c, jnp.broadcast_to(rows, (n, D_HEAD, D_HEAD)), 0.0).reshape(n * D_HEAD, D_HEAD)
    ones = jnp.ones((D_HEAD, D_HEAD), BF16)
    hi, mid, lo = _split3(d)
    return (_dot(hi, ones) + _dot(mid, ones) + _dot(lo, ones)).reshape(n, D_HEAD, D_HEAD)


def _through_vmem(x):
    def body(ref):
        ref[...] = x
        return ref[...]
    return pl.run_scoped(body, pltpu.VMEM(x.shape, x.dtype))


def _row_bcast(col, width):
    n, rows, _ = col.shape
    vals = _through_vmem(_pad_rows(col, width))
    lane = lax.broadcasted_iota(jnp.int32, (n, rows, LANES), 2)
    pick = _through_vmem(jnp.where(lane == 0, 1.0, 0.0).astype(BF16))
    hi, mid, lo = _split3(vals)
    return _bdot_nt(pick, hi) + _bdot_nt(pick, mid) + _bdot_nt(pick, lo)


def _chunk_geometry(seq_len):
    if seq_len % CHUNK == 0:
        return CHUNK, CHUNK, CHUNK
    valid = math.gcd(seq_len, CHUNK)
    assert valid == seq_len and valid <= SUBLANES, "unsupported sequence length"
    return SUBLANES, valid, LANES


def _rope_kernel(inv_ref, cos_ref, sin_ref, *, pos0):
    shape = cos_ref.shape
    pos = lax.broadcasted_iota(jnp.int32, shape, 0).astype(F32) + pos0
    ang = pos * inv_ref[...]
    lane = lax.broadcasted_iota(jnp.int32, shape, 1)
    cos_ref[...] = jnp.cos(ang)
    sin_ref[...] = jnp.where(lane < D_HEAD // 2, -jnp.sin(ang), jnp.sin(ang))


def _rope_tables(rows, pos0):
    inv = ROPE_BASE ** (-jnp.linspace(0.0, 1.0, D_HEAD // 2, dtype=F32))
    inv2 = jnp.concatenate([inv, inv])[None, :]
    return pl.pallas_call(
        functools.partial(_rope_kernel, pos0=float(pos0)),
        out_shape=(jax.ShapeDtypeStruct((rows, D_HEAD), F32),) * 2,
        name="rope_tables",
    )(inv2)


def _norm_kernel(x_ref, w_ref, h_ref):
    h_ref[...] = _rms(x_ref[...], w_ref[...]).astype(BF16)


def _norm(x, w):
    n = x.shape[0]
    tm = min(NORM_TOKEN_TILE, n)
    assert n % tm == 0
    return pl.pallas_call(
        _norm_kernel,
        grid=(n // tm,),
        in_specs=[pl.BlockSpec((tm, D_MODEL), lambda i: (i, 0)), pl.BlockSpec((1, D_MODEL), lambda i: (0, 0))],
        out_specs=pl.BlockSpec((tm, D_MODEL), lambda i: (i, 0)),
        out_shape=jax.ShapeDtypeStruct((n, D_MODEL), BF16),
        compiler_params=pltpu.CompilerParams(dimension_semantics=("arbitrary",), vmem_limit_bytes=VMEM_LIMIT),
        name="norm_in",
    )(x, w)


class _Geo:
    def __init__(self, batch, seq_len):
        self.rows, self.valid, self.width = _chunk_geometry(seq_len)
        if self.valid == self.rows:
            self.seqs, self.chunks = 1, seq_len // self.rows
            self.group = min(GROUP, self.chunks)
        else:
            self.seqs, self.chunks = min(SEQ_BLOCK, batch), 1
            self.group = self.seqs
        assert batch % self.seqs == 0 and self.chunks % min(self.group, self.chunks) == 0
        self.batch = batch
        self.problems = self.seqs * self.chunks
        self.n_groups = self.problems // self.group
        self.group_tokens = self.group * self.valid
        self.tokens = self.problems * self.valid
        self.padded = self.valid < self.rows

    def project(self, h_ref, w, g):
        return _dot(h_ref[pl.ds(g * self.group_tokens, self.group_tokens), :], w)

    def split(self, z, relay):
        n_parts = z.shape[1] // D_HEAD
        cols = lambda p: slice(p * D_HEAD, (p + 1) * D_HEAD)
        if not self.padded:
            return [z[:, cols(p)].reshape(self.group, self.rows, D_HEAD) for p in range(n_parts)]
        out = []
        for p in range(n_parts):
            flat_ref, pad_ref = relay[2 * p], relay[2 * p + 1]
            flat_ref[...] = z[:, cols(p)]
            pad_ref[...] = jnp.zeros(pad_ref.shape, F32)
            for t in range(self.valid):
                pad_ref[pl.ds(t, self.seqs, stride=self.rows), :] = flat_ref[
                    pl.ds(t, self.seqs, stride=self.valid), :]
            out.append(pad_ref[...].reshape(self.seqs, self.rows, D_HEAD))
        return out

    def store(self, y_ref, g, y, relay):
        if not self.padded:
            n = self.group * self.rows
            y_ref[pl.ds(g * n, n), :] = y.reshape(n, D_HEAD).astype(y_ref.dtype)
            return
        pad = relay[1]
        pad[...] = y.reshape(self.seqs * self.rows, D_HEAD)
        for t in range(self.valid):
            y_ref[pl.ds(t, self.seqs, stride=self.valid), :] = pad[pl.ds(t, self.seqs, stride=self.rows), :].astype(
                y_ref.dtype)

    def rows_of(self, g):
        return pl.ds(g * self.group, self.group)

    def row_mask(self, shape):
        if not self.padded:
            return None
        return lax.broadcasted_iota(jnp.int32, shape, 1) < self.valid

    def for_groups(self, body):
        if self.n_groups == 1:
            body(0)
        else:
            lax.fori_loop(0, self.n_groups, lambda g, c: (body(g), c)[1], 0)

    def for_groups_static(self, body):
        for g in range(self.n_groups):
            body(g)

    def scan_chunks(self, step, init):
        if self.chunks == 1:
            return step(pl.ds(0, self.seqs), init)
        return lax.fori_loop(0, self.chunks, lambda c, carry: step(pl.ds(c * self.seqs, self.seqs), carry), init,
                             unroll=min(SCAN_UNROLL, self.chunks))


def _join_columns(w_refs):
    return jnp.concatenate([r[...] for r in w_refs], axis=1)


def _transpose_minor(x):
    return jnp.stack([x[i].T for i in range(x.shape[0])])


def _state_views(refs, kinds, layer, layered):
    head = pl.program_id(1)
    views = []
    for ref, kind in zip(refs, kinds):
        for l in (range(ref.shape[0]) if layered else [layer]):
            view = ref.at[l] if layered else ref
            if kind == 'vec':
                view = view.at[:, pl.ds(head, 1), :]
            if l == layer:
                views.append(view)
            else:
                view[...] = jnp.zeros(view.shape, F32)
    return views


def _split_refs(refs, counts):
    out, k = [], 0
    for c in counts:
        out.append(refs[k:k + c])
        k += c
    return out


def _hgrn_lower_bound(lb_ref, layer):
    x = lb_ref[...]
    e = jnp.exp(x - jnp.max(x, axis=0, keepdims=True))
    soft = e / jnp.sum(e, axis=0, keepdims=True)
    cum = soft[0:1]
    for j in range(1, layer + 1):
        cum = cum + soft[j:j + 1]
    return cum - soft[0:1]


def _hgrn_placement(geo):
    sub = min(HG_SUB, geo.rows)
    r = np.arange(sub * D_HEAD)[:, None] // D_HEAD
    c = np.arange(geo.width)[None, :]
    return jnp.asarray(np.stack([r + i * sub == c for i in range(geo.rows // sub)]), BF16)


def _hgrn_intra(q, k, b2, place_ref, geo):
    n = q.shape[0]
    sub = min(HG_SUB, geo.rows)
    sub_row = lax.broadcasted_iota(jnp.int32, (n, sub, geo.width), 1)
    lane = lax.broadcasted_iota(jnp.int32, (n, sub, geo.width), 2)
    blocks = []
    for i in range(geo.rows // sub):
        r0 = i * sub
        b_i, q_i, k_i = (a[:, r0:r0 + sub] for a in (b2, q, k))
        terms = []
        for s in range(sub):
            lo = s // SUBLANES * SUBLANES
            e = jnp.exp2(jnp.minimum(b_i[:, lo:] - b_i[:, s:s + 1], 0.0))
            term = q_i[:, lo:] * e * k_i[:, s:s + 1]
            if lo:
                term = jnp.concatenate([jnp.zeros((n, lo, D_HEAD), F32), term], axis=1)
            terms.append(term)
        cat = jnp.concatenate(terms, axis=2).reshape(n * sub, sub * D_HEAD)
        a_i = _dot(cat.astype(BF16), place_ref[i]).reshape(n, sub, geo.width)
        a_i = jnp.where(sub_row >= lane - r0, a_i, 0.0)
        if i > 0:
            b_ref = b_i[:, 0:1]
            q_dec = q_i * jnp.exp2(b_i - b_ref)
            k_dec = _pad_rows(k[:, 0:r0] * jnp.exp2(b_ref - b2[:, 0:r0]), geo.width)
            a_i = jnp.where(lane < r0, _bdot_nt(q_dec.astype(BF16), k_dec.astype(BF16)), a_i)
        blocks.append(a_i)
    return blocks[0] if len(blocks) == 1 else jnp.concatenate(blocks, axis=1)


def _hgrn_kernel(*refs, layer, geo, has_state, fresh):
    ((h_ref,), w_refs, (lb_ref, nw_ref, place), s0, _, (y_ref,), s_out,
     (qt_scr, o_scr, u_scr, dc_scr, gate_scr), relay) = _split_refs(
        refs, (1, 4, 3, int(has_state), int(not fresh), 1, 1, 5, len(refs)))
    (s_ref,) = _state_views(s_out, ['mat'], layer, fresh)
    lb = _hgrn_lower_bound(lb_ref, layer)
    rows, width = geo.rows, geo.width
    w = _join_columns(w_refs)

    def phase_a(g):
        hq, hf, hi, hg = geo.split(geo.project(h_ref, w, g), relay)
        q = _silu(hq)
        f = lb + (1.0 - lb) * _sigmoid(hf)
        lf = jnp.log(f)
        k = 1.0 - f
        mask = geo.row_mask(lf.shape)
        if mask is not None:
            lf = jnp.where(mask, lf, 0.0)
            k = jnp.where(mask, k, 0.0)
        b2 = _seg_cumsum(lf, rows) * LOG2E
        sl = geo.rows_of(g)
        gate_scr[sl] = hg
        qt_scr[sl] = (q * jnp.exp2(b2)).astype(BF16)
        a = _hgrn_intra(q, k, b2, place, geo)
        vb = _pad_rows(hi, width).astype(BF16)
        o_scr[sl] = _bdot(a.astype(BF16), vb)
        b_last = b2[:, rows - 1:rows]
        k_w = k * jnp.exp2(b_last - b2)
        u_scr[sl] = _bdot_tn(vb, _pad_rows(k_w, width).astype(BF16))
        dc_scr[sl] = jnp.exp2(b_last)

    geo.for_groups_static(phase_a)

    def step(sl, state_t):
        o_scr[sl] = o_scr[sl] + _bdot_nt(qt_scr[sl], state_t.astype(BF16))
        return dc_scr[sl] * state_t + u_scr[sl]

    init = _transpose_minor(s0[0][...]) if has_state else jnp.zeros((geo.seqs, D_HEAD, D_HEAD), F32)
    s_ref[...] = _transpose_minor(geo.scan_chunks(step, init))

    def epilogue(g):
        sl = geo.rows_of(g)
        geo.store(y_ref, g, _rms(o_scr[sl], nw_ref[...]) * _silu(gate_scr[sl]), relay)

    geo.for_groups(epilogue)


def _mlstm_kernel(*refs, layer, geo, has_state, fresh):
    ((h_ref,), w_refs, (gb_ref, nw_ref), st0, _, (y_ref,), st_out,
     (q_scr, v_scr, u_scr, nu_scr, bc_scr, ml_scr, rs_scr, bl_scr, mu_scr, gate_scr), relay) = _split_refs(
        refs, (1, 5, 2, 3 * int(has_state), 3 * int(not fresh), 1, 3, 10, len(refs)))
    kinds = ['mat', 'vec', 'vec']
    c_ref, n_ref, m_ref = _state_views(st_out, kinds, layer, fresh)
    st0 = _state_views(st0, kinds, layer, False) if has_state else st0
    rows, width = geo.rows, geo.width
    head = pl.program_id(1)
    scale = D_HEAD ** -0.5
    t_i = lax.broadcasted_iota(jnp.int32, (1, rows, width), 1)
    s_i = lax.broadcasted_iota(jnp.int32, (1, rows, width), 2)
    ok = s_i <= t_i
    if geo.padded:
        ok = ok & (s_i < geo.valid)
    w = _join_columns(w_refs)

    def phase_a(g):
        q, k, v, og, gates = geo.split(geo.project(h_ref, w, g), relay)
        gates = gates + gb_ref[...]
        lf_all = _log_sigmoid(gates)
        mask = geo.row_mask(gates.shape)
        if mask is not None:
            lf_all = jnp.where(mask, lf_all, 0.0)
        bc_all = _seg_cumsum(lf_all, rows)
        lane = lax.broadcasted_iota(jnp.int32, gates.shape, 2)
        wide = lambda x: jnp.broadcast_to(x, gates.shape)
        ig = wide(jnp.sum(jnp.where(lane == head, gates, 0.0), axis=2, keepdims=True))
        bc_col = jnp.sum(jnp.where(lane == head + N_HEAD, bc_all, 0.0), axis=2, keepdims=True)
        bc = wide(bc_col)
        log_d = jnp.where(ok, bc_col + _row_bcast(ig - bc, width), -jnp.inf)
        m_col = jnp.max(log_d, axis=2, keepdims=True)
        m_loc = wide(m_col)
        k = k * scale
        qb = q.astype(BF16)
        vb = _pad_rows(v, width).astype(BF16)
        s_m = _bdot_nt(qb, _pad_rows(k, width).astype(BF16)) * jnp.exp(log_d - m_col)
        sl = geo.rows_of(g)
        gate_scr[sl] = og
        q_scr[sl] = qb
        v_scr[sl] = _bdot(s_m.astype(BF16), vb)
        rs_scr[sl] = wide(jnp.sum(s_m, axis=2, keepdims=True))
        bc_scr[sl] = bc
        ml_scr[sl] = m_loc
        b_last = bc[:, rows - 1:rows]
        m_upd = m_loc[:, rows - 1:rows]
        w_exp = b_last - bc + ig - m_upd
        if mask is not None:
            w_exp = jnp.where(mask, w_exp, -jnp.inf)
        k_w = k * jnp.exp(w_exp)
        u_scr[sl] = _bdot_tn(_pad_rows(k_w, width).astype(BF16), vb)
        nu_scr[sl] = jnp.sum(k_w, axis=1, keepdims=True)
        bl_scr[sl] = b_last
        mu_scr[sl] = m_upd

    geo.for_groups_static(phase_a)

    def step(sl, c_state):
        n_state, m_prev = n_ref[...], m_ref[...]
        inter_log = bc_scr[sl] + m_prev
        m_t = jnp.maximum(inter_log, ml_scr[sl])
        w_inter = jnp.exp(inter_log - m_t)
        w_loc = jnp.exp(ml_scr[sl] - m_t)
        qb = q_scr[sl]
        num = w_inter * _bdot(qb, c_state.astype(BF16)) + w_loc * v_scr[sl]
        q_n = jnp.broadcast_to(jnp.sum(qb.astype(F32) * n_state, axis=2, keepdims=True), m_t.shape)
        nq = w_inter * q_n + w_loc * rs_scr[sl]
        v_scr[sl] = num / jnp.maximum(jnp.abs(nq), jnp.exp(-m_t))
        m_new = m_t[:, rows - 1:rows]
        w_prev = jnp.exp(bl_scr[sl] + m_prev - m_new)
        w_upd = jnp.exp(mu_scr[sl] - m_new)
        n_ref[...] = w_prev * n_state + w_upd * nu_scr[sl]
        m_ref[...] = m_new
        return w_prev * c_state + w_upd * u_scr[sl]

    if has_state:
        c_init = st0[0][...]
        n_ref[...] = st0[1][...]
        m_ref[...] = st0[2][...]
    else:
        c_init = jnp.zeros((geo.seqs, D_HEAD, D_HEAD), F32)
        n_ref[...] = jnp.zeros(n_ref.shape, F32)
        m_ref[...] = jnp.zeros(m_ref.shape, F32)
    c_ref[...] = geo.scan_chunks(step, c_init)

    def epilogue(g):
        sl = geo.rows_of(g)
        geo.store(y_ref, g, _rms(_sigmoid(gate_scr[sl]) * v_scr[sl], nw_ref[...]), relay)

    geo.for_groups(epilogue)


def _ret_kernel(*refs, layer, geo, has_state, fresh):
    ((h_ref,), w_refs, (cos_ref, sin_ref, nw_ref), r0, _, (y_ref,), r_out,
     (q_scr, o_scr, u_scr, gate_scr), relay) = _split_refs(
        refs, (1, 4, 3, int(has_state), int(not fresh), 1, 1, 4, len(refs)))
    (r_ref,) = _state_views(r_out, ['mat'], layer, fresh)
    rows, width, valid = geo.rows, geo.width, geo.valid
    head = pl.program_id(1)
    log_gamma = jnp.zeros((1, 1, 1), F32)
    for h in range(N_HEAD):
        log_gamma = jnp.where(head == h, math.log(1.0 - 2.0 ** (-5.0 - h)), log_gamma)
    scale = D_HEAD ** -0.5
    t_i = lax.broadcasted_iota(jnp.int32, (1, rows, width), 1)
    s_i = lax.broadcasted_iota(jnp.int32, (1, rows, width), 2)
    decay = jnp.where(s_i <= t_i, jnp.exp((t_i - s_i).astype(F32) * log_gamma), 0.0)
    t_col = lax.broadcasted_iota(jnp.int32, (1, rows, D_HEAD), 1).astype(F32)
    w_in = jnp.exp((t_col + 1.0) * log_gamma)
    w_st = jnp.exp((valid - 1.0 - t_col) * log_gamma)
    g_all = jnp.exp(valid * log_gamma)
    w = _join_columns(w_refs)

    def rotate(x, cos, sin):
        flat = x.reshape(x.shape[0] * rows, D_HEAD)
        return x * cos + pltpu.roll(flat, D_HEAD // 2, axis=1).reshape(x.shape) * sin

    def phase_a(g):
        q, k, v, rg = geo.split(geo.project(h_ref, w, g), relay)
        if geo.chunks > 1:
            cos, sin = cos_ref[geo.rows_of(g)], sin_ref[geo.rows_of(g)]
        else:
            cos, sin = cos_ref[...], sin_ref[...]
        qr = rotate(q, cos, sin).astype(BF16)
        kr = rotate(k, cos, sin) * scale
        vb = _pad_rows(v, width).astype(BF16)
        a = _bdot_nt(qr, _pad_rows(kr, width).astype(BF16)) * decay
        sl = geo.rows_of(g)
        gate_scr[sl] = rg
        q_scr[sl] = qr
        o_scr[sl] = _bdot(a.astype(BF16), vb)
        u_scr[sl] = _bdot_tn(_pad_rows(kr * w_st, width).astype(BF16), vb)

    geo.for_groups_static(phase_a)

    def step(sl, state):
        o_scr[sl] = o_scr[sl] + _bdot(q_scr[sl], state.astype(BF16)) * w_in
        return g_all * state + u_scr[sl]

    init = r0[0][...] if has_state else jnp.zeros((geo.seqs, D_HEAD, D_HEAD), F32)
    r_ref[...] = geo.scan_chunks(step, init)

    def epilogue(g):
        sl = geo.rows_of(g)
        geo.store(y_ref, g, _rms(o_scr[sl], nw_ref[...]) * _silu(gate_scr[sl]), relay)

    geo.for_groups(epilogue)


def _mixer_call(kernel, name, geo, layer, h, weights, extra, states_in, states_prev, scratch):
    seqs, n_cols = geo.seqs, len(weights) * D_HEAD
    fresh = isinstance(states_prev[0], jax.ShapeDtypeStruct)

    def weight_spec(first, per_head):
        return pl.BlockSpec((None, D_MODEL, D_HEAD), lambda bi, j: (layer, 0, first + per_head * j))

    def state_spec(arr, layered=False):
        lead, first = (arr.shape[0], 0) if layered else (None, layer)
        if len(arr.shape) == 5:
            return pl.BlockSpec((lead, seqs, None, D_HEAD, D_HEAD), lambda bi, j: (first, bi, j, 0, 0))
        return pl.BlockSpec((lead, seqs, N_HEAD, D_HEAD), lambda bi, j: (first, bi, 0, 0))

    has_state = states_in is not None
    args = [h] + [a for a, _, _ in weights] + [a for a, _ in extra]
    in_specs = [pl.BlockSpec((geo.tokens, D_MODEL), lambda bi, j: (bi, 0))] + [
        weight_spec(first, per_head) for _, first, per_head in weights] + [s for _, s in extra]
    if has_state:
        args += list(states_in)
        in_specs += [state_spec(a) for a in states_in]
    n_in = len(args)
    if not fresh:
        args += list(states_prev)
        in_specs += [pl.BlockSpec(memory_space=pl.ANY) for _ in states_prev]
    out_specs = [pl.BlockSpec((geo.tokens, D_HEAD), lambda bi, j: (bi, j))] + [
        state_spec(a, fresh) for a in states_prev]
    out_shape = [jax.ShapeDtypeStruct((h.shape[0], MIX_W), F32 if geo.padded else BF16)] + [
        jax.ShapeDtypeStruct(a.shape, F32) for a in states_prev]
    scratch_shapes = list(scratch) + [pltpu.VMEM((geo.problems, geo.rows, D_HEAD), F32)]
    if geo.padded:
        scratch_shapes += [pltpu.VMEM((geo.group_tokens, D_HEAD), F32),
                           pltpu.VMEM((seqs * geo.rows, D_HEAD), F32)] * (n_cols // D_HEAD)
    return pl.pallas_call(
        functools.partial(kernel, layer=layer, geo=geo, has_state=has_state, fresh=fresh),
        grid=(geo.batch // seqs, N_HEAD),
        in_specs=in_specs,
        out_specs=out_specs,
        out_shape=out_shape,
        scratch_shapes=scratch_shapes,
        input_output_aliases={} if fresh else {n_in + i: 1 + i for i in range(len(states_prev))},
        compiler_params=pltpu.CompilerParams(
            dimension_semantics=("arbitrary", "arbitrary"), vmem_limit_bytes=VMEM_LIMIT),
        name=name,
    )(*args)


def _head_block(rows):
    return pl.BlockSpec((rows, D_HEAD), lambda bi, j: (0, j))


def _part_blocks(w_main, mixer):
    return [(w_main, (mixer * 4 + p) * N_HEAD, 1) for p in range(4)]


def _hgrn(geo, layer, h, w_main, hgrn_lb, nw, state_in, state_prev):
    n = geo.problems
    scratch = [pltpu.VMEM((n, geo.rows, D_HEAD), BF16), pltpu.VMEM((n, geo.rows, D_HEAD), F32),
               pltpu.VMEM((n, D_HEAD, D_HEAD), F32), pltpu.VMEM((n, 1, D_HEAD), F32)]
    place = _hgrn_placement(geo)
    return _mixer_call(
        _hgrn_kernel, "hgrn2", geo, layer, h, _part_blocks(w_main, 0),
        [(hgrn_lb, _head_block(hgrn_lb.shape[0])), (nw, _head_block(1)), (place, _resident(place.shape))],
        None if state_in is None else [state_in], [state_prev], scratch)


def _mlstm(geo, layer, h, w_main, w_gate, gate_b, nw, states_in, states_prev):
    n = geo.problems
    col = lambda r: pltpu.VMEM((n, r, LANES), F32)
    scratch = [pltpu.VMEM((n, geo.rows, D_HEAD), BF16), pltpu.VMEM((n, geo.rows, D_HEAD), F32),
               pltpu.VMEM((n, D_HEAD, D_HEAD), F32), pltpu.VMEM((n, 1, D_HEAD), F32),
               col(geo.rows), col(geo.rows), col(geo.rows), col(1), col(1)]
    return _mixer_call(
        _mlstm_kernel, "mlstm", geo, layer, h, _part_blocks(w_main, 1) + [(w_gate, 0, 0)],
        [(gate_b, pl.BlockSpec((1, LANES), lambda bi, j: (0, 0))), (nw, _head_block(1))],
        states_in, states_prev, scratch)


def _retention(geo, layer, h, w_main, cos, sin, nw, state_in, state_prev):
    n = geo.problems
    scratch = [pltpu.VMEM((n, geo.rows, D_HEAD), BF16), pltpu.VMEM((n, geo.rows, D_HEAD), F32),
               pltpu.VMEM((n, D_HEAD, D_HEAD), F32)]
    table = pl.BlockSpec(cos.shape, lambda bi, j: (0, 0, 0))
    return _mixer_call(
        _ret_kernel, "retention", geo, layer, h, _part_blocks(w_main, 2),
        [(cos, table), (sin, table), (nw, _head_block(1))],
        None if state_in is None else [state_in], [state_prev], scratch)


def _merge_kernel(x_ref, h_ref, yh_ref, ym_ref, yr_ref, wg_ref, bg_ref, wb_ref, wo_ref, n2_ref, x1_ref, h2_ref):
    hb = h_ref[...]
    merged = None
    for j, y_ref in enumerate((yh_ref, ym_ref, yr_ref)):
        cols = slice(j * D_MODEL, (j + 1) * D_MODEL)
        gate = _sigmoid(_dot(hb, wg_ref[:, cols]) + bg_ref[:, cols])
        term = gate * _dot(y_ref[...].astype(BF16), wb_ref[j])
        merged = term if merged is None else merged + term
    x1 = x_ref[...] + _dot(merged.astype(BF16), wo_ref[...])
    x1_ref[...] = x1
    h2_ref[...] = _rms(x1, n2_ref[...]).astype(BF16)


def _merge(x, h, yh, ym, yr, w_bgate, gate_b, w_branch, w_out, norm2_w):
    n = x.shape[0]
    tm = min(TOKEN_TILE, n)
    gate_w = N_BRANCH * D_MODEL
    row = lambda width: pl.BlockSpec((tm, width), lambda i: (i, 0))
    vec = lambda width: pl.BlockSpec((1, width), lambda i: (0, 0))
    return pl.pallas_call(
        _merge_kernel,
        grid=(n // tm,),
        in_specs=[
            row(D_MODEL), row(D_MODEL), row(MIX_W), row(MIX_W), row(MIX_W),
            _resident((D_MODEL, gate_w)), vec(gate_w),
            _resident((N_BRANCH, MIX_W, D_MODEL)),
            _resident((D_MODEL, D_MODEL)),
            vec(D_MODEL),
        ],
        out_specs=[row(D_MODEL), row(D_MODEL)],
        out_shape=[jax.ShapeDtypeStruct((n, D_MODEL), F32), jax.ShapeDtypeStruct((n, D_MODEL), BF16)],
        compiler_params=pltpu.CompilerParams(
            dimension_semantics=("arbitrary",), vmem_limit_bytes=VMEM_LIMIT),
        name="merge",
    )(x, h, yh, ym, yr, w_bgate, gate_b, w_branch, w_out, norm2_w)


def _ffn_kernel(*refs, seq_len, has_state, final_norm, tail_rows):
    refs = list(refs)
    x1_ref, h2_ref, wg_ref, wu_ref, wd_ref, cw_ref, cb_ref, nw_ref = refs[:8]
    refs = refs[8:]
    p1_ref = p2_ref = None
    if has_state:
        p1_ref, p2_ref = refs[:2]
        refs = refs[2:]
    if final_norm:
        out_ref, tail_ref, carry_scr, a_scr = refs
    else:
        out_ref, hn_ref, tail_ref, carry_scr, a_scr = refs
    tm = x1_ref.shape[0]
    i = pl.program_id(0)

    @pl.when(i == 0)
    def _():
        carry_scr[...] = jnp.zeros(carry_scr.shape, F32)

    t = (i * tm + lax.broadcasted_iota(jnp.int32, (tm, FF_TILE), 0)) & (seq_len - 1)
    h2 = h2_ref[...]
    acc = x1_ref[...]
    for j in range(D_FF // FF_TILE):
        cols = slice(j * FF_TILE, (j + 1) * FF_TILE)
        a = _dot(h2, wg_ref[:, cols])
        u = _dot(h2, wu_ref[:, cols])
        a_scr[0:SUBLANES, :] = carry_scr[:, cols]
        a_scr[SUBLANES:, :] = a
        carry_scr[:, cols] = a[tm - SUBLANES:, :]
        prev1 = jnp.where(t >= 1, a_scr[SUBLANES - 1:SUBLANES - 1 + tm, :], 0.0)
        prev2 = jnp.where(t >= 2, a_scr[SUBLANES - 2:SUBLANES - 2 + tm, :], 0.0)
        if has_state:
            prev1 = prev1 + p1_ref[:, cols]
            prev2 = prev2 + p2_ref[:, cols]
        conv = prev2 * cw_ref[0:1, cols] + prev1 * cw_ref[1:2, cols] + a * cw_ref[2:3, cols] + cb_ref[:, cols]
        acc = acc + _dot((_silu(conv) * u).astype(BF16), wd_ref[cols, :])
        if tail_rows == tm:
            tail_ref[:, cols] = a
        else:
            tail_ref[0, :, cols] = a[tm - tail_rows:, :]
    if final_norm:
        out_ref[...] = _rms(acc, nw_ref[...])
    else:
        out_ref[...] = acc
        hn_ref[...] = _rms(acc, nw_ref[...]).astype(BF16)


def _ffn(x1, h2, wg, wu, wd, conv_w, conv_b, seq_len, conv_state, norm_w, final_norm):
    n = x1.shape[0]
    has_state = conv_state is not None
    tm = min(TOKEN_TILE // 2 if has_state else TOKEN_TILE, n)
    assert n % tm == 0 and seq_len & (seq_len - 1) == 0
    batch = n // seq_len
    row = lambda width: pl.BlockSpec((tm, width), lambda i: (i, 0))
    full = lambda shape: pl.BlockSpec(shape, lambda i: (0,) * len(shape))
    in_specs = [row(D_MODEL), row(D_MODEL), _resident((D_MODEL, D_FF)), _resident((D_MODEL, D_FF)),
                _resident((D_FF, D_MODEL)), full((CONV_W, D_FF)), full((1, D_FF)), full((1, D_MODEL))]
    args = [x1, h2, wg, wu, wd, conv_w, conv_b, norm_w]
    if has_state:
        p1 = jnp.pad(conv_state[:, 1:2], ((0, 0), (0, seq_len - 1), (0, 0))).reshape(n, D_FF)
        p2 = jnp.pad(conv_state, ((0, 0), (0, seq_len - (CONV_W - 1)), (0, 0))).reshape(n, D_FF)
        in_specs += [row(D_FF), row(D_FF)]
        args += [p1, p2]
    if seq_len % tm == 0:
        tail_rows = CONV_W - 1
        tail_spec = pl.BlockSpec((1, tail_rows, D_FF), lambda i: ((i * tm) // seq_len, 0, 0))
        tail_shape = jax.ShapeDtypeStruct((batch, tail_rows, D_FF), F32)
    else:
        assert tm % seq_len == 0 and seq_len >= CONV_W - 1
        tail_rows = tm
        tail_spec = row(D_FF)
        tail_shape = jax.ShapeDtypeStruct((n, D_FF), F32)
    out_specs, out_shape = [row(D_MODEL)], [jax.ShapeDtypeStruct((n, D_MODEL), F32)]
    if not final_norm:
        out_specs.append(row(D_MODEL))
        out_shape.append(jax.ShapeDtypeStruct((n, D_MODEL), BF16))
    *outs, tail = pl.pallas_call(
        functools.partial(_ffn_kernel, seq_len=seq_len, has_state=has_state,
                          final_norm=final_norm, tail_rows=tail_rows),
        grid=(n // tm,),
        in_specs=in_specs,
        out_specs=out_specs + [tail_spec],
        out_shape=out_shape + [tail_shape],
        scratch_shapes=[pltpu.VMEM((SUBLANES, D_FF), F32), pltpu.VMEM((tm + SUBLANES, FF_TILE), F32)],
        compiler_params=pltpu.CompilerParams(
            dimension_semantics=("arbitrary",), vmem_limit_bytes=VMEM_LIMIT),
        name="conv_ffn",
    )(*args)
    if tail_rows == tm:
        tail = tail.reshape(batch, seq_len, D_FF)[:, seq_len - (CONV_W - 1):, :]
    return outs, tail


def _regroup_kernel(a_ref, b_ref, c_ref, d_ref, main_ref, gate_ref, bgate_ref):
    shift = ML_GATE_OFF % LANES + 2 * N_HEAD
    main_ref[:, 0:ML_GATE_OFF] = a_ref[...].astype(BF16)
    b, c = b_ref[...], c_ref[...]
    ret = jnp.concatenate([b, c[:, 0:LANES]], axis=1)
    ret = pltpu.roll(ret, ret.shape[1] - shift, axis=1)[:, 0:4 * MIX_W]
    main_ref[:, ML_GATE_OFF:] = ret.astype(BF16)
    lane = lax.broadcasted_iota(jnp.int32, (b.shape[0], LANES), 1)
    gate_ref[...] = jnp.where(lane < 2 * N_HEAD, b[:, 0:LANES], 0.0).astype(BF16)
    gates = jnp.concatenate([c, d_ref[...]], axis=1)
    gates = pltpu.roll(gates, gates.shape[1] - shift, axis=1)[:, 0:N_BRANCH * D_MODEL]
    bgate_ref[...] = gates.astype(BF16)


def _regroup_w_in(w_in):
    depth = w_in.shape[0]
    rows = REGROUP_ROW_TILE
    n_main, n_gates = 3 * 4 * MIX_W, N_BRANCH * D_MODEL
    assert ML_GATE_OFF % LANES == 0 and GATE_OFF - 2 * N_HEAD == n_main
    spec = lambda width, first: pl.BlockSpec((None, rows, width), lambda l, i: (l, i, first))
    return pl.pallas_call(
        _regroup_kernel,
        grid=(depth, D_MODEL // rows),
        in_specs=[spec(ML_GATE_OFF, 0), spec(4 * MIX_W, ML_GATE_OFF // (4 * MIX_W)),
                  spec(n_gates, n_main // n_gates), spec(LANES, (n_main + n_gates) // LANES)],
        out_specs=[spec(n_main, 0), spec(LANES, 0), spec(n_gates, 0)],
        out_shape=[jax.ShapeDtypeStruct((depth, D_MODEL, n_main), BF16),
                   jax.ShapeDtypeStruct((depth, D_MODEL, LANES), BF16),
                   jax.ShapeDtypeStruct((depth, D_MODEL, n_gates), BF16)],
        compiler_params=pltpu.CompilerParams(
            dimension_semantics=("arbitrary", "arbitrary"), vmem_limit_bytes=VMEM_LIMIT),
        name="regroup_w_in",
    )(w_in, w_in, w_in, w_in)


def _prep_weights(w_in, w_branch, w_out, ffn_w_gate, ffn_w_up, ffn_w_down):
    w_main, w_gate, w_bgate = _regroup_w_in(w_in)
    return (w_main, w_gate, w_bgate, w_branch.astype(BF16), w_out.astype(BF16),
            ffn_w_gate.astype(BF16), ffn_w_up.astype(BF16), ffn_w_down.astype(BF16))


def _trunk(x3, pos0, states, params, prepped):
    (norm1_w, mlstm_gate_b, branch_gate_b, hgrn_lb, hgrn_norm_w, mlstm_norm_w, ret_norm_w,
     norm2_w, ffn_conv_w, ffn_conv_b, final_norm_w) = params
    w_main, w_gate, w_bgate, w_branch, w_out, w_ffg, w_ffu, w_ffd = prepped
    batch, seq_len, _ = x3.shape
    n = batch * seq_len
    depth = norm1_w.shape[0]
    geo = _Geo(batch, seq_len)
    cos, sin = (t.reshape(geo.chunks, geo.rows, D_HEAD) for t in _rope_tables(geo.chunks * geo.rows, pos0))

    mat = jax.ShapeDtypeStruct((depth, batch, N_HEAD, D_HEAD, D_HEAD), F32)
    vec = jax.ShapeDtypeStruct((depth, batch, N_HEAD, D_HEAD), F32)
    if states is None:
        hg_in = ml_in = ret_in = conv_in = None
    else:
        hg_in, ml_c, ml_n, ml_m, ret_in, conv_in = states
        ml_in = [ml_c, ml_n, jnp.broadcast_to(ml_m[..., None], vec.shape)]
    hg_out, c_out, r_out, n_out, m_out = mat, mat, mat, vec, vec

    x = x3.reshape(n, D_MODEL)
    h = _norm(x, norm1_w[0:1])
    conv_tails = []
    for l in range(depth):
        last = l == depth - 1
        y_hg, hg_out = _hgrn(geo, l, h, w_main, hgrn_lb, hgrn_norm_w[l:l + 1], hg_in, hg_out)
        gate_b = jnp.pad(mlstm_gate_b[l:l + 1], ((0, 0), (0, LANES - 2 * N_HEAD)))
        y_ml, c_out, n_out, m_out = _mlstm(geo, l, h, w_main, w_gate, gate_b, mlstm_norm_w[l:l + 1], ml_in,
                                           [c_out, n_out, m_out])
        y_ret, r_out = _retention(geo, l, h, w_main, cos, sin, ret_norm_w[l:l + 1], ret_in, r_out)
        x1, h2 = _merge(x, h, y_hg, y_ml, y_ret, w_bgate[l], branch_gate_b[l:l + 1], w_branch[l], w_out[l],
                        norm2_w[l:l + 1])
        outs, conv_tail = _ffn(x1, h2, w_ffg[l], w_ffu[l], w_ffd[l], ffn_conv_w[l], ffn_conv_b[l:l + 1],
                               seq_len, None if conv_in is None else conv_in[l],
                               final_norm_w[None, :] if last else norm1_w[l + 1:l + 2], last)
        x = outs[0]
        h = None if last else outs[1]
        conv_tails.append(conv_tail)
    return x.reshape(batch, seq_len, D_MODEL), (
        hg_out, c_out, n_out, m_out[:, :, :, 0], r_out,
        jnp.stack(conv_tails))


def kernel(x_prompt, x_sample, state_hgrn, state_mlstm_C, state_mlstm_n, state_mlstm_m, state_ret,
           state_ffn_conv, norm1_w, w_in, mlstm_gate_b, branch_gate_b, hgrn_lb, hgrn_norm_w,
           mlstm_norm_w, ret_norm_w, w_branch, w_out, norm2_w, ffn_w_gate, ffn_w_up, ffn_conv_w,
           ffn_conv_b, ffn_w_down, final_norm_w):
    params = (norm1_w, mlstm_gate_b, branch_gate_b, hgrn_lb, hgrn_norm_w, mlstm_norm_w, ret_norm_w,
              norm2_w, ffn_conv_w, ffn_conv_b, final_norm_w)
    prepped = _prep_weights(w_in, w_branch, w_out, ffn_w_gate, ffn_w_up, ffn_w_down)
    y_p, (hg_p, c_p, n_p, m_p, r_p, cv_p) = _trunk(x_prompt, 0, None, params, prepped)
    y_s, (hg_s, c_s, n_s, m_s, r_s, cv_s) = _trunk(
        x_sample, PAST_LEN,
        (state_hgrn, state_mlstm_C, state_mlstm_n, state_mlstm_m, state_ret, state_ffn_conv),
        params, prepped)
    return (y_p, y_s, hg_p, hg_s, c_p, c_s, n_p, n_s, m_p, m_s, r_p, r_s, cv_p, cv_s)
```

```python
import functools
import math

import numpy as np
import jax
import jax.numpy as jnp
from jax import lax
from jax.experimental import pallas as pl
from jax.experimental.pallas import tpu as pltpu

F32 = jnp.float32
BF16 = jnp.bfloat16

D_MODEL = 1024
MIX_W = 512
N_HEAD = 4
D_HEAD = 128
N_BRANCH = 3
D_FF = 2816
CONV_W = 3
CHUNK = 64
EPS = 1e-6
ROPE_BASE = 10000.0
PAST_LEN = 16384

HG_OFF = 0
ML_OFF = 4 * MIX_W
ML_GATE_OFF = ML_OFF + 4 * MIX_W
RET_OFF = ML_GATE_OFF + 2 * N_HEAD
GATE_OFF = RET_OFF + 4 * MIX_W
LOG2E = 1.4426950408889634

LANES = 128
SUBLANES = 8
VMEM_LIMIT = 56 * 1024 * 1024

TOKEN_TILE = 512
NORM_TOKEN_TILE = 1024
REGROUP_ROW_TILE = 256
FF_TILE = 2816
HG_SUB = 16
GROUP = 8
SEQ_BLOCK = 64
SCAN_UNROLL = 8


def _resident(shape):
    return pl.BlockSpec(shape, lambda *_: (0,) * len(shape), pipeline_mode=pl.Buffered(1))


def _dot(a, b):
    return jnp.dot(a, b, preferred_element_type=F32)


def _dot_tn(a, b):
    return lax.dot_general(a, b, (((0,), (0,)), ((), ())), preferred_element_type=F32)


def _bdot(a, b):
    return jnp.einsum('nlk,nkv->nlv', a, b, preferred_element_type=F32)


def _bdot_nt(a, b):
    return jnp.einsum('nqd,nkd->nqk', a, b, preferred_element_type=F32)


def _bdot_tn(a, b):
    return jnp.stack([_dot_tn(a[i], b[i]) for i in range(a.shape[0])])


def _split3(x):
    hi = x.astype(BF16)
    r = x - hi.astype(F32)
    mid = r.astype(BF16)
    lo = (r - mid.astype(F32)).astype(BF16)
    return hi, mid, lo


def _sigmoid(x):
    return 1.0 / (1.0 + jnp.exp(-x))


def _silu(x):
    return x * _sigmoid(x)


def _log_sigmoid(x):
    return jnp.minimum(x, 0.0) - jnp.log(1.0 + jnp.exp(-jnp.abs(x)))


def _rms(x, w):
    return x * lax.rsqrt(jnp.mean(x * x, axis=-1, keepdims=True) + EPS) * w


def _pad_rows(x, n):
    if x.shape[1] >= n:
        return x
    return jnp.concatenate([x, jnp.zeros((x.shape[0], n - x.shape[1], x.shape[2]), x.dtype)], axis=1)


def _seg_cumsum(x, seg):
    shape = x.shape
    flat = x.reshape(shape[0] * seg, shape[2])
    row = lax.broadcasted_iota(jnp.int32, flat.shape, 0) & (seg - 1)
    s = 1
    while s < seg:
        flat = flat + jnp.where(row >= s, pltpu.roll(flat, s, axis=0), 0.0)
        s *= 2
    return flat.reshape(shape)


def _col_bcast(rows):
    n = rows.shape[0]
    r = lax.broadcasted_iota(jnp.int32, (n, D_HEAD, D_HEAD), 1)
    c = lax.broadcasted_iota(jnp.int32, (n, D_HEAD, D_HEAD), 2)
    d = jnp.where(r == c, jnp.broadcast_to(rows, (n, D_HEAD, D_HEAD)), 0.0).reshape(n * D_HEAD, D_HEAD)
    ones = jnp.ones((D_HEAD, D_HEAD), BF16)
    hi, mid, lo = _split3(d)
    return (_dot(hi, ones) + _dot(mid, ones) + _dot(lo, ones)).reshape(n, D_HEAD, D_HEAD)


def _through_vmem(x):
    def body(ref):
        ref[...] = x
        return ref[...]
    return pl.run_scoped(body, pltpu.VMEM(x.shape, x.dtype))


def _row_bcast(col, width):
    n, rows, _ = col.shape
    vals = _through_vmem(_pad_rows(col, width))
    lane = lax.broadcasted_iota(jnp.int32, (n, rows, LANES), 2)
    pick = _through_vmem(jnp.where(lane == 0, 1.0, 0.0).astype(BF16))
    hi, mid, lo = _split3(vals)
    return _bdot_nt(pick, hi) + _bdot_nt(pick, mid) + _bdot_nt(pick, lo)


def _chunk_geometry(seq_len):
    if seq_len % CHUNK == 0:
        return CHUNK, CHUNK, CHUNK
    valid = math.gcd(seq_len, CHUNK)
    assert valid == seq_len and valid <= SUBLANES, "unsupported sequence length"
    return SUBLANES, valid, LANES


def _rope_kernel(inv_ref, cos_ref, sin_ref, *, pos0):
    shape = cos_ref.shape
    pos = lax.broadcasted_iota(jnp.int32, shape, 0).astype(F32) + pos0
    ang = pos * inv_ref[...]
    lane = lax.broadcasted_iota(jnp.int32, shape, 1)
    cos_ref[...] = jnp.cos(ang)
    sin_ref[...] = jnp.where(lane < D_HEAD // 2, -jnp.sin(ang), jnp.sin(ang))


def _rope_tables(rows, pos0):
    inv = ROPE_BASE ** (-jnp.linspace(0.0, 1.0, D_HEAD // 2, dtype=F32))
    inv2 = jnp.concatenate([inv, inv])[None, :]
    return pl.pallas_call(
        functools.partial(_rope_kernel, pos0=float(pos0)),
        out_shape=(jax.ShapeDtypeStruct((rows, D_HEAD), F32),) * 2,
        name="rope_tables",
    )(inv2)


def _norm_kernel(x_ref, w_ref, h_ref):
    h_ref[...] = _rms(x_ref[...], w_ref[...]).astype(BF16)


def _norm(x, w):
    n = x.shape[0]
    tm = min(NORM_TOKEN_TILE, n)
    assert n % tm == 0
    return pl.pallas_call(
        _norm_kernel,
        grid=(n // tm,),
        in_specs=[pl.BlockSpec((tm, D_MODEL), lambda i: (i, 0)), pl.BlockSpec((1, D_MODEL), lambda i: (0, 0))],
        out_specs=pl.BlockSpec((tm, D_MODEL), lambda i: (i, 0)),
        out_shape=jax.ShapeDtypeStruct((n, D_MODEL), BF16),
        compiler_params=pltpu.CompilerParams(dimension_semantics=("arbitrary",), vmem_limit_bytes=VMEM_LIMIT),
        name="norm_in",
    )(x, w)


class _Geo:
    def __init__(self, batch, seq_len):
        self.rows, self.valid, self.width = _chunk_geometry(seq_len)
        if self.valid == self.rows:
            self.seqs, self.chunks = 1, seq_len // self.rows
            self.group = min(GROUP, self.chunks)
        else:
            self.seqs, self.chunks = min(SEQ_BLOCK, batch), 1
            self.group = self.seqs
        assert batch % self.seqs == 0 and self.chunks % min(self.group, self.chunks) == 0
        self.batch = batch
        self.problems = self.seqs * self.chunks
        self.n_groups = self.problems // self.group
        self.group_tokens = self.group * self.valid
        self.tokens = self.problems * self.valid
        self.padded = self.valid < self.rows

    def project(self, h_ref, w, g):
        return _dot(h_ref[pl.ds(g * self.group_tokens, self.group_tokens), :], w)

    def split(self, z, relay):
        n_parts = z.shape[1] // D_HEAD
        cols = lambda p: slice(p * D_HEAD, (p + 1) * D_HEAD)
        if not self.padded:
            return [z[:, cols(p)].reshape(self.group, self.rows, D_HEAD) for p in range(n_parts)]
        out = []
        for p in range(n_parts):
            flat_ref, pad_ref = relay[2 * p], relay[2 * p + 1]
            flat_ref[...] = z[:, cols(p)]
            pad_ref[...] = jnp.zeros(pad_ref.shape, F32)
            for t in range(self.valid):
                pad_ref[pl.ds(t, self.seqs, stride=self.rows), :] = flat_ref[
                    pl.ds(t, self.seqs, stride=self.valid), :]
            out.append(pad_ref[...].reshape(self.seqs, self.rows, D_HEAD))
        return out

    def store(self, y_ref, g, y, relay):
        if not self.padded:
            n = self.group * self.rows
            y_ref[pl.ds(g * n, n), :] = y.reshape(n, D_HEAD).astype(y_ref.dtype)
            return
        pad = relay[1]
        pad[...] = y.reshape(self.seqs * self.rows, D_HEAD)
        for t in range(self.valid):
            y_ref[pl.ds(t, self.seqs, stride=self.valid), :] = pad[pl.ds(t, self.seqs, stride=self.rows), :].astype(
                y_ref.dtype)

    def rows_of(self, g):
        return pl.ds(g * self.group, self.group)

    def row_mask(self, shape):
        if not self.padded:
            return None
        return lax.broadcasted_iota(jnp.int32, shape, 1) < self.valid

    def for_groups(self, body):
        if self.n_groups == 1:
            body(0)
        else:
            lax.fori_loop(0, self.n_groups, lambda g, c: (body(g), c)[1], 0)

    def for_groups_static(self, body):
        for g in range(self.n_groups):
            body(g)

    def scan_chunks(self, step, init):
        if self.chunks == 1:
            return step(pl.ds(0, self.seqs), init)
        return lax.fori_loop(0, self.chunks, lambda c, carry: step(pl.ds(c * self.seqs, self.seqs), carry), init,
                             unroll=min(SCAN_UNROLL, self.chunks))


def _join_columns(w_refs):
    return jnp.concatenate([r[...] for r in w_refs], axis=1)


def _transpose_minor(x):
    return jnp.stack([x[i].T for i in range(x.shape[0])])


def _state_views(refs, kinds, layer, layered):
    head = pl.program_id(1)
    views = []
    for ref, kind in zip(refs, kinds):
        for l in (range(ref.shape[0]) if layered else [layer]):
            view = ref.at[l] if layered else ref
            if kind == 'vec':
                view = view.at[:, pl.ds(head, 1), :]
            if l == layer:
                views.append(view)
            else:
                view[...] = jnp.zeros(view.shape, F32)
    return views


def _split_refs(refs, counts):
    out, k = [], 0
    for c in counts:
        out.append(refs[k:k + c])
        k += c
    return out


def _hgrn_lower_bound(lb_ref, layer):
    x = lb_ref[...]
    e = jnp.exp(x - jnp.max(x, axis=0, keepdims=True))
    soft = e / jnp.sum(e, axis=0, keepdims=True)
    cum = soft[0:1]
    for j in range(1, layer + 1):
        cum = cum + soft[j:j + 1]
    return cum - soft[0:1]


def _hgrn_placement(geo):
    sub = min(HG_SUB, geo.rows)
    r = np.arange(sub * D_HEAD)[:, None] // D_HEAD
    c = np.arange(geo.width)[None, :]
    return jnp.asarray(np.stack([r + i * sub == c for i in range(geo.rows // sub)]), BF16)


def _hgrn_intra(q, k, b2, place_ref, geo):
    n = q.shape[0]
    sub = min(HG_SUB, geo.rows)
    sub_row = lax.broadcasted_iota(jnp.int32, (n, sub, geo.width), 1)
    lane = lax.broadcasted_iota(jnp.int32, (n, sub, geo.width), 2)
    blocks = []
    for i in range(geo.rows // sub):
        r0 = i * sub
        b_i, q_i, k_i = (a[:, r0:r0 + sub] for a in (b2, q, k))
        terms = []
        for s in range(sub):
            lo = s // SUBLANES * SUBLANES
            e = jnp.exp2(jnp.minimum(b_i[:, lo:] - b_i[:, s:s + 1], 0.0))
            term = q_i[:, lo:] * e * k_i[:, s:s + 1]
            if lo:
                term = jnp.concatenate([jnp.zeros((n, lo, D_HEAD), F32), term], axis=1)
            terms.append(term)
        cat = jnp.concatenate(terms, axis=2).reshape(n * sub, sub * D_HEAD)
        a_i = _dot(cat.astype(BF16), place_ref[i]).reshape(n, sub, geo.width)
        a_i = jnp.where(sub_row >= lane - r0, a_i, 0.0)
        if i > 0:
            b_ref = b_i[:, 0:1]
            q_dec = q_i * jnp.exp2(b_i - b_ref)
            k_dec = _pad_rows(k[:, 0:r0] * jnp.exp2(b_ref - b2[:, 0:r0]), geo.width)
            a_i = jnp.where(lane < r0, _bdot_nt(q_dec.astype(BF16), k_dec.astype(BF16)), a_i)
        blocks.append(a_i)
    return blocks[0] if len(blocks) == 1 else jnp.concatenate(blocks, axis=1)


def _hgrn_kernel(*refs, layer, geo, has_state, fresh):
    ((h_ref,), w_refs, (lb_ref, nw_ref, place), s0, _, (y_ref,), s_out,
     (qt_scr, o_scr, u_scr, dc_scr, gate_scr), relay) = _split_refs(
        refs, (1, 4, 3, int(has_state), int(not fresh), 1, 1, 5, len(refs)))
    (s_ref,) = _state_views(s_out, ['mat'], layer, fresh)
    lb = _hgrn_lower_bound(lb_ref, layer)
    rows, width = geo.rows, geo.width
    w = _join_columns(w_refs)

    def phase_a(g):
        hq, hf, hi, hg = geo.split(geo.project(h_ref, w, g), relay)
        q = _silu(hq)
        f = lb + (1.0 - lb) * _sigmoid(hf)
        lf = jnp.log(f)
        k = 1.0 - f
        mask = geo.row_mask(lf.shape)
        if mask is not None:
            lf = jnp.where(mask, lf, 0.0)
            k = jnp.where(mask, k, 0.0)
        b2 = _seg_cumsum(lf, rows) * LOG2E
        sl = geo.rows_of(g)
        gate_scr[sl] = hg
        qt_scr[sl] = (q * jnp.exp2(b2)).astype(BF16)
        a = _hgrn_intra(q, k, b2, place, geo)
        vb = _pad_rows(hi, width).astype(BF16)
        o_scr[sl] = _bdot(a.astype(BF16), vb)
        b_last = b2[:, rows - 1:rows]
        k_w = k * jnp.exp2(b_last - b2)
        u_scr[sl] = _bdot_tn(vb, _pad_rows(k_w, width).astype(BF16))
        dc_scr[sl] = jnp.exp2(b_last)

    geo.for_groups_static(phase_a)

    def step(sl, state_t):
        o_scr[sl] = o_scr[sl] + _bdot_nt(qt_scr[sl], state_t.astype(BF16))
        return dc_scr[sl] * state_t + u_scr[sl]

    init = _transpose_minor(s0[0][...]) if has_state else jnp.zeros((geo.seqs, D_HEAD, D_HEAD), F32)
    s_ref[...] = _transpose_minor(geo.scan_chunks(step, init))

    def epilogue(g):
        sl = geo.rows_of(g)
        geo.store(y_ref, g, _rms(o_scr[sl], nw_ref[...]) * _silu(gate_scr[sl]), relay)

    geo.for_groups(epilogue)


def _mlstm_kernel(*refs, layer, geo, has_state, fresh):
    ((h_ref,), w_refs, (gb_ref, nw_ref), st0, _, (y_ref,), st_out,
     (q_scr, v_scr, u_scr, nu_scr, bc_scr, ml_scr, rs_scr, bl_scr, mu_scr, gate_scr), relay) = _split_refs(
        refs, (1, 5, 2, 3 * int(has_state), 3 * int(not fresh), 1, 3, 10, len(refs)))
    kinds = ['mat', 'vec', 'vec']
    c_ref, n_ref, m_ref = _state_views(st_out, kinds, layer, fresh)
    st0 = _state_views(st0, kinds, layer, False) if has_state else st0
    rows, width = geo.rows, geo.width
    head = pl.program_id(1)
    scale = D_HEAD ** -0.5
    t_i = lax.broadcasted_iota(jnp.int32, (1, rows, width), 1)
    s_i = lax.broadcasted_iota(jnp.int32, (1, rows, width), 2)
    ok = s_i <= t_i
    if geo.padded:
        ok = ok & (s_i < geo.valid)
    w = _join_columns(w_refs)

    def phase_a(g):
        q, k, v, og, gates = geo.split(geo.project(h_ref, w, g), relay)
        gates = gates + gb_ref[...]
        lf_all = _log_sigmoid(gates)
        mask = geo.row_mask(gates.shape)
        if mask is not None:
            lf_all = jnp.where(mask, lf_all, 0.0)
        bc_all = _seg_cumsum(lf_all, rows)
        lane = lax.broadcasted_iota(jnp.int32, gates.shape, 2)
        wide = lambda x: jnp.broadcast_to(x, gates.shape)
        ig = wide(jnp.sum(jnp.where(lane == head, gates, 0.0), axis=2, keepdims=True))
        bc_col = jnp.sum(jnp.where(lane == head + N_HEAD, bc_all, 0.0), axis=2, keepdims=True)
        bc = wide(bc_col)
        log_d = jnp.where(ok, bc_col + _row_bcast(ig - bc, width), -jnp.inf)
        m_col = jnp.max(log_d, axis=2, keepdims=True)
        m_loc = wide(m_col)
        k = k * scale
        qb = q.astype(BF16)
        vb = _pad_rows(v, width).astype(BF16)
        s_m = _bdot_nt(qb, _pad_rows(k, width).astype(BF16)) * jnp.exp(log_d - m_col)
        sl = geo.rows_of(g)
        gate_scr[sl] = og
        q_scr[sl] = qb
        v_scr[sl] = _bdot(s_m.astype(BF16), vb)
        rs_scr[sl] = wide(jnp.sum(s_m, axis=2, keepdims=True))
        bc_scr[sl] = bc
        ml_scr[sl] = m_loc
        b_last = bc[:, rows - 1:rows]
        m_upd = m_loc[:, rows - 1:rows]
        w_exp = b_last - bc + ig - m_upd
        if mask is not None:
            w_exp = jnp.where(mask, w_exp, -jnp.inf)
        k_w = k * jnp.exp(w_exp)
        u_scr[sl] = _bdot_tn(_pad_rows(k_w, width).astype(BF16), vb)
        nu_scr[sl] = jnp.sum(k_w, axis=1, keepdims=True)
        bl_scr[sl] = b_last
        mu_scr[sl] = m_upd

    geo.for_groups_static(phase_a)

    def step(sl, c_state):
        n_state, m_prev = n_ref[...], m_ref[...]
        inter_log = bc_scr[sl] + m_prev
        m_t = jnp.maximum(inter_log, ml_scr[sl])
        w_inter = jnp.exp(inter_log - m_t)
        w_loc = jnp.exp(ml_scr[sl] - m_t)
        qb = q_scr[sl]
        num = w_inter * _bdot(qb, c_state.astype(BF16)) + w_loc * v_scr[sl]
        q_n = jnp.broadcast_to(jnp.sum(qb.astype(F32) * n_state, axis=2, keepdims=True), m_t.shape)
        nq = w_inter * q_n + w_loc * rs_scr[sl]
        v_scr[sl] = num / jnp.maximum(jnp.abs(nq), jnp.exp(-m_t))
        m_new = m_t[:, rows - 1:rows]
        w_prev = jnp.exp(bl_scr[sl] + m_prev - m_new)
        w_upd = jnp.exp(mu_scr[sl] - m_new)
        n_ref[...] = w_prev * n_state + w_upd * nu_scr[sl]
        m_ref[...] = m_new
        return w_prev * c_state + w_upd * u_scr[sl]

    if has_state:
        c_init = st0[0][...]
        n_ref[...] = st0[1][...]
        m_ref[...] = st0[2][...]
    else:
        c_init = jnp.zeros((geo.seqs, D_HEAD, D_HEAD), F32)
        n_ref[...] = jnp.zeros(n_ref.shape, F32)
        m_ref[...] = jnp.zeros(m_ref.shape, F32)
    c_ref[...] = geo.scan_chunks(step, c_init)

    def epilogue(g):
        sl = geo.rows_of(g)
        geo.store(y_ref, g, _rms(_sigmoid(gate_scr[sl]) * v_scr[sl], nw_ref[...]), relay)

    geo.for_groups(epilogue)


def _ret_kernel(*refs, layer, geo, has_state, fresh):
    ((h_ref,), w_refs, (cos_ref, sin_ref, nw_ref), r0, _, (y_ref,), r_out,
     (q_scr, o_scr, u_scr, gate_scr), relay) = _split_refs(
        refs, (1, 4, 3, int(has_state), int(not fresh), 1, 1, 4, len(refs)))
    (r_ref,) = _state_views(r_out, ['mat'], layer, fresh)
    rows, width, valid = geo.rows, geo.width, geo.valid
    head = pl.program_id(1)
    log_gamma = jnp.zeros((1, 1, 1), F32)
    for h in range(N_HEAD):
        log_gamma = jnp.where(head == h, math.log(1.0 - 2.0 ** (-5.0 - h)), log_gamma)
    scale = D_HEAD ** -0.5
    t_i = lax.broadcasted_iota(jnp.int32, (1, rows, width), 1)
    s_i = lax.broadcasted_iota(jnp.int32, (1, rows, width), 2)
    decay = jnp.where(s_i <= t_i, jnp.exp((t_i - s_i).astype(F32) * log_gamma), 0.0)
    t_col = lax.broadcasted_iota(jnp.int32, (1, rows, D_HEAD), 1).astype(F32)
    w_in = jnp.exp((t_col + 1.0) * log_gamma)
    w_st = jnp.exp((valid - 1.0 - t_col) * log_gamma)
    g_all = jnp.exp(valid * log_gamma)
    w = _join_columns(w_refs)

    def rotate(x, cos, sin):
        flat = x.reshape(x.shape[0] * rows, D_HEAD)
        return x * cos + pltpu.roll(flat, D_HEAD // 2, axis=1).reshape(x.shape) * sin

    def phase_a(g):
        q, k, v, rg = geo.split(geo.project(h_ref, w, g), relay)
        if geo.chunks > 1:
            cos, sin = cos_ref[geo.rows_of(g)], sin_ref[geo.rows_of(g)]
        else:
            cos, sin = cos_ref[...], sin_ref[...]
        qr = rotate(q, cos, sin).astype(BF16)
        kr = rotate(k, cos, sin) * scale
        vb = _pad_rows(v, width).astype(BF16)
        a = _bdot_nt(qr, _pad_rows(kr, width).astype(BF16)) * decay
        sl = geo.rows_of(g)
        gate_scr[sl] = rg
        q_scr[sl] = qr
        o_scr[sl] = _bdot(a.astype(BF16), vb)
        u_scr[sl] = _bdot_tn(_pad_rows(kr * w_st, width).astype(BF16), vb)

    geo.for_groups_static(phase_a)

    def step(sl, state):
        o_scr[sl] = o_scr[sl] + _bdot(q_scr[sl], state.astype(BF16)) * w_in
        return g_all * state + u_scr[sl]

    init = r0[0][...] if has_state else jnp.zeros((geo.seqs, D_HEAD, D_HEAD), F32)
    r_ref[...] = geo.scan_chunks(step, init)

    def epilogue(g):
        sl = geo.rows_of(g)
        geo.store(y_ref, g, _rms(o_scr[sl], nw_ref[...]) * _silu(gate_scr[sl]), relay)

    geo.for_groups(epilogue)


def _mixer_call(kernel, name, geo, layer, h, weights, extra, states_in, states_prev, scratch):
    seqs, n_cols = geo.seqs, len(weights) * D_HEAD
    fresh = isinstance(states_prev[0], jax.ShapeDtypeStruct)

    def weight_spec(first, per_head):
        return pl.BlockSpec((None, D_MODEL, D_HEAD), lambda bi, j: (layer, 0, first + per_head * j))

    def state_spec(arr, layered=False):
        lead, first = (arr.shape[0], 0) if layered else (None, layer)
        if len(arr.shape) == 5:
            return pl.BlockSpec((lead, seqs, None, D_HEAD, D_HEAD), lambda bi, j: (first, bi, j, 0, 0))
        return pl.BlockSpec((lead, seqs, N_HEAD, D_HEAD), lambda bi, j: (first, bi, 0, 0))

    has_state = states_in is not None
    args = [h] + [a for a, _, _ in weights] + [a for a, _ in extra]
    in_specs = [pl.BlockSpec((geo.tokens, D_MODEL), lambda bi, j: (bi, 0))] + [
        weight_spec(first, per_head) for _, first, per_head in weights] + [s for _, s in extra]
    if has_state:
        args += list(states_in)
        in_specs += [state_spec(a) for a in states_in]
    n_in = len(args)
    if not fresh:
        args += list(states_prev)
        in_specs += [pl.BlockSpec(memory_space=pl.ANY) for _ in states_prev]
    out_specs = [pl.BlockSpec((geo.tokens, D_HEAD), lambda bi, j: (bi, j))] + [
        state_spec(a, fresh) for a in states_prev]
    out_shape = [jax.ShapeDtypeStruct((h.shape[0], MIX_W), F32 if geo.padded else BF16)] + [
        jax.ShapeDtypeStruct(a.shape, F32) for a in states_prev]
    scratch_shapes = list(scratch) + [pltpu.VMEM((geo.problems, geo.rows, D_HEAD), F32)]
    if geo.padded:
        scratch_shapes += [pltpu.VMEM((geo.group_tokens, D_HEAD), F32),
                           pltpu.VMEM((seqs * geo.rows, D_HEAD), F32)] * (n_cols // D_HEAD)
    return pl.pallas_call(
        functools.partial(kernel, layer=layer, geo=geo, has_state=has_state, fresh=fresh),
        grid=(geo.batch // seqs, N_HEAD),
        in_specs=in_specs,
        out_specs=out_specs,
        out_shape=out_shape,
        scratch_shapes=scratch_shapes,
        input_output_aliases={} if fresh else {n_in + i: 1 + i for i in range(len(states_prev))},
        compiler_params=pltpu.CompilerParams(
            dimension_semantics=("arbitrary", "arbitrary"), vmem_limit_bytes=VMEM_LIMIT),
        name=name,
    )(*args)


def _head_block(rows):
    return pl.BlockSpec((rows, D_HEAD), lambda bi, j: (0, j))


def _part_blocks(w_main, mixer):
    return [(w_main, (mixer * 4 + p) * N_HEAD, 1) for p in range(4)]


def _hgrn(geo, layer, h, w_main, hgrn_lb, nw, state_in, state_prev):
    n = geo.problems
    scratch = [pltpu.VMEM((n, geo.rows, D_HEAD), BF16), pltpu.VMEM((n, geo.rows, D_HEAD), F32),
               pltpu.VMEM((n, D_HEAD, D_HEAD), F32), pltpu.VMEM((n, 1, D_HEAD), F32)]
    place = _hgrn_placement(geo)
    return _mixer_call(
        _hgrn_kernel, "hgrn2", geo, layer, h, _part_blocks(w_main, 0),
        [(hgrn_lb, _head_block(hgrn_lb.shape[0])), (nw, _head_block(1)), (place, _resident(place.shape))],
        None if state_in is None else [state_in], [state_prev], scratch)


def _mlstm(geo, layer, h, w_main, w_gate, gate_b, nw, states_in, states_prev):
    n = geo.problems
    col = lambda r: pltpu.VMEM((n, r, LANES), F32)
    scratch = [pltpu.VMEM((n, geo.rows, D_HEAD), BF16), pltpu.VMEM((n, geo.rows, D_HEAD), F32),
               pltpu.VMEM((n, D_HEAD, D_HEAD), F32), pltpu.VMEM((n, 1, D_HEAD), F32),
               col(geo.rows), col(geo.rows), col(geo.rows), col(1), col(1)]
    return _mixer_call(
        _mlstm_kernel, "mlstm", geo, layer, h, _part_blocks(w_main, 1) + [(w_gate, 0, 0)],
        [(gate_b, pl.BlockSpec((1, LANES), lambda bi, j: (0, 0))), (nw, _head_block(1))],
        states_in, states_prev, scratch)


def _retention(geo, layer, h, w_main, cos, sin, nw, state_in, state_prev):
    n = geo.problems
    scratch = [pltpu.VMEM((n, geo.rows, D_HEAD), BF16), pltpu.VMEM((n, geo.rows, D_HEAD), F32),
               pltpu.VMEM((n, D_HEAD, D_HEAD), F32)]
    table = pl.BlockSpec(cos.shape, lambda bi, j: (0, 0, 0))
    return _mixer_call(
        _ret_kernel, "retention", geo, layer, h, _part_blocks(w_main, 2),
        [(cos, table), (sin, table), (nw, _head_block(1))],
        None if state_in is None else [state_in], [state_prev], scratch)


def _merge_kernel(x_ref, h_ref, yh_ref, ym_ref, yr_ref, wg_ref, bg_ref, wb_ref, wo_ref, n2_ref, x1_ref, h2_ref):
    hb = h_ref[...]
    merged = None
    for j, y_ref in enumerate((yh_ref, ym_ref, yr_ref)):
        cols = slice(j * D_MODEL, (j + 1) * D_MODEL)
        gate = _sigmoid(_dot(hb, wg_ref[:, cols]) + bg_ref[:, cols])
        term = gate * _dot(y_ref[...].astype(BF16), wb_ref[j])
        merged = term if merged is None else merged + term
    x1 = x_ref[...] + _dot(merged.astype(BF16), wo_ref[...])
    x1_ref[...] = x1
    h2_ref[...] = _rms(x1, n2_ref[...]).astype(BF16)


def _merge(x, h, yh, ym, yr, w_bgate, gate_b, w_branch, w_out, norm2_w):
    n = x.shape[0]
    tm = min(TOKEN_TILE, n)
    gate_w = N_BRANCH * D_MODEL
    row = lambda width: pl.BlockSpec((tm, width), lambda i: (i, 0))
    vec = lambda width: pl.BlockSpec((1, width), lambda i: (0, 0))
    return pl.pallas_call(
        _merge_kernel,
        grid=(n // tm,),
        in_specs=[
            row(D_MODEL), row(D_MODEL), row(MIX_W), row(MIX_W), row(MIX_W),
            _resident((D_MODEL, gate_w)), vec(gate_w),
            _resident((N_BRANCH, MIX_W, D_MODEL)),
            _resident((D_MODEL, D_MODEL)),
            vec(D_MODEL),
        ],
        out_specs=[row(D_MODEL), row(D_MODEL)],
        out_shape=[jax.ShapeDtypeStruct((n, D_MODEL), F32), jax.ShapeDtypeStruct((n, D_MODEL), BF16)],
        compiler_params=pltpu.CompilerParams(
            dimension_semantics=("arbitrary",), vmem_limit_bytes=VMEM_LIMIT),
        name="merge",
    )(x, h, yh, ym, yr, w_bgate, gate_b, w_branch, w_out, norm2_w)


def _ffn_kernel(*refs, seq_len, has_state, final_norm, tail_rows):
    refs = list(refs)
    x1_ref, h2_ref, wg_ref, wu_ref, wd_ref, cw_ref, cb_ref, nw_ref = refs[:8]
    refs = refs[8:]
    p1_ref = p2_ref = None
    if has_state:
        p1_ref, p2_ref = refs[:2]
        refs = refs[2:]
    if final_norm:
        out_ref, tail_ref, carry_scr, a_scr = refs
    else:
        out_ref, hn_ref, tail_ref, carry_scr, a_scr = refs
    tm = x1_ref.shape[0]
    i = pl.program_id(0)

    @pl.when(i == 0)
    def _():
        carry_scr[...] = jnp.zeros(carry_scr.shape, F32)

    t = (i * tm + lax.broadcasted_iota(jnp.int32, (tm, FF_TILE), 0)) & (seq_len - 1)
    h2 = h2_ref[...]
    acc = x1_ref[...]
    for j in range(D_FF // FF_TILE):
        cols = slice(j * FF_TILE, (j + 1) * FF_TILE)
        a = _dot(h2, wg_ref[:, cols])
        u = _dot(h2, wu_ref[:, cols])
        a_scr[0:SUBLANES, :] = carry_scr[:, cols]
        a_scr[SUBLANES:, :] = a
        carry_scr[:, cols] = a[tm - SUBLANES:, :]
        prev1 = jnp.where(t >= 1, a_scr[SUBLANES - 1:SUBLANES - 1 + tm, :], 0.0)
        prev2 = jnp.where(t >= 2, a_scr[SUBLANES - 2:SUBLANES - 2 + tm, :], 0.0)
        if has_state:
            prev1 = prev1 + p1_ref[:, cols]
            prev2 = prev2 + p2_ref[:, cols]
        conv = prev2 * cw_ref[0:1, cols] + prev1 * cw_ref[1:2, cols] + a * cw_ref[2:3, cols] + cb_ref[:, cols]
        acc = acc + _dot((_silu(conv) * u).astype(BF16), wd_ref[cols, :])
        if tail_rows == tm:
            tail_ref[:, cols] = a
        else:
            tail_ref[0, :, cols] = a[tm - tail_rows:, :]
    if final_norm:
        out_ref[...] = _rms(acc, nw_ref[...])
    else:
        out_ref[...] = acc
        hn_ref[...] = _rms(acc, nw_ref[...]).astype(BF16)


def _ffn(x1, h2, wg, wu, wd, conv_w, conv_b, seq_len, conv_state, norm_w, final_norm):
    n = x1.shape[0]
    has_state = conv_state is not None
    tm = min(TOKEN_TILE // 2 if has_state else TOKEN_TILE, n)
    assert n % tm == 0 and seq_len & (seq_len - 1) == 0
    batch = n // seq_len
    row = lambda width: pl.BlockSpec((tm, width), lambda i: (i, 0))
    full = lambda shape: pl.BlockSpec(shape, lambda i: (0,) * len(shape))
    in_specs = [row(D_MODEL), row(D_MODEL), _resident((D_MODEL, D_FF)), _resident((D_MODEL, D_FF)),
                _resident((D_FF, D_MODEL)), full((CONV_W, D_FF)), full((1, D_FF)), full((1, D_MODEL))]
    args = [x1, h2, wg, wu, wd, conv_w, conv_b, norm_w]
    if has_state:
        p1 = jnp.pad(conv_state[:, 1:2], ((0, 0), (0, seq_len - 1), (0, 0))).reshape(n, D_FF)
        p2 = jnp.pad(conv_state, ((0, 0), (0, seq_len - (CONV_W - 1)), (0, 0))).reshape(n, D_FF)
        in_specs += [row(D_FF), row(D_FF)]
        args += [p1, p2]
    if seq_len % tm == 0:
        tail_rows = CONV_W - 1
        tail_spec = pl.BlockSpec((1, tail_rows, D_FF), lambda i: ((i * tm) // seq_len, 0, 0))
        tail_shape = jax.ShapeDtypeStruct((batch, tail_rows, D_FF), F32)
    else:
        assert tm % seq_len == 0 and seq_len >= CONV_W - 1
        tail_rows = tm
        tail_spec = row(D_FF)
        tail_shape = jax.ShapeDtypeStruct((n, D_FF), F32)
    out_specs, out_shape = [row(D_MODEL)], [jax.ShapeDtypeStruct((n, D_MODEL), F32)]
    if not final_norm:
        out_specs.append(row(D_MODEL))
        out_shape.append(jax.ShapeDtypeStruct((n, D_MODEL), BF16))
    *outs, tail = pl.pallas_call(
        functools.partial(_ffn_kernel, seq_len=seq_len, has_state=has_state,
                          final_norm=final_norm, tail_rows=tail_rows),
        grid=(n // tm,),
        in_specs=in_specs,
        out_specs=out_specs + [tail_spec],
        out_shape=out_shape + [tail_shape],
        scratch_shapes=[pltpu.VMEM((SUBLANES, D_FF), F32), pltpu.VMEM((tm + SUBLANES, FF_TILE), F32)],
        compiler_params=pltpu.CompilerParams(
            dimension_semantics=("arbitrary",), vmem_limit_bytes=VMEM_LIMIT),
        name="conv_ffn",
    )(*args)
    if tail_rows == tm:
        tail = tail.reshape(batch, seq_len, D_FF)[:, seq_len - (CONV_W - 1):, :]
    return outs, tail


def _regroup_kernel(a_ref, b_ref, c_ref, d_ref, main_ref, gate_ref, bgate_ref):
    shift = ML_GATE_OFF % LANES + 2 * N_HEAD
    main_ref[:, 0:ML_GATE_OFF] = a_ref[...]
    b, c = b_ref[...].astype(F32), c_ref[...].astype(F32)
    ret = jnp.concatenate([b, c[:, 0:LANES]], axis=1)
    ret = pltpu.roll(ret, ret.shape[1] - shift, axis=1)[:, 0:4 * MIX_W]
    main_ref[:, ML_GATE_OFF:] = ret.astype(BF16)
    lane = lax.broadcasted_iota(jnp.int32, (b.shape[0], LANES), 1)
    gate_ref[...] = jnp.where(lane < 2 * N_HEAD, b[:, 0:LANES], 0.0).astype(BF16)
    gates = jnp.concatenate([c, d_ref[...].astype(F32)], axis=1)
    gates = pltpu.roll(gates, gates.shape[1] - shift, axis=1)[:, 0:N_BRANCH * D_MODEL]
    bgate_ref[...] = gates.astype(BF16)


def _regroup_w_in(w_in):
    depth = w_in.shape[0]
    rows = REGROUP_ROW_TILE
    n_main, n_gates = 3 * 4 * MIX_W, N_BRANCH * D_MODEL
    assert ML_GATE_OFF % LANES == 0 and GATE_OFF - 2 * N_HEAD == n_main
    spec = lambda width, first: pl.BlockSpec((None, rows, width), lambda l, i: (l, i, first))
    return pl.pallas_call(
        _regroup_kernel,
        grid=(depth, D_MODEL // rows),
        in_specs=[spec(ML_GATE_OFF, 0), spec(4 * MIX_W, ML_GATE_OFF // (4 * MIX_W)),
                  spec(n_gates, n_main // n_gates), spec(LANES, (n_main + n_gates) // LANES)],
        out_specs=[spec(n_main, 0), spec(LANES, 0), spec(n_gates, 0)],
        out_shape=[jax.ShapeDtypeStruct((depth, D_MODEL, n_main), BF16),
                   jax.ShapeDtypeStruct((depth, D_MODEL, LANES), BF16),
                   jax.ShapeDtypeStruct((depth, D_MODEL, n_gates), BF16)],
        compiler_params=pltpu.CompilerParams(
            dimension_semantics=("arbitrary", "arbitrary"), vmem_limit_bytes=VMEM_LIMIT),
        name="regroup_w_in",
    )(w_in, w_in, w_in, w_in)


def _prep_weights(w_in, w_branch, w_out, ffn_w_gate, ffn_w_up, ffn_w_down):
    w_main, w_gate, w_bgate = _regroup_w_in(w_in.astype(BF16))
    return (w_main, w_gate, w_bgate, w_branch.astype(BF16), w_out.astype(BF16),
            ffn_w_gate.astype(BF16), ffn_w_up.astype(BF16), ffn_w_down.astype(BF16))


def _trunk(x3, pos0, states, params, prepped):
    (norm1_w, mlstm_gate_b, branch_gate_b, hgrn_lb, hgrn_norm_w, mlstm_norm_w, ret_norm_w,
     norm2_w, ffn_conv_w, ffn_conv_b, final_norm_w) = params
    w_main, w_gate, w_bgate, w_branch, w_out, w_ffg, w_ffu, w_ffd = prepped
    batch, seq_len, _ = x3.shape
    n = batch * seq_len
    depth = norm1_w.shape[0]
    geo = _Geo(batch, seq_len)
    cos, sin = (t.reshape(geo.chunks, geo.rows, D_HEAD) for t in _rope_tables(geo.chunks * geo.rows, pos0))

    mat = jax.ShapeDtypeStruct((depth, batch, N_HEAD, D_HEAD, D_HEAD), F32)
    vec = jax.ShapeDtypeStruct((depth, batch, N_HEAD, D_HEAD), F32)
    if states is None:
        hg_in = ml_in = ret_in = conv_in = None
    else:
        hg_in, ml_c, ml_n, ml_m, ret_in, conv_in = states
        ml_in = [ml_c, ml_n, jnp.broadcast_to(ml_m[..., None], vec.shape)]
    hg_out, c_out, r_out, n_out, m_out = mat, mat, mat, vec, vec

    x = x3.reshape(n, D_MODEL)
    h = _norm(x, norm1_w[0:1])
    conv_tails = []
    for l in range(depth):
        last = l == depth - 1
        y_hg, hg_out = _hgrn(geo, l, h, w_main, hgrn_lb, hgrn_norm_w[l:l + 1], hg_in, hg_out)
        gate_b = jnp.pad(mlstm_gate_b[l:l + 1], ((0, 0), (0, LANES - 2 * N_HEAD)))
        y_ml, c_out, n_out, m_out = _mlstm(geo, l, h, w_main, w_gate, gate_b, mlstm_norm_w[l:l + 1], ml_in,
                                           [c_out, n_out, m_out])
        y_ret, r_out = _retention(geo, l, h, w_main, cos, sin, ret_norm_w[l:l + 1], ret_in, r_out)
        x1, h2 = _merge(x, h, y_hg, y_ml, y_ret, w_bgate[l], branch_gate_b[l:l + 1], w_branch[l], w_out[l],
                        norm2_w[l:l + 1])
        outs, conv_tail = _ffn(x1, h2, w_ffg[l], w_ffu[l], w_ffd[l], ffn_conv_w[l], ffn_conv_b[l:l + 1],
                               seq_len, None if conv_in is None else conv_in[l],
                               final_norm_w[None, :] if last else norm1_w[l + 1:l + 2], last)
        x = outs[0]
        h = None if last else outs[1]
        conv_tails.append(conv_tail)
    return x.reshape(batch, seq_len, D_MODEL), (
        hg_out, c_out, n_out, m_out[:, :, :, 0], r_out,
        jnp.stack(conv_tails))


def kernel(x_prompt, x_sample, state_hgrn, state_mlstm_C, state_mlstm_n, state_mlstm_m, state_ret,
           state_ffn_conv, norm1_w, w_in, mlstm_gate_b, branch_gate_b, hgrn_lb, hgrn_norm_w,
           mlstm_norm_w, ret_norm_w, w_branch, w_out, norm2_w, ffn_w_gate, ffn_w_up, ffn_conv_w,
           ffn_conv_b, ffn_w_down, final_norm_w):
    params = (norm1_w, mlstm_gate_b, branch_gate_b, hgrn_lb, hgrn_norm_w, mlstm_norm_w, ret_norm_w,
              norm2_w, ffn_conv_w, ffn_conv_b, final_norm_w)
    prepped = _prep_weights(w_in, w_branch, w_out, ffn_w_gate, ffn_w_up, ffn_w_down)
    y_p, (hg_p, c_p, n_p, m_p, r_p, cv_p) = _trunk(x_prompt, 0, None, params, prepped)
    y_s, (hg_s, c_s, n_s, m_s, r_s, cv_s) = _trunk(
        x_sample, PAST_LEN,
        (state_hgrn, state_mlstm_C, state_mlstm_n, state_mlstm_m, state_ret, state_ffn_conv),
        params, prepped)
    return (y_p, y_s, hg_p, hg_s, c_p, c_s, n_p, n_s, m_p, m_s, r_p, r_s, cv_p, cv_s)
```

```python
import functools
import math

import numpy as np
import jax
import jax.numpy as jnp
from jax import lax
from jax.experimental import pallas as pl
from jax.experimental.pallas import tpu as pltpu

F32 = jnp.float32
BF16 = jnp.bfloat16

D_MODEL = 1024
MIX_W = 512
N_HEAD = 4
D_HEAD = 128
N_BRANCH = 3
D_FF = 2816
CONV_W = 3
CHUNK = 64
EPS = 1e-6
ROPE_BASE = 10000.0
PAST_LEN = 16384

HG_OFF = 0
ML_OFF = 4 * MIX_W
ML_GATE_OFF = ML_OFF + 4 * MIX_W
RET_OFF = ML_GATE_OFF + 2 * N_HEAD
GATE_OFF = RET_OFF + 4 * MIX_W
LOG2E = 1.4426950408889634

LANES = 128
SUBLANES = 8
VMEM_LIMIT = 56 * 1024 * 1024

TOKEN_TILE = 512
NORM_TOKEN_TILE = 1024
FF_TILE = 2816
HG_SUB = 16
GROUP = 32
SEQ_BLOCK = 64
SCAN_UNROLL = 8


def _resident(shape):
    return pl.BlockSpec(shape, lambda *_: (0,) * len(shape), pipeline_mode=pl.Buffered(1))


def _dot(a, b):
    return jnp.dot(a, b, preferred_element_type=F32)


def _dot_tn(a, b):
    return lax.dot_general(a, b, (((0,), (0,)), ((), ())), preferred_element_type=F32)


def _bdot(a, b):
    return jnp.einsum('nlk,nkv->nlv', a, b, preferred_element_type=F32)


def _bdot_nt(a, b):
    return jnp.einsum('nqd,nkd->nqk', a, b, preferred_element_type=F32)


def _bdot_tn(a, b):
    return jnp.stack([_dot_tn(a[i], b[i]) for i in range(a.shape[0])])


def _split3(x):
    hi = x.astype(BF16)
    r = x - hi.astype(F32)
    mid = r.astype(BF16)
    lo = (r - mid.astype(F32)).astype(BF16)
    return hi, mid, lo


def _sigmoid(x):
    return 1.0 / (1.0 + jnp.exp(-x))


def _silu(x):
    return x * _sigmoid(x)


def _log_sigmoid(x):
    return jnp.minimum(x, 0.0) - jnp.log(1.0 + jnp.exp(-jnp.abs(x)))


def _rms(x, w):
    return x * lax.rsqrt(jnp.mean(x * x, axis=-1, keepdims=True) + EPS) * w


def _pad_rows(x, n):
    if x.shape[1] >= n:
        return x
    return jnp.concatenate([x, jnp.zeros((x.shape[0], n - x.shape[1], x.shape[2]), x.dtype)], axis=1)


def _seg_cumsum(x, seg):
    shape = x.shape
    flat = x.reshape(shape[0] * seg, shape[2])
    row = lax.broadcasted_iota(jnp.int32, flat.shape, 0) & (seg - 1)
    s = 1
    while s < seg:
        flat = flat + jnp.where(row >= s, pltpu.roll(flat, s, axis=0), 0.0)
        s *= 2
    return flat.reshape(shape)


def _col_bcast(rows):
    n = rows.shape[0]
    r = lax.broadcasted_iota(jnp.int32, (n, D_HEAD, D_HEAD), 1)
    c = lax.broadcasted_iota(jnp.int32, (n, D_HEAD, D_HEAD), 2)
    d = jnp.where(r == c, jnp.broadcast_to(rows, (n, D_HEAD, D_HEAD)), 0.0).reshape(n * D_HEAD, D_HEAD)
    ones = jnp.ones((D_HEAD, D_HEAD), BF16)
    hi, mid, lo = _split3(d)
    return (_dot(hi, ones) + _dot(mid, ones) + _dot(lo, ones)).reshape(n, D_HEAD, D_HEAD)


def _through_vmem(x):
    def body(ref):
        ref[...] = x
        return ref[...]
    return pl.run_scoped(body, pltpu.VMEM(x.shape, x.dtype))


def _row_bcast(col, width):
    n, rows, _ = col.shape
    vals = _through_vmem(_pad_rows(col, width))
    lane = lax.broadcasted_iota(jnp.int32, (n, rows, LANES), 2)
    pick = _through_vmem(jnp.where(lane == 0, 1.0, 0.0).astype(BF16))
    hi, mid, lo = _split3(vals)
    return _bdot_nt(pick, hi) + _bdot_nt(pick, mid) + _bdot_nt(pick, lo)


def _chunk_geometry(seq_len):
    if seq_len % CHUNK == 0:
        return CHUNK, CHUNK, CHUNK
    valid = math.gcd(seq_len, CHUNK)
    assert valid == seq_len and valid <= SUBLANES, "unsupported sequence length"
    return SUBLANES, valid, LANES


def _rope_kernel(inv_ref, cos_ref, sin_ref, *, pos0):
    shape = cos_ref.shape
    pos = lax.broadcasted_iota(jnp.int32, shape, 0).astype(F32) + pos0
    ang = pos * inv_ref[...]
    lane = lax.broadcasted_iota(jnp.int32, shape, 1)
    cos_ref[...] = jnp.cos(ang)
    sin_ref[...] = jnp.where(lane < D_HEAD // 2, -jnp.sin(ang), jnp.sin(ang))


def _rope_tables(rows, pos0):
    inv = ROPE_BASE ** (-jnp.linspace(0.0, 1.0, D_HEAD // 2, dtype=F32))
    inv2 = jnp.concatenate([inv, inv])[None, :]
    return pl.pallas_call(
        functools.partial(_rope_kernel, pos0=float(pos0)),
        out_shape=(jax.ShapeDtypeStruct((rows, D_HEAD), F32),) * 2,
        name="rope_tables",
    )(inv2)


def _norm_kernel(x_ref, w_ref, h_ref):
    h_ref[...] = _rms(x_ref[...], w_ref[...]).astype(BF16)


def _norm(x, w):
    n = x.shape[0]
    tm = min(NORM_TOKEN_TILE, n)
    assert n % tm == 0
    return pl.pallas_call(
        _norm_kernel,
        grid=(n // tm,),
        in_specs=[pl.BlockSpec((tm, D_MODEL), lambda i: (i, 0)), pl.BlockSpec((1, D_MODEL), lambda i: (0, 0))],
        out_specs=pl.BlockSpec((tm, D_MODEL), lambda i: (i, 0)),
        out_shape=jax.ShapeDtypeStruct((n, D_MODEL), BF16),
        compiler_params=pltpu.CompilerParams(dimension_semantics=("arbitrary",), vmem_limit_bytes=VMEM_LIMIT),
        name="norm_in",
    )(x, w)


class _Geo:
    def __init__(self, batch, seq_len):
        self.rows, self.valid, self.width = _chunk_geometry(seq_len)
        if self.valid == self.rows:
            self.seqs, self.chunks = 1, seq_len // self.rows
            self.group = min(GROUP, self.chunks)
        else:
            self.seqs, self.chunks = min(SEQ_BLOCK, batch), 1
            self.group = self.seqs
        assert batch % self.seqs == 0 and self.chunks % min(self.group, self.chunks) == 0
        self.batch = batch
        self.problems = self.seqs * self.chunks
        self.n_groups = self.problems // self.group
        self.group_tokens = self.group * self.valid
        self.tokens = self.problems * self.valid
        self.padded = self.valid < self.rows

    def project(self, h_ref, w, g):
        return _dot(h_ref[pl.ds(g * self.group_tokens, self.group_tokens), :], w)

    def split(self, z, relay):
        n_parts = z.shape[1] // D_HEAD
        cols = lambda p: slice(p * D_HEAD, (p + 1) * D_HEAD)
        if not self.padded:
            return [z[:, cols(p)].reshape(self.group, self.rows, D_HEAD) for p in range(n_parts)]
        out = []
        for p in range(n_parts):
            flat_ref, pad_ref = relay[2 * p], relay[2 * p + 1]
            flat_ref[...] = z[:, cols(p)]
            pad_ref[...] = jnp.zeros(pad_ref.shape, F32)
            for t in range(self.valid):
                pad_ref[pl.ds(t, self.seqs, stride=self.rows), :] = flat_ref[
                    pl.ds(t, self.seqs, stride=self.valid), :]
            out.append(pad_ref[...].reshape(self.seqs, self.rows, D_HEAD))
        return out

    def store(self, y_ref, g, y, relay):
        if not self.padded:
            n = self.group * self.rows
            y_ref[pl.ds(g * n, n), :] = y.reshape(n, D_HEAD).astype(y_ref.dtype)
            return
        pad = relay[1]
        pad[...] = y.reshape(self.seqs * self.rows, D_HEAD)
        for t in range(self.valid):
            y_ref[pl.ds(t, self.seqs, stride=self.valid), :] = pad[pl.ds(t, self.seqs, stride=self.rows), :].astype(
                y_ref.dtype)

    def rows_of(self, g):
        return pl.ds(g * self.group, self.group)

    def row_mask(self, shape):
        if not self.padded:
            return None
        return lax.broadcasted_iota(jnp.int32, shape, 1) < self.valid

    def for_groups(self, body):
        if self.n_groups == 1:
            body(0)
        else:
            lax.fori_loop(0, self.n_groups, lambda g, c: (body(g), c)[1], 0)

    def for_groups_static(self, body):
        for g in range(self.n_groups):
            body(g)

    def scan_chunks(self, step, init):
        if self.chunks == 1:
            return step(pl.ds(0, self.seqs), init)
        return lax.fori_loop(0, self.chunks, lambda c, carry: step(pl.ds(c * self.seqs, self.seqs), carry), init,
                             unroll=min(SCAN_UNROLL, self.chunks))


def _join_columns(w_refs):
    return jnp.concatenate([r[...] for r in w_refs], axis=1)


def _transpose_minor(x):
    return jnp.stack([x[i].T for i in range(x.shape[0])])


def _state_views(refs, kinds, layer, layered):
    head = pl.program_id(1)
    views = []
    for ref, kind in zip(refs, kinds):
        for l in (range(ref.shape[0]) if layered else [layer]):
            view = ref.at[l] if layered else ref
            if kind == 'vec':
                view = view.at[:, pl.ds(head, 1), :]
            if l == layer:
                views.append(view)
            else:
                view[...] = jnp.zeros(view.shape, F32)
    return views


def _split_refs(refs, counts):
    out, k = [], 0
    for c in counts:
        out.append(refs[k:k + c])
        k += c
    return out


def _hgrn_lower_bound(lb_ref, layer):
    x = lb_ref[...]
    e = jnp.exp(x - jnp.max(x, axis=0, keepdims=True))
    soft = e / jnp.sum(e, axis=0, keepdims=True)
    cum = soft[0:1]
    for j in range(1, layer + 1):
        cum = cum + soft[j:j + 1]
    return cum - soft[0:1]


def _hgrn_placement(geo):
    sub = min(HG_SUB, geo.rows)
    r = np.arange(sub * D_HEAD)[:, None] // D_HEAD
    c = np.arange(geo.width)[None, :]
    return jnp.asarray(np.stack([r + i * sub == c for i in range(geo.rows // sub)]), BF16)


def _hgrn_intra(q, k, b2, place_ref, geo):
    n = q.shape[0]
    sub = min(HG_SUB, geo.rows)
    sub_row = lax.broadcasted_iota(jnp.int32, (n, sub, geo.width), 1)
    lane = lax.broadcasted_iota(jnp.int32, (n, sub, geo.width), 2)
    blocks = []
    for i in range(geo.rows // sub):
        r0 = i * sub
        b_i, q_i, k_i = (a[:, r0:r0 + sub] for a in (b2, q, k))
        terms = []
        for s in range(sub):
            lo = s // SUBLANES * SUBLANES
            e = jnp.exp2(jnp.minimum(b_i[:, lo:] - b_i[:, s:s + 1], 0.0))
            term = q_i[:, lo:] * e * k_i[:, s:s + 1]
            if lo:
                term = jnp.concatenate([jnp.zeros((n, lo, D_HEAD), F32), term], axis=1)
            terms.append(term)
        cat = jnp.concatenate(terms, axis=2).reshape(n * sub, sub * D_HEAD)
        a_i = _dot(cat.astype(BF16), place_ref[i]).reshape(n, sub, geo.width)
        a_i = jnp.where(sub_row >= lane - r0, a_i, 0.0)
        if i > 0:
            b_ref = b_i[:, 0:1]
            q_dec = q_i * jnp.exp2(b_i - b_ref)
            k_dec = _pad_rows(k[:, 0:r0] * jnp.exp2(b_ref - b2[:, 0:r0]), geo.width)
            a_i = jnp.where(lane < r0, _bdot_nt(q_dec.astype(BF16), k_dec.astype(BF16)), a_i)
        blocks.append(a_i)
    return blocks[0] if len(blocks) == 1 else jnp.concatenate(blocks, axis=1)


def _hgrn_kernel(*refs, layer, geo, has_state, fresh):
    ((h_ref,), w_refs, (lb_ref, nw_ref, place), s0, _, (y_ref,), s_out,
     (qt_scr, o_scr, u_scr, dc_scr, gate_scr), relay) = _split_refs(
        refs, (1, 4, 3, int(has_state), int(not fresh), 1, 1, 5, len(refs)))
    (s_ref,) = _state_views(s_out, ['mat'], layer, fresh)
    lb = _hgrn_lower_bound(lb_ref, layer)
    rows, width = geo.rows, geo.width
    w = _join_columns(w_refs)

    def phase_a(g):
        hq, hf, hi, hg = geo.split(geo.project(h_ref, w, g), relay)
        q = _silu(hq)
        f = lb + (1.0 - lb) * _sigmoid(hf)
        lf = jnp.log(f)
        k = 1.0 - f
        mask = geo.row_mask(lf.shape)
        if mask is not None:
            lf = jnp.where(mask, lf, 0.0)
            k = jnp.where(mask, k, 0.0)
        b2 = _seg_cumsum(lf, rows) * LOG2E
        sl = geo.rows_of(g)
        gate_scr[sl] = hg
        qt_scr[sl] = (q * jnp.exp2(b2)).astype(BF16)
        a = _hgrn_intra(q, k, b2, place, geo)
        vb = _pad_rows(hi, width).astype(BF16)
        o_scr[sl] = _bdot(a.astype(BF16), vb)
        b_last = b2[:, rows - 1:rows]
        k_w = k * jnp.exp2(b_last - b2)
        u_scr[sl] = _bdot_tn(vb, _pad_rows(k_w, width).astype(BF16))
        dc_scr[sl] = jnp.exp2(b_last)

    geo.for_groups_static(phase_a)

    def step(sl, state_t):
        o_scr[sl] = o_scr[sl] + _bdot_nt(qt_scr[sl], state_t.astype(BF16))
        return dc_scr[sl] * state_t + u_scr[sl]

    init = _transpose_minor(s0[0][...]) if has_state else jnp.zeros((geo.seqs, D_HEAD, D_HEAD), F32)
    s_ref[...] = _transpose_minor(geo.scan_chunks(step, init))

    def epilogue(g):
        sl = geo.rows_of(g)
        geo.store(y_ref, g, _rms(o_scr[sl], nw_ref[...]) * _silu(gate_scr[sl]), relay)

    geo.for_groups(epilogue)


def _mlstm_kernel(*refs, layer, geo, has_state, fresh):
    ((h_ref,), w_refs, (gb_ref, nw_ref), st0, _, (y_ref,), st_out,
     (q_scr, v_scr, u_scr, nu_scr, bc_scr, ml_scr, rs_scr, bl_scr, mu_scr, gate_scr), relay) = _split_refs(
        refs, (1, 5, 2, 3 * int(has_state), 3 * int(not fresh), 1, 3, 10, len(refs)))
    kinds = ['mat', 'vec', 'vec']
    c_ref, n_ref, m_ref = _state_views(st_out, kinds, layer, fresh)
    st0 = _state_views(st0, kinds, layer, False) if has_state else st0
    rows, width = geo.rows, geo.width
    head = pl.program_id(1)
    scale = D_HEAD ** -0.5
    t_i = lax.broadcasted_iota(jnp.int32, (1, rows, width), 1)
    s_i = lax.broadcasted_iota(jnp.int32, (1, rows, width), 2)
    ok = s_i <= t_i
    if geo.padded:
        ok = ok & (s_i < geo.valid)
    w = _join_columns(w_refs)

    def phase_a(g):
        q, k, v, og, gates = geo.split(geo.project(h_ref, w, g), relay)
        gates = gates + gb_ref[...]
        lf_all = _log_sigmoid(gates)
        mask = geo.row_mask(gates.shape)
        if mask is not None:
            lf_all = jnp.where(mask, lf_all, 0.0)
        bc_all = _seg_cumsum(lf_all, rows)
        lane = lax.broadcasted_iota(jnp.int32, gates.shape, 2)
        wide = lambda x: jnp.broadcast_to(x, gates.shape)
        ig = wide(jnp.sum(jnp.where(lane == head, gates, 0.0), axis=2, keepdims=True))
        bc_col = jnp.sum(jnp.where(lane == head + N_HEAD, bc_all, 0.0), axis=2, keepdims=True)
        bc = wide(bc_col)
        log_d = jnp.where(ok, bc_col + _row_bcast(ig - bc, width), -jnp.inf)
        m_col = jnp.max(log_d, axis=2, keepdims=True)
        m_loc = wide(m_col)
        k = k * scale
        qb = q.astype(BF16)
        vb = _pad_rows(v, width).astype(BF16)
        s_m = _bdot_nt(qb, _pad_rows(k, width).astype(BF16)) * jnp.exp(log_d - m_col)
        sl = geo.rows_of(g)
        gate_scr[sl] = og
        q_scr[sl] = qb
        v_scr[sl] = _bdot(s_m.astype(BF16), vb)
        rs_scr[sl] = wide(jnp.sum(s_m, axis=2, keepdims=True))
        bc_scr[sl] = bc
        ml_scr[sl] = m_loc
        b_last = bc[:, rows - 1:rows]
        m_upd = m_loc[:, rows - 1:rows]
        w_exp = b_last - bc + ig - m_upd
        if mask is not None:
            w_exp = jnp.where(mask, w_exp, -jnp.inf)
        k_w = k * jnp.exp(w_exp)
        u_scr[sl] = _bdot_tn(_pad_rows(k_w, width).astype(BF16), vb)
        nu_scr[sl] = jnp.sum(k_w, axis=1, keepdims=True)
        bl_scr[sl] = b_last
        mu_scr[sl] = m_upd

    geo.for_groups_static(phase_a)

    def step(sl, c_state):
        n_state, m_prev = n_ref[...], m_ref[...]
        inter_log = bc_scr[sl] + m_prev
        m_t = jnp.maximum(inter_log, ml_scr[sl])
        w_inter = jnp.exp(inter_log - m_t)
        w_loc = jnp.exp(ml_scr[sl] - m_t)
        qb = q_scr[sl]
        num = w_inter * _bdot(qb, c_state.astype(BF16)) + w_loc * v_scr[sl]
        q_n = jnp.broadcast_to(jnp.sum(qb.astype(F32) * n_state, axis=2, keepdims=True), m_t.shape)
        nq = w_inter * q_n + w_loc * rs_scr[sl]
        v_scr[sl] = num / jnp.maximum(jnp.abs(nq), jnp.exp(-m_t))
        m_new = m_t[:, rows - 1:rows]
        w_prev = jnp.exp(bl_scr[sl] + m_prev - m_new)
        w_upd = jnp.exp(mu_scr[sl] - m_new)
        n_ref[...] = w_prev * n_state + w_upd * nu_scr[sl]
        m_ref[...] = m_new
        return w_prev * c_state + w_upd * u_scr[sl]

    if has_state:
        c_init = st0[0][...]
        n_ref[...] = st0[1][...]
        m_ref[...] = st0[2][...]
    else:
        c_init = jnp.zeros((geo.seqs, D_HEAD, D_HEAD), F32)
        n_ref[...] = jnp.zeros(n_ref.shape, F32)
        m_ref[...] = jnp.zeros(m_ref.shape, F32)
    c_ref[...] = geo.scan_chunks(step, c_init)

    def epilogue(g):
        sl = geo.rows_of(g)
        geo.store(y_ref, g, _rms(_sigmoid(gate_scr[sl]) * v_scr[sl], nw_ref[...]), relay)

    geo.for_groups(epilogue)


def _ret_kernel(*refs, layer, geo, has_state, fresh):
    ((h_ref,), w_refs, (cos_ref, sin_ref, nw_ref), r0, _, (y_ref,), r_out,
     (q_scr, o_scr, u_scr, gate_scr), relay) = _split_refs(
        refs, (1, 4, 3, int(has_state), int(not fresh), 1, 1, 4, len(refs)))
    (r_ref,) = _state_views(r_out, ['mat'], layer, fresh)
    rows, width, valid = geo.rows, geo.width, geo.valid
    head = pl.program_id(1)
    log_gamma = jnp.zeros((1, 1, 1), F32)
    for h in range(N_HEAD):
        log_gamma = jnp.where(head == h, math.log(1.0 - 2.0 ** (-5.0 - h)), log_gamma)
    scale = D_HEAD ** -0.5
    t_i = lax.broadcasted_iota(jnp.int32, (1, rows, width), 1)
    s_i = lax.broadcasted_iota(jnp.int32, (1, rows, width), 2)
    decay = jnp.where(s_i <= t_i, jnp.exp((t_i - s_i).astype(F32) * log_gamma), 0.0)
    t_col = lax.broadcasted_iota(jnp.int32, (1, rows, D_HEAD), 1).astype(F32)
    w_in = jnp.exp((t_col + 1.0) * log_gamma)
    w_st = jnp.exp((valid - 1.0 - t_col) * log_gamma)
    g_all = jnp.exp(valid * log_gamma)
    w = _join_columns(w_refs)

    def rotate(x, cos, sin):
        flat = x.reshape(x.shape[0] * rows, D_HEAD)
        return x * cos + pltpu.roll(flat, D_HEAD // 2, axis=1).reshape(x.shape) * sin

    def phase_a(g):
        q, k, v, rg = geo.split(geo.project(h_ref, w, g), relay)
        if geo.chunks > 1:
            cos, sin = cos_ref[geo.rows_of(g)], sin_ref[geo.rows_of(g)]
        else:
            cos, sin = cos_ref[...], sin_ref[...]
        qr = rotate(q, cos, sin).astype(BF16)
        kr = rotate(k, cos, sin) * scale
        vb = _pad_rows(v, width).astype(BF16)
        a = _bdot_nt(qr, _pad_rows(kr, width).astype(BF16)) * decay
        sl = geo.rows_of(g)
        gate_scr[sl] = rg
        q_scr[sl] = qr
        o_scr[sl] = _bdot(a.astype(BF16), vb)
        u_scr[sl] = _bdot_tn(_pad_rows(kr * w_st, width).astype(BF16), vb)

    geo.for_groups_static(phase_a)

    def step(sl, state):
        o_scr[sl] = o_scr[sl] + _bdot(q_scr[sl], state.astype(BF16)) * w_in
        return g_all * state + u_scr[sl]

    init = r0[0][...] if has_state else jnp.zeros((geo.seqs, D_HEAD, D_HEAD), F32)
    r_ref[...] = geo.scan_chunks(step, init)

    def epilogue(g):
        sl = geo.rows_of(g)
        geo.store(y_ref, g, _rms(o_scr[sl], nw_ref[...]) * _silu(gate_scr[sl]), relay)

    geo.for_groups(epilogue)


def _mixer_call(kernel, name, geo, layer, h, weights, extra, states_in, states_prev, scratch):
    seqs, n_cols = geo.seqs, len(weights) * D_HEAD
    fresh = isinstance(states_prev[0], jax.ShapeDtypeStruct)

    def weight_spec(first, per_head):
        return pl.BlockSpec((None, D_MODEL, D_HEAD), lambda bi, j: (layer, 0, first + per_head * j))

    def state_spec(arr, layered=False):
        lead, first = (arr.shape[0], 0) if layered else (None, layer)
        if len(arr.shape) == 5:
            return pl.BlockSpec((lead, seqs, None, D_HEAD, D_HEAD), lambda bi, j: (first, bi, j, 0, 0))
        return pl.BlockSpec((lead, seqs, N_HEAD, D_HEAD), lambda bi, j: (first, bi, 0, 0))

    has_state = states_in is not None
    args = [h] + [a for a, _, _ in weights] + [a for a, _ in extra]
    in_specs = [pl.BlockSpec((geo.tokens, D_MODEL), lambda bi, j: (bi, 0))] + [
        weight_spec(first, per_head) for _, first, per_head in weights] + [s for _, s in extra]
    if has_state:
        args += list(states_in)
        in_specs += [state_spec(a) for a in states_in]
    n_in = len(args)
    if not fresh:
        args += list(states_prev)
        in_specs += [pl.BlockSpec(memory_space=pl.ANY) for _ in states_prev]
    out_specs = [pl.BlockSpec((geo.tokens, D_HEAD), lambda bi, j: (bi, j))] + [
        state_spec(a, fresh) for a in states_prev]
    out_shape = [jax.ShapeDtypeStruct((h.shape[0], MIX_W), F32 if geo.padded else BF16)] + [
        jax.ShapeDtypeStruct(a.shape, F32) for a in states_prev]
    scratch_shapes = list(scratch) + [pltpu.VMEM((geo.problems, geo.rows, D_HEAD), F32)]
    if geo.padded:
        scratch_shapes += [pltpu.VMEM((geo.group_tokens, D_HEAD), F32),
                           pltpu.VMEM((seqs * geo.rows, D_HEAD), F32)] * (n_cols // D_HEAD)
    return pl.pallas_call(
        functools.partial(kernel, layer=layer, geo=geo, has_state=has_state, fresh=fresh),
        grid=(geo.batch // seqs, N_HEAD),
        in_specs=in_specs,
        out_specs=out_specs,
        out_shape=out_shape,
        scratch_shapes=scratch_shapes,
        input_output_aliases={} if fresh else {n_in + i: 1 + i for i in range(len(states_prev))},
        compiler_params=pltpu.CompilerParams(
            dimension_semantics=("arbitrary", "arbitrary"), vmem_limit_bytes=VMEM_LIMIT),
        name=name,
    )(*args)


def _head_block(rows):
    return pl.BlockSpec((rows, D_HEAD), lambda bi, j: (0, j))


def _part_blocks(w_main, mixer):
    return [(w_main, (mixer * 4 + p) * N_HEAD, 1) for p in range(4)]


def _hgrn(geo, layer, h, w_main, hgrn_lb, nw, state_in, state_prev):
    n = geo.problems
    scratch = [pltpu.VMEM((n, geo.rows, D_HEAD), BF16), pltpu.VMEM((n, geo.rows, D_HEAD), F32),
               pltpu.VMEM((n, D_HEAD, D_HEAD), F32), pltpu.VMEM((n, 1, D_HEAD), F32)]
    place = _hgrn_placement(geo)
    return _mixer_call(
        _hgrn_kernel, "hgrn2", geo, layer, h, _part_blocks(w_main, 0),
        [(hgrn_lb, _head_block(hgrn_lb.shape[0])), (nw, _head_block(1)), (place, _resident(place.shape))],
        None if state_in is None else [state_in], [state_prev], scratch)


def _mlstm(geo, layer, h, w_main, w_gate, gate_b, nw, states_in, states_prev):
    n = geo.problems
    col = lambda r: pltpu.VMEM((n, r, LANES), F32)
    scratch = [pltpu.VMEM((n, geo.rows, D_HEAD), BF16), pltpu.VMEM((n, geo.rows, D_HEAD), F32),
               pltpu.VMEM((n, D_HEAD, D_HEAD), F32), pltpu.VMEM((n, 1, D_HEAD), F32),
               col(geo.rows), col(geo.rows), col(geo.rows), col(1), col(1)]
    return _mixer_call(
        _mlstm_kernel, "mlstm", geo, layer, h, _part_blocks(w_main, 1) + [(w_gate, 0, 0)],
        [(gate_b, pl.BlockSpec((1, LANES), lambda bi, j: (0, 0))), (nw, _head_block(1))],
        states_in, states_prev, scratch)


def _retention(geo, layer, h, w_main, cos, sin, nw, state_in, state_prev):
    n = geo.problems
    scratch = [pltpu.VMEM((n, geo.rows, D_HEAD), BF16), pltpu.VMEM((n, geo.rows, D_HEAD), F32),
               pltpu.VMEM((n, D_HEAD, D_HEAD), F32)]
    table = pl.BlockSpec(cos.shape, lambda bi, j: (0, 0, 0))
    return _mixer_call(
        _ret_kernel, "retention", geo, layer, h, _part_blocks(w_main, 2),
        [(cos, table), (sin, table), (nw, _head_block(1))],
        None if state_in is None else [state_in], [state_prev], scratch)


def _merge_kernel(x_ref, h_ref, yh_ref, ym_ref, yr_ref, wg_ref, bg_ref, wb_ref, wo_ref, n2_ref, x1_ref, h2_ref):
    hb = h_ref[...]
    merged = None
    for j, y_ref in enumerate((yh_ref, ym_ref, yr_ref)):
        cols = slice(j * D_MODEL, (j + 1) * D_MODEL)
        gate = _sigmoid(_dot(hb, wg_ref[:, cols]) + bg_ref[:, cols])
        term = gate * _dot(y_ref[...].astype(BF16), wb_ref[j])
        merged = term if merged is None else merged + term
    x1 = x_ref[...] + _dot(merged.astype(BF16), wo_ref[...])
    x1_ref[...] = x1
    h2_ref[...] = _rms(x1, n2_ref[...]).astype(BF16)


def _merge(x, h, yh, ym, yr, w_bgate, gate_b, w_branch, w_out, norm2_w):
    n = x.shape[0]
    tm = min(TOKEN_TILE, n)
    gate_w = N_BRANCH * D_MODEL
    row = lambda width: pl.BlockSpec((tm, width), lambda i: (i, 0))
    vec = lambda width: pl.BlockSpec((1, width), lambda i: (0, 0))
    return pl.pallas_call(
        _merge_kernel,
        grid=(n // tm,),
        in_specs=[
            row(D_MODEL), row(D_MODEL), row(MIX_W), row(MIX_W), row(MIX_W),
            _resident((D_MODEL, gate_w)), vec(gate_w),
            _resident((N_BRANCH, MIX_W, D_MODEL)),
            _resident((D_MODEL, D_MODEL)),
            vec(D_MODEL),
        ],
        out_specs=[row(D_MODEL), row(D_MODEL)],
        out_shape=[jax.ShapeDtypeStruct((n, D_MODEL), F32), jax.ShapeDtypeStruct((n, D_MODEL), BF16)],
        compiler_params=pltpu.CompilerParams(
            dimension_semantics=("arbitrary",), vmem_limit_bytes=VMEM_LIMIT),
        name="merge",
    )(x, h, yh, ym, yr, w_bgate, gate_b, w_branch, w_out, norm2_w)


def _ffn_kernel(*refs, seq_len, has_state, final_norm, tail_rows):
    refs = list(refs)
    x1_ref, h2_ref, wg_ref, wu_ref, wd_ref, cw_ref, cb_ref, nw_ref = refs[:8]
    refs = refs[8:]
    p1_ref = p2_ref = None
    if has_state:
        p1_ref, p2_ref = refs[:2]
        refs = refs[2:]
    if final_norm:
        out_ref, tail_ref, carry_scr, a_scr = refs
    else:
        out_ref, hn_ref, tail_ref, carry_scr, a_scr = refs
    tm = x1_ref.shape[0]
    i = pl.program_id(0)

    @pl.when(i == 0)
    def _():
        carry_scr[...] = jnp.zeros(carry_scr.shape, F32)

    t = (i * tm + lax.broadcasted_iota(jnp.int32, (tm, FF_TILE), 0)) & (seq_len - 1)
    h2 = h2_ref[...]
    acc = x1_ref[...]
    for j in range(D_FF // FF_TILE):
        cols = slice(j * FF_TILE, (j + 1) * FF_TILE)
        a = _dot(h2, wg_ref[:, cols])
        u = _dot(h2, wu_ref[:, cols])
        a_scr[0:SUBLANES, :] = carry_scr[:, cols]
        a_scr[SUBLANES:, :] = a
        carry_scr[:, cols] = a[tm - SUBLANES:, :]
        prev1 = jnp.where(t >= 1, a_scr[SUBLANES - 1:SUBLANES - 1 + tm, :], 0.0)
        prev2 = jnp.where(t >= 2, a_scr[SUBLANES - 2:SUBLANES - 2 + tm, :], 0.0)
        if has_state:
            prev1 = prev1 + p1_ref[:, cols]
            prev2 = prev2 + p2_ref[:, cols]
        conv = prev2 * cw_ref[0:1, cols] + prev1 * cw_ref[1:2, cols] + a * cw_ref[2:3, cols] + cb_ref[:, cols]
        acc = acc + _dot((_silu(conv) * u).astype(BF16), wd_ref[cols, :])
        if tail_rows == tm:
            tail_ref[:, cols] = a
        else:
            tail_ref[0, :, cols] = a[tm - tail_rows:, :]
    if final_norm:
        out_ref[...] = _rms(acc, nw_ref[...])
    else:
        out_ref[...] = acc
        hn_ref[...] = _rms(acc, nw_ref[...]).astype(BF16)


def _ffn(x1, h2, wg, wu, wd, conv_w, conv_b, seq_len, conv_state, norm_w, final_norm):
    n = x1.shape[0]
    has_state = conv_state is not None
    tm = min(TOKEN_TILE // 2 if has_state else TOKEN_TILE, n)
    assert n % tm == 0 and seq_len & (seq_len - 1) == 0
    batch = n // seq_len
    row = lambda width: pl.BlockSpec((tm, width), lambda i: (i, 0))
    full = lambda shape: pl.BlockSpec(shape, lambda i: (0,) * len(shape))
    in_specs = [row(D_MODEL), row(D_MODEL), _resident((D_MODEL, D_FF)), _resident((D_MODEL, D_FF)),
                _resident((D_FF, D_MODEL)), full((CONV_W, D_FF)), full((1, D_FF)), full((1, D_MODEL))]
    args = [x1, h2, wg, wu, wd, conv_w, conv_b, norm_w]
    if has_state:
        p1 = jnp.pad(conv_state[:, 1:2], ((0, 0), (0, seq_len - 1), (0, 0))).reshape(n, D_FF)
        p2 = jnp.pad(conv_state, ((0, 0), (0, seq_len - (CONV_W - 1)), (0, 0))).reshape(n, D_FF)
        in_specs += [row(D_FF), row(D_FF)]
        args += [p1, p2]
    if seq_len % tm == 0:
        tail_rows = CONV_W - 1
        tail_spec = pl.BlockSpec((1, tail_rows, D_FF), lambda i: ((i * tm) // seq_len, 0, 0))
        tail_shape = jax.ShapeDtypeStruct((batch, tail_rows, D_FF), F32)
    else:
        assert tm % seq_len == 0 and seq_len >= CONV_W - 1
        tail_rows = tm
        tail_spec = row(D_FF)
        tail_shape = jax.ShapeDtypeStruct((n, D_FF), F32)
    out_specs, out_shape = [row(D_MODEL)], [jax.ShapeDtypeStruct((n, D_MODEL), F32)]
    if not final_norm:
        out_specs.append(row(D_MODEL))
        out_shape.append(jax.ShapeDtypeStruct((n, D_MODEL), BF16))
    *outs, tail = pl.pallas_call(
        functools.partial(_ffn_kernel, seq_len=seq_len, has_state=has_state,
                          final_norm=final_norm, tail_rows=tail_rows),
        grid=(n // tm,),
        in_specs=in_specs,
        out_specs=out_specs + [tail_spec],
        out_shape=out_shape + [tail_shape],
        scratch_shapes=[pltpu.VMEM((SUBLANES, D_FF), F32), pltpu.VMEM((tm + SUBLANES, FF_TILE), F32)],
        compiler_params=pltpu.CompilerParams(
            dimension_semantics=("arbitrary",), vmem_limit_bytes=VMEM_LIMIT),
        name="conv_ffn",
    )(*args)
    if tail_rows == tm:
        tail = tail.reshape(batch, seq_len, D_FF)[:, seq_len - (CONV_W - 1):, :]
    return outs, tail


def _prep_weights(w_in, w_branch, w_out, ffn_w_gate, ffn_w_up, ffn_w_down):
    w_in = w_in.astype(BF16)
    w_main = jnp.concatenate([w_in[:, :, HG_OFF:ML_GATE_OFF], w_in[:, :, RET_OFF:GATE_OFF]], axis=2)
    w_gate = jnp.pad(w_in[:, :, ML_GATE_OFF:RET_OFF], ((0, 0), (0, 0), (0, LANES - 2 * N_HEAD)))
    return (w_main, w_gate,
            w_in[:, :, GATE_OFF:], w_branch.astype(BF16), w_out.astype(BF16),
            ffn_w_gate.astype(BF16), ffn_w_up.astype(BF16), ffn_w_down.astype(BF16))


def _trunk(x3, pos0, states, params, prepped):
    (norm1_w, mlstm_gate_b, branch_gate_b, hgrn_lb, hgrn_norm_w, mlstm_norm_w, ret_norm_w,
     norm2_w, ffn_conv_w, ffn_conv_b, final_norm_w) = params
    w_main, w_gate, w_bgate, w_branch, w_out, w_ffg, w_ffu, w_ffd = prepped
    batch, seq_len, _ = x3.shape
    n = batch * seq_len
    depth = norm1_w.shape[0]
    geo = _Geo(batch, seq_len)
    cos, sin = (t.reshape(geo.chunks, geo.rows, D_HEAD) for t in _rope_tables(geo.chunks * geo.rows, pos0))

    mat = jax.ShapeDtypeStruct((depth, batch, N_HEAD, D_HEAD, D_HEAD), F32)
    vec = jax.ShapeDtypeStruct((depth, batch, N_HEAD, D_HEAD), F32)
    if states is None:
        hg_in = ml_in = ret_in = conv_in = None
    else:
        hg_in, ml_c, ml_n, ml_m, ret_in, conv_in = states
        ml_in = [ml_c, ml_n, jnp.broadcast_to(ml_m[..., None], vec.shape)]
    hg_out, c_out, r_out, n_out, m_out = mat, mat, mat, vec, vec

    x = x3.reshape(n, D_MODEL)
    h = _norm(x, norm1_w[0:1])
    conv_tails = []
    for l in range(depth):
        last = l == depth - 1
        y_hg, hg_out = _hgrn(geo, l, h, w_main, hgrn_lb, hgrn_norm_w[l:l + 1], hg_in, hg_out)
        gate_b = jnp.pad(mlstm_gate_b[l:l + 1], ((0, 0), (0, LANES - 2 * N_HEAD)))
        y_ml, c_out, n_out, m_out = _mlstm(geo, l, h, w_main, w_gate, gate_b, mlstm_norm_w[l:l + 1], ml_in,
                                           [c_out, n_out, m_out])
        y_ret, r_out = _retention(geo, l, h, w_main, cos, sin, ret_norm_w[l:l + 1], ret_in, r_out)
        x1, h2 = _merge(x, h, y_hg, y_ml, y_ret, w_bgate[l], branch_gate_b[l:l + 1], w_branch[l], w_out[l],
                        norm2_w[l:l + 1])
        outs, conv_tail = _ffn(x1, h2, w_ffg[l], w_ffu[l], w_ffd[l], ffn_conv_w[l], ffn_conv_b[l:l + 1],
                               seq_len, None if conv_in is None else conv_in[l],
                               final_norm_w[None, :] if last else norm1_w[l + 1:l + 2], last)
        x = outs[0]
        h = None if last else outs[1]
        conv_tails.append(conv_tail)
    return x.reshape(batch, seq_len, D_MODEL), (
        hg_out, c_out, n_out, m_out[:, :, :, 0], r_out,
        jnp.stack(conv_tails))


def kernel(x_prompt, x_sample, state_hgrn, state_mlstm_C, state_mlstm_n, state_mlstm_m, state_ret,
           state_ffn_conv, norm1_w, w_in, mlstm_gate_b, branch_gate_b, hgrn_lb, hgrn_norm_w,
           mlstm_norm_w, ret_norm_w, w_branch, w_out, norm2_w, ffn_w_gate, ffn_w_up, ffn_conv_w,
           ffn_conv_b, ffn_w_down, final_norm_w):
    params = (norm1_w, mlstm_gate_b, branch_gate_b, hgrn_lb, hgrn_norm_w, mlstm_norm_w, ret_norm_w,
              norm2_w, ffn_conv_w, ffn_conv_b, final_norm_w)
    prepped = _prep_weights(w_in, w_branch, w_out, ffn_w_gate, ffn_w_up, ffn_w_down)
    y_p, (hg_p, c_p, n_p, m_p, r_p, cv_p) = _trunk(x_prompt, 0, None, params, prepped)
    y_s, (hg_s, c_s, n_s, m_s, r_s, cv_s) = _trunk(
        x_sample, PAST_LEN,
        (state_hgrn, state_mlstm_C, state_mlstm_n, state_mlstm_m, state_ret, state_ffn_conv),
        params, prepped)
    return (y_p, y_s, hg_p, hg_s, c_p, c_s, n_p, n_s, m_p, m_s, r_p, r_s, cv_p, cv_s)
```

```python
import functools
import math

import numpy as np
import jax
import jax.numpy as jnp
from jax import lax
from jax.experimental import pallas as pl
from jax.experimental.pallas import tpu as pltpu

F32 = jnp.float32
BF16 = jnp.bfloat16

D_MODEL = 1024
MIX_W = 512
N_HEAD = 4
D_HEAD = 128
N_BRANCH = 3
D_FF = 2816
CONV_W = 3
CHUNK = 64
EPS = 1e-6
ROPE_BASE = 10000.0
PAST_LEN = 16384

HG_OFF = 0
ML_OFF = 4 * MIX_W
ML_GATE_OFF = ML_OFF + 4 * MIX_W
RET_OFF = ML_GATE_OFF + 2 * N_HEAD
GATE_OFF = RET_OFF + 4 * MIX_W
LOG2E = 1.4426950408889634

LANES = 128
SUBLANES = 8
VMEM_LIMIT = 56 * 1024 * 1024

TOKEN_TILE = 512
NORM_TOKEN_TILE = 1024
FF_TILE = 2816
HG_SUB = 16
GROUP = 32
SEQ_BLOCK = 64
SCAN_UNROLL = 32


def _resident(shape):
    return pl.BlockSpec(shape, lambda *_: (0,) * len(shape), pipeline_mode=pl.Buffered(1))


def _dot(a, b):
    return jnp.dot(a, b, preferred_element_type=F32)


def _dot_tn(a, b):
    return lax.dot_general(a, b, (((0,), (0,)), ((), ())), preferred_element_type=F32)


def _bdot(a, b):
    return jnp.einsum('nlk,nkv->nlv', a, b, preferred_element_type=F32)


def _bdot_nt(a, b):
    return jnp.einsum('nqd,nkd->nqk', a, b, preferred_element_type=F32)


def _bdot_tn(a, b):
    return jnp.stack([_dot_tn(a[i], b[i]) for i in range(a.shape[0])])


def _split3(x):
    hi = x.astype(BF16)
    r = x - hi.astype(F32)
    mid = r.astype(BF16)
    lo = (r - mid.astype(F32)).astype(BF16)
    return hi, mid, lo


def _sigmoid(x):
    return 1.0 / (1.0 + jnp.exp(-x))


def _silu(x):
    return x * _sigmoid(x)


def _log_sigmoid(x):
    return jnp.minimum(x, 0.0) - jnp.log(1.0 + jnp.exp(-jnp.abs(x)))


def _rms(x, w):
    return x * lax.rsqrt(jnp.mean(x * x, axis=-1, keepdims=True) + EPS) * w


def _pad_rows(x, n):
    if x.shape[1] >= n:
        return x
    return jnp.concatenate([x, jnp.zeros((x.shape[0], n - x.shape[1], x.shape[2]), x.dtype)], axis=1)


def _seg_cumsum(x, seg):
    shape = x.shape
    flat = x.reshape(shape[0] * seg, shape[2])
    row = lax.broadcasted_iota(jnp.int32, flat.shape, 0) & (seg - 1)
    s = 1
    while s < seg:
        flat = flat + jnp.where(row >= s, pltpu.roll(flat, s, axis=0), 0.0)
        s *= 2
    return flat.reshape(shape)


def _col_bcast(rows):
    n = rows.shape[0]
    r = lax.broadcasted_iota(jnp.int32, (n, D_HEAD, D_HEAD), 1)
    c = lax.broadcasted_iota(jnp.int32, (n, D_HEAD, D_HEAD), 2)
    d = jnp.where(r == c, jnp.broadcast_to(rows, (n, D_HEAD, D_HEAD)), 0.0).reshape(n * D_HEAD, D_HEAD)
    ones = jnp.ones((D_HEAD, D_HEAD), BF16)
    hi, mid, lo = _split3(d)
    return (_dot(hi, ones) + _dot(mid, ones) + _dot(lo, ones)).reshape(n, D_HEAD, D_HEAD)


def _through_vmem(x):
    def body(ref):
        ref[...] = x
        return ref[...]
    return pl.run_scoped(body, pltpu.VMEM(x.shape, x.dtype))


def _row_bcast(col, width):
    n, rows, _ = col.shape
    vals = _through_vmem(_pad_rows(col, width))
    lane = lax.broadcasted_iota(jnp.int32, (n, rows, LANES), 2)
    pick = _through_vmem(jnp.where(lane == 0, 1.0, 0.0).astype(BF16))
    hi, mid, lo = _split3(vals)
    return _bdot_nt(pick, hi) + _bdot_nt(pick, mid) + _bdot_nt(pick, lo)


def _chunk_geometry(seq_len):
    if seq_len % CHUNK == 0:
        return CHUNK, CHUNK, CHUNK
    valid = math.gcd(seq_len, CHUNK)
    assert valid == seq_len and valid <= SUBLANES, "unsupported sequence length"
    return SUBLANES, valid, LANES


def _rope_kernel(inv_ref, cos_ref, sin_ref, *, pos0):
    shape = cos_ref.shape
    pos = lax.broadcasted_iota(jnp.int32, shape, 0).astype(F32) + pos0
    ang = pos * inv_ref[...]
    lane = lax.broadcasted_iota(jnp.int32, shape, 1)
    cos_ref[...] = jnp.cos(ang)
    sin_ref[...] = jnp.where(lane < D_HEAD // 2, -jnp.sin(ang), jnp.sin(ang))


def _rope_tables(rows, pos0):
    inv = ROPE_BASE ** (-jnp.linspace(0.0, 1.0, D_HEAD // 2, dtype=F32))
    inv2 = jnp.concatenate([inv, inv])[None, :]
    return pl.pallas_call(
        functools.partial(_rope_kernel, pos0=float(pos0)),
        out_shape=(jax.ShapeDtypeStruct((rows, D_HEAD), F32),) * 2,
        name="rope_tables",
    )(inv2)


def _norm_kernel(x_ref, w_ref, wg_ref, h_ref, g_ref):
    hb = _rms(x_ref[...], w_ref[...]).astype(BF16)
    h_ref[...] = hb
    g_ref[...] = _dot(hb, wg_ref[...])


def _norm(x, w, w_gate):
    n = x.shape[0]
    tm = min(NORM_TOKEN_TILE, n)
    assert n % tm == 0
    row = lambda width: pl.BlockSpec((tm, width), lambda i: (i, 0))
    return pl.pallas_call(
        _norm_kernel,
        grid=(n // tm,),
        in_specs=[row(D_MODEL), pl.BlockSpec((1, D_MODEL), lambda i: (0, 0)), _resident((D_MODEL, LANES))],
        out_specs=[row(D_MODEL), row(LANES)],
        out_shape=[jax.ShapeDtypeStruct((n, D_MODEL), BF16), jax.ShapeDtypeStruct((n, LANES), F32)],
        compiler_params=pltpu.CompilerParams(dimension_semantics=("arbitrary",), vmem_limit_bytes=VMEM_LIMIT),
        name="norm_in",
    )(x, w, w_gate)


class _Geo:
    def __init__(self, batch, seq_len):
        self.rows, self.valid, self.width = _chunk_geometry(seq_len)
        if self.valid == self.rows:
            self.seqs, self.chunks = 1, seq_len // self.rows
            self.group = min(GROUP, self.chunks)
        else:
            self.seqs, self.chunks = min(SEQ_BLOCK, batch), 1
            self.group = self.seqs
        assert batch % self.seqs == 0 and self.chunks % min(self.group, self.chunks) == 0
        self.batch = batch
        self.problems = self.seqs * self.chunks
        self.n_groups = self.problems // self.group
        self.group_tokens = self.group * self.valid
        self.tokens = self.problems * self.valid
        self.padded = self.valid < self.rows

    def project(self, h_ref, w, g):
        return _dot(h_ref[pl.ds(g * self.group_tokens, self.group_tokens), :], w)

    def split(self, z, relay):
        n_parts = z.shape[1] // D_HEAD
        cols = lambda p: slice(p * D_HEAD, (p + 1) * D_HEAD)
        if not self.padded:
            return [z[:, cols(p)].reshape(self.group, self.rows, D_HEAD) for p in range(n_parts)]
        out = []
        for p in range(n_parts):
            flat_ref, pad_ref = relay[2 * p], relay[2 * p + 1]
            flat_ref[...] = z[:, cols(p)]
            pad_ref[...] = jnp.zeros(pad_ref.shape, F32)
            for t in range(self.valid):
                pad_ref[pl.ds(t, self.seqs, stride=self.rows), :] = flat_ref[
                    pl.ds(t, self.seqs, stride=self.valid), :]
            out.append(pad_ref[...].reshape(self.seqs, self.rows, D_HEAD))
        return out

    def store(self, y_ref, g, y, relay):
        if not self.padded:
            n = self.group * self.rows
            y_ref[pl.ds(g * n, n), :] = y.reshape(n, D_HEAD).astype(y_ref.dtype)
            return
        pad = relay[1]
        pad[...] = y.reshape(self.seqs * self.rows, D_HEAD)
        for t in range(self.valid):
            y_ref[pl.ds(t, self.seqs, stride=self.valid), :] = pad[pl.ds(t, self.seqs, stride=self.rows), :].astype(
                y_ref.dtype)

    def rows_of(self, g):
        return pl.ds(g * self.group, self.group)

    def row_mask(self, shape):
        if not self.padded:
            return None
        return lax.broadcasted_iota(jnp.int32, shape, 1) < self.valid

    def for_groups(self, body):
        if self.n_groups == 1:
            body(0)
        else:
            lax.fori_loop(0, self.n_groups, lambda g, c: (body(g), c)[1], 0)

    def for_groups_static(self, body):
        for g in range(self.n_groups):
            body(g)

    def scan_chunks(self, step, init):
        if self.chunks == 1:
            return step(pl.ds(0, self.seqs), init)
        return lax.fori_loop(0, self.chunks, lambda c, carry: step(pl.ds(c * self.seqs, self.seqs), carry), init,
                             unroll=min(SCAN_UNROLL, self.chunks))


def _join_columns(w_refs):
    return jnp.concatenate([r[...] for r in w_refs], axis=1)


def _transpose_minor(x):
    return jnp.stack([x[i].T for i in range(x.shape[0])])


def _state_views(refs, kinds, layer, layered):
    head = pl.program_id(1)
    views = []
    for ref, kind in zip(refs, kinds):
        for l in (range(ref.shape[0]) if layered else [layer]):
            view = ref.at[l] if layered else ref
            if kind == 'vec':
                view = view.at[:, pl.ds(head, 1), :]
            if l == layer:
                views.append(view)
            else:
                view[...] = jnp.zeros(view.shape, F32)
    return views


def _split_refs(refs, counts):
    out, k = [], 0
    for c in counts:
        out.append(refs[k:k + c])
        k += c
    return out


def _hgrn_lower_bound(lb_ref, layer):
    x = lb_ref[...]
    e = jnp.exp(x - jnp.max(x, axis=0, keepdims=True))
    soft = e / jnp.sum(e, axis=0, keepdims=True)
    cum = soft[0:1]
    for j in range(1, layer + 1):
        cum = cum + soft[j:j + 1]
    return cum - soft[0:1]


def _hgrn_placement(geo):
    sub = min(HG_SUB, geo.rows)
    r = np.arange(sub * D_HEAD)[:, None] // D_HEAD
    c = np.arange(geo.width)[None, :]
    return jnp.asarray(np.stack([r + i * sub == c for i in range(geo.rows // sub)]), BF16)


def _hgrn_intra(q, k, b2, place_ref, geo):
    n = q.shape[0]
    sub = min(HG_SUB, geo.rows)
    sub_row = lax.broadcasted_iota(jnp.int32, (n, sub, geo.width), 1)
    lane = lax.broadcasted_iota(jnp.int32, (n, sub, geo.width), 2)
    blocks = []
    for i in range(geo.rows // sub):
        r0 = i * sub
        b_i, q_i, k_i = (a[:, r0:r0 + sub] for a in (b2, q, k))
        terms = []
        for s in range(sub):
            lo = s // SUBLANES * SUBLANES
            e = jnp.exp2(jnp.minimum(b_i[:, lo:] - b_i[:, s:s + 1], 0.0))
            term = q_i[:, lo:] * e * k_i[:, s:s + 1]
            if lo:
                term = jnp.concatenate([jnp.zeros((n, lo, D_HEAD), F32), term], axis=1)
            terms.append(term)
        cat = jnp.concatenate(terms, axis=2).reshape(n * sub, sub * D_HEAD)
        a_i = _dot(cat.astype(BF16), place_ref[i]).reshape(n, sub, geo.width)
        a_i = jnp.where(sub_row >= lane - r0, a_i, 0.0)
        if i > 0:
            b_ref = b_i[:, 0:1]
            q_dec = q_i * jnp.exp2(b_i - b_ref)
            k_dec = _pad_rows(k[:, 0:r0] * jnp.exp2(b_ref - b2[:, 0:r0]), geo.width)
            a_i = jnp.where(lane < r0, _bdot_nt(q_dec.astype(BF16), k_dec.astype(BF16)), a_i)
        blocks.append(a_i)
    return blocks[0] if len(blocks) == 1 else jnp.concatenate(blocks, axis=1)


def _hgrn_kernel(*refs, layer, geo, has_state, fresh):
    ((h_ref,), w_refs, (lb_ref, nw_ref, place), s0, _, (y_ref,), s_out,
     (qt_scr, o_scr, u_scr, dc_scr, gate_scr), relay) = _split_refs(
        refs, (1, 4, 3, int(has_state), int(not fresh), 1, 1, 5, len(refs)))
    (s_ref,) = _state_views(s_out, ['mat'], layer, fresh)
    lb = _hgrn_lower_bound(lb_ref, layer)
    rows, width = geo.rows, geo.width
    w = _join_columns(w_refs)

    def phase_a(g):
        hq, hf, hi, hg = geo.split(geo.project(h_ref, w, g), relay)
        q = _silu(hq)
        f = lb + (1.0 - lb) * _sigmoid(hf)
        lf = jnp.log(f)
        k = 1.0 - f
        mask = geo.row_mask(lf.shape)
        if mask is not None:
            lf = jnp.where(mask, lf, 0.0)
            k = jnp.where(mask, k, 0.0)
        b2 = _seg_cumsum(lf, rows) * LOG2E
        sl = geo.rows_of(g)
        gate_scr[sl] = hg
        qt_scr[sl] = (q * jnp.exp2(b2)).astype(BF16)
        a = _hgrn_intra(q, k, b2, place, geo)
        vb = _pad_rows(hi, width).astype(BF16)
        o_scr[sl] = _bdot(a.astype(BF16), vb)
        b_last = b2[:, rows - 1:rows]
        k_w = k * jnp.exp2(b_last - b2)
        u_scr[sl] = _bdot_tn(vb, _pad_rows(k_w, width).astype(BF16))
        dc_scr[sl] = jnp.exp2(b_last)

    geo.for_groups_static(phase_a)

    def step(sl, state_t):
        o_scr[sl] = o_scr[sl] + _bdot_nt(qt_scr[sl], state_t.astype(BF16))
        return dc_scr[sl] * state_t + u_scr[sl]

    init = _transpose_minor(s0[0][...]) if has_state else jnp.zeros((geo.seqs, D_HEAD, D_HEAD), F32)
    s_ref[...] = _transpose_minor(geo.scan_chunks(step, init))

    def epilogue(g):
        sl = geo.rows_of(g)
        geo.store(y_ref, g, _rms(o_scr[sl], nw_ref[...]) * _silu(gate_scr[sl]), relay)

    geo.for_groups(epilogue)


def _mlstm_kernel(*refs, layer, geo, has_state, fresh):
    ((h_ref,), w_refs, (g_ref, gb_ref, nw_ref), st0, _, (y_ref,), st_out,
     (q_scr, v_scr, u_scr, nu_scr, bc_scr, ml_scr, rs_scr, bl_scr, mu_scr, gate_scr), relay) = _split_refs(
        refs, (1, 4, 3, 3 * int(has_state), 3 * int(not fresh), 1, 3, 10, len(refs)))
    kinds = ['mat', 'vec', 'vec']
    c_ref, n_ref, m_ref = _state_views(st_out, kinds, layer, fresh)
    st0 = _state_views(st0, kinds, layer, False) if has_state else st0
    rows, width = geo.rows, geo.width
    head = pl.program_id(1)
    scale = D_HEAD ** -0.5
    t_i = lax.broadcasted_iota(jnp.int32, (1, rows, width), 1)
    s_i = lax.broadcasted_iota(jnp.int32, (1, rows, width), 2)
    ok = s_i <= t_i
    if geo.padded:
        ok = ok & (s_i < geo.valid)
    w = _join_columns(w_refs)

    def phase_a(g):
        q, k, v, og = geo.split(geo.project(h_ref, w, g), relay)
        (gates,) = geo.split(g_ref[pl.ds(g * geo.group_tokens, geo.group_tokens), :], relay)
        gates = gates + gb_ref[...]
        lf_all = _log_sigmoid(gates)
        mask = geo.row_mask(gates.shape)
        if mask is not None:
            lf_all = jnp.where(mask, lf_all, 0.0)
        bc_all = _seg_cumsum(lf_all, rows)
        lane = lax.broadcasted_iota(jnp.int32, gates.shape, 2)
        wide = lambda x: jnp.broadcast_to(x, gates.shape)
        ig = wide(jnp.sum(jnp.where(lane == head, gates, 0.0), axis=2, keepdims=True))
        bc_col = jnp.sum(jnp.where(lane == head + N_HEAD, bc_all, 0.0), axis=2, keepdims=True)
        bc = wide(bc_col)
        log_d = jnp.where(ok, bc_col + _row_bcast(ig - bc, width), -jnp.inf)
        m_col = jnp.max(log_d, axis=2, keepdims=True)
        m_loc = wide(m_col)
        k = k * scale
        qb = q.astype(BF16)
        vb = _pad_rows(v, width).astype(BF16)
        s_m = _bdot_nt(qb, _pad_rows(k, width).astype(BF16)) * jnp.exp(log_d - m_col)
        sl = geo.rows_of(g)
        gate_scr[sl] = og
        q_scr[sl] = qb
        v_scr[sl] = _bdot(s_m.astype(BF16), vb)
        rs_scr[sl] = wide(jnp.sum(s_m, axis=2, keepdims=True))
        bc_scr[sl] = bc
        ml_scr[sl] = m_loc
        b_last = bc[:, rows - 1:rows]
        m_upd = m_loc[:, rows - 1:rows]
        w_exp = b_last - bc + ig - m_upd
        if mask is not None:
            w_exp = jnp.where(mask, w_exp, -jnp.inf)
        k_w = k * jnp.exp(w_exp)
        u_scr[sl] = _bdot_tn(_pad_rows(k_w, width).astype(BF16), vb)
        nu_scr[sl] = jnp.sum(k_w, axis=1, keepdims=True)
        bl_scr[sl] = b_last
        mu_scr[sl] = m_upd

    geo.for_groups_static(phase_a)

    def step(sl, c_state):
        n_state, m_prev = n_ref[...], m_ref[...]
        inter_log = bc_scr[sl] + m_prev
        m_t = jnp.maximum(inter_log, ml_scr[sl])
        w_inter = jnp.exp(inter_log - m_t)
        w_loc = jnp.exp(ml_scr[sl] - m_t)
        qb = q_scr[sl]
        num = w_inter * _bdot(qb, c_state.astype(BF16)) + w_loc * v_scr[sl]
        q_n = jnp.broadcast_to(jnp.sum(qb.astype(F32) * n_state, axis=2, keepdims=True), m_t.shape)
        nq = w_inter * q_n + w_loc * rs_scr[sl]
        v_scr[sl] = num / jnp.maximum(jnp.abs(nq), jnp.exp(-m_t))
        m_new = m_t[:, rows - 1:rows]
        w_prev = jnp.exp(bl_scr[sl] + m_prev - m_new)
        w_upd = jnp.exp(mu_scr[sl] - m_new)
        n_ref[...] = w_prev * n_state + w_upd * nu_scr[sl]
        m_ref[...] = m_new
        return w_prev * c_state + w_upd * u_scr[sl]

    if has_state:
        c_init = st0[0][...]
        n_ref[...] = st0[1][...]
        m_ref[...] = st0[2][...]
    else:
        c_init = jnp.zeros((geo.seqs, D_HEAD, D_HEAD), F32)
        n_ref[...] = jnp.zeros(n_ref.shape, F32)
        m_ref[...] = jnp.zeros(m_ref.shape, F32)
    c_ref[...] = geo.scan_chunks(step, c_init)

    def epilogue(g):
        sl = geo.rows_of(g)
        geo.store(y_ref, g, _rms(_sigmoid(gate_scr[sl]) * v_scr[sl], nw_ref[...]), relay)

    geo.for_groups(epilogue)


def _ret_kernel(*refs, layer, geo, has_state, fresh):
    ((h_ref,), w_refs, (cos_ref, sin_ref, nw_ref), r0, _, (y_ref,), r_out,
     (q_scr, o_scr, u_scr, gate_scr), relay) = _split_refs(
        refs, (1, 4, 3, int(has_state), int(not fresh), 1, 1, 4, len(refs)))
    (r_ref,) = _state_views(r_out, ['mat'], layer, fresh)
    rows, width, valid = geo.rows, geo.width, geo.valid
    head = pl.program_id(1)
    log_gamma = jnp.zeros((1, 1, 1), F32)
    for h in range(N_HEAD):
        log_gamma = jnp.where(head == h, math.log(1.0 - 2.0 ** (-5.0 - h)), log_gamma)
    scale = D_HEAD ** -0.5
    t_i = lax.broadcasted_iota(jnp.int32, (1, rows, width), 1)
    s_i = lax.broadcasted_iota(jnp.int32, (1, rows, width), 2)
    decay = jnp.where(s_i <= t_i, jnp.exp((t_i - s_i).astype(F32) * log_gamma), 0.0)
    t_col = lax.broadcasted_iota(jnp.int32, (1, rows, D_HEAD), 1).astype(F32)
    w_in = jnp.exp((t_col + 1.0) * log_gamma)
    w_st = jnp.exp((valid - 1.0 - t_col) * log_gamma)
    g_all = jnp.exp(valid * log_gamma)
    w = _join_columns(w_refs)

    def rotate(x, cos, sin):
        flat = x.reshape(x.shape[0] * rows, D_HEAD)
        return x * cos + pltpu.roll(flat, D_HEAD // 2, axis=1).reshape(x.shape) * sin

    def phase_a(g):
        q, k, v, rg = geo.split(geo.project(h_ref, w, g), relay)
        if geo.chunks > 1:
            cos, sin = cos_ref[geo.rows_of(g)], sin_ref[geo.rows_of(g)]
        else:
            cos, sin = cos_ref[...], sin_ref[...]
        qr = rotate(q, cos, sin).astype(BF16)
        kr = rotate(k, cos, sin) * scale
        vb = _pad_rows(v, width).astype(BF16)
        a = _bdot_nt(qr, _pad_rows(kr, width).astype(BF16)) * decay
        sl = geo.rows_of(g)
        gate_scr[sl] = rg
        q_scr[sl] = qr
        o_scr[sl] = _bdot(a.astype(BF16), vb)
        u_scr[sl] = _bdot_tn(_pad_rows(kr * w_st, width).astype(BF16), vb)

    geo.for_groups_static(phase_a)

    def step(sl, state):
        o_scr[sl] = o_scr[sl] + _bdot(q_scr[sl], state.astype(BF16)) * w_in
        return g_all * state + u_scr[sl]

    init = r0[0][...] if has_state else jnp.zeros((geo.seqs, D_HEAD, D_HEAD), F32)
    r_ref[...] = geo.scan_chunks(step, init)

    def epilogue(g):
        sl = geo.rows_of(g)
        geo.store(y_ref, g, _rms(o_scr[sl], nw_ref[...]) * _silu(gate_scr[sl]), relay)

    geo.for_groups(epilogue)


def _mixer_call(kernel, name, geo, layer, h, weights, extra, states_in, states_prev, scratch):
    seqs, n_cols = geo.seqs, len(weights) * D_HEAD
    fresh = isinstance(states_prev[0], jax.ShapeDtypeStruct)

    def weight_spec(first, per_head):
        return pl.BlockSpec((None, D_MODEL, D_HEAD), lambda bi, j: (layer, 0, first + per_head * j))

    def state_spec(arr, layered=False):
        lead, first = (arr.shape[0], 0) if layered else (None, layer)
        if len(arr.shape) == 5:
            return pl.BlockSpec((lead, seqs, None, D_HEAD, D_HEAD), lambda bi, j: (first, bi, j, 0, 0))
        return pl.BlockSpec((lead, seqs, N_HEAD, D_HEAD), lambda bi, j: (first, bi, 0, 0))

    has_state = states_in is not None
    args = [h] + [a for a, _, _ in weights] + [a for a, _ in extra]
    in_specs = [pl.BlockSpec((geo.tokens, D_MODEL), lambda bi, j: (bi, 0))] + [
        weight_spec(first, per_head) for _, first, per_head in weights] + [s for _, s in extra]
    if has_state:
        args += list(states_in)
        in_specs += [state_spec(a) for a in states_in]
    n_in = len(args)
    if not fresh:
        args += list(states_prev)
        in_specs += [pl.BlockSpec(memory_space=pl.ANY) for _ in states_prev]
    out_specs = [pl.BlockSpec((geo.tokens, D_HEAD), lambda bi, j: (bi, j))] + [
        state_spec(a, fresh) for a in states_prev]
    out_shape = [jax.ShapeDtypeStruct((h.shape[0], MIX_W), F32 if geo.padded else BF16)] + [
        jax.ShapeDtypeStruct(a.shape, F32) for a in states_prev]
    scratch_shapes = list(scratch) + [pltpu.VMEM((geo.problems, geo.rows, D_HEAD), F32)]
    if geo.padded:
        scratch_shapes += [pltpu.VMEM((geo.group_tokens, D_HEAD), F32),
                           pltpu.VMEM((seqs * geo.rows, D_HEAD), F32)] * (n_cols // D_HEAD)
    return pl.pallas_call(
        functools.partial(kernel, layer=layer, geo=geo, has_state=has_state, fresh=fresh),
        grid=(geo.batch // seqs, N_HEAD),
        in_specs=in_specs,
        out_specs=out_specs,
        out_shape=out_shape,
        scratch_shapes=scratch_shapes,
        input_output_aliases={} if fresh else {n_in + i: 1 + i for i in range(len(states_prev))},
        compiler_params=pltpu.CompilerParams(
            dimension_semantics=("arbitrary", "arbitrary"), vmem_limit_bytes=VMEM_LIMIT),
        name=name,
    )(*args)


def _head_block(rows):
    return pl.BlockSpec((rows, D_HEAD), lambda bi, j: (0, j))


def _part_blocks(w_main, mixer):
    return [(w_main, (mixer * 4 + p) * N_HEAD, 1) for p in range(4)]


def _hgrn(geo, layer, h, w_main, hgrn_lb, nw, state_in, state_prev):
    n = geo.problems
    scratch = [pltpu.VMEM((n, geo.rows, D_HEAD), BF16), pltpu.VMEM((n, geo.rows, D_HEAD), F32),
               pltpu.VMEM((n, D_HEAD, D_HEAD), F32), pltpu.VMEM((n, 1, D_HEAD), F32)]
    place = _hgrn_placement(geo)
    return _mixer_call(
        _hgrn_kernel, "hgrn2", geo, layer, h, _part_blocks(w_main, 0),
        [(hgrn_lb, _head_block(hgrn_lb.shape[0])), (nw, _head_block(1)), (place, _resident(place.shape))],
        None if state_in is None else [state_in], [state_prev], scratch)


def _mlstm(geo, layer, h, w_main, gates, gate_b, nw, states_in, states_prev):
    n = geo.problems
    col = lambda r: pltpu.VMEM((n, r, LANES), F32)
    scratch = [pltpu.VMEM((n, geo.rows, D_HEAD), BF16), pltpu.VMEM((n, geo.rows, D_HEAD), F32),
               pltpu.VMEM((n, D_HEAD, D_HEAD), F32), pltpu.VMEM((n, 1, D_HEAD), F32),
               col(geo.rows), col(geo.rows), col(geo.rows), col(1), col(1)]
    return _mixer_call(
        _mlstm_kernel, "mlstm", geo, layer, h, _part_blocks(w_main, 1),
        [(gates, pl.BlockSpec((geo.tokens, LANES), lambda bi, j: (bi, 0))),
         (gate_b, pl.BlockSpec((1, LANES), lambda bi, j: (0, 0))), (nw, _head_block(1))],
        states_in, states_prev, scratch)


def _retention(geo, layer, h, w_main, cos, sin, nw, state_in, state_prev):
    n = geo.problems
    scratch = [pltpu.VMEM((n, geo.rows, D_HEAD), BF16), pltpu.VMEM((n, geo.rows, D_HEAD), F32),
               pltpu.VMEM((n, D_HEAD, D_HEAD), F32)]
    table = pl.BlockSpec(cos.shape, lambda bi, j: (0, 0, 0))
    return _mixer_call(
        _ret_kernel, "retention", geo, layer, h, _part_blocks(w_main, 2),
        [(cos, table), (sin, table), (nw, _head_block(1))],
        None if state_in is None else [state_in], [state_prev], scratch)


def _merge_kernel(x_ref, h_ref, yh_ref, ym_ref, yr_ref, wg_ref, bg_ref, wb_ref, wo_ref, n2_ref, x1_ref, h2_ref):
    hb = h_ref[...]
    merged = None
    for j, y_ref in enumerate((yh_ref, ym_ref, yr_ref)):
        cols = slice(j * D_MODEL, (j + 1) * D_MODEL)
        gate = _sigmoid(_dot(hb, wg_ref[:, cols]) + bg_ref[:, cols])
        term = gate * _dot(y_ref[...].astype(BF16), wb_ref[j])
        merged = term if merged is None else merged + term
    x1 = x_ref[...] + _dot(merged.astype(BF16), wo_ref[...])
    x1_ref[...] = x1
    h2_ref[...] = _rms(x1, n2_ref[...]).astype(BF16)


def _merge(x, h, yh, ym, yr, w_bgate, gate_b, w_branch, w_out, norm2_w):
    n = x.shape[0]
    tm = min(TOKEN_TILE, n)
    gate_w = N_BRANCH * D_MODEL
    row = lambda width: pl.BlockSpec((tm, width), lambda i: (i, 0))
    vec = lambda width: pl.BlockSpec((1, width), lambda i: (0, 0))
    return pl.pallas_call(
        _merge_kernel,
        grid=(n // tm,),
        in_specs=[
            row(D_MODEL), row(D_MODEL), row(MIX_W), row(MIX_W), row(MIX_W),
            _resident((D_MODEL, gate_w)), vec(gate_w),
            _resident((N_BRANCH, MIX_W, D_MODEL)),
            _resident((D_MODEL, D_MODEL)),
            vec(D_MODEL),
        ],
        out_specs=[row(D_MODEL), row(D_MODEL)],
        out_shape=[jax.ShapeDtypeStruct((n, D_MODEL), F32), jax.ShapeDtypeStruct((n, D_MODEL), BF16)],
        compiler_params=pltpu.CompilerParams(
            dimension_semantics=("arbitrary",), vmem_limit_bytes=VMEM_LIMIT),
        name="merge",
    )(x, h, yh, ym, yr, w_bgate, gate_b, w_branch, w_out, norm2_w)


def _ffn_kernel(*refs, seq_len, has_state, final_norm, tail_rows):
    refs = list(refs)
    x1_ref, h2_ref, wg_ref, wu_ref, wd_ref, cw_ref, cb_ref, nw_ref = refs[:8]
    refs = refs[8:]
    p1_ref = p2_ref = None
    if has_state:
        p1_ref, p2_ref = refs[:2]
        refs = refs[2:]
    if final_norm:
        out_ref, tail_ref, carry_scr, a_scr = refs
    else:
        wgn_ref, out_ref, hn_ref, gn_ref, tail_ref, carry_scr, a_scr = refs
    tm = x1_ref.shape[0]
    i = pl.program_id(0)

    @pl.when(i == 0)
    def _():
        carry_scr[...] = jnp.zeros(carry_scr.shape, F32)

    t = (i * tm + lax.broadcasted_iota(jnp.int32, (tm, FF_TILE), 0)) & (seq_len - 1)
    h2 = h2_ref[...]
    acc = x1_ref[...]
    for j in range(D_FF // FF_TILE):
        cols = slice(j * FF_TILE, (j + 1) * FF_TILE)
        a = _dot(h2, wg_ref[:, cols])
        u = _dot(h2, wu_ref[:, cols])
        a_scr[0:SUBLANES, :] = carry_scr[:, cols]
        a_scr[SUBLANES:, :] = a
        carry_scr[:, cols] = a[tm - SUBLANES:, :]
        prev1 = jnp.where(t >= 1, a_scr[SUBLANES - 1:SUBLANES - 1 + tm, :], 0.0)
        prev2 = jnp.where(t >= 2, a_scr[SUBLANES - 2:SUBLANES - 2 + tm, :], 0.0)
        if has_state:
            prev1 = prev1 + p1_ref[:, cols]
            prev2 = prev2 + p2_ref[:, cols]
        conv = prev2 * cw_ref[0:1, cols] + prev1 * cw_ref[1:2, cols] + a * cw_ref[2:3, cols] + cb_ref[:, cols]
        acc = acc + _dot((_silu(conv) * u).astype(BF16), wd_ref[cols, :])
        if tail_rows == tm:
            tail_ref[:, cols] = a
        else:
            tail_ref[0, :, cols] = a[tm - tail_rows:, :]
    if final_norm:
        out_ref[...] = _rms(acc, nw_ref[...])
    else:
        out_ref[...] = acc
        hn = _rms(acc, nw_ref[...]).astype(BF16)
        hn_ref[...] = hn
        gn_ref[...] = _dot(hn, wgn_ref[...])


def _ffn(x1, h2, wg, wu, wd, conv_w, conv_b, seq_len, conv_state, norm_w, next_w_gate):
    final_norm = next_w_gate is None
    n = x1.shape[0]
    has_state = conv_state is not None
    tm = min(TOKEN_TILE // 2 if has_state else TOKEN_TILE, n)
    assert n % tm == 0 and seq_len & (seq_len - 1) == 0
    batch = n // seq_len
    row = lambda width: pl.BlockSpec((tm, width), lambda i: (i, 0))
    full = lambda shape: pl.BlockSpec(shape, lambda i: (0,) * len(shape))
    in_specs = [row(D_MODEL), row(D_MODEL), _resident((D_MODEL, D_FF)), _resident((D_MODEL, D_FF)),
                _resident((D_FF, D_MODEL)), full((CONV_W, D_FF)), full((1, D_FF)), full((1, D_MODEL))]
    args = [x1, h2, wg, wu, wd, conv_w, conv_b, norm_w]
    if has_state:
        p1 = jnp.pad(conv_state[:, 1:2], ((0, 0), (0, seq_len - 1), (0, 0))).reshape(n, D_FF)
        p2 = jnp.pad(conv_state, ((0, 0), (0, seq_len - (CONV_W - 1)), (0, 0))).reshape(n, D_FF)
        in_specs += [row(D_FF), row(D_FF)]
        args += [p1, p2]
    if seq_len % tm == 0:
        tail_rows = CONV_W - 1
        tail_spec = pl.BlockSpec((1, tail_rows, D_FF), lambda i: ((i * tm) // seq_len, 0, 0))
        tail_shape = jax.ShapeDtypeStruct((batch, tail_rows, D_FF), F32)
    else:
        assert tm % seq_len == 0 and seq_len >= CONV_W - 1
        tail_rows = tm
        tail_spec = row(D_FF)
        tail_shape = jax.ShapeDtypeStruct((n, D_FF), F32)
    out_specs, out_shape = [row(D_MODEL)], [jax.ShapeDtypeStruct((n, D_MODEL), F32)]
    if not final_norm:
        in_specs.append(_resident((D_MODEL, LANES)))
        args.append(next_w_gate)
        out_specs += [row(D_MODEL), row(LANES)]
        out_shape += [jax.ShapeDtypeStruct((n, D_MODEL), BF16), jax.ShapeDtypeStruct((n, LANES), F32)]
    *outs, tail = pl.pallas_call(
        functools.partial(_ffn_kernel, seq_len=seq_len, has_state=has_state,
                          final_norm=final_norm, tail_rows=tail_rows),
        grid=(n // tm,),
        in_specs=in_specs,
        out_specs=out_specs + [tail_spec],
        out_shape=out_shape + [tail_shape],
        scratch_shapes=[pltpu.VMEM((SUBLANES, D_FF), F32), pltpu.VMEM((tm + SUBLANES, FF_TILE), F32)],
        compiler_params=pltpu.CompilerParams(
            dimension_semantics=("arbitrary",), vmem_limit_bytes=VMEM_LIMIT),
        name="conv_ffn",
    )(*args)
    if tail_rows == tm:
        tail = tail.reshape(batch, seq_len, D_FF)[:, seq_len - (CONV_W - 1):, :]
    return outs, tail


def _prep_weights(w_in, w_branch, w_out, ffn_w_gate, ffn_w_up, ffn_w_down):
    w_in = w_in.astype(BF16)
    w_main = jnp.concatenate([w_in[:, :, HG_OFF:ML_GATE_OFF], w_in[:, :, RET_OFF:GATE_OFF]], axis=2)
    w_gate = jnp.pad(w_in[:, :, ML_GATE_OFF:RET_OFF], ((0, 0), (0, 0), (0, LANES - 2 * N_HEAD)))
    return (w_main, w_gate,
            w_in[:, :, GATE_OFF:], w_branch.astype(BF16), w_out.astype(BF16),
            ffn_w_gate.astype(BF16), ffn_w_up.astype(BF16), ffn_w_down.astype(BF16))


def _trunk(x3, pos0, states, params, prepped):
    (norm1_w, mlstm_gate_b, branch_gate_b, hgrn_lb, hgrn_norm_w, mlstm_norm_w, ret_norm_w,
     norm2_w, ffn_conv_w, ffn_conv_b, final_norm_w) = params
    w_main, w_gate, w_bgate, w_branch, w_out, w_ffg, w_ffu, w_ffd = prepped
    batch, seq_len, _ = x3.shape
    n = batch * seq_len
    depth = norm1_w.shape[0]
    geo = _Geo(batch, seq_len)
    cos, sin = (t.reshape(geo.chunks, geo.rows, D_HEAD) for t in _rope_tables(geo.chunks * geo.rows, pos0))

    mat = jax.ShapeDtypeStruct((depth, batch, N_HEAD, D_HEAD, D_HEAD), F32)
    vec = jax.ShapeDtypeStruct((depth, batch, N_HEAD, D_HEAD), F32)
    if states is None:
        hg_in = ml_in = ret_in = conv_in = None
    else:
        hg_in, ml_c, ml_n, ml_m, ret_in, conv_in = states
        ml_in = [ml_c, ml_n, jnp.broadcast_to(ml_m[..., None], vec.shape)]
    hg_out, c_out, r_out, n_out, m_out = mat, mat, mat, vec, vec

    x = x3.reshape(n, D_MODEL)
    h, gates = _norm(x, norm1_w[0:1], w_gate[0])
    conv_tails = []
    for l in range(depth):
        last = l == depth - 1
        y_hg, hg_out = _hgrn(geo, l, h, w_main, hgrn_lb, hgrn_norm_w[l:l + 1], hg_in, hg_out)
        gate_b = jnp.pad(mlstm_gate_b[l:l + 1], ((0, 0), (0, LANES - 2 * N_HEAD)))
        y_ml, c_out, n_out, m_out = _mlstm(geo, l, h, w_main, gates, gate_b, mlstm_norm_w[l:l + 1], ml_in,
                                           [c_out, n_out, m_out])
        y_ret, r_out = _retention(geo, l, h, w_main, cos, sin, ret_norm_w[l:l + 1], ret_in, r_out)
        x1, h2 = _merge(x, h, y_hg, y_ml, y_ret, w_bgate[l], branch_gate_b[l:l + 1], w_branch[l], w_out[l],
                        norm2_w[l:l + 1])
        outs, conv_tail = _ffn(x1, h2, w_ffg[l], w_ffu[l], w_ffd[l], ffn_conv_w[l], ffn_conv_b[l:l + 1],
                               seq_len, None if conv_in is None else conv_in[l],
                               final_norm_w[None, :] if last else norm1_w[l + 1:l + 2],
                               None if last else w_gate[l + 1])
        x = outs[0]
        h, gates = (None, None) if last else outs[1:]
        conv_tails.append(conv_tail)
    return x.reshape(batch, seq_len, D_MODEL), (
        hg_out, c_out, n_out, m_out[:, :, :, 0], r_out,
        jnp.stack(conv_tails))


def kernel(x_prompt, x_sample, state_hgrn, state_mlstm_C, state_mlstm_n, state_mlstm_m, state_ret,
           state_ffn_conv, norm1_w, w_in, mlstm_gate_b, branch_gate_b, hgrn_lb, hgrn_norm_w,
           mlstm_norm_w, ret_norm_w, w_branch, w_out, norm2_w, ffn_w_gate, ffn_w_up, ffn_conv_w,
           ffn_conv_b, ffn_w_down, final_norm_w):
    params = (norm1_w, mlstm_gate_b, branch_gate_b, hgrn_lb, hgrn_norm_w, mlstm_norm_w, ret_norm_w,
              norm2_w, ffn_conv_w, ffn_conv_b, final_norm_w)
    prepped = _prep_weights(w_in, w_branch, w_out, ffn_w_gate, ffn_w_up, ffn_w_down)
    y_p, (hg_p, c_p, n_p, m_p, r_p, cv_p) = _trunk(x_prompt, 0, None, params, prepped)
    y_s, (hg_s, c_s, n_s, m_s, r_s, cv_s) = _trunk(
        x_sample, PAST_LEN,
        (state_hgrn, state_mlstm_C, state_mlstm_n, state_mlstm_m, state_ret, state_ffn_conv),
        params, prepped)
    return (y_p, y_s, hg_p, hg_s, c_p, c_s, n_p, n_s, m_p, m_s, r_p, r_s, cv_p, cv_s)
```

```python
import functools
import math

import numpy as np
import jax
import jax.numpy as jnp
from jax import lax
from jax.experimental import pallas as pl
from jax.experimental.pallas import tpu as pltpu

F32 = jnp.float32
BF16 = jnp.bfloat16

D_MODEL = 1024
MIX_W = 512
N_HEAD = 4
D_HEAD = 128
N_BRANCH = 3
D_FF = 2816
CONV_W = 3
CHUNK = 64
EPS = 1e-6
ROPE_BASE = 10000.0
PAST_LEN = 16384

HG_OFF = 0
ML_OFF = 4 * MIX_W
ML_GATE_OFF = ML_OFF + 4 * MIX_W
RET_OFF = ML_GATE_OFF + 2 * N_HEAD
GATE_OFF = RET_OFF + 4 * MIX_W
LOG2E = 1.4426950408889634

LANES = 128
SUBLANES = 8
VMEM_LIMIT = 56 * 1024 * 1024

TOKEN_TILE = 512
NORM_TOKEN_TILE = 1024
FF_TILE = 2816
HG_SUB = 16
GROUP = 32
SEQ_BLOCK = 64
SCAN_UNROLL = 32


def _resident(shape):
    return pl.BlockSpec(shape, lambda *_: (0,) * len(shape), pipeline_mode=pl.Buffered(1))


def _dot(a, b):
    return jnp.dot(a, b, preferred_element_type=F32)


def _dot_tn(a, b):
    return lax.dot_general(a, b, (((0,), (0,)), ((), ())), preferred_element_type=F32)


def _bdot(a, b):
    return jnp.einsum('nlk,nkv->nlv', a, b, preferred_element_type=F32)


def _bdot_nt(a, b):
    return jnp.einsum('nqd,nkd->nqk', a, b, preferred_element_type=F32)


def _bdot_tn(a, b):
    return jnp.stack([_dot_tn(a[i], b[i]) for i in range(a.shape[0])])


def _split3(x):
    hi = x.astype(BF16)
    r = x - hi.astype(F32)
    mid = r.astype(BF16)
    lo = (r - mid.astype(F32)).astype(BF16)
    return hi, mid, lo


def _sigmoid(x):
    return 1.0 / (1.0 + jnp.exp(-x))


def _silu(x):
    return x * _sigmoid(x)


def _log_sigmoid(x):
    return jnp.minimum(x, 0.0) - jnp.log(1.0 + jnp.exp(-jnp.abs(x)))


def _rms(x, w):
    return x * lax.rsqrt(jnp.mean(x * x, axis=-1, keepdims=True) + EPS) * w


def _pad_rows(x, n):
    if x.shape[1] >= n:
        return x
    return jnp.concatenate([x, jnp.zeros((x.shape[0], n - x.shape[1], x.shape[2]), x.dtype)], axis=1)


def _seg_cumsum(x, seg):
    shape = x.shape
    flat = x.reshape(shape[0] * seg, shape[2])
    row = lax.broadcasted_iota(jnp.int32, flat.shape, 0) & (seg - 1)
    s = 1
    while s < seg:
        flat = flat + jnp.where(row >= s, pltpu.roll(flat, s, axis=0), 0.0)
        s *= 2
    return flat.reshape(shape)


def _col_bcast(rows):
    n = rows.shape[0]
    r = lax.broadcasted_iota(jnp.int32, (n, D_HEAD, D_HEAD), 1)
    c = lax.broadcasted_iota(jnp.int32, (n, D_HEAD, D_HEAD), 2)
    d = jnp.where(r == c, jnp.broadcast_to(rows, (n, D_HEAD, D_HEAD)), 0.0).reshape(n * D_HEAD, D_HEAD)
    ones = jnp.ones((D_HEAD, D_HEAD), BF16)
    hi, mid, lo = _split3(d)
    return (_dot(hi, ones) + _dot(mid, ones) + _dot(lo, ones)).reshape(n, D_HEAD, D_HEAD)


def _through_vmem(x):
    def body(ref):
        ref[...] = x
        return ref[...]
    return pl.run_scoped(body, pltpu.VMEM(x.shape, x.dtype))


def _row_bcast(col, width):
    n, rows, _ = col.shape
    vals = _through_vmem(_pad_rows(col, width))
    lane = lax.broadcasted_iota(jnp.int32, (n, rows, LANES), 2)
    pick = _through_vmem(jnp.where(lane == 0, 1.0, 0.0).astype(BF16))
    hi, mid, lo = _split3(vals)
    return _bdot_nt(pick, hi) + _bdot_nt(pick, mid) + _bdot_nt(pick, lo)


def _chunk_geometry(seq_len):
    if seq_len % CHUNK == 0:
        return CHUNK, CHUNK, CHUNK
    valid = math.gcd(seq_len, CHUNK)
    assert valid == seq_len and valid <= SUBLANES, "unsupported sequence length"
    return SUBLANES, valid, LANES


def _rope_kernel(inv_ref, cos_ref, sin_ref, *, pos0):
    shape = cos_ref.shape
    pos = lax.broadcasted_iota(jnp.int32, shape, 0).astype(F32) + pos0
    ang = pos * inv_ref[...]
    lane = lax.broadcasted_iota(jnp.int32, shape, 1)
    cos_ref[...] = jnp.cos(ang)
    sin_ref[...] = jnp.where(lane < D_HEAD // 2, -jnp.sin(ang), jnp.sin(ang))


def _rope_tables(rows, pos0):
    inv = ROPE_BASE ** (-jnp.linspace(0.0, 1.0, D_HEAD // 2, dtype=F32))
    inv2 = jnp.concatenate([inv, inv])[None, :]
    return pl.pallas_call(
        functools.partial(_rope_kernel, pos0=float(pos0)),
        out_shape=(jax.ShapeDtypeStruct((rows, D_HEAD), F32),) * 2,
        name="rope_tables",
    )(inv2)


def _norm_kernel(x_ref, w_ref, wg_ref, h_ref, g_ref):
    hb = _rms(x_ref[...], w_ref[...]).astype(BF16)
    h_ref[...] = hb
    g_ref[...] = _dot(hb, wg_ref[...])


def _norm(x, w, w_gate):
    n = x.shape[0]
    tm = min(NORM_TOKEN_TILE, n)
    assert n % tm == 0
    row = lambda width: pl.BlockSpec((tm, width), lambda i: (i, 0))
    return pl.pallas_call(
        _norm_kernel,
        grid=(n // tm,),
        in_specs=[row(D_MODEL), pl.BlockSpec((1, D_MODEL), lambda i: (0, 0)), _resident((D_MODEL, LANES))],
        out_specs=[row(D_MODEL), row(LANES)],
        out_shape=[jax.ShapeDtypeStruct((n, D_MODEL), BF16), jax.ShapeDtypeStruct((n, LANES), F32)],
        compiler_params=pltpu.CompilerParams(dimension_semantics=("arbitrary",), vmem_limit_bytes=VMEM_LIMIT),
        name="norm_in",
    )(x, w, w_gate)


class _Geo:
    def __init__(self, batch, seq_len):
        self.rows, self.valid, self.width = _chunk_geometry(seq_len)
        if self.valid == self.rows:
            self.seqs, self.chunks = 1, seq_len // self.rows
            self.group = min(GROUP, self.chunks)
        else:
            self.seqs, self.chunks = min(SEQ_BLOCK, batch), 1
            self.group = self.seqs
        assert batch % self.seqs == 0 and self.chunks % min(self.group, self.chunks) == 0
        self.batch = batch
        self.problems = self.seqs * self.chunks
        self.n_groups = self.problems // self.group
        self.group_tokens = self.group * self.valid
        self.tokens = self.problems * self.valid
        self.padded = self.valid < self.rows

    def project(self, h_ref, w, g):
        return _dot(h_ref[pl.ds(g * self.group_tokens, self.group_tokens), :], w)

    def split(self, z, relay):
        n_parts = z.shape[1] // D_HEAD
        cols = lambda p: slice(p * D_HEAD, (p + 1) * D_HEAD)
        if not self.padded:
            return [z[:, cols(p)].reshape(self.group, self.rows, D_HEAD) for p in range(n_parts)]
        out = []
        for p in range(n_parts):
            flat_ref, pad_ref = relay[2 * p], relay[2 * p + 1]
            flat_ref[...] = z[:, cols(p)]
            pad_ref[...] = jnp.zeros(pad_ref.shape, F32)
            for t in range(self.valid):
                pad_ref[pl.ds(t, self.seqs, stride=self.rows), :] = flat_ref[
                    pl.ds(t, self.seqs, stride=self.valid), :]
            out.append(pad_ref[...].reshape(self.seqs, self.rows, D_HEAD))
        return out

    def store(self, y_ref, g, y, relay):
        if not self.padded:
            n = self.group * self.rows
            y_ref[pl.ds(g * n, n), :] = y.reshape(n, D_HEAD).astype(y_ref.dtype)
            return
        pad = relay[1]
        pad[...] = y.reshape(self.seqs * self.rows, D_HEAD)
        for t in range(self.valid):
            y_ref[pl.ds(t, self.seqs, stride=self.valid), :] = pad[pl.ds(t, self.seqs, stride=self.rows), :].astype(
                y_ref.dtype)

    def rows_of(self, g):
        return pl.ds(g * self.group, self.group)

    def row_mask(self, shape):
        if not self.padded:
            return None
        return lax.broadcasted_iota(jnp.int32, shape, 1) < self.valid

    def for_groups(self, body):
        if self.n_groups == 1:
            body(0)
        else:
            lax.fori_loop(0, self.n_groups, lambda g, c: (body(g), c)[1], 0)

    def for_groups_static(self, body):
        for g in range(self.n_groups):
            body(g)

    def scan_chunks(self, step, init):
        if self.chunks == 1:
            return step(pl.ds(0, self.seqs), init)
        return lax.fori_loop(0, self.chunks, lambda c, carry: step(pl.ds(c * self.seqs, self.seqs), carry), init,
                             unroll=min(SCAN_UNROLL, self.chunks))


def _join_columns(w_refs):
    return jnp.concatenate([r[...] for r in w_refs], axis=1)


def _transpose_minor(x):
    return jnp.stack([x[i].T for i in range(x.shape[0])])


def _state_views(refs, kinds, layer, layered):
    head = pl.program_id(1)
    views = []
    for ref, kind in zip(refs, kinds):
        for l in (range(ref.shape[0]) if layered else [layer]):
            view = ref.at[l] if layered else ref
            if kind == 'vec':
                view = view.at[:, pl.ds(head, 1), :]
            if l == layer:
                views.append(view)
            else:
                view[...] = jnp.zeros(view.shape, F32)
    return views


def _split_refs(refs, counts):
    out, k = [], 0
    for c in counts:
        out.append(refs[k:k + c])
        k += c
    return out


def _hgrn_lower_bound(lb_ref, layer):
    x = lb_ref[...]
    e = jnp.exp(x - jnp.max(x, axis=0, keepdims=True))
    soft = e / jnp.sum(e, axis=0, keepdims=True)
    cum = soft[0:1]
    for j in range(1, layer + 1):
        cum = cum + soft[j:j + 1]
    return cum - soft[0:1]


def _hgrn_placement(geo):
    sub = min(HG_SUB, geo.rows)
    r = np.arange(sub * D_HEAD)[:, None] // D_HEAD
    c = np.arange(geo.width)[None, :]
    return jnp.asarray(np.stack([r + i * sub == c for i in range(geo.rows // sub)]), BF16)


def _hgrn_intra(q, k, b2, place_ref, geo):
    n = q.shape[0]
    sub = min(HG_SUB, geo.rows)
    sub_row = lax.broadcasted_iota(jnp.int32, (n, sub, geo.width), 1)
    lane = lax.broadcasted_iota(jnp.int32, (n, sub, geo.width), 2)
    blocks = []
    for i in range(geo.rows // sub):
        r0 = i * sub
        b_i, q_i, k_i = (a[:, r0:r0 + sub] for a in (b2, q, k))
        terms = []
        for s in range(sub):
            lo = s // SUBLANES * SUBLANES
            e = jnp.exp2(jnp.minimum(b_i[:, lo:] - b_i[:, s:s + 1], 0.0))
            term = q_i[:, lo:] * e * k_i[:, s:s + 1]
            if lo:
                term = jnp.concatenate([jnp.zeros((n, lo, D_HEAD), F32), term], axis=1)
            terms.append(term)
        cat = jnp.concatenate(terms, axis=2).reshape(n * sub, sub * D_HEAD)
        a_i = _dot(cat.astype(BF16), place_ref[i]).reshape(n, sub, geo.width)
        a_i = jnp.where(sub_row >= lane - r0, a_i, 0.0)
        if i > 0:
            b_ref = b_i[:, 0:1]
            q_dec = q_i * jnp.exp2(b_i - b_ref)
            k_dec = _pad_rows(k[:, 0:r0] * jnp.exp2(b_ref - b2[:, 0:r0]), geo.width)
            a_i = jnp.where(lane < r0, _bdot_nt(q_dec.astype(BF16), k_dec.astype(BF16)), a_i)
        blocks.append(a_i)
    return blocks[0] if len(blocks) == 1 else jnp.concatenate(blocks, axis=1)


def _hgrn_kernel(*refs, layer, geo, has_state, fresh):
    ((h_ref,), w_refs, (lb_ref, nw_ref, place), s0, _, (y_ref,), s_out,
     (qt_scr, o_scr, u_scr, dc_scr, gate_scr), relay) = _split_refs(
        refs, (1, 4, 3, int(has_state), int(not fresh), 1, 1, 5, len(refs)))
    (s_ref,) = _state_views(s_out, ['mat'], layer, fresh)
    lb = _hgrn_lower_bound(lb_ref, layer)
    rows, width = geo.rows, geo.width
    w = _join_columns(w_refs)

    def phase_a(g):
        hq, hf, hi, hg = geo.split(geo.project(h_ref, w, g), relay)
        q = _silu(hq)
        f = lb + (1.0 - lb) * _sigmoid(hf)
        lf = jnp.log(f)
        k = 1.0 - f
        mask = geo.row_mask(lf.shape)
        if mask is not None:
            lf = jnp.where(mask, lf, 0.0)
            k = jnp.where(mask, k, 0.0)
        b2 = _seg_cumsum(lf, rows) * LOG2E
        sl = geo.rows_of(g)
        gate_scr[sl] = hg
        qt_scr[sl] = (q * jnp.exp2(b2)).astype(BF16)
        a = _hgrn_intra(q, k, b2, place, geo)
        vb = _pad_rows(hi, width).astype(BF16)
        o_scr[sl] = _bdot(a.astype(BF16), vb)
        b_last = b2[:, rows - 1:rows]
        k_w = k * jnp.exp2(b_last - b2)
        u_scr[sl] = _bdot_tn(vb, _pad_rows(k_w, width).astype(BF16))
        dc_scr[sl] = jnp.exp2(b_last)

    geo.for_groups_static(phase_a)

    def step(sl, state_t):
        o_scr[sl] = o_scr[sl] + _bdot_nt(qt_scr[sl], state_t.astype(BF16))
        return dc_scr[sl] * state_t + u_scr[sl]

    init = _transpose_minor(s0[0][...]) if has_state else jnp.zeros((geo.seqs, D_HEAD, D_HEAD), F32)
    s_ref[...] = _transpose_minor(geo.scan_chunks(step, init))

    def epilogue(g):
        sl = geo.rows_of(g)
        geo.store(y_ref, g, _rms(o_scr[sl], nw_ref[...]) * _silu(gate_scr[sl]), relay)

    geo.for_groups(epilogue)


def _mlstm_kernel(*refs, layer, geo, has_state, fresh):
    ((h_ref,), w_refs, (g_ref, gb_ref, nw_ref), st0, _, (y_ref,), st_out,
     (q_scr, v_scr, u_scr, nu_scr, bc_scr, ml_scr, rs_scr, bl_scr, mu_scr, gate_scr), relay) = _split_refs(
        refs, (1, 4, 3, 3 * int(has_state), 3 * int(not fresh), 1, 3, 10, len(refs)))
    kinds = ['mat', 'vec', 'vec']
    c_ref, n_ref, m_ref = _state_views(st_out, kinds, layer, fresh)
    st0 = _state_views(st0, kinds, layer, False) if has_state else st0
    rows, width = geo.rows, geo.width
    head = pl.program_id(1)
    scale = D_HEAD ** -0.5
    t_i = lax.broadcasted_iota(jnp.int32, (1, rows, width), 1)
    s_i = lax.broadcasted_iota(jnp.int32, (1, rows, width), 2)
    ok = s_i <= t_i
    if geo.padded:
        ok = ok & (s_i < geo.valid)
    w = _join_columns(w_refs)

    def phase_a(g):
        q, k, v, og = geo.split(geo.project(h_ref, w, g), relay)
        (gates,) = geo.split(g_ref[pl.ds(g * geo.group_tokens, geo.group_tokens), :], relay)
        gates = gates + gb_ref[...]
        lf_all = _log_sigmoid(gates)
        mask = geo.row_mask(gates.shape)
        if mask is not None:
            lf_all = jnp.where(mask, lf_all, 0.0)
        bc_all = _seg_cumsum(lf_all, rows)
        lane = lax.broadcasted_iota(jnp.int32, gates.shape, 2)
        wide = lambda x: jnp.broadcast_to(x, gates.shape)
        ig = wide(jnp.sum(jnp.where(lane == head, gates, 0.0), axis=2, keepdims=True))
        bc_col = jnp.sum(jnp.where(lane == head + N_HEAD, bc_all, 0.0), axis=2, keepdims=True)
        bc = wide(bc_col)
        log_d = jnp.where(ok, bc_col + _row_bcast(ig - bc, width), -jnp.inf)
        m_col = jnp.max(log_d, axis=2, keepdims=True)
        m_loc = wide(m_col)
        k = k * scale
        qb = q.astype(BF16)
        vb = _pad_rows(v, width).astype(BF16)
        s_m = _bdot_nt(qb, _pad_rows(k, width).astype(BF16)) * jnp.exp(log_d - m_col)
        sl = geo.rows_of(g)
        gate_scr[sl] = og
        q_scr[sl] = qb
        v_scr[sl] = _bdot(s_m.astype(BF16), vb)
        rs_scr[sl] = wide(jnp.sum(s_m, axis=2, keepdims=True))
        bc_scr[sl] = bc
        ml_scr[sl] = m_loc
        b_last = bc[:, rows - 1:rows]
        m_upd = m_loc[:, rows - 1:rows]
        w_exp = b_last - bc + ig - m_upd
        if mask is not None:
            w_exp = jnp.where(mask, w_exp, -jnp.inf)
        k_w = k * jnp.exp(w_exp)
        u_scr[sl] = _bdot_tn(_pad_rows(k_w, width).astype(BF16), vb)
        nu_scr[sl] = jnp.sum(k_w, axis=1, keepdims=True)
        bl_scr[sl] = b_last
        mu_scr[sl] = m_upd

    geo.for_groups_static(phase_a)

    def step(sl, c_state):
        n_state, m_prev = n_ref[...], m_ref[...]
        inter_log = bc_scr[sl] + m_prev
        m_t = jnp.maximum(inter_log, ml_scr[sl])
        w_inter = jnp.exp(inter_log - m_t)
        w_loc = jnp.exp(ml_scr[sl] - m_t)
        qb = q_scr[sl]
        num = w_inter * _bdot(qb, c_state.astype(BF16)) + w_loc * v_scr[sl]
        q_n = jnp.broadcast_to(jnp.sum(qb.astype(F32) * n_state, axis=2, keepdims=True), m_t.shape)
        nq = w_inter * q_n + w_loc * rs_scr[sl]
        v_scr[sl] = num / jnp.maximum(jnp.abs(nq), jnp.exp(-m_t))
        m_new = m_t[:, rows - 1:rows]
        w_prev = jnp.exp(bl_scr[sl] + m_prev - m_new)
        w_upd = jnp.exp(mu_scr[sl] - m_new)
        n_ref[...] = w_prev * n_state + w_upd * nu_scr[sl]
        m_ref[...] = m_new
        return w_prev * c_state + w_upd * u_scr[sl]

    if has_state:
        c_init = st0[0][...]
        n_ref[...] = st0[1][...]
        m_ref[...] = st0[2][...]
    else:
        c_init = jnp.zeros((geo.seqs, D_HEAD, D_HEAD), F32)
        n_ref[...] = jnp.zeros(n_ref.shape, F32)
        m_ref[...] = jnp.zeros(m_ref.shape, F32)
    c_ref[...] = geo.scan_chunks(step, c_init)

    def epilogue(g):
        sl = geo.rows_of(g)
        geo.store(y_ref, g, _rms(_sigmoid(gate_scr[sl]) * v_scr[sl], nw_ref[...]), relay)

    geo.for_groups(epilogue)


def _ret_kernel(*refs, layer, geo, has_state, fresh):
    ((h_ref,), w_refs, (cos_ref, sin_ref, nw_ref), r0, _, (y_ref,), r_out,
     (q_scr, o_scr, u_scr, gate_scr), relay) = _split_refs(
        refs, (1, 4, 3, int(has_state), int(not fresh), 1, 1, 4, len(refs)))
    (r_ref,) = _state_views(r_out, ['mat'], layer, fresh)
    rows, width, valid = geo.rows, geo.width, geo.valid
    head = pl.program_id(1)
    log_gamma = jnp.zeros((1, 1, 1), F32)
    for h in range(N_HEAD):
        log_gamma = jnp.where(head == h, math.log(1.0 - 2.0 ** (-5.0 - h)), log_gamma)
    scale = D_HEAD ** -0.5
    t_i = lax.broadcasted_iota(jnp.int32, (1, rows, width), 1)
    s_i = lax.broadcasted_iota(jnp.int32, (1, rows, width), 2)
    decay = jnp.where(s_i <= t_i, jnp.exp((t_i - s_i).astype(F32) * log_gamma), 0.0)
    t_col = lax.broadcasted_iota(jnp.int32, (1, rows, D_HEAD), 1).astype(F32)
    w_in = jnp.exp((t_col + 1.0) * log_gamma)
    w_st = jnp.exp((valid - 1.0 - t_col) * log_gamma)
    g_all = jnp.exp(valid * log_gamma)
    w = _join_columns(w_refs)

    def rotate(x, cos, sin):
        flat = x.reshape(x.shape[0] * rows, D_HEAD)
        return x * cos + pltpu.roll(flat, D_HEAD // 2, axis=1).reshape(x.shape) * sin

    def phase_a(g):
        q, k, v, rg = geo.split(geo.project(h_ref, w, g), relay)
        if geo.chunks > 1:
            cos, sin = cos_ref[geo.rows_of(g)], sin_ref[geo.rows_of(g)]
        else:
            cos, sin = cos_ref[...], sin_ref[...]
        qr = rotate(q, cos, sin).astype(BF16)
        kr = rotate(k, cos, sin) * scale
        vb = _pad_rows(v, width).astype(BF16)
        a = _bdot_nt(qr, _pad_rows(kr, width).astype(BF16)) * decay
        sl = geo.rows_of(g)
        gate_scr[sl] = rg
        q_scr[sl] = qr
        o_scr[sl] = _bdot(a.astype(BF16), vb)
        u_scr[sl] = _bdot_tn(_pad_rows(kr * w_st, width).astype(BF16), vb)

    geo.for_groups_static(phase_a)

    def step(sl, state):
        o_scr[sl] = o_scr[sl] + _bdot(q_scr[sl], state.astype(BF16)) * w_in
        return g_all * state + u_scr[sl]

    init = r0[0][...] if has_state else jnp.zeros((geo.seqs, D_HEAD, D_HEAD), F32)
    r_ref[...] = geo.scan_chunks(step, init)

    def epilogue(g):
        sl = geo.rows_of(g)
        geo.store(y_ref, g, _rms(o_scr[sl], nw_ref[...]) * _silu(gate_scr[sl]), relay)

    geo.for_groups(epilogue)


def _mixer_spec(kernel, name, geo, layer, h, weights, extra, states_in, states_prev, scratch):
    seqs, n_cols = geo.seqs, len(weights) * D_HEAD
    fresh = isinstance(states_prev[0], jax.ShapeDtypeStruct)

    def weight_spec(first, per_head):
        return pl.BlockSpec((None, D_MODEL, D_HEAD), lambda bi, j: (layer, 0, first + per_head * j))

    def state_spec(arr, layered=False):
        lead, first = (arr.shape[0], 0) if layered else (None, layer)
        if len(arr.shape) == 5:
            return pl.BlockSpec((lead, seqs, None, D_HEAD, D_HEAD), lambda bi, j: (first, bi, j, 0, 0))
        return pl.BlockSpec((lead, seqs, N_HEAD, D_HEAD), lambda bi, j: (first, bi, 0, 0))

    has_state = states_in is not None
    args = [h] + [a for a, _, _ in weights] + [a for a, _ in extra]
    in_specs = [pl.BlockSpec((geo.tokens, D_MODEL), lambda bi, j: (bi, 0))] + [
        weight_spec(first, per_head) for _, first, per_head in weights] + [s for _, s in extra]
    if has_state:
        args += list(states_in)
        in_specs += [state_spec(a) for a in states_in]
    n_in = len(args)
    if not fresh:
        args += list(states_prev)
        in_specs += [pl.BlockSpec(memory_space=pl.ANY) for _ in states_prev]
    out_specs = [pl.BlockSpec((geo.tokens, D_HEAD), lambda bi, j: (bi, j))] + [
        state_spec(a, fresh) for a in states_prev]
    out_shape = [jax.ShapeDtypeStruct((h.shape[0], MIX_W), F32 if geo.padded else BF16)] + [
        jax.ShapeDtypeStruct(a.shape, F32) for a in states_prev]
    scratch_shapes = list(scratch) + [pltpu.VMEM((geo.problems, geo.rows, D_HEAD), F32)]
    if geo.padded:
        scratch_shapes += [pltpu.VMEM((geo.group_tokens, D_HEAD), F32),
                           pltpu.VMEM((seqs * geo.rows, D_HEAD), F32)] * (n_cols // D_HEAD)
    return dict(
        kernel=functools.partial(kernel, layer=layer, geo=geo, has_state=has_state, fresh=fresh), name=name,
        args=args, in_specs=in_specs, out_specs=out_specs, out_shape=out_shape, scratch_shapes=scratch_shapes,
        aliases={} if fresh else {n_in + i: 1 + i for i in range(len(states_prev))})


def _run_mixers(geo, specs):
    n_ins = [len(sp["args"]) - 1 for sp in specs]
    n_outs = [len(sp["out_shape"]) for sp in specs]
    n_scr = [len(sp["scratch_shapes"]) for sp in specs]
    in_off, out_off, scr_off = (np.concatenate([[0], np.cumsum(c)]).tolist() for c in (n_ins, n_outs, n_scr))
    total_in, total_out = 1 + in_off[-1], out_off[-1]

    def body(*refs):
        ins, outs, scr = refs[1:total_in], refs[total_in:total_in + total_out], refs[total_in + total_out:]
        for m, sp in enumerate(specs):
            sp["kernel"](refs[0], *ins[in_off[m]:in_off[m + 1]], *outs[out_off[m]:out_off[m + 1]],
                         *scr[scr_off[m]:scr_off[m + 1]])

    aliases = {}
    for m, sp in enumerate(specs):
        for i, o in sp["aliases"].items():
            aliases[i + in_off[m]] = o + out_off[m]
    flat = pl.pallas_call(
        body,
        grid=(geo.batch // geo.seqs, N_HEAD),
        in_specs=specs[0]["in_specs"][:1] + [s for sp in specs for s in sp["in_specs"][1:]],
        out_specs=[s for sp in specs for s in sp["out_specs"]],
        out_shape=[s for sp in specs for s in sp["out_shape"]],
        scratch_shapes=[s for sp in specs for s in sp["scratch_shapes"]],
        input_output_aliases=aliases,
        compiler_params=pltpu.CompilerParams(
            dimension_semantics=("arbitrary", "arbitrary"), vmem_limit_bytes=VMEM_LIMIT),
        name="+".join(sp["name"] for sp in specs),
    )(specs[0]["args"][0], *[a for sp in specs for a in sp["args"][1:]])
    return [flat[out_off[m]:out_off[m + 1]] for m in range(len(specs))]


def _head_block(rows):
    return pl.BlockSpec((rows, D_HEAD), lambda bi, j: (0, j))


def _part_blocks(w_main, mixer):
    return [(w_main, (mixer * 4 + p) * N_HEAD, 1) for p in range(4)]


def _hgrn(geo, layer, h, w_main, hgrn_lb, nw, state_in, state_prev):
    n = geo.problems
    scratch = [pltpu.VMEM((n, geo.rows, D_HEAD), BF16), pltpu.VMEM((n, geo.rows, D_HEAD), F32),
               pltpu.VMEM((n, D_HEAD, D_HEAD), F32), pltpu.VMEM((n, 1, D_HEAD), F32)]
    place = _hgrn_placement(geo)
    return _mixer_spec(
        _hgrn_kernel, "hgrn2", geo, layer, h, _part_blocks(w_main, 0),
        [(hgrn_lb, _head_block(hgrn_lb.shape[0])), (nw, _head_block(1)), (place, _resident(place.shape))],
        None if state_in is None else [state_in], [state_prev], scratch)


def _mlstm(geo, layer, h, w_main, gates, gate_b, nw, states_in, states_prev):
    n = geo.problems
    col = lambda r: pltpu.VMEM((n, r, LANES), F32)
    scratch = [pltpu.VMEM((n, geo.rows, D_HEAD), BF16), pltpu.VMEM((n, geo.rows, D_HEAD), F32),
               pltpu.VMEM((n, D_HEAD, D_HEAD), F32), pltpu.VMEM((n, 1, D_HEAD), F32),
               col(geo.rows), col(geo.rows), col(geo.rows), col(1), col(1)]
    return _mixer_spec(
        _mlstm_kernel, "mlstm", geo, layer, h, _part_blocks(w_main, 1),
        [(gates, pl.BlockSpec((geo.tokens, LANES), lambda bi, j: (bi, 0))),
         (gate_b, pl.BlockSpec((1, LANES), lambda bi, j: (0, 0))), (nw, _head_block(1))],
        states_in, states_prev, scratch)


def _retention(geo, layer, h, w_main, cos, sin, nw, state_in, state_prev):
    n = geo.problems
    scratch = [pltpu.VMEM((n, geo.rows, D_HEAD), BF16), pltpu.VMEM((n, geo.rows, D_HEAD), F32),
               pltpu.VMEM((n, D_HEAD, D_HEAD), F32)]
    table = pl.BlockSpec(cos.shape, lambda bi, j: (0, 0, 0))
    return _mixer_spec(
        _ret_kernel, "retention", geo, layer, h, _part_blocks(w_main, 2),
        [(cos, table), (sin, table), (nw, _head_block(1))],
        None if state_in is None else [state_in], [state_prev], scratch)


def _merge_kernel(x_ref, h_ref, yh_ref, ym_ref, yr_ref, wg_ref, bg_ref, wb_ref, wo_ref, n2_ref, x1_ref, h2_ref):
    hb = h_ref[...]
    merged = None
    for j, y_ref in enumerate((yh_ref, ym_ref, yr_ref)):
        cols = slice(j * D_MODEL, (j + 1) * D_MODEL)
        gate = _sigmoid(_dot(hb, wg_ref[:, cols]) + bg_ref[:, cols])
        term = gate * _dot(y_ref[...].astype(BF16), wb_ref[j])
        merged = term if merged is None else merged + term
    x1 = x_ref[...] + _dot(merged.astype(BF16), wo_ref[...])
    x1_ref[...] = x1
    h2_ref[...] = _rms(x1, n2_ref[...]).astype(BF16)


def _merge(x, h, yh, ym, yr, w_bgate, gate_b, w_branch, w_out, norm2_w):
    n = x.shape[0]
    tm = min(TOKEN_TILE, n)
    gate_w = N_BRANCH * D_MODEL
    row = lambda width: pl.BlockSpec((tm, width), lambda i: (i, 0))
    vec = lambda width: pl.BlockSpec((1, width), lambda i: (0, 0))
    return pl.pallas_call(
        _merge_kernel,
        grid=(n // tm,),
        in_specs=[
            row(D_MODEL), row(D_MODEL), row(MIX_W), row(MIX_W), row(MIX_W),
            _resident((D_MODEL, gate_w)), vec(gate_w),
            _resident((N_BRANCH, MIX_W, D_MODEL)),
            _resident((D_MODEL, D_MODEL)),
            vec(D_MODEL),
        ],
        out_specs=[row(D_MODEL), row(D_MODEL)],
        out_shape=[jax.ShapeDtypeStruct((n, D_MODEL), F32), jax.ShapeDtypeStruct((n, D_MODEL), BF16)],
        compiler_params=pltpu.CompilerParams(
            dimension_semantics=("arbitrary",), vmem_limit_bytes=VMEM_LIMIT),
        name="merge",
    )(x, h, yh, ym, yr, w_bgate, gate_b, w_branch, w_out, norm2_w)


def _ffn_kernel(*refs, seq_len, has_state, final_norm, tail_rows):
    refs = list(refs)
    x1_ref, h2_ref, wg_ref, wu_ref, wd_ref, cw_ref, cb_ref, nw_ref = refs[:8]
    refs = refs[8:]
    p1_ref = p2_ref = None
    if has_state:
        p1_ref, p2_ref = refs[:2]
        refs = refs[2:]
    if final_norm:
        out_ref, tail_ref, carry_scr, a_scr = refs
    else:
        wgn_ref, out_ref, hn_ref, gn_ref, tail_ref, carry_scr, a_scr = refs
    tm = x1_ref.shape[0]
    i = pl.program_id(0)

    @pl.when(i == 0)
    def _():
        carry_scr[...] = jnp.zeros(carry_scr.shape, F32)

    t = (i * tm + lax.broadcasted_iota(jnp.int32, (tm, FF_TILE), 0)) & (seq_len - 1)
    h2 = h2_ref[...]
    acc = x1_ref[...]
    for j in range(D_FF // FF_TILE):
        cols = slice(j * FF_TILE, (j + 1) * FF_TILE)
        a = _dot(h2, wg_ref[:, cols])
        u = _dot(h2, wu_ref[:, cols])
        a_scr[0:SUBLANES, :] = carry_scr[:, cols]
        a_scr[SUBLANES:, :] = a
        carry_scr[:, cols] = a[tm - SUBLANES:, :]
        prev1 = jnp.where(t >= 1, a_scr[SUBLANES - 1:SUBLANES - 1 + tm, :], 0.0)
        prev2 = jnp.where(t >= 2, a_scr[SUBLANES - 2:SUBLANES - 2 + tm, :], 0.0)
        if has_state:
            prev1 = prev1 + p1_ref[:, cols]
            prev2 = prev2 + p2_ref[:, cols]
        conv = prev2 * cw_ref[0:1, cols] + prev1 * cw_ref[1:2, cols] + a * cw_ref[2:3, cols] + cb_ref[:, cols]
        acc = acc + _dot((_silu(conv) * u).astype(BF16), wd_ref[cols, :])
        if tail_rows == tm:
            tail_ref[:, cols] = a
        else:
            tail_ref[0, :, cols] = a[tm - tail_rows:, :]
    if final_norm:
        out_ref[...] = _rms(acc, nw_ref[...])
    else:
        out_ref[...] = acc
        hn = _rms(acc, nw_ref[...]).astype(BF16)
        hn_ref[...] = hn
        gn_ref[...] = _dot(hn, wgn_ref[...])


def _ffn(x1, h2, wg, wu, wd, conv_w, conv_b, seq_len, conv_state, norm_w, next_w_gate):
    final_norm = next_w_gate is None
    n = x1.shape[0]
    has_state = conv_state is not None
    tm = min(TOKEN_TILE // 2 if has_state else TOKEN_TILE, n)
    assert n % tm == 0 and seq_len & (seq_len - 1) == 0
    batch = n // seq_len
    row = lambda width: pl.BlockSpec((tm, width), lambda i: (i, 0))
    full = lambda shape: pl.BlockSpec(shape, lambda i: (0,) * len(shape))
    in_specs = [row(D_MODEL), row(D_MODEL), _resident((D_MODEL, D_FF)), _resident((D_MODEL, D_FF)),
                _resident((D_FF, D_MODEL)), full((CONV_W, D_FF)), full((1, D_FF)), full((1, D_MODEL))]
    args = [x1, h2, wg, wu, wd, conv_w, conv_b, norm_w]
    if has_state:
        p1 = jnp.pad(conv_state[:, 1:2], ((0, 0), (0, seq_len - 1), (0, 0))).reshape(n, D_FF)
        p2 = jnp.pad(conv_state, ((0, 0), (0, seq_len - (CONV_W - 1)), (0, 0))).reshape(n, D_FF)
        in_specs += [row(D_FF), row(D_FF)]
        args += [p1, p2]
    if seq_len % tm == 0:
        tail_rows = CONV_W - 1
        tail_spec = pl.BlockSpec((1, tail_rows, D_FF), lambda i: ((i * tm) // seq_len, 0, 0))
        tail_shape = jax.ShapeDtypeStruct((batch, tail_rows, D_FF), F32)
    else:
        assert tm % seq_len == 0 and seq_len >= CONV_W - 1
        tail_rows = tm
        tail_spec = row(D_FF)
        tail_shape = jax.ShapeDtypeStruct((n, D_FF), F32)
    out_specs, out_shape = [row(D_MODEL)], [jax.ShapeDtypeStruct((n, D_MODEL), F32)]
    if not final_norm:
        in_specs.append(_resident((D_MODEL, LANES)))
        args.append(next_w_gate)
        out_specs += [row(D_MODEL), row(LANES)]
        out_shape += [jax.ShapeDtypeStruct((n, D_MODEL), BF16), jax.ShapeDtypeStruct((n, LANES), F32)]
    *outs, tail = pl.pallas_call(
        functools.partial(_ffn_kernel, seq_len=seq_len, has_state=has_state,
                          final_norm=final_norm, tail_rows=tail_rows),
        grid=(n // tm,),
        in_specs=in_specs,
        out_specs=out_specs + [tail_spec],
        out_shape=out_shape + [tail_shape],
        scratch_shapes=[pltpu.VMEM((SUBLANES, D_FF), F32), pltpu.VMEM((tm + SUBLANES, FF_TILE), F32)],
        compiler_params=pltpu.CompilerParams(
            dimension_semantics=("arbitrary",), vmem_limit_bytes=VMEM_LIMIT),
        name="conv_ffn",
    )(*args)
    if tail_rows == tm:
        tail = tail.reshape(batch, seq_len, D_FF)[:, seq_len - (CONV_W - 1):, :]
    return outs, tail


def _prep_weights(w_in, w_branch, w_out, ffn_w_gate, ffn_w_up, ffn_w_down):
    w_in = w_in.astype(BF16)
    w_main = jnp.concatenate([w_in[:, :, HG_OFF:ML_GATE_OFF], w_in[:, :, RET_OFF:GATE_OFF]], axis=2)
    w_gate = jnp.pad(w_in[:, :, ML_GATE_OFF:RET_OFF], ((0, 0), (0, 0), (0, LANES - 2 * N_HEAD)))
    return (w_main, w_gate,
            w_in[:, :, GATE_OFF:], w_branch.astype(BF16), w_out.astype(BF16),
            ffn_w_gate.astype(BF16), ffn_w_up.astype(BF16), ffn_w_down.astype(BF16))


def _trunk(x3, pos0, states, params, prepped):
    (norm1_w, mlstm_gate_b, branch_gate_b, hgrn_lb, hgrn_norm_w, mlstm_norm_w, ret_norm_w,
     norm2_w, ffn_conv_w, ffn_conv_b, final_norm_w) = params
    w_main, w_gate, w_bgate, w_branch, w_out, w_ffg, w_ffu, w_ffd = prepped
    batch, seq_len, _ = x3.shape
    n = batch * seq_len
    depth = norm1_w.shape[0]
    geo = _Geo(batch, seq_len)
    cos, sin = (t.reshape(geo.chunks, geo.rows, D_HEAD) for t in _rope_tables(geo.chunks * geo.rows, pos0))

    mat = jax.ShapeDtypeStruct((depth, batch, N_HEAD, D_HEAD, D_HEAD), F32)
    vec = jax.ShapeDtypeStruct((depth, batch, N_HEAD, D_HEAD), F32)
    if states is None:
        hg_in = ml_in = ret_in = conv_in = None
    else:
        hg_in, ml_c, ml_n, ml_m, ret_in, conv_in = states
        ml_in = [ml_c, ml_n, jnp.broadcast_to(ml_m[..., None], vec.shape)]
    hg_out, c_out, r_out, n_out, m_out = mat, mat, mat, vec, vec

    x = x3.reshape(n, D_MODEL)
    h, gates = _norm(x, norm1_w[0:1], w_gate[0])
    conv_tails = []
    for l in range(depth):
        last = l == depth - 1
        gate_b = jnp.pad(mlstm_gate_b[l:l + 1], ((0, 0), (0, LANES - 2 * N_HEAD)))
        specs = [_hgrn(geo, l, h, w_main, hgrn_lb, hgrn_norm_w[l:l + 1], hg_in, hg_out),
                 _mlstm(geo, l, h, w_main, gates, gate_b, mlstm_norm_w[l:l + 1], ml_in, [c_out, n_out, m_out]),
                 _retention(geo, l, h, w_main, cos, sin, ret_norm_w[l:l + 1], ret_in, r_out)]
        runs = [specs] if not geo.padded else [[sp] for sp in specs]
        (y_hg, hg_out), (y_ml, c_out, n_out, m_out), (y_ret, r_out) = [
            out for run in runs for out in _run_mixers(geo, run)]
        x1, h2 = _merge(x, h, y_hg, y_ml, y_ret, w_bgate[l], branch_gate_b[l:l + 1], w_branch[l], w_out[l],
                        norm2_w[l:l + 1])
        outs, conv_tail = _ffn(x1, h2, w_ffg[l], w_ffu[l], w_ffd[l], ffn_conv_w[l], ffn_conv_b[l:l + 1],
                               seq_len, None if conv_in is None else conv_in[l],
                               final_norm_w[None, :] if last else norm1_w[l + 1:l + 2],
                               None if last else w_gate[l + 1])
        x = outs[0]
        h, gates = (None, None) if last else outs[1:]
        conv_tails.append(conv_tail)
    return x.reshape(batch, seq_len, D_MODEL), (
        hg_out, c_out, n_out, m_out[:, :, :, 0], r_out,
        jnp.stack(conv_tails))


def kernel(x_prompt, x_sample, state_hgrn, state_mlstm_C, state_mlstm_n, state_mlstm_m, state_ret,
           state_ffn_conv, norm1_w, w_in, mlstm_gate_b, branch_gate_b, hgrn_lb, hgrn_norm_w,
           mlstm_norm_w, ret_norm_w, w_branch, w_out, norm2_w, ffn_w_gate, ffn_w_up, ffn_conv_w,
           ffn_conv_b, ffn_w_down, final_norm_w):
    params = (norm1_w, mlstm_gate_b, branch_gate_b, hgrn_lb, hgrn_norm_w, mlstm_norm_w, ret_norm_w,
              norm2_w, ffn_conv_w, ffn_conv_b, final_norm_w)
    prepped = _prep_weights(w_in, w_branch, w_out, ffn_w_gate, ffn_w_up, ffn_w_down)
    y_p, (hg_p, c_p, n_p, m_p, r_p, cv_p) = _trunk(x_prompt, 0, None, params, prepped)
    y_s, (hg_s, c_s, n_s, m_s, r_s, cv_s) = _trunk(
        x_sample, PAST_LEN,
        (state_hgrn, state_mlstm_C, state_mlstm_n, state_mlstm_m, state_ret, state_ffn_conv),
        params, prepped)
    return (y_p, y_s, hg_p, hg_s, c_p, c_s, n_p, n_s, m_p, m_s, r_p, r_s, cv_p, cv_s)
```

```python
import functools
import math

import numpy as np
import jax
import jax.numpy as jnp
from jax import lax
from jax.experimental import pallas as pl
from jax.experimental.pallas import tpu as pltpu

F32 = jnp.float32
BF16 = jnp.bfloat16

D_MODEL = 1024
MIX_W = 512
N_HEAD = 4
D_HEAD = 128
N_BRANCH = 3
D_FF = 2816
CONV_W = 3
CHUNK = 64
EPS = 1e-6
ROPE_BASE = 10000.0
PAST_LEN = 16384

HG_OFF = 0
ML_OFF = 4 * MIX_W
ML_GATE_OFF = ML_OFF + 4 * MIX_W
RET_OFF = ML_GATE_OFF + 2 * N_HEAD
GATE_OFF = RET_OFF + 4 * MIX_W
LOG2E = 1.4426950408889634

LANES = 128
SUBLANES = 8
VMEM_LIMIT = 56 * 1024 * 1024

TOKEN_TILE = 512
MERGE_TOKEN_TILE = 1024
NORM_TOKEN_TILE = 1024
FF_TILE = 2816
HG_SUB = 16
GROUP = 32
SEQ_BLOCK = 64
SCAN_UNROLL = 32


def _resident(shape):
    return pl.BlockSpec(shape, lambda *_: (0,) * len(shape), pipeline_mode=pl.Buffered(1))


def _dot(a, b):
    return jnp.dot(a, b, preferred_element_type=F32)


def _dot_tn(a, b):
    return lax.dot_general(a, b, (((0,), (0,)), ((), ())), preferred_element_type=F32)


def _bdot(a, b):
    return jnp.einsum('nlk,nkv->nlv', a, b, preferred_element_type=F32)


def _bdot_nt(a, b):
    return jnp.einsum('nqd,nkd->nqk', a, b, preferred_element_type=F32)


def _bdot_tn(a, b):
    return jnp.stack([_dot_tn(a[i], b[i]) for i in range(a.shape[0])])


def _split3(x):
    hi = x.astype(BF16)
    r = x - hi.astype(F32)
    mid = r.astype(BF16)
    lo = (r - mid.astype(F32)).astype(BF16)
    return hi, mid, lo


def _sigmoid(x):
    return 1.0 / (1.0 + jnp.exp(-x))


def _silu(x):
    return x * _sigmoid(x)


def _log_sigmoid(x):
    return jnp.minimum(x, 0.0) - jnp.log(1.0 + jnp.exp(-jnp.abs(x)))


def _rms(x, w):
    return x * lax.rsqrt(jnp.mean(x * x, axis=-1, keepdims=True) + EPS) * w


def _pad_rows(x, n):
    if x.shape[1] >= n:
        return x
    return jnp.concatenate([x, jnp.zeros((x.shape[0], n - x.shape[1], x.shape[2]), x.dtype)], axis=1)


def _seg_cumsum(x, seg):
    shape = x.shape
    flat = x.reshape(shape[0] * seg, shape[2])
    row = lax.broadcasted_iota(jnp.int32, flat.shape, 0) & (seg - 1)
    s = 1
    while s < seg:
        flat = flat + jnp.where(row >= s, pltpu.roll(flat, s, axis=0), 0.0)
        s *= 2
    return flat.reshape(shape)


def _col_bcast(rows):
    n = rows.shape[0]
    r = lax.broadcasted_iota(jnp.int32, (n, D_HEAD, D_HEAD), 1)
    c = lax.broadcasted_iota(jnp.int32, (n, D_HEAD, D_HEAD), 2)
    d = jnp.where(r == c, jnp.broadcast_to(rows, (n, D_HEAD, D_HEAD)), 0.0).reshape(n * D_HEAD, D_HEAD)
    ones = jnp.ones((D_HEAD, D_HEAD), BF16)
    hi, mid, lo = _split3(d)
    return (_dot(hi, ones) + _dot(mid, ones) + _dot(lo, ones)).reshape(n, D_HEAD, D_HEAD)


def _through_vmem(x):
    def body(ref):
        ref[...] = x
        return ref[...]
    return pl.run_scoped(body, pltpu.VMEM(x.shape, x.dtype))


def _row_bcast(col, width):
    n, rows, _ = col.shape
    vals = _through_vmem(_pad_rows(col, width))
    lane = lax.broadcasted_iota(jnp.int32, (n, rows, LANES), 2)
    pick = _through_vmem(jnp.where(lane == 0, 1.0, 0.0).astype(BF16))
    hi, mid, lo = _split3(vals)
    return _bdot_nt(pick, hi) + _bdot_nt(pick, mid) + _bdot_nt(pick, lo)


def _chunk_geometry(seq_len):
    if seq_len % CHUNK == 0:
        return CHUNK, CHUNK, CHUNK
    valid = math.gcd(seq_len, CHUNK)
    assert valid == seq_len and valid <= SUBLANES, "unsupported sequence length"
    return SUBLANES, valid, LANES


def _rope_kernel(inv_ref, cos_ref, sin_ref, *, pos0):
    shape = cos_ref.shape
    pos = lax.broadcasted_iota(jnp.int32, shape, 0).astype(F32) + pos0
    ang = pos * inv_ref[...]
    lane = lax.broadcasted_iota(jnp.int32, shape, 1)
    cos_ref[...] = jnp.cos(ang)
    sin_ref[...] = jnp.where(lane < D_HEAD // 2, -jnp.sin(ang), jnp.sin(ang))


def _rope_tables(rows, pos0):
    inv = ROPE_BASE ** (-jnp.linspace(0.0, 1.0, D_HEAD // 2, dtype=F32))
    inv2 = jnp.concatenate([inv, inv])[None, :]
    return pl.pallas_call(
        functools.partial(_rope_kernel, pos0=float(pos0)),
        out_shape=(jax.ShapeDtypeStruct((rows, D_HEAD), F32),) * 2,
        name="rope_tables",
    )(inv2)


def _norm_kernel(x_ref, w_ref, wg_ref, h_ref, g_ref):
    hb = _rms(x_ref[...], w_ref[...]).astype(BF16)
    h_ref[...] = hb
    g_ref[...] = _dot(hb, wg_ref[...])


def _norm(x, w, w_gate):
    n = x.shape[0]
    tm = min(NORM_TOKEN_TILE, n)
    assert n % tm == 0
    row = lambda width: pl.BlockSpec((tm, width), lambda i: (i, 0))
    return pl.pallas_call(
        _norm_kernel,
        grid=(n // tm,),
        in_specs=[row(D_MODEL), pl.BlockSpec((1, D_MODEL), lambda i: (0, 0)), _resident((D_MODEL, LANES))],
        out_specs=[row(D_MODEL), row(LANES)],
        out_shape=[jax.ShapeDtypeStruct((n, D_MODEL), BF16), jax.ShapeDtypeStruct((n, LANES), F32)],
        compiler_params=pltpu.CompilerParams(dimension_semantics=("arbitrary",), vmem_limit_bytes=VMEM_LIMIT),
        name="norm_in",
    )(x, w, w_gate)


class _Geo:
    def __init__(self, batch, seq_len):
        self.rows, self.valid, self.width = _chunk_geometry(seq_len)
        if self.valid == self.rows:
            self.seqs, self.chunks = 1, seq_len // self.rows
            self.group = min(GROUP, self.chunks)
        else:
            self.seqs, self.chunks = min(SEQ_BLOCK, batch), 1
            self.group = self.seqs
        assert batch % self.seqs == 0 and self.chunks % min(self.group, self.chunks) == 0
        self.batch = batch
        self.problems = self.seqs * self.chunks
        self.n_groups = self.problems // self.group
        self.group_tokens = self.group * self.valid
        self.tokens = self.problems * self.valid
        self.padded = self.valid < self.rows

    def project(self, h_ref, w, g):
        return _dot(h_ref[pl.ds(g * self.group_tokens, self.group_tokens), :], w)

    def split(self, z, relay):
        n_parts = z.shape[1] // D_HEAD
        cols = lambda p: slice(p * D_HEAD, (p + 1) * D_HEAD)
        if not self.padded:
            return [z[:, cols(p)].reshape(self.group, self.rows, D_HEAD) for p in range(n_parts)]
        out = []
        for p in range(n_parts):
            flat_ref, pad_ref = relay[2 * p], relay[2 * p + 1]
            flat_ref[...] = z[:, cols(p)]
            pad_ref[...] = jnp.zeros(pad_ref.shape, F32)
            for t in range(self.valid):
                pad_ref[pl.ds(t, self.seqs, stride=self.rows), :] = flat_ref[
                    pl.ds(t, self.seqs, stride=self.valid), :]
            out.append(pad_ref[...].reshape(self.seqs, self.rows, D_HEAD))
        return out

    def store(self, y_ref, g, y, relay):
        if not self.padded:
            n = self.group * self.rows
            y_ref[pl.ds(g * n, n), :] = y.reshape(n, D_HEAD).astype(y_ref.dtype)
            return
        pad = relay[1]
        pad[...] = y.reshape(self.seqs * self.rows, D_HEAD)
        for t in range(self.valid):
            y_ref[pl.ds(t, self.seqs, stride=self.valid), :] = pad[pl.ds(t, self.seqs, stride=self.rows), :].astype(
                y_ref.dtype)

    def rows_of(self, g):
        return pl.ds(g * self.group, self.group)

    def row_mask(self, shape):
        if not self.padded:
            return None
        return lax.broadcasted_iota(jnp.int32, shape, 1) < self.valid

    def for_groups(self, body):
        if self.n_groups == 1:
            body(0)
        else:
            lax.fori_loop(0, self.n_groups, lambda g, c: (body(g), c)[1], 0)

    def for_groups_static(self, body):
        for g in range(self.n_groups):
            body(g)

    def scan_chunks(self, step, init):
        if self.chunks == 1:
            return step(pl.ds(0, self.seqs), init)
        return lax.fori_loop(0, self.chunks, lambda c, carry: step(pl.ds(c * self.seqs, self.seqs), carry), init,
                             unroll=min(SCAN_UNROLL, self.chunks))


def _join_columns(w_refs):
    return jnp.concatenate([r[...] for r in w_refs], axis=1)


def _transpose_minor(x):
    return jnp.stack([x[i].T for i in range(x.shape[0])])


def _state_views(refs, kinds, layer, layered):
    head = pl.program_id(1)
    views = []
    for ref, kind in zip(refs, kinds):
        for l in (range(ref.shape[0]) if layered else [layer]):
            view = ref.at[l] if layered else ref
            if kind == 'vec':
                view = view.at[:, pl.ds(head, 1), :]
            if l == layer:
                views.append(view)
            else:
                view[...] = jnp.zeros(view.shape, F32)
    return views


def _split_refs(refs, counts):
    out, k = [], 0
    for c in counts:
        out.append(refs[k:k + c])
        k += c
    return out


def _hgrn_lower_bound(lb_ref, layer):
    x = lb_ref[...]
    e = jnp.exp(x - jnp.max(x, axis=0, keepdims=True))
    soft = e / jnp.sum(e, axis=0, keepdims=True)
    cum = soft[0:1]
    for j in range(1, layer + 1):
        cum = cum + soft[j:j + 1]
    return cum - soft[0:1]


def _hgrn_placement(geo):
    sub = min(HG_SUB, geo.rows)
    r = np.arange(sub * D_HEAD)[:, None] // D_HEAD
    c = np.arange(geo.width)[None, :]
    return jnp.asarray(np.stack([r + i * sub == c for i in range(geo.rows // sub)]), BF16)


def _hgrn_intra(q, k, b2, place_ref, geo):
    n = q.shape[0]
    sub = min(HG_SUB, geo.rows)
    sub_row = lax.broadcasted_iota(jnp.int32, (n, sub, geo.width), 1)
    lane = lax.broadcasted_iota(jnp.int32, (n, sub, geo.width), 2)
    blocks = []
    for i in range(geo.rows // sub):
        r0 = i * sub
        b_i, q_i, k_i = (a[:, r0:r0 + sub] for a in (b2, q, k))
        terms = []
        for s in range(sub):
            lo = s // SUBLANES * SUBLANES
            e = jnp.exp2(jnp.minimum(b_i[:, lo:] - b_i[:, s:s + 1], 0.0))
            term = q_i[:, lo:] * e * k_i[:, s:s + 1]
            if lo:
                term = jnp.concatenate([jnp.zeros((n, lo, D_HEAD), F32), term], axis=1)
            terms.append(term)
        cat = jnp.concatenate(terms, axis=2).reshape(n * sub, sub * D_HEAD)
        a_i = _dot(cat.astype(BF16), place_ref[i]).reshape(n, sub, geo.width)
        a_i = jnp.where(sub_row >= lane - r0, a_i, 0.0)
        if i > 0:
            b_ref = b_i[:, 0:1]
            q_dec = q_i * jnp.exp2(b_i - b_ref)
            k_dec = _pad_rows(k[:, 0:r0] * jnp.exp2(b_ref - b2[:, 0:r0]), geo.width)
            a_i = jnp.where(lane < r0, _bdot_nt(q_dec.astype(BF16), k_dec.astype(BF16)), a_i)
        blocks.append(a_i)
    return blocks[0] if len(blocks) == 1 else jnp.concatenate(blocks, axis=1)


def _hgrn_kernel(*refs, layer, geo, has_state, fresh):
    ((h_ref,), w_refs, (lb_ref, nw_ref, place), s0, _, (y_ref,), s_out,
     (qt_scr, o_scr, u_scr, dc_scr, gate_scr), relay) = _split_refs(
        refs, (1, 4, 3, int(has_state), int(not fresh), 1, 1, 5, len(refs)))
    (s_ref,) = _state_views(s_out, ['mat'], layer, fresh)
    lb = _hgrn_lower_bound(lb_ref, layer)
    rows, width = geo.rows, geo.width
    w = _join_columns(w_refs)

    def phase_a(g):
        hq, hf, hi, hg = geo.split(geo.project(h_ref, w, g), relay)
        q = _silu(hq)
        f = lb + (1.0 - lb) * _sigmoid(hf)
        lf = jnp.log(f)
        k = 1.0 - f
        mask = geo.row_mask(lf.shape)
        if mask is not None:
            lf = jnp.where(mask, lf, 0.0)
            k = jnp.where(mask, k, 0.0)
        b2 = _seg_cumsum(lf, rows) * LOG2E
        sl = geo.rows_of(g)
        gate_scr[sl] = hg
        qt_scr[sl] = (q * jnp.exp2(b2)).astype(BF16)
        a = _hgrn_intra(q, k, b2, place, geo)
        vb = _pad_rows(hi, width).astype(BF16)
        o_scr[sl] = _bdot(a.astype(BF16), vb)
        b_last = b2[:, rows - 1:rows]
        k_w = k * jnp.exp2(b_last - b2)
        u_scr[sl] = _bdot_tn(vb, _pad_rows(k_w, width).astype(BF16))
        dc_scr[sl] = jnp.exp2(b_last)

    geo.for_groups_static(phase_a)

    def step(sl, state_t):
        o_scr[sl] = o_scr[sl] + _bdot_nt(qt_scr[sl], state_t.astype(BF16))
        return dc_scr[sl] * state_t + u_scr[sl]

    init = _transpose_minor(s0[0][...]) if has_state else jnp.zeros((geo.seqs, D_HEAD, D_HEAD), F32)
    s_ref[...] = _transpose_minor(geo.scan_chunks(step, init))

    def epilogue(g):
        sl = geo.rows_of(g)
        geo.store(y_ref, g, _rms(o_scr[sl], nw_ref[...]) * _silu(gate_scr[sl]), relay)

    geo.for_groups(epilogue)


def _mlstm_kernel(*refs, layer, geo, has_state, fresh):
    ((h_ref,), w_refs, (g_ref, gb_ref, nw_ref), st0, _, (y_ref,), st_out,
     (q_scr, v_scr, u_scr, nu_scr, bc_scr, ml_scr, rs_scr, bl_scr, mu_scr, gate_scr), relay) = _split_refs(
        refs, (1, 4, 3, 3 * int(has_state), 3 * int(not fresh), 1, 3, 10, len(refs)))
    kinds = ['mat', 'vec', 'vec']
    c_ref, n_ref, m_ref = _state_views(st_out, kinds, layer, fresh)
    st0 = _state_views(st0, kinds, layer, False) if has_state else st0
    rows, width = geo.rows, geo.width
    head = pl.program_id(1)
    scale = D_HEAD ** -0.5
    t_i = lax.broadcasted_iota(jnp.int32, (1, rows, width), 1)
    s_i = lax.broadcasted_iota(jnp.int32, (1, rows, width), 2)
    ok = s_i <= t_i
    if geo.padded:
        ok = ok & (s_i < geo.valid)
    w = _join_columns(w_refs)

    def phase_a(g):
        q, k, v, og = geo.split(geo.project(h_ref, w, g), relay)
        (gates,) = geo.split(g_ref[pl.ds(g * geo.group_tokens, geo.group_tokens), :], relay)
        gates = gates + gb_ref[...]
        lf_all = _log_sigmoid(gates)
        mask = geo.row_mask(gates.shape)
        if mask is not None:
            lf_all = jnp.where(mask, lf_all, 0.0)
        bc_all = _seg_cumsum(lf_all, rows)
        lane = lax.broadcasted_iota(jnp.int32, gates.shape, 2)
        wide = lambda x: jnp.broadcast_to(x, gates.shape)
        ig = wide(jnp.sum(jnp.where(lane == head, gates, 0.0), axis=2, keepdims=True))
        bc_col = jnp.sum(jnp.where(lane == head + N_HEAD, bc_all, 0.0), axis=2, keepdims=True)
        bc = wide(bc_col)
        log_d = jnp.where(ok, bc_col + _row_bcast(ig - bc, width), -jnp.inf)
        m_col = jnp.max(log_d, axis=2, keepdims=True)
        m_loc = wide(m_col)
        k = k * scale
        qb = q.astype(BF16)
        vb = _pad_rows(v, width).astype(BF16)
        s_m = _bdot_nt(qb, _pad_rows(k, width).astype(BF16)) * jnp.exp(log_d - m_col)
        sl = geo.rows_of(g)
        gate_scr[sl] = og
        q_scr[sl] = qb
        v_scr[sl] = _bdot(s_m.astype(BF16), vb)
        rs_scr[sl] = wide(jnp.sum(s_m, axis=2, keepdims=True))
        bc_scr[sl] = bc
        ml_scr[sl] = m_loc
        b_last = bc[:, rows - 1:rows]
        m_upd = m_loc[:, rows - 1:rows]
        w_exp = b_last - bc + ig - m_upd
        if mask is not None:
            w_exp = jnp.where(mask, w_exp, -jnp.inf)
        k_w = k * jnp.exp(w_exp)
        u_scr[sl] = _bdot_tn(_pad_rows(k_w, width).astype(BF16), vb)
        nu_scr[sl] = jnp.sum(k_w, axis=1, keepdims=True)
        bl_scr[sl] = b_last
        mu_scr[sl] = m_upd

    geo.for_groups_static(phase_a)

    def step(sl, c_state):
        n_state, m_prev = n_ref[...], m_ref[...]
        inter_log = bc_scr[sl] + m_prev
        m_t = jnp.maximum(inter_log, ml_scr[sl])
        w_inter = jnp.exp(inter_log - m_t)
        w_loc = jnp.exp(ml_scr[sl] - m_t)
        qb = q_scr[sl]
        num = w_inter * _bdot(qb, c_state.astype(BF16)) + w_loc * v_scr[sl]
        q_n = jnp.broadcast_to(jnp.sum(qb.astype(F32) * n_state, axis=2, keepdims=True), m_t.shape)
        nq = w_inter * q_n + w_loc * rs_scr[sl]
        v_scr[sl] = num / jnp.maximum(jnp.abs(nq), jnp.exp(-m_t))
        m_new = m_t[:, rows - 1:rows]
        w_prev = jnp.exp(bl_scr[sl] + m_prev - m_new)
        w_upd = jnp.exp(mu_scr[sl] - m_new)
        n_ref[...] = w_prev * n_state + w_upd * nu_scr[sl]
        m_ref[...] = m_new
        return w_prev * c_state + w_upd * u_scr[sl]

    if has_state:
        c_init = st0[0][...]
        n_ref[...] = st0[1][...]
        m_ref[...] = st0[2][...]
    else:
        c_init = jnp.zeros((geo.seqs, D_HEAD, D_HEAD), F32)
        n_ref[...] = jnp.zeros(n_ref.shape, F32)
        m_ref[...] = jnp.zeros(m_ref.shape, F32)
    c_ref[...] = geo.scan_chunks(step, c_init)

    def epilogue(g):
        sl = geo.rows_of(g)
        geo.store(y_ref, g, _rms(_sigmoid(gate_scr[sl]) * v_scr[sl], nw_ref[...]), relay)

    geo.for_groups(epilogue)


def _ret_kernel(*refs, layer, geo, has_state, fresh):
    ((h_ref,), w_refs, (cos_ref, sin_ref, nw_ref), r0, _, (y_ref,), r_out,
     (q_scr, o_scr, u_scr, gate_scr), relay) = _split_refs(
        refs, (1, 4, 3, int(has_state), int(not fresh), 1, 1, 4, len(refs)))
    (r_ref,) = _state_views(r_out, ['mat'], layer, fresh)
    rows, width, valid = geo.rows, geo.width, geo.valid
    head = pl.program_id(1)
    log_gamma = jnp.zeros((1, 1, 1), F32)
    for h in range(N_HEAD):
        log_gamma = jnp.where(head == h, math.log(1.0 - 2.0 ** (-5.0 - h)), log_gamma)
    scale = D_HEAD ** -0.5
    t_i = lax.broadcasted_iota(jnp.int32, (1, rows, width), 1)
    s_i = lax.broadcasted_iota(jnp.int32, (1, rows, width), 2)
    decay = jnp.where(s_i <= t_i, jnp.exp((t_i - s_i).astype(F32) * log_gamma), 0.0)
    t_col = lax.broadcasted_iota(jnp.int32, (1, rows, D_HEAD), 1).astype(F32)
    w_in = jnp.exp((t_col + 1.0) * log_gamma)
    w_st = jnp.exp((valid - 1.0 - t_col) * log_gamma)
    g_all = jnp.exp(valid * log_gamma)
    w = _join_columns(w_refs)

    def rotate(x, cos, sin):
        flat = x.reshape(x.shape[0] * rows, D_HEAD)
        return x * cos + pltpu.roll(flat, D_HEAD // 2, axis=1).reshape(x.shape) * sin

    def phase_a(g):
        q, k, v, rg = geo.split(geo.project(h_ref, w, g), relay)
        if geo.chunks > 1:
            cos, sin = cos_ref[geo.rows_of(g)], sin_ref[geo.rows_of(g)]
        else:
            cos, sin = cos_ref[...], sin_ref[...]
        qr = rotate(q, cos, sin).astype(BF16)
        kr = rotate(k, cos, sin) * scale
        vb = _pad_rows(v, width).astype(BF16)
        a = _bdot_nt(qr, _pad_rows(kr, width).astype(BF16)) * decay
        sl = geo.rows_of(g)
        gate_scr[sl] = rg
        q_scr[sl] = qr
        o_scr[sl] = _bdot(a.astype(BF16), vb)
        u_scr[sl] = _bdot_tn(_pad_rows(kr * w_st, width).astype(BF16), vb)

    geo.for_groups_static(phase_a)

    def step(sl, state):
        o_scr[sl] = o_scr[sl] + _bdot(q_scr[sl], state.astype(BF16)) * w_in
        return g_all * state + u_scr[sl]

    init = r0[0][...] if has_state else jnp.zeros((geo.seqs, D_HEAD, D_HEAD), F32)
    r_ref[...] = geo.scan_chunks(step, init)

    def epilogue(g):
        sl = geo.rows_of(g)
        geo.store(y_ref, g, _rms(o_scr[sl], nw_ref[...]) * _silu(gate_scr[sl]), relay)

    geo.for_groups(epilogue)


def _mixer_spec(kernel, name, geo, layer, h, weights, extra, states_in, states_prev, scratch):
    seqs, n_cols = geo.seqs, len(weights) * D_HEAD
    fresh = isinstance(states_prev[0], jax.ShapeDtypeStruct)

    def weight_spec(first, per_head):
        return pl.BlockSpec((None, D_MODEL, D_HEAD), lambda bi, j: (layer, 0, first + per_head * j))

    def state_spec(arr, layered=False):
        lead, first = (arr.shape[0], 0) if layered else (None, layer)
        if len(arr.shape) == 5:
            return pl.BlockSpec((lead, seqs, None, D_HEAD, D_HEAD), lambda bi, j: (first, bi, j, 0, 0))
        return pl.BlockSpec((lead, seqs, N_HEAD, D_HEAD), lambda bi, j: (first, bi, 0, 0))

    has_state = states_in is not None
    args = [h] + [a for a, _, _ in weights] + [a for a, _ in extra]
    in_specs = [pl.BlockSpec((geo.tokens, D_MODEL), lambda bi, j: (bi, 0))] + [
        weight_spec(first, per_head) for _, first, per_head in weights] + [s for _, s in extra]
    if has_state:
        args += list(states_in)
        in_specs += [state_spec(a) for a in states_in]
    n_in = len(args)
    if not fresh:
        args += list(states_prev)
        in_specs += [pl.BlockSpec(memory_space=pl.ANY) for _ in states_prev]
    out_specs = [pl.BlockSpec((geo.tokens, D_HEAD), lambda bi, j: (bi, j))] + [
        state_spec(a, fresh) for a in states_prev]
    out_shape = [jax.ShapeDtypeStruct((h.shape[0], MIX_W), F32 if geo.padded else BF16)] + [
        jax.ShapeDtypeStruct(a.shape, F32) for a in states_prev]
    scratch_shapes = list(scratch) + [pltpu.VMEM((geo.problems, geo.rows, D_HEAD), F32)]
    if geo.padded:
        scratch_shapes += [pltpu.VMEM((geo.group_tokens, D_HEAD), F32),
                           pltpu.VMEM((seqs * geo.rows, D_HEAD), F32)] * (n_cols // D_HEAD)
    return dict(
        kernel=functools.partial(kernel, layer=layer, geo=geo, has_state=has_state, fresh=fresh), name=name,
        args=args, in_specs=in_specs, out_specs=out_specs, out_shape=out_shape, scratch_shapes=scratch_shapes,
        aliases={} if fresh else {n_in + i: 1 + i for i in range(len(states_prev))})


def _run_mixers(geo, specs):
    n_ins = [len(sp["args"]) - 1 for sp in specs]
    n_outs = [len(sp["out_shape"]) for sp in specs]
    n_scr = [len(sp["scratch_shapes"]) for sp in specs]
    in_off, out_off, scr_off = (np.concatenate([[0], np.cumsum(c)]).tolist() for c in (n_ins, n_outs, n_scr))
    total_in, total_out = 1 + in_off[-1], out_off[-1]

    def body(*refs):
        ins, outs, scr = refs[1:total_in], refs[total_in:total_in + total_out], refs[total_in + total_out:]
        for m, sp in enumerate(specs):
            sp["kernel"](refs[0], *ins[in_off[m]:in_off[m + 1]], *outs[out_off[m]:out_off[m + 1]],
                         *scr[scr_off[m]:scr_off[m + 1]])

    aliases = {}
    for m, sp in enumerate(specs):
        for i, o in sp["aliases"].items():
            aliases[i + in_off[m]] = o + out_off[m]
    flat = pl.pallas_call(
        body,
        grid=(geo.batch // geo.seqs, N_HEAD),
        in_specs=specs[0]["in_specs"][:1] + [s for sp in specs for s in sp["in_specs"][1:]],
        out_specs=[s for sp in specs for s in sp["out_specs"]],
        out_shape=[s for sp in specs for s in sp["out_shape"]],
        scratch_shapes=[s for sp in specs for s in sp["scratch_shapes"]],
        input_output_aliases=aliases,
        compiler_params=pltpu.CompilerParams(
            dimension_semantics=("arbitrary", "arbitrary"), vmem_limit_bytes=VMEM_LIMIT),
        name="+".join(sp["name"] for sp in specs),
    )(specs[0]["args"][0], *[a for sp in specs for a in sp["args"][1:]])
    return [flat[out_off[m]:out_off[m + 1]] for m in range(len(specs))]


def _head_block(rows):
    return pl.BlockSpec((rows, D_HEAD), lambda bi, j: (0, j))


def _part_blocks(w_main, mixer):
    return [(w_main, (mixer * 4 + p) * N_HEAD, 1) for p in range(4)]


def _hgrn(geo, layer, h, w_main, hgrn_lb, nw, state_in, state_prev):
    n = geo.problems
    scratch = [pltpu.VMEM((n, geo.rows, D_HEAD), BF16), pltpu.VMEM((n, geo.rows, D_HEAD), F32),
               pltpu.VMEM((n, D_HEAD, D_HEAD), F32), pltpu.VMEM((n, 1, D_HEAD), F32)]
    place = _hgrn_placement(geo)
    return _mixer_spec(
        _hgrn_kernel, "hgrn2", geo, layer, h, _part_blocks(w_main, 0),
        [(hgrn_lb, _head_block(hgrn_lb.shape[0])), (nw, _head_block(1)), (place, _resident(place.shape))],
        None if state_in is None else [state_in], [state_prev], scratch)


def _mlstm(geo, layer, h, w_main, gates, gate_b, nw, states_in, states_prev):
    n = geo.problems
    col = lambda r: pltpu.VMEM((n, r, LANES), F32)
    scratch = [pltpu.VMEM((n, geo.rows, D_HEAD), BF16), pltpu.VMEM((n, geo.rows, D_HEAD), F32),
               pltpu.VMEM((n, D_HEAD, D_HEAD), F32), pltpu.VMEM((n, 1, D_HEAD), F32),
               col(geo.rows), col(geo.rows), col(geo.rows), col(1), col(1)]
    return _mixer_spec(
        _mlstm_kernel, "mlstm", geo, layer, h, _part_blocks(w_main, 1),
        [(gates, pl.BlockSpec((geo.tokens, LANES), lambda bi, j: (bi, 0))),
         (gate_b, pl.BlockSpec((1, LANES), lambda bi, j: (0, 0))), (nw, _head_block(1))],
        states_in, states_prev, scratch)


def _retention(geo, layer, h, w_main, cos, sin, nw, state_in, state_prev):
    n = geo.problems
    scratch = [pltpu.VMEM((n, geo.rows, D_HEAD), BF16), pltpu.VMEM((n, geo.rows, D_HEAD), F32),
               pltpu.VMEM((n, D_HEAD, D_HEAD), F32)]
    table = pl.BlockSpec(cos.shape, lambda bi, j: (0, 0, 0))
    return _mixer_spec(
        _ret_kernel, "retention", geo, layer, h, _part_blocks(w_main, 2),
        [(cos, table), (sin, table), (nw, _head_block(1))],
        None if state_in is None else [state_in], [state_prev], scratch)


def _merge_kernel(x_ref, h_ref, yh_ref, ym_ref, yr_ref, wg_ref, bg_ref, wb_ref, wo_ref, n2_ref, x1_ref, h2_ref):
    hb = h_ref[...]
    merged = None
    for j, y_ref in enumerate((yh_ref, ym_ref, yr_ref)):
        cols = slice(j * D_MODEL, (j + 1) * D_MODEL)
        gate = _sigmoid(_dot(hb, wg_ref[:, cols]) + bg_ref[:, cols])
        term = gate * _dot(y_ref[...].astype(BF16), wb_ref[j])
        merged = term if merged is None else merged + term
    x1 = x_ref[...] + _dot(merged.astype(BF16), wo_ref[...])
    x1_ref[...] = x1
    h2_ref[...] = _rms(x1, n2_ref[...]).astype(BF16)


def _merge(x, h, yh, ym, yr, w_bgate, gate_b, w_branch, w_out, norm2_w):
    n = x.shape[0]
    tm = min(MERGE_TOKEN_TILE, n)
    gate_w = N_BRANCH * D_MODEL
    row = lambda width: pl.BlockSpec((tm, width), lambda i: (i, 0))
    vec = lambda width: pl.BlockSpec((1, width), lambda i: (0, 0))
    return pl.pallas_call(
        _merge_kernel,
        grid=(n // tm,),
        in_specs=[
            row(D_MODEL), row(D_MODEL), row(MIX_W), row(MIX_W), row(MIX_W),
            _resident((D_MODEL, gate_w)), vec(gate_w),
            _resident((N_BRANCH, MIX_W, D_MODEL)),
            _resident((D_MODEL, D_MODEL)),
            vec(D_MODEL),
        ],
        out_specs=[row(D_MODEL), row(D_MODEL)],
        out_shape=[jax.ShapeDtypeStruct((n, D_MODEL), F32), jax.ShapeDtypeStruct((n, D_MODEL), BF16)],
        compiler_params=pltpu.CompilerParams(
            dimension_semantics=("arbitrary",), vmem_limit_bytes=VMEM_LIMIT),
        name="merge",
    )(x, h, yh, ym, yr, w_bgate, gate_b, w_branch, w_out, norm2_w)


def _ffn_kernel(*refs, seq_len, has_state, final_norm, tail_rows):
    refs = list(refs)
    x1_ref, h2_ref, wg_ref, wu_ref, wd_ref, cw_ref, cb_ref, nw_ref = refs[:8]
    refs = refs[8:]
    p1_ref = p2_ref = None
    if has_state:
        p1_ref, p2_ref = refs[:2]
        refs = refs[2:]
    if final_norm:
        out_ref, tail_ref, carry_scr, a_scr = refs
    else:
        wgn_ref, out_ref, hn_ref, gn_ref, tail_ref, carry_scr, a_scr = refs
    tm = x1_ref.shape[0]
    i = pl.program_id(0)

    @pl.when(i == 0)
    def _():
        carry_scr[...] = jnp.zeros(carry_scr.shape, F32)

    t = (i * tm + lax.broadcasted_iota(jnp.int32, (tm, FF_TILE), 0)) & (seq_len - 1)
    h2 = h2_ref[...]
    acc = x1_ref[...]
    for j in range(D_FF // FF_TILE):
        cols = slice(j * FF_TILE, (j + 1) * FF_TILE)
        a = _dot(h2, wg_ref[:, cols])
        u = _dot(h2, wu_ref[:, cols])
        a_scr[0:SUBLANES, :] = carry_scr[:, cols]
        a_scr[SUBLANES:, :] = a
        carry_scr[:, cols] = a[tm - SUBLANES:, :]
        prev1 = jnp.where(t >= 1, a_scr[SUBLANES - 1:SUBLANES - 1 + tm, :], 0.0)
        prev2 = jnp.where(t >= 2, a_scr[SUBLANES - 2:SUBLANES - 2 + tm, :], 0.0)
        if has_state:
            prev1 = prev1 + p1_ref[:, cols]
            prev2 = prev2 + p2_ref[:, cols]
        conv = prev2 * cw_ref[0:1, cols] + prev1 * cw_ref[1:2, cols] + a * cw_ref[2:3, cols] + cb_ref[:, cols]
        acc = acc + _dot((_silu(conv) * u).astype(BF16), wd_ref[cols, :])
        if tail_rows == tm:
            tail_ref[:, cols] = a
        else:
            tail_ref[0, :, cols] = a[tm - tail_rows:, :]
    if final_norm:
        out_ref[...] = _rms(acc, nw_ref[...])
    else:
        out_ref[...] = acc
        hn = _rms(acc, nw_ref[...]).astype(BF16)
        hn_ref[...] = hn
        gn_ref[...] = _dot(hn, wgn_ref[...])


def _ffn(x1, h2, wg, wu, wd, conv_w, conv_b, seq_len, conv_state, norm_w, next_w_gate):
    final_norm = next_w_gate is None
    n = x1.shape[0]
    has_state = conv_state is not None
    tm = min(TOKEN_TILE // 2 if has_state else TOKEN_TILE, n)
    assert n % tm == 0 and seq_len & (seq_len - 1) == 0
    batch = n // seq_len
    row = lambda width: pl.BlockSpec((tm, width), lambda i: (i, 0))
    full = lambda shape: pl.BlockSpec(shape, lambda i: (0,) * len(shape))
    in_specs = [row(D_MODEL), row(D_MODEL), _resident((D_MODEL, D_FF)), _resident((D_MODEL, D_FF)),
                _resident((D_FF, D_MODEL)), full((CONV_W, D_FF)), full((1, D_FF)), full((1, D_MODEL))]
    args = [x1, h2, wg, wu, wd, conv_w, conv_b, norm_w]
    if has_state:
        p1 = jnp.pad(conv_state[:, 1:2], ((0, 0), (0, seq_len - 1), (0, 0))).reshape(n, D_FF)
        p2 = jnp.pad(conv_state, ((0, 0), (0, seq_len - (CONV_W - 1)), (0, 0))).reshape(n, D_FF)
        in_specs += [row(D_FF), row(D_FF)]
        args += [p1, p2]
    if seq_len % tm == 0:
        tail_rows = CONV_W - 1
        tail_spec = pl.BlockSpec((1, tail_rows, D_FF), lambda i: ((i * tm) // seq_len, 0, 0))
        tail_shape = jax.ShapeDtypeStruct((batch, tail_rows, D_FF), F32)
    else:
        assert tm % seq_len == 0 and seq_len >= CONV_W - 1
        tail_rows = tm
        tail_spec = row(D_FF)
        tail_shape = jax.ShapeDtypeStruct((n, D_FF), F32)
    out_specs, out_shape = [row(D_MODEL)], [jax.ShapeDtypeStruct((n, D_MODEL), F32)]
    if not final_norm:
        in_specs.append(_resident((D_MODEL, LANES)))
        args.append(next_w_gate)
        out_specs += [row(D_MODEL), row(LANES)]
        out_shape += [jax.ShapeDtypeStruct((n, D_MODEL), BF16), jax.ShapeDtypeStruct((n, LANES), F32)]
    *outs, tail = pl.pallas_call(
        functools.partial(_ffn_kernel, seq_len=seq_len, has_state=has_state,
                          final_norm=final_norm, tail_rows=tail_rows),
        grid=(n // tm,),
        in_specs=in_specs,
        out_specs=out_specs + [tail_spec],
        out_shape=out_shape + [tail_shape],
        scratch_shapes=[pltpu.VMEM((SUBLANES, D_FF), F32), pltpu.VMEM((tm + SUBLANES, FF_TILE), F32)],
        compiler_params=pltpu.CompilerParams(
            dimension_semantics=("arbitrary",), vmem_limit_bytes=VMEM_LIMIT),
        name="conv_ffn",
    )(*args)
    if tail_rows == tm:
        tail = tail.reshape(batch, seq_len, D_FF)[:, seq_len - (CONV_W - 1):, :]
    return outs, tail


def _prep_weights(w_in, w_branch, w_out, ffn_w_gate, ffn_w_up, ffn_w_down):
    w_in = w_in.astype(BF16)
    w_main = jnp.concatenate([w_in[:, :, HG_OFF:ML_GATE_OFF], w_in[:, :, RET_OFF:GATE_OFF]], axis=2)
    w_gate = jnp.pad(w_in[:, :, ML_GATE_OFF:RET_OFF], ((0, 0), (0, 0), (0, LANES - 2 * N_HEAD)))
    return (w_main, w_gate,
            w_in[:, :, GATE_OFF:], w_branch.astype(BF16), w_out.astype(BF16),
            ffn_w_gate.astype(BF16), ffn_w_up.astype(BF16), ffn_w_down.astype(BF16))


def _trunk(x3, pos0, states, params, prepped):
    (norm1_w, mlstm_gate_b, branch_gate_b, hgrn_lb, hgrn_norm_w, mlstm_norm_w, ret_norm_w,
     norm2_w, ffn_conv_w, ffn_conv_b, final_norm_w) = params
    w_main, w_gate, w_bgate, w_branch, w_out, w_ffg, w_ffu, w_ffd = prepped
    batch, seq_len, _ = x3.shape
    n = batch * seq_len
    depth = norm1_w.shape[0]
    geo = _Geo(batch, seq_len)
    cos, sin = (t.reshape(geo.chunks, geo.rows, D_HEAD) for t in _rope_tables(geo.chunks * geo.rows, pos0))

    mat = jax.ShapeDtypeStruct((depth, batch, N_HEAD, D_HEAD, D_HEAD), F32)
    vec = jax.ShapeDtypeStruct((depth, batch, N_HEAD, D_HEAD), F32)
    if states is None:
        hg_in = ml_in = ret_in = conv_in = None
    else:
        hg_in, ml_c, ml_n, ml_m, ret_in, conv_in = states
        ml_in = [ml_c, ml_n, jnp.broadcast_to(ml_m[..., None], vec.shape)]
    hg_out, c_out, r_out, n_out, m_out = mat, mat, mat, vec, vec

    x = x3.reshape(n, D_MODEL)
    h, gates = _norm(x, norm1_w[0:1], w_gate[0])
    conv_tails = []
    for l in range(depth):
        last = l == depth - 1
        gate_b = jnp.pad(mlstm_gate_b[l:l + 1], ((0, 0), (0, LANES - 2 * N_HEAD)))
        specs = [_hgrn(geo, l, h, w_main, hgrn_lb, hgrn_norm_w[l:l + 1], hg_in, hg_out),
                 _mlstm(geo, l, h, w_main, gates, gate_b, mlstm_norm_w[l:l + 1], ml_in, [c_out, n_out, m_out]),
                 _retention(geo, l, h, w_main, cos, sin, ret_norm_w[l:l + 1], ret_in, r_out)]
        runs = [specs] if not geo.padded else [[sp] for sp in specs]
        (y_hg, hg_out), (y_ml, c_out, n_out, m_out), (y_ret, r_out) = [
            out for run in runs for out in _run_mixers(geo, run)]
        x1, h2 = _merge(x, h, y_hg, y_ml, y_ret, w_bgate[l], branch_gate_b[l:l + 1], w_branch[l], w_out[l],
                        norm2_w[l:l + 1])
        outs, conv_tail = _ffn(x1, h2, w_ffg[l], w_ffu[l], w_ffd[l], ffn_conv_w[l], ffn_conv_b[l:l + 1],
                               seq_len, None if conv_in is None else conv_in[l],
                               final_norm_w[None, :] if last else norm1_w[l + 1:l + 2],
                               None if last else w_gate[l + 1])
        x = outs[0]
        h, gates = (None, None) if last else outs[1:]
        conv_tails.append(conv_tail)
    return x.reshape(batch, seq_len, D_MODEL), (
        hg_out, c_out, n_out, m_out[:, :, :, 0], r_out,
        jnp.stack(conv_tails))


def kernel(x_prompt, x_sample, state_hgrn, state_mlstm_C, state_mlstm_n, state_mlstm_m, state_ret,
           state_ffn_conv, norm1_w, w_in, mlstm_gate_b, branch_gate_b, hgrn_lb, hgrn_norm_w,
           mlstm_norm_w, ret_norm_w, w_branch, w_out, norm2_w, ffn_w_gate, ffn_w_up, ffn_conv_w,
           ffn_conv_b, ffn_w_down, final_norm_w):
    params = (norm1_w, mlstm_gate_b, branch_gate_b, hgrn_lb, hgrn_norm_w, mlstm_norm_w, ret_norm_w,
              norm2_w, ffn_conv_w, ffn_conv_b, final_norm_w)
    prepped = _prep_weights(w_in, w_branch, w_out, ffn_w_gate, ffn_w_up, ffn_w_down)
    y_p, (hg_p, c_p, n_p, m_p, r_p, cv_p) = _trunk(x_prompt, 0, None, params, prepped)
    y_s, (hg_s, c_s, n_s, m_s, r_s, cv_s) = _trunk(
        x_sample, PAST_LEN,
        (state_hgrn, state_mlstm_C, state_mlstm_n, state_mlstm_m, state_ret, state_ffn_conv),
        params, prepped)
    return (y_p, y_s, hg_p, hg_s, c_p, c_s, n_p, n_s, m_p, m_s, r_p, r_s, cv_p, cv_s)
```

```python
import functools
import math

import numpy as np
import jax
import jax.numpy as jnp
from jax import lax
from jax.experimental import pallas as pl
from jax.experimental.pallas import tpu as pltpu

F32 = jnp.float32
BF16 = jnp.bfloat16

D_MODEL = 1024
MIX_W = 512
N_HEAD = 4
D_HEAD = 128
N_BRANCH = 3
D_FF = 2816
CONV_W = 3
CHUNK = 64
EPS = 1e-6
ROPE_BASE = 10000.0
PAST_LEN = 16384

HG_OFF = 0
ML_OFF = 4 * MIX_W
ML_GATE_OFF = ML_OFF + 4 * MIX_W
RET_OFF = ML_GATE_OFF + 2 * N_HEAD
GATE_OFF = RET_OFF + 4 * MIX_W
LOG2E = 1.4426950408889634

LANES = 128
SUBLANES = 8
VMEM_LIMIT = 56 * 1024 * 1024

TOKEN_TILE = 512
NORM_TOKEN_TILE = 1024
FF_TILE = 2816
HG_SUB = 16
GROUP = 32
SEQ_BLOCK = 64
SCAN_UNROLL = 32


def _resident(shape):
    return pl.BlockSpec(shape, lambda *_: (0,) * len(shape), pipeline_mode=pl.Buffered(1))


def _dot(a, b):
    return jnp.dot(a, b, preferred_element_type=F32)


def _dot_tn(a, b):
    return lax.dot_general(a, b, (((0,), (0,)), ((), ())), preferred_element_type=F32)


def _bdot(a, b):
    return jnp.einsum('nlk,nkv->nlv', a, b, preferred_element_type=F32)


def _bdot_nt(a, b):
    return jnp.einsum('nqd,nkd->nqk', a, b, preferred_element_type=F32)


def _bdot_tn(a, b):
    return jnp.stack([_dot_tn(a[i], b[i]) for i in range(a.shape[0])])


def _split3(x):
    hi = x.astype(BF16)
    r = x - hi.astype(F32)
    mid = r.astype(BF16)
    lo = (r - mid.astype(F32)).astype(BF16)
    return hi, mid, lo


def _sigmoid(x):
    return 1.0 / (1.0 + jnp.exp(-x))


def _silu(x):
    return x * _sigmoid(x)


def _log_sigmoid(x):
    return jnp.minimum(x, 0.0) - jnp.log(1.0 + jnp.exp(-jnp.abs(x)))


def _rms(x, w):
    return x * lax.rsqrt(jnp.mean(x * x, axis=-1, keepdims=True) + EPS) * w


def _pad_rows(x, n):
    if x.shape[1] >= n:
        return x
    return jnp.concatenate([x, jnp.zeros((x.shape[0], n - x.shape[1], x.shape[2]), x.dtype)], axis=1)


def _seg_cumsum(x, seg):
    shape = x.shape
    flat = x.reshape(shape[0] * seg, shape[2])
    row = lax.broadcasted_iota(jnp.int32, flat.shape, 0) & (seg - 1)
    s = 1
    while s < seg:
        flat = flat + jnp.where(row >= s, pltpu.roll(flat, s, axis=0), 0.0)
        s *= 2
    return flat.reshape(shape)


def _through_vmem(x):
    def body(ref):
        ref[...] = x
        return ref[...]
    return pl.run_scoped(body, pltpu.VMEM(x.shape, x.dtype))


def _row_bcast(col, width):
    n, rows, _ = col.shape
    vals = _through_vmem(_pad_rows(col, width))
    lane = lax.broadcasted_iota(jnp.int32, (n, rows, LANES), 2)
    pick = _through_vmem(jnp.where(lane == 0, 1.0, 0.0).astype(BF16))
    hi, mid, lo = _split3(vals)
    return _bdot_nt(pick, hi) + _bdot_nt(pick, mid) + _bdot_nt(pick, lo)


def _chunk_geometry(seq_len):
    if seq_len % CHUNK == 0:
        return CHUNK, CHUNK, CHUNK
    valid = math.gcd(seq_len, CHUNK)
    assert valid == seq_len and valid <= SUBLANES, "unsupported sequence length"
    return SUBLANES, valid, LANES


def _rope_kernel(inv_ref, cos_ref, sin_ref, *, pos0):
    shape = cos_ref.shape
    pos = lax.broadcasted_iota(jnp.int32, shape, 0).astype(F32) + pos0
    ang = pos * inv_ref[...]
    lane = lax.broadcasted_iota(jnp.int32, shape, 1)
    cos_ref[...] = jnp.cos(ang)
    sin_ref[...] = jnp.where(lane < D_HEAD // 2, -jnp.sin(ang), jnp.sin(ang))


def _rope_tables(rows, pos0):
    inv = ROPE_BASE ** (-jnp.linspace(0.0, 1.0, D_HEAD // 2, dtype=F32))
    inv2 = jnp.concatenate([inv, inv])[None, :]
    return pl.pallas_call(
        functools.partial(_rope_kernel, pos0=float(pos0)),
        out_shape=(jax.ShapeDtypeStruct((rows, D_HEAD), F32),) * 2,
        name="rope_tables",
    )(inv2)


def _norm_kernel(x_ref, w_ref, wg_ref, h_ref, g_ref):
    hb = _rms(x_ref[...], w_ref[...]).astype(BF16)
    h_ref[...] = hb
    g_ref[...] = _dot(hb, wg_ref[...])


def _norm(x, w, w_gate):
    n = x.shape[0]
    tm = min(NORM_TOKEN_TILE, n)
    assert n % tm == 0
    row = lambda width: pl.BlockSpec((tm, width), lambda i: (i, 0))
    return pl.pallas_call(
        _norm_kernel,
        grid=(n // tm,),
        in_specs=[row(D_MODEL), pl.BlockSpec((1, D_MODEL), lambda i: (0, 0)), _resident((D_MODEL, LANES))],
        out_specs=[row(D_MODEL), row(LANES)],
        out_shape=[jax.ShapeDtypeStruct((n, D_MODEL), BF16), jax.ShapeDtypeStruct((n, LANES), F32)],
        compiler_params=pltpu.CompilerParams(dimension_semantics=("arbitrary",), vmem_limit_bytes=VMEM_LIMIT),
        name="norm_in",
    )(x, w, w_gate)


class _Geo:
    def __init__(self, batch, seq_len):
        self.rows, self.valid, self.width = _chunk_geometry(seq_len)
        if self.valid == self.rows:
            self.seqs, self.chunks = 1, seq_len // self.rows
            self.group = min(GROUP, self.chunks)
        else:
            self.seqs, self.chunks = min(SEQ_BLOCK, batch), 1
            self.group = self.seqs
        assert batch % self.seqs == 0 and self.chunks % min(self.group, self.chunks) == 0
        self.batch = batch
        self.problems = self.seqs * self.chunks
        self.n_groups = self.problems // self.group
        self.group_tokens = self.group * self.valid
        self.tokens = self.problems * self.valid
        self.padded = self.valid < self.rows

    def project(self, h_ref, w, g):
        return _dot(h_ref[pl.ds(g * self.group_tokens, self.group_tokens), :], w)

    def split(self, z, relay):
        n_parts = z.shape[1] // D_HEAD
        cols = lambda p: slice(p * D_HEAD, (p + 1) * D_HEAD)
        if not self.padded:
            return [z[:, cols(p)].reshape(self.group, self.rows, D_HEAD) for p in range(n_parts)]
        out = []
        for p in range(n_parts):
            flat_ref, pad_ref = relay[2 * p], relay[2 * p + 1]
            flat_ref[...] = z[:, cols(p)]
            pad_ref[...] = jnp.zeros(pad_ref.shape, F32)
            for t in range(self.valid):
                pad_ref[pl.ds(t, self.seqs, stride=self.rows), :] = flat_ref[
                    pl.ds(t, self.seqs, stride=self.valid), :]
            out.append(pad_ref[...].reshape(self.seqs, self.rows, D_HEAD))
        return out

    def store(self, y_ref, g, y, relay):
        if not self.padded:
            n = self.group * self.rows
            y_ref[pl.ds(g * n, n), :] = y.reshape(n, D_HEAD).astype(y_ref.dtype)
            return
        pad = relay[1]
        pad[...] = y.reshape(self.seqs * self.rows, D_HEAD)
        for t in range(self.valid):
            y_ref[pl.ds(t, self.seqs, stride=self.valid), :] = pad[pl.ds(t, self.seqs, stride=self.rows), :].astype(
                y_ref.dtype)

    def rows_of(self, g):
        return pl.ds(g * self.group, self.group)

    def row_mask(self, shape):
        if not self.padded:
            return None
        return lax.broadcasted_iota(jnp.int32, shape, 1) < self.valid

    def for_groups(self, body):
        if self.n_groups == 1:
            body(0)
        else:
            lax.fori_loop(0, self.n_groups, lambda g, c: (body(g), c)[1], 0)

    def for_groups_static(self, body):
        for g in range(self.n_groups):
            body(g)

    def scan_chunks(self, step, init):
        if self.chunks == 1:
            return step(pl.ds(0, self.seqs), init)
        return lax.fori_loop(0, self.chunks, lambda c, carry: step(pl.ds(c * self.seqs, self.seqs), carry), init,
                             unroll=min(SCAN_UNROLL, self.chunks))


def _join_columns(w_refs):
    return jnp.concatenate([r[...] for r in w_refs], axis=1)


def _transpose_minor(x):
    return jnp.stack([x[i].T for i in range(x.shape[0])])


def _state_views(refs, kinds, layer, layered):
    head = pl.program_id(1)
    views = []
    for ref, kind in zip(refs, kinds):
        for l in (range(ref.shape[0]) if layered else [layer]):
            view = ref.at[l] if layered else ref
            if kind == 'vec':
                view = view.at[:, pl.ds(head, 1), :]
            if l == layer:
                views.append(view)
            else:
                view[...] = jnp.zeros(view.shape, F32)
    return views


def _split_refs(refs, counts):
    out, k = [], 0
    for c in counts:
        out.append(refs[k:k + c])
        k += c
    return out


def _hgrn_lower_bound(lb_ref, layer):
    x = lb_ref[...]
    e = jnp.exp(x - jnp.max(x, axis=0, keepdims=True))
    soft = e / jnp.sum(e, axis=0, keepdims=True)
    cum = soft[0:1]
    for j in range(1, layer + 1):
        cum = cum + soft[j:j + 1]
    return cum - soft[0:1]


def _hgrn_placement(geo):
    sub = min(HG_SUB, geo.rows)
    r = np.arange(sub * D_HEAD)[:, None] // D_HEAD
    c = np.arange(geo.width)[None, :]
    return jnp.asarray(np.stack([r + i * sub == c for i in range(geo.rows // sub)]), BF16)


def _hgrn_intra(q, k, b2, place_ref, geo):
    n = q.shape[0]
    sub = min(HG_SUB, geo.rows)
    sub_row = lax.broadcasted_iota(jnp.int32, (n, sub, geo.width), 1)
    lane = lax.broadcasted_iota(jnp.int32, (n, sub, geo.width), 2)
    blocks = []
    for i in range(geo.rows // sub):
        r0 = i * sub
        b_i, q_i, k_i = (a[:, r0:r0 + sub] for a in (b2, q, k))
        terms = []
        for s in range(sub):
            lo = s // SUBLANES * SUBLANES
            e = jnp.exp2(jnp.minimum(b_i[:, lo:] - b_i[:, s:s + 1], 0.0))
            term = q_i[:, lo:] * e * k_i[:, s:s + 1]
            if lo:
                term = jnp.concatenate([jnp.zeros((n, lo, D_HEAD), F32), term], axis=1)
            terms.append(term)
        cat = jnp.concatenate(terms, axis=2).reshape(n * sub, sub * D_HEAD)
        a_i = _dot(cat.astype(BF16), place_ref[i]).reshape(n, sub, geo.width)
        a_i = jnp.where(sub_row >= lane - r0, a_i, 0.0)
        if i > 0:
            b_ref = b_i[:, 0:1]
            q_dec = q_i * jnp.exp2(b_i - b_ref)
            k_dec = _pad_rows(k[:, 0:r0] * jnp.exp2(b_ref - b2[:, 0:r0]), geo.width)
            a_i = jnp.where(lane < r0, _bdot_nt(q_dec.astype(BF16), k_dec.astype(BF16)), a_i)
        blocks.append(a_i)
    return blocks[0] if len(blocks) == 1 else jnp.concatenate(blocks, axis=1)


def _hgrn_kernel(*refs, layer, geo, has_state, fresh):
    ((h_ref,), w_refs, (lb_ref, nw_ref, place), s0, _, (y_ref,), s_out,
     (qt_scr, o_scr, u_scr, dc_scr, gate_scr), relay) = _split_refs(
        refs, (1, 4, 3, int(has_state), int(not fresh), 1, 1, 5, len(refs)))
    (s_ref,) = _state_views(s_out, ['mat'], layer, fresh)
    lb = _hgrn_lower_bound(lb_ref, layer)
    rows, width = geo.rows, geo.width
    w = _join_columns(w_refs)

    def phase_a(g):
        hq, hf, hi, hg = geo.split(geo.project(h_ref, w, g), relay)
        q = _silu(hq)
        f = lb + (1.0 - lb) * _sigmoid(hf)
        lf = jnp.log(f)
        k = 1.0 - f
        mask = geo.row_mask(lf.shape)
        if mask is not None:
            lf = jnp.where(mask, lf, 0.0)
            k = jnp.where(mask, k, 0.0)
        b2 = _seg_cumsum(lf, rows) * LOG2E
        sl = geo.rows_of(g)
        gate_scr[sl] = hg
        qt_scr[sl] = (q * jnp.exp2(b2)).astype(BF16)
        a = _hgrn_intra(q, k, b2, place, geo)
        vb = _pad_rows(hi, width).astype(BF16)
        o_scr[sl] = _bdot(a.astype(BF16), vb)
        b_last = b2[:, rows - 1:rows]
        k_w = k * jnp.exp2(b_last - b2)
        u_scr[sl] = _bdot_tn(vb, _pad_rows(k_w, width).astype(BF16))
        dc_scr[sl] = jnp.exp2(b_last)

    geo.for_groups_static(phase_a)

    def step(sl, state_t):
        o_scr[sl] = o_scr[sl] + _bdot_nt(qt_scr[sl], state_t.astype(BF16))
        return dc_scr[sl] * state_t + u_scr[sl]

    init = _transpose_minor(s0[0][...]) if has_state else jnp.zeros((geo.seqs, D_HEAD, D_HEAD), F32)
    s_ref[...] = _transpose_minor(geo.scan_chunks(step, init))

    def epilogue(g):
        sl = geo.rows_of(g)
        geo.store(y_ref, g, _rms(o_scr[sl], nw_ref[...]) * _silu(gate_scr[sl]), relay)

    geo.for_groups(epilogue)


def _mlstm_kernel(*refs, layer, geo, has_state, fresh):
    ((h_ref,), w_refs, (g_ref, gb_ref, nw_ref), st0, _, (y_ref,), st_out,
     (q_scr, v_scr, u_scr, nu_scr, bc_scr, ml_scr, rs_scr, bl_scr, mu_scr, gate_scr), relay) = _split_refs(
        refs, (1, 4, 3, 3 * int(has_state), 3 * int(not fresh), 1, 3, 10, len(refs)))
    kinds = ['mat', 'vec', 'vec']
    c_ref, n_ref, m_ref = _state_views(st_out, kinds, layer, fresh)
    st0 = _state_views(st0, kinds, layer, False) if has_state else st0
    rows, width = geo.rows, geo.width
    head = pl.program_id(1)
    scale = D_HEAD ** -0.5
    t_i = lax.broadcasted_iota(jnp.int32, (1, rows, width), 1)
    s_i = lax.broadcasted_iota(jnp.int32, (1, rows, width), 2)
    ok = s_i <= t_i
    if geo.padded:
        ok = ok & (s_i < geo.valid)
    w = _join_columns(w_refs)

    def phase_a(g):
        q, k, v, og = geo.split(geo.project(h_ref, w, g), relay)
        (gates,) = geo.split(g_ref[pl.ds(g * geo.group_tokens, geo.group_tokens), :], relay)
        gates = gates + gb_ref[...]
        lf_all = _log_sigmoid(gates)
        mask = geo.row_mask(gates.shape)
        if mask is not None:
            lf_all = jnp.where(mask, lf_all, 0.0)
        bc_all = _seg_cumsum(lf_all, rows)
        lane = lax.broadcasted_iota(jnp.int32, gates.shape, 2)
        wide = lambda x: jnp.broadcast_to(x, gates.shape)
        ig = wide(jnp.sum(jnp.where(lane == head, gates, 0.0), axis=2, keepdims=True))
        bc_col = jnp.sum(jnp.where(lane == head + N_HEAD, bc_all, 0.0), axis=2, keepdims=True)
        bc = wide(bc_col)
        log_d = jnp.where(ok, bc_col + _row_bcast(ig - bc, width), -jnp.inf)
        m_col = jnp.max(log_d, axis=2, keepdims=True)
        m_loc = wide(m_col)
        k = k * scale
        qb = q.astype(BF16)
        vb = _pad_rows(v, width).astype(BF16)
        s_m = _bdot_nt(qb, _pad_rows(k, width).astype(BF16)) * jnp.exp(log_d - m_col)
        sl = geo.rows_of(g)
        gate_scr[sl] = og
        q_scr[sl] = qb
        v_scr[sl] = _bdot(s_m.astype(BF16), vb)
        rs_scr[sl] = wide(jnp.sum(s_m, axis=2, keepdims=True))
        bc_scr[sl] = bc
        ml_scr[sl] = m_loc
        b_last = bc[:, rows - 1:rows]
        m_upd = m_loc[:, rows - 1:rows]
        w_exp = b_last - bc + ig - m_upd
        if mask is not None:
            w_exp = jnp.where(mask, w_exp, -jnp.inf)
        k_w = k * jnp.exp(w_exp)
        u_scr[sl] = _bdot_tn(_pad_rows(k_w, width).astype(BF16), vb)
        nu_scr[sl] = jnp.sum(k_w, axis=1, keepdims=True)
        bl_scr[sl] = b_last
        mu_scr[sl] = m_upd

    geo.for_groups_static(phase_a)

    def step(sl, c_state):
        n_state, m_prev = n_ref[...], m_ref[...]
        inter_log = bc_scr[sl] + m_prev
        m_t = jnp.maximum(inter_log, ml_scr[sl])
        w_inter = jnp.exp(inter_log - m_t)
        w_loc = jnp.exp(ml_scr[sl] - m_t)
        qb = q_scr[sl]
        num = w_inter * _bdot(qb, c_state.astype(BF16)) + w_loc * v_scr[sl]
        q_n = jnp.broadcast_to(jnp.sum(qb.astype(F32) * n_state, axis=2, keepdims=True), m_t.shape)
        nq = w_inter * q_n + w_loc * rs_scr[sl]
        v_scr[sl] = num / jnp.maximum(jnp.abs(nq), jnp.exp(-m_t))
        m_new = m_t[:, rows - 1:rows]
        w_prev = jnp.exp(bl_scr[sl] + m_prev - m_new)
        w_upd = jnp.exp(mu_scr[sl] - m_new)
        n_ref[...] = w_prev * n_state + w_upd * nu_scr[sl]
        m_ref[...] = m_new
        return w_prev * c_state + w_upd * u_scr[sl]

    if has_state:
        c_init = st0[0][...]
        n_ref[...] = st0[1][...]
        m_ref[...] = st0[2][...]
    else:
        c_init = jnp.zeros((geo.seqs, D_HEAD, D_HEAD), F32)
        n_ref[...] = jnp.zeros(n_ref.shape, F32)
        m_ref[...] = jnp.zeros(m_ref.shape, F32)
    c_ref[...] = geo.scan_chunks(step, c_init)

    def epilogue(g):
        sl = geo.rows_of(g)
        geo.store(y_ref, g, _rms(_sigmoid(gate_scr[sl]) * v_scr[sl], nw_ref[...]), relay)

    geo.for_groups(epilogue)


def _ret_kernel(*refs, layer, geo, has_state, fresh):
    ((h_ref,), w_refs, (cos_ref, sin_ref, nw_ref), r0, _, (y_ref,), r_out,
     (q_scr, o_scr, u_scr, gate_scr), relay) = _split_refs(
        refs, (1, 4, 3, int(has_state), int(not fresh), 1, 1, 4, len(refs)))
    (r_ref,) = _state_views(r_out, ['mat'], layer, fresh)
    rows, width, valid = geo.rows, geo.width, geo.valid
    head = pl.program_id(1)
    log_gamma = jnp.zeros((1, 1, 1), F32)
    for h in range(N_HEAD):
        log_gamma = jnp.where(head == h, math.log(1.0 - 2.0 ** (-5.0 - h)), log_gamma)
    scale = D_HEAD ** -0.5
    t_i = lax.broadcasted_iota(jnp.int32, (1, rows, width), 1)
    s_i = lax.broadcasted_iota(jnp.int32, (1, rows, width), 2)
    decay = jnp.where(s_i <= t_i, jnp.exp((t_i - s_i).astype(F32) * log_gamma), 0.0)
    t_col = lax.broadcasted_iota(jnp.int32, (1, rows, D_HEAD), 1).astype(F32)
    w_in = jnp.exp((t_col + 1.0) * log_gamma)
    w_st = jnp.exp((valid - 1.0 - t_col) * log_gamma)
    g_all = jnp.exp(valid * log_gamma)
    w = _join_columns(w_refs)

    def rotate(x, cos, sin):
        flat = x.reshape(x.shape[0] * rows, D_HEAD)
        return x * cos + pltpu.roll(flat, D_HEAD // 2, axis=1).reshape(x.shape) * sin

    def phase_a(g):
        q, k, v, rg = geo.split(geo.project(h_ref, w, g), relay)
        if geo.chunks > 1:
            cos, sin = cos_ref[geo.rows_of(g)], sin_ref[geo.rows_of(g)]
        else:
            cos, sin = cos_ref[...], sin_ref[...]
        qr = rotate(q, cos, sin).astype(BF16)
        kr = rotate(k, cos, sin) * scale
        vb = _pad_rows(v, width).astype(BF16)
        a = _bdot_nt(qr, _pad_rows(kr, width).astype(BF16)) * decay
        sl = geo.rows_of(g)
        gate_scr[sl] = rg
        q_scr[sl] = qr
        o_scr[sl] = _bdot(a.astype(BF16), vb)
        u_scr[sl] = _bdot_tn(_pad_rows(kr * w_st, width).astype(BF16), vb)

    geo.for_groups_static(phase_a)

    def step(sl, state):
        o_scr[sl] = o_scr[sl] + _bdot(q_scr[sl], state.astype(BF16)) * w_in
        return g_all * state + u_scr[sl]

    init = r0[0][...] if has_state else jnp.zeros((geo.seqs, D_HEAD, D_HEAD), F32)
    r_ref[...] = geo.scan_chunks(step, init)

    def epilogue(g):
        sl = geo.rows_of(g)
        geo.store(y_ref, g, _rms(o_scr[sl], nw_ref[...]) * _silu(gate_scr[sl]), relay)

    geo.for_groups(epilogue)


def _mixer_spec(kernel, name, geo, layer, h, weights, extra, states_in, states_prev, scratch):
    seqs, n_cols = geo.seqs, len(weights) * D_HEAD
    fresh = isinstance(states_prev[0], jax.ShapeDtypeStruct)

    def weight_spec(first, per_head):
        return pl.BlockSpec((None, D_MODEL, D_HEAD), lambda bi, j: (layer, 0, first + per_head * j))

    def state_spec(arr, layered=False):
        lead, first = (arr.shape[0], 0) if layered else (None, layer)
        if len(arr.shape) == 5:
            return pl.BlockSpec((lead, seqs, None, D_HEAD, D_HEAD), lambda bi, j: (first, bi, j, 0, 0))
        return pl.BlockSpec((lead, seqs, N_HEAD, D_HEAD), lambda bi, j: (first, bi, 0, 0))

    has_state = states_in is not None
    args = [h] + [a for a, _, _ in weights] + [a for a, _ in extra]
    in_specs = [pl.BlockSpec((geo.tokens, D_MODEL), lambda bi, j: (bi, 0))] + [
        weight_spec(first, per_head) for _, first, per_head in weights] + [s for _, s in extra]
    if has_state:
        args += list(states_in)
        in_specs += [state_spec(a) for a in states_in]
    n_in = len(args)
    if not fresh:
        args += list(states_prev)
        in_specs += [pl.BlockSpec(memory_space=pl.ANY) for _ in states_prev]
    out_specs = [pl.BlockSpec((geo.tokens, D_HEAD), lambda bi, j: (bi, j))] + [
        state_spec(a, fresh) for a in states_prev]
    out_shape = [jax.ShapeDtypeStruct((h.shape[0], MIX_W), F32 if geo.padded else BF16)] + [
        jax.ShapeDtypeStruct(a.shape, F32) for a in states_prev]
    scratch_shapes = list(scratch) + [pltpu.VMEM((geo.problems, geo.rows, D_HEAD), F32)]
    if geo.padded:
        scratch_shapes += [pltpu.VMEM((geo.group_tokens, D_HEAD), F32),
                           pltpu.VMEM((seqs * geo.rows, D_HEAD), F32)] * (n_cols // D_HEAD)
    return dict(
        kernel=functools.partial(kernel, layer=layer, geo=geo, has_state=has_state, fresh=fresh), name=name,
        args=args, in_specs=in_specs, out_specs=out_specs, out_shape=out_shape, scratch_shapes=scratch_shapes,
        aliases={} if fresh else {n_in + i: 1 + i for i in range(len(states_prev))})


def _run_mixers(geo, specs):
    n_ins = [len(sp["args"]) - 1 for sp in specs]
    n_outs = [len(sp["out_shape"]) for sp in specs]
    n_scr = [len(sp["scratch_shapes"]) for sp in specs]
    in_off, out_off, scr_off = (np.concatenate([[0], np.cumsum(c)]).tolist() for c in (n_ins, n_outs, n_scr))
    total_in, total_out = 1 + in_off[-1], out_off[-1]

    def body(*refs):
        ins, outs, scr = refs[1:total_in], refs[total_in:total_in + total_out], refs[total_in + total_out:]
        for m, sp in enumerate(specs):
            sp["kernel"](refs[0], *ins[in_off[m]:in_off[m + 1]], *outs[out_off[m]:out_off[m + 1]],
                         *scr[scr_off[m]:scr_off[m + 1]])

    aliases = {}
    for m, sp in enumerate(specs):
        for i, o in sp["aliases"].items():
            aliases[i + in_off[m]] = o + out_off[m]
    flat = pl.pallas_call(
        body,
        grid=(geo.batch // geo.seqs, N_HEAD),
        in_specs=specs[0]["in_specs"][:1] + [s for sp in specs for s in sp["in_specs"][1:]],
        out_specs=[s for sp in specs for s in sp["out_specs"]],
        out_shape=[s for sp in specs for s in sp["out_shape"]],
        scratch_shapes=[s for sp in specs for s in sp["scratch_shapes"]],
        input_output_aliases=aliases,
        compiler_params=pltpu.CompilerParams(
            dimension_semantics=("arbitrary", "arbitrary"), vmem_limit_bytes=VMEM_LIMIT),
        name="+".join(sp["name"] for sp in specs),
    )(specs[0]["args"][0], *[a for sp in specs for a in sp["args"][1:]])
    return [flat[out_off[m]:out_off[m + 1]] for m in range(len(specs))]


def _head_block(rows):
    return pl.BlockSpec((rows, D_HEAD), lambda bi, j: (0, j))


def _part_blocks(w_main, mixer):
    return [(w_main, (mixer * 4 + p) * N_HEAD, 1) for p in range(4)]


def _hgrn(geo, layer, h, w_main, hgrn_lb, nw, state_in, state_prev):
    n = geo.problems
    scratch = [pltpu.VMEM((n, geo.rows, D_HEAD), BF16), pltpu.VMEM((n, geo.rows, D_HEAD), F32),
               pltpu.VMEM((n, D_HEAD, D_HEAD), F32), pltpu.VMEM((n, 1, D_HEAD), F32)]
    place = _hgrn_placement(geo)
    return _mixer_spec(
        _hgrn_kernel, "hgrn2", geo, layer, h, _part_blocks(w_main, 0),
        [(hgrn_lb, _head_block(hgrn_lb.shape[0])), (nw, _head_block(1)), (place, _resident(place.shape))],
        None if state_in is None else [state_in], [state_prev], scratch)


def _mlstm(geo, layer, h, w_main, gates, gate_b, nw, states_in, states_prev):
    n = geo.problems
    col = lambda r: pltpu.VMEM((n, r, LANES), F32)
    scratch = [pltpu.VMEM((n, geo.rows, D_HEAD), BF16), pltpu.VMEM((n, geo.rows, D_HEAD), F32),
               pltpu.VMEM((n, D_HEAD, D_HEAD), F32), pltpu.VMEM((n, 1, D_HEAD), F32),
               col(geo.rows), col(geo.rows), col(geo.rows), col(1), col(1)]
    return _mixer_spec(
        _mlstm_kernel, "mlstm", geo, layer, h, _part_blocks(w_main, 1),
        [(gates, pl.BlockSpec((geo.tokens, LANES), lambda bi, j: (bi, 0))),
         (gate_b, pl.BlockSpec((1, LANES), lambda bi, j: (0, 0))), (nw, _head_block(1))],
        states_in, states_prev, scratch)


def _retention(geo, layer, h, w_main, cos, sin, nw, state_in, state_prev):
    n = geo.problems
    scratch = [pltpu.VMEM((n, geo.rows, D_HEAD), BF16), pltpu.VMEM((n, geo.rows, D_HEAD), F32),
               pltpu.VMEM((n, D_HEAD, D_HEAD), F32)]
    table = pl.BlockSpec(cos.shape, lambda bi, j: (0, 0, 0))
    return _mixer_spec(
        _ret_kernel, "retention", geo, layer, h, _part_blocks(w_main, 2),
        [(cos, table), (sin, table), (nw, _head_block(1))],
        None if state_in is None else [state_in], [state_prev], scratch)


def _merge_kernel(x_ref, h_ref, yh_ref, ym_ref, yr_ref, wg_ref, bg_ref, wb_ref, wo_ref, n2_ref, x1_ref, h2_ref):
    hb = h_ref[...]
    merged = None
    for j, y_ref in enumerate((yh_ref, ym_ref, yr_ref)):
        cols = slice(j * D_MODEL, (j + 1) * D_MODEL)
        gate = _sigmoid(_dot(hb, wg_ref[:, cols]) + bg_ref[:, cols])
        term = gate * _dot(y_ref[...].astype(BF16), wb_ref[j])
        merged = term if merged is None else merged + term
    x1 = x_ref[...] + _dot(merged.astype(BF16), wo_ref[...])
    x1_ref[...] = x1
    h2_ref[...] = _rms(x1, n2_ref[...]).astype(BF16)


def _merge(x, h, yh, ym, yr, w_bgate, gate_b, w_branch, w_out, norm2_w):
    n = x.shape[0]
    tm = min(TOKEN_TILE, n)
    gate_w = N_BRANCH * D_MODEL
    row = lambda width: pl.BlockSpec((tm, width), lambda i: (i, 0))
    vec = lambda width: pl.BlockSpec((1, width), lambda i: (0, 0))
    return pl.pallas_call(
        _merge_kernel,
        grid=(n // tm,),
        in_specs=[
            row(D_MODEL), row(D_MODEL), row(MIX_W), row(MIX_W), row(MIX_W),
            _resident((D_MODEL, gate_w)), vec(gate_w),
            _resident((N_BRANCH, MIX_W, D_MODEL)),
            _resident((D_MODEL, D_MODEL)),
            vec(D_MODEL),
        ],
        out_specs=[row(D_MODEL), row(D_MODEL)],
        out_shape=[jax.ShapeDtypeStruct((n, D_MODEL), F32), jax.ShapeDtypeStruct((n, D_MODEL), BF16)],
        compiler_params=pltpu.CompilerParams(
            dimension_semantics=("arbitrary",), vmem_limit_bytes=VMEM_LIMIT),
        name="merge",
    )(x, h, yh, ym, yr, w_bgate, gate_b, w_branch, w_out, norm2_w)


def _ffn_kernel(*refs, seq_len, has_state, final_norm, tail_rows):
    refs = list(refs)
    x1_ref, h2_ref, wg_ref, wu_ref, wd_ref, cw_ref, cb_ref, nw_ref = refs[:8]
    refs = refs[8:]
    p1_ref = p2_ref = None
    if has_state:
        p1_ref, p2_ref = refs[:2]
        refs = refs[2:]
    if final_norm:
        out_ref, tail_ref, carry_scr, a_scr = refs
    else:
        wgn_ref, out_ref, hn_ref, gn_ref, tail_ref, carry_scr, a_scr = refs
    tm = x1_ref.shape[0]
    i = pl.program_id(0)

    @pl.when(i == 0)
    def _():
        carry_scr[...] = jnp.zeros(carry_scr.shape, F32)

    t = (i * tm + lax.broadcasted_iota(jnp.int32, (tm, FF_TILE), 0)) & (seq_len - 1)
    h2 = h2_ref[...]
    acc = x1_ref[...]
    for j in range(D_FF // FF_TILE):
        cols = slice(j * FF_TILE, (j + 1) * FF_TILE)
        a = _dot(h2, wg_ref[:, cols])
        u = _dot(h2, wu_ref[:, cols])
        a_scr[0:SUBLANES, :] = carry_scr[:, cols]
        a_scr[SUBLANES:, :] = a
        carry_scr[:, cols] = a[tm - SUBLANES:, :]
        prev1 = jnp.where(t >= 1, a_scr[SUBLANES - 1:SUBLANES - 1 + tm, :], 0.0)
        prev2 = jnp.where(t >= 2, a_scr[SUBLANES - 2:SUBLANES - 2 + tm, :], 0.0)
        if has_state:
            prev1 = prev1 + p1_ref[:, cols]
            prev2 = prev2 + p2_ref[:, cols]
        conv = prev2 * cw_ref[0:1, cols] + prev1 * cw_ref[1:2, cols] + a * cw_ref[2:3, cols] + cb_ref[:, cols]
        acc = acc + _dot((_silu(conv) * u).astype(BF16), wd_ref[cols, :])
        if tail_rows == tm:
            tail_ref[:, cols] = a
        else:
            tail_ref[0, :, cols] = a[tm - tail_rows:, :]
    if final_norm:
        out_ref[...] = _rms(acc, nw_ref[...])
    else:
        out_ref[...] = acc
        hn = _rms(acc, nw_ref[...]).astype(BF16)
        hn_ref[...] = hn
        gn_ref[...] = _dot(hn, wgn_ref[...])


def _ffn(x1, h2, wg, wu, wd, conv_w, conv_b, seq_len, conv_state, norm_w, next_w_gate):
    final_norm = next_w_gate is None
    n = x1.shape[0]
    has_state = conv_state is not None
    tm = min(TOKEN_TILE // 2 if has_state else TOKEN_TILE, n)
    assert n % tm == 0 and seq_len & (seq_len - 1) == 0
    batch = n // seq_len
    row = lambda width: pl.BlockSpec((tm, width), lambda i: (i, 0))
    full = lambda shape: pl.BlockSpec(shape, lambda i: (0,) * len(shape))
    in_specs = [row(D_MODEL), row(D_MODEL), _resident((D_MODEL, D_FF)), _resident((D_MODEL, D_FF)),
                _resident((D_FF, D_MODEL)), full((CONV_W, D_FF)), full((1, D_FF)), full((1, D_MODEL))]
    args = [x1, h2, wg, wu, wd, conv_w, conv_b, norm_w]
    if has_state:
        p1 = jnp.pad(conv_state[:, 1:2], ((0, 0), (0, seq_len - 1), (0, 0))).reshape(n, D_FF)
        p2 = jnp.pad(conv_state, ((0, 0), (0, seq_len - (CONV_W - 1)), (0, 0))).reshape(n, D_FF)
        in_specs += [row(D_FF), row(D_FF)]
        args += [p1, p2]
    if seq_len % tm == 0:
        tail_rows = CONV_W - 1
        tail_spec = pl.BlockSpec((1, tail_rows, D_FF), lambda i: ((i * tm) // seq_len, 0, 0))
        tail_shape = jax.ShapeDtypeStruct((batch, tail_rows, D_FF), F32)
    else:
        assert tm % seq_len == 0 and seq_len >= CONV_W - 1
        tail_rows = tm
        tail_spec = row(D_FF)
        tail_shape = jax.ShapeDtypeStruct((n, D_FF), F32)
    out_specs, out_shape = [row(D_MODEL)], [jax.ShapeDtypeStruct((n, D_MODEL), F32)]
    if not final_norm:
        in_specs.append(_resident((D_MODEL, LANES)))
        args.append(next_w_gate)
        out_specs += [row(D_MODEL), row(LANES)]
        out_shape += [jax.ShapeDtypeStruct((n, D_MODEL), BF16), jax.ShapeDtypeStruct((n, LANES), F32)]
    *outs, tail = pl.pallas_call(
        functools.partial(_ffn_kernel, seq_len=seq_len, has_state=has_state,
                          final_norm=final_norm, tail_rows=tail_rows),
        grid=(n // tm,),
        in_specs=in_specs,
        out_specs=out_specs + [tail_spec],
        out_shape=out_shape + [tail_shape],
        scratch_shapes=[pltpu.VMEM((SUBLANES, D_FF), F32), pltpu.VMEM((tm + SUBLANES, FF_TILE), F32)],
        compiler_params=pltpu.CompilerParams(
            dimension_semantics=("arbitrary",), vmem_limit_bytes=VMEM_LIMIT),
        name="conv_ffn",
    )(*args)
    if tail_rows == tm:
        tail = tail.reshape(batch, seq_len, D_FF)[:, seq_len - (CONV_W - 1):, :]
    return outs, tail


def _prep_weights(w_in, w_branch, w_out, ffn_w_gate, ffn_w_up, ffn_w_down):
    w_in = w_in.astype(BF16)
    w_main = jnp.concatenate([w_in[:, :, HG_OFF:ML_GATE_OFF], w_in[:, :, RET_OFF:GATE_OFF]], axis=2)
    w_gate = jnp.pad(w_in[:, :, ML_GATE_OFF:RET_OFF], ((0, 0), (0, 0), (0, LANES - 2 * N_HEAD)))
    return (w_main, w_gate,
            w_in[:, :, GATE_OFF:], w_branch.astype(BF16), w_out.astype(BF16),
            ffn_w_gate.astype(BF16), ffn_w_up.astype(BF16), ffn_w_down.astype(BF16))


def _trunk(x3, pos0, states, params, prepped):
    (norm1_w, mlstm_gate_b, branch_gate_b, hgrn_lb, hgrn_norm_w, mlstm_norm_w, ret_norm_w,
     norm2_w, ffn_conv_w, ffn_conv_b, final_norm_w) = params
    w_main, w_gate, w_bgate, w_branch, w_out, w_ffg, w_ffu, w_ffd = prepped
    batch, seq_len, _ = x3.shape
    n = batch * seq_len
    depth = norm1_w.shape[0]
    geo = _Geo(batch, seq_len)
    cos, sin = (t.reshape(geo.chunks, geo.rows, D_HEAD) for t in _rope_tables(geo.chunks * geo.rows, pos0))

    mat = jax.ShapeDtypeStruct((depth, batch, N_HEAD, D_HEAD, D_HEAD), F32)
    vec = jax.ShapeDtypeStruct((depth, batch, N_HEAD, D_HEAD), F32)
    if states is None:
        hg_in = ml_in = ret_in = conv_in = None
    else:
        hg_in, ml_c, ml_n, ml_m, ret_in, conv_in = states
        ml_in = [ml_c, ml_n, jnp.broadcast_to(ml_m[..., None], vec.shape)]
    hg_out, c_out, r_out, n_out, m_out = mat, mat, mat, vec, vec

    x = x3.reshape(n, D_MODEL)
    h, gates = _norm(x, norm1_w[0:1], w_gate[0])
    conv_tails = []
    for l in range(depth):
        last = l == depth - 1
        gate_b = jnp.pad(mlstm_gate_b[l:l + 1], ((0, 0), (0, LANES - 2 * N_HEAD)))
        specs = [_hgrn(geo, l, h, w_main, hgrn_lb, hgrn_norm_w[l:l + 1], hg_in, hg_out),
                 _mlstm(geo, l, h, w_main, gates, gate_b, mlstm_norm_w[l:l + 1], ml_in, [c_out, n_out, m_out]),
                 _retention(geo, l, h, w_main, cos, sin, ret_norm_w[l:l + 1], ret_in, r_out)]
        runs = [specs] if not geo.padded else [[sp] for sp in specs]
        (y_hg, hg_out), (y_ml, c_out, n_out, m_out), (y_ret, r_out) = [
            out for run in runs for out in _run_mixers(geo, run)]
        x1, h2 = _merge(x, h, y_hg, y_ml, y_ret, w_bgate[l], branch_gate_b[l:l + 1], w_branch[l], w_out[l],
                        norm2_w[l:l + 1])
        outs, conv_tail = _ffn(x1, h2, w_ffg[l], w_ffu[l], w_ffd[l], ffn_conv_w[l], ffn_conv_b[l:l + 1],
                               seq_len, None if conv_in is None else conv_in[l],
                               final_norm_w[None, :] if last else norm1_w[l + 1:l + 2],
                               None if last else w_gate[l + 1])
        x = outs[0]
        h, gates = (None, None) if last else outs[1:]
        conv_tails.append(conv_tail)
    return x.reshape(batch, seq_len, D_MODEL), (
        hg_out, c_out, n_out, m_out[:, :, :, 0], r_out,
        jnp.stack(conv_tails))


def kernel(x_prompt, x_sample, state_hgrn, state_mlstm_C, state_mlstm_n, state_mlstm_m, state_ret,
           state_ffn_conv, norm1_w, w_in, mlstm_gate_b, branch_gate_b, hgrn_lb, hgrn_norm_w,
           mlstm_norm_w, ret_norm_w, w_branch, w_out, norm2_w, ffn_w_gate, ffn_w_up, ffn_conv_w,
           ffn_conv_b, ffn_w_down, final_norm_w):
    params = (norm1_w, mlstm_gate_b, branch_gate_b, hgrn_lb, hgrn_norm_w, mlstm_norm_w, ret_norm_w,
              norm2_w, ffn_conv_w, ffn_conv_b, final_norm_w)
    prepped = _prep_weights(w_in, w_branch, w_out, ffn_w_gate, ffn_w_up, ffn_w_down)
    y_p, (hg_p, c_p, n_p, m_p, r_p, cv_p) = _trunk(x_prompt, 0, None, params, prepped)
    y_s, (hg_s, c_s, n_s, m_s, r_s, cv_s) = _trunk(
        x_sample, PAST_LEN,
        (state_hgrn, state_mlstm_C, state_mlstm_n, state_mlstm_m, state_ret, state_ffn_conv),
        params, prepped)
    return (y_p, y_s, hg_p, hg_s, c_p, c_s, n_p, n_s, m_p, m_s, r_p, r_s, cv_p, cv_s)
```

```python
import functools
import math

import numpy as np
import jax
import jax.numpy as jnp
from jax import lax
from jax.experimental import pallas as pl
from jax.experimental.pallas import tpu as pltpu

F32 = jnp.float32
BF16 = jnp.bfloat16

D_MODEL = 1024
MIX_W = 512
N_HEAD = 4
D_HEAD = 128
N_BRANCH = 3
D_FF = 2816
CONV_W = 3
CHUNK = 64
EPS = 1e-6
ROPE_BASE = 10000.0
PAST_LEN = 16384

HG_OFF = 0
ML_OFF = 4 * MIX_W
ML_GATE_OFF = ML_OFF + 4 * MIX_W
RET_OFF = ML_GATE_OFF + 2 * N_HEAD
GATE_OFF = RET_OFF + 4 * MIX_W
LOG2E = 1.4426950408889634

LANES = 128
SUBLANES = 8
VMEM_LIMIT = 56 * 1024 * 1024

TOKEN_TILE = 512
NORM_TOKEN_TILE = 1024
FF_TILE = 2816
HG_SUB = 16
GROUP = 32
SEQ_BLOCK = 64
SCAN_UNROLL = 32


def _resident(shape):
    return pl.BlockSpec(shape, lambda *_: (0,) * len(shape), pipeline_mode=pl.Buffered(1))


def _dot(a, b):
    return jnp.dot(a, b, preferred_element_type=F32)


def _dot_tn(a, b):
    return lax.dot_general(a, b, (((0,), (0,)), ((), ())), preferred_element_type=F32)


def _bdot(a, b):
    return jnp.einsum('nlk,nkv->nlv', a, b, preferred_element_type=F32)


def _bdot_nt(a, b):
    return jnp.einsum('nqd,nkd->nqk', a, b, preferred_element_type=F32)


def _bdot_tn(a, b):
    return jnp.stack([_dot_tn(a[i], b[i]) for i in range(a.shape[0])])


def _split3(x):
    hi = x.astype(BF16)
    r = x - hi.astype(F32)
    mid = r.astype(BF16)
    lo = (r - mid.astype(F32)).astype(BF16)
    return hi, mid, lo


def _sigmoid(x):
    return 1.0 / (1.0 + jnp.exp(-x))


def _silu(x):
    return x * _sigmoid(x)


def _log_sigmoid(x):
    return jnp.minimum(x, 0.0) - jnp.log(1.0 + jnp.exp(-jnp.abs(x)))


def _rms(x, w):
    return x * lax.rsqrt(jnp.mean(x * x, axis=-1, keepdims=True) + EPS) * w


def _pad_rows(x, n):
    if x.shape[1] >= n:
        return x
    return jnp.concatenate([x, jnp.zeros((x.shape[0], n - x.shape[1], x.shape[2]), x.dtype)], axis=1)


def _seg_cumsum(x, seg):
    shape = x.shape
    flat = x.reshape(shape[0] * seg, shape[2])
    row = lax.broadcasted_iota(jnp.int32, flat.shape, 0) & (seg - 1)
    s = 1
    while s < seg:
        flat = flat + jnp.where(row >= s, pltpu.roll(flat, s, axis=0), 0.0)
        s *= 2
    return flat.reshape(shape)


def _through_vmem(x):
    def body(ref):
        ref[...] = x
        return ref[...]
    return pl.run_scoped(body, pltpu.VMEM(x.shape, x.dtype))


def _row_bcast(col, width):
    n, rows, _ = col.shape
    if rows == width:
        vals = _through_vmem(col)
        return jnp.stack([vals[i].T[0:rows] for i in range(n)])
    vals = _through_vmem(_pad_rows(col, width))
    lane = lax.broadcasted_iota(jnp.int32, (n, rows, LANES), 2)
    pick = _through_vmem(jnp.where(lane == 0, 1.0, 0.0).astype(BF16))
    hi, mid, lo = _split3(vals)
    return _bdot_nt(pick, hi) + _bdot_nt(pick, mid) + _bdot_nt(pick, lo)


def _chunk_geometry(seq_len):
    if seq_len % CHUNK == 0:
        return CHUNK, CHUNK, CHUNK
    valid = math.gcd(seq_len, CHUNK)
    assert valid == seq_len and valid <= SUBLANES, "unsupported sequence length"
    return SUBLANES, valid, LANES


def _rope_kernel(inv_ref, cos_ref, sin_ref, *, pos0):
    shape = cos_ref.shape
    pos = lax.broadcasted_iota(jnp.int32, shape, 0).astype(F32) + pos0
    ang = pos * inv_ref[...]
    lane = lax.broadcasted_iota(jnp.int32, shape, 1)
    cos_ref[...] = jnp.cos(ang)
    sin_ref[...] = jnp.where(lane < D_HEAD // 2, -jnp.sin(ang), jnp.sin(ang))


def _rope_tables(rows, pos0):
    inv = ROPE_BASE ** (-jnp.linspace(0.0, 1.0, D_HEAD // 2, dtype=F32))
    inv2 = jnp.concatenate([inv, inv])[None, :]
    return pl.pallas_call(
        functools.partial(_rope_kernel, pos0=float(pos0)),
        out_shape=(jax.ShapeDtypeStruct((rows, D_HEAD), F32),) * 2,
        name="rope_tables",
    )(inv2)


def _norm_kernel(x_ref, w_ref, wg_ref, h_ref, g_ref):
    hb = _rms(x_ref[...], w_ref[...]).astype(BF16)
    h_ref[...] = hb
    g_ref[...] = _dot(hb, wg_ref[...])


def _norm(x, w, w_gate):
    n = x.shape[0]
    tm = min(NORM_TOKEN_TILE, n)
    assert n % tm == 0
    row = lambda width: pl.BlockSpec((tm, width), lambda i: (i, 0))
    return pl.pallas_call(
        _norm_kernel,
        grid=(n // tm,),
        in_specs=[row(D_MODEL), pl.BlockSpec((1, D_MODEL), lambda i: (0, 0)), _resident((D_MODEL, LANES))],
        out_specs=[row(D_MODEL), row(LANES)],
        out_shape=[jax.ShapeDtypeStruct((n, D_MODEL), BF16), jax.ShapeDtypeStruct((n, LANES), F32)],
        compiler_params=pltpu.CompilerParams(dimension_semantics=("arbitrary",), vmem_limit_bytes=VMEM_LIMIT),
        name="norm_in",
    )(x, w, w_gate)


class _Geo:
    def __init__(self, batch, seq_len):
        self.rows, self.valid, self.width = _chunk_geometry(seq_len)
        if self.valid == self.rows:
            self.seqs, self.chunks = 1, seq_len // self.rows
            self.group = min(GROUP, self.chunks)
        else:
            self.seqs, self.chunks = min(SEQ_BLOCK, batch), 1
            self.group = self.seqs
        assert batch % self.seqs == 0 and self.chunks % min(self.group, self.chunks) == 0
        self.batch = batch
        self.problems = self.seqs * self.chunks
        self.n_groups = self.problems // self.group
        self.group_tokens = self.group * self.valid
        self.tokens = self.problems * self.valid
        self.padded = self.valid < self.rows

    def project(self, h_ref, w, g):
        return _dot(h_ref[pl.ds(g * self.group_tokens, self.group_tokens), :], w)

    def split(self, z, relay):
        n_parts = z.shape[1] // D_HEAD
        cols = lambda p: slice(p * D_HEAD, (p + 1) * D_HEAD)
        if not self.padded:
            return [z[:, cols(p)].reshape(self.group, self.rows, D_HEAD) for p in range(n_parts)]
        out = []
        for p in range(n_parts):
            flat_ref, pad_ref = relay[2 * p], relay[2 * p + 1]
            flat_ref[...] = z[:, cols(p)]
            pad_ref[...] = jnp.zeros(pad_ref.shape, F32)
            for t in range(self.valid):
                pad_ref[pl.ds(t, self.seqs, stride=self.rows), :] = flat_ref[
                    pl.ds(t, self.seqs, stride=self.valid), :]
            out.append(pad_ref[...].reshape(self.seqs, self.rows, D_HEAD))
        return out

    def store(self, y_ref, g, y, relay):
        if not self.padded:
            n = self.group * self.rows
            y_ref[pl.ds(g * n, n), :] = y.reshape(n, D_HEAD).astype(y_ref.dtype)
            return
        pad = relay[1]
        pad[...] = y.reshape(self.seqs * self.rows, D_HEAD)
        for t in range(self.valid):
            y_ref[pl.ds(t, self.seqs, stride=self.valid), :] = pad[pl.ds(t, self.seqs, stride=self.rows), :].astype(
                y_ref.dtype)

    def rows_of(self, g):
        return pl.ds(g * self.group, self.group)

    def row_mask(self, shape):
        if not self.padded:
            return None
        return lax.broadcasted_iota(jnp.int32, shape, 1) < self.valid

    def for_groups(self, body):
        if self.n_groups == 1:
            body(0)
        else:
            lax.fori_loop(0, self.n_groups, lambda g, c: (body(g), c)[1], 0)

    def for_groups_static(self, body):
        for g in range(self.n_groups):
            body(g)

    def scan_chunks(self, step, init):
        if self.chunks == 1:
            return step(pl.ds(0, self.seqs), init)
        return lax.fori_loop(0, self.chunks, lambda c, carry: step(pl.ds(c * self.seqs, self.seqs), carry), init,
                             unroll=min(SCAN_UNROLL, self.chunks))


def _join_columns(w_refs):
    return jnp.concatenate([r[...] for r in w_refs], axis=1)


def _transpose_minor(x):
    return jnp.stack([x[i].T for i in range(x.shape[0])])


def _state_views(refs, kinds, layer, layered):
    head = pl.program_id(1)
    views = []
    for ref, kind in zip(refs, kinds):
        for l in (range(ref.shape[0]) if layered else [layer]):
            view = ref.at[l] if layered else ref
            if kind == 'vec':
                view = view.at[:, pl.ds(head, 1), :]
            if l == layer:
                views.append(view)
            else:
                view[...] = jnp.zeros(view.shape, F32)
    return views


def _split_refs(refs, counts):
    out, k = [], 0
    for c in counts:
        out.append(refs[k:k + c])
        k += c
    return out


def _hgrn_lower_bound(lb_ref, layer):
    x = lb_ref[...]
    e = jnp.exp(x - jnp.max(x, axis=0, keepdims=True))
    soft = e / jnp.sum(e, axis=0, keepdims=True)
    cum = soft[0:1]
    for j in range(1, layer + 1):
        cum = cum + soft[j:j + 1]
    return cum - soft[0:1]


def _hgrn_placement(geo):
    sub = min(HG_SUB, geo.rows)
    r = np.arange(sub * D_HEAD)[:, None] // D_HEAD
    c = np.arange(geo.width)[None, :]
    return jnp.asarray(np.stack([r + i * sub == c for i in range(geo.rows // sub)]), BF16)


def _hgrn_intra(q, k, b2, place_ref, geo):
    n = q.shape[0]
    sub = min(HG_SUB, geo.rows)
    sub_row = lax.broadcasted_iota(jnp.int32, (n, sub, geo.width), 1)
    lane = lax.broadcasted_iota(jnp.int32, (n, sub, geo.width), 2)
    blocks = []
    for i in range(geo.rows // sub):
        r0 = i * sub
        b_i, q_i, k_i = (a[:, r0:r0 + sub] for a in (b2, q, k))
        terms = []
        for s in range(sub):
            lo = s // SUBLANES * SUBLANES
            e = jnp.exp2(jnp.minimum(b_i[:, lo:] - b_i[:, s:s + 1], 0.0))
            term = q_i[:, lo:] * e * k_i[:, s:s + 1]
            if lo:
                term = jnp.concatenate([jnp.zeros((n, lo, D_HEAD), F32), term], axis=1)
            terms.append(term)
        cat = jnp.concatenate(terms, axis=2).reshape(n * sub, sub * D_HEAD)
        a_i = _dot(cat.astype(BF16), place_ref[i]).reshape(n, sub, geo.width)
        a_i = jnp.where(sub_row >= lane - r0, a_i, 0.0)
        if i > 0:
            b_ref = b_i[:, 0:1]
            q_dec = q_i * jnp.exp2(b_i - b_ref)
            k_dec = _pad_rows(k[:, 0:r0] * jnp.exp2(b_ref - b2[:, 0:r0]), geo.width)
            a_i = jnp.where(lane < r0, _bdot_nt(q_dec.astype(BF16), k_dec.astype(BF16)), a_i)
        blocks.append(a_i)
    return blocks[0] if len(blocks) == 1 else jnp.concatenate(blocks, axis=1)


def _hgrn_kernel(*refs, layer, geo, has_state, fresh):
    ((h_ref,), w_refs, (lb_ref, nw_ref, place), s0, _, (y_ref,), s_out,
     (qt_scr, o_scr, u_scr, dc_scr, gate_scr), relay) = _split_refs(
        refs, (1, 4, 3, int(has_state), int(not fresh), 1, 1, 5, len(refs)))
    (s_ref,) = _state_views(s_out, ['mat'], layer, fresh)
    lb = _hgrn_lower_bound(lb_ref, layer)
    rows, width = geo.rows, geo.width
    w = _join_columns(w_refs)

    def phase_a(g):
        hq, hf, hi, hg = geo.split(geo.project(h_ref, w, g), relay)
        q = _silu(hq)
        f = lb + (1.0 - lb) * _sigmoid(hf)
        lf = jnp.log(f)
        k = 1.0 - f
        mask = geo.row_mask(lf.shape)
        if mask is not None:
            lf = jnp.where(mask, lf, 0.0)
            k = jnp.where(mask, k, 0.0)
        b2 = _seg_cumsum(lf, rows) * LOG2E
        sl = geo.rows_of(g)
        gate_scr[sl] = hg
        qt_scr[sl] = (q * jnp.exp2(b2)).astype(BF16)
        a = _hgrn_intra(q, k, b2, place, geo)
        vb = _pad_rows(hi, width).astype(BF16)
        o_scr[sl] = _bdot(a.astype(BF16), vb)
        b_last = b2[:, rows - 1:rows]
        k_w = k * jnp.exp2(b_last - b2)
        u_scr[sl] = _bdot_tn(vb, _pad_rows(k_w, width).astype(BF16))
        dc_scr[sl] = jnp.exp2(b_last)

    geo.for_groups_static(phase_a)

    def step(sl, state_t):
        o_scr[sl] = o_scr[sl] + _bdot_nt(qt_scr[sl], state_t.astype(BF16))
        return dc_scr[sl] * state_t + u_scr[sl]

    init = _transpose_minor(s0[0][...]) if has_state else jnp.zeros((geo.seqs, D_HEAD, D_HEAD), F32)
    s_ref[...] = _transpose_minor(geo.scan_chunks(step, init))

    def epilogue(g):
        sl = geo.rows_of(g)
        geo.store(y_ref, g, _rms(o_scr[sl], nw_ref[...]) * _silu(gate_scr[sl]), relay)

    geo.for_groups(epilogue)


def _mlstm_kernel(*refs, layer, geo, has_state, fresh):
    ((h_ref,), w_refs, (g_ref, gb_ref, nw_ref), st0, _, (y_ref,), st_out,
     (q_scr, v_scr, u_scr, nu_scr, bc_scr, ml_scr, rs_scr, bl_scr, mu_scr, gate_scr), relay) = _split_refs(
        refs, (1, 4, 3, 3 * int(has_state), 3 * int(not fresh), 1, 3, 10, len(refs)))
    kinds = ['mat', 'vec', 'vec']
    c_ref, n_ref, m_ref = _state_views(st_out, kinds, layer, fresh)
    st0 = _state_views(st0, kinds, layer, False) if has_state else st0
    rows, width = geo.rows, geo.width
    head = pl.program_id(1)
    scale = D_HEAD ** -0.5
    t_i = lax.broadcasted_iota(jnp.int32, (1, rows, width), 1)
    s_i = lax.broadcasted_iota(jnp.int32, (1, rows, width), 2)
    ok = s_i <= t_i
    if geo.padded:
        ok = ok & (s_i < geo.valid)
    w = _join_columns(w_refs)

    def phase_a(g):
        q, k, v, og = geo.split(geo.project(h_ref, w, g), relay)
        (gates,) = geo.split(g_ref[pl.ds(g * geo.group_tokens, geo.group_tokens), :], relay)
        gates = gates + gb_ref[...]
        lf_all = _log_sigmoid(gates)
        mask = geo.row_mask(gates.shape)
        if mask is not None:
            lf_all = jnp.where(mask, lf_all, 0.0)
        bc_all = _seg_cumsum(lf_all, rows)
        lane = lax.broadcasted_iota(jnp.int32, gates.shape, 2)
        wide = lambda x: jnp.broadcast_to(x, gates.shape)
        ig = wide(jnp.sum(jnp.where(lane == head, gates, 0.0), axis=2, keepdims=True))
        bc_col = jnp.sum(jnp.where(lane == head + N_HEAD, bc_all, 0.0), axis=2, keepdims=True)
        bc = wide(bc_col)
        log_d = jnp.where(ok, bc_col + _row_bcast(ig - bc, width), -jnp.inf)
        m_col = jnp.max(log_d, axis=2, keepdims=True)
        m_loc = wide(m_col)
        k = k * scale
        qb = q.astype(BF16)
        vb = _pad_rows(v, width).astype(BF16)
        s_m = _bdot_nt(qb, _pad_rows(k, width).astype(BF16)) * jnp.exp(log_d - m_col)
        sl = geo.rows_of(g)
        gate_scr[sl] = og
        q_scr[sl] = qb
        v_scr[sl] = _bdot(s_m.astype(BF16), vb)
        rs_scr[sl] = wide(jnp.sum(s_m, axis=2, keepdims=True))
        bc_scr[sl] = bc
        ml_scr[sl] = m_loc
        b_last = bc[:, rows - 1:rows]
        m_upd = m_loc[:, rows - 1:rows]
        w_exp = b_last - bc + ig - m_upd
        if mask is not None:
            w_exp = jnp.where(mask, w_exp, -jnp.inf)
        k_w = k * jnp.exp(w_exp)
        u_scr[sl] = _bdot_tn(_pad_rows(k_w, width).astype(BF16), vb)
        nu_scr[sl] = jnp.sum(k_w, axis=1, keepdims=True)
        bl_scr[sl] = b_last
        mu_scr[sl] = m_upd

    geo.for_groups_static(phase_a)

    def step(sl, c_state):
        n_state, m_prev = n_ref[...], m_ref[...]
        inter_log = bc_scr[sl] + m_prev
        m_t = jnp.maximum(inter_log, ml_scr[sl])
        w_inter = jnp.exp(inter_log - m_t)
        w_loc = jnp.exp(ml_scr[sl] - m_t)
        qb = q_scr[sl]
        num = w_inter * _bdot(qb, c_state.astype(BF16)) + w_loc * v_scr[sl]
        q_n = jnp.broadcast_to(jnp.sum(qb.astype(F32) * n_state, axis=2, keepdims=True), m_t.shape)
        nq = w_inter * q_n + w_loc * rs_scr[sl]
        v_scr[sl] = num / jnp.maximum(jnp.abs(nq), jnp.exp(-m_t))
        m_new = m_t[:, rows - 1:rows]
        w_prev = jnp.exp(bl_scr[sl] + m_prev - m_new)
        w_upd = jnp.exp(mu_scr[sl] - m_new)
        n_ref[...] = w_prev * n_state + w_upd * nu_scr[sl]
        m_ref[...] = m_new
        return w_prev * c_state + w_upd * u_scr[sl]

    if has_state:
        c_init = st0[0][...]
        n_ref[...] = st0[1][...]
        m_ref[...] = st0[2][...]
    else:
        c_init = jnp.zeros((geo.seqs, D_HEAD, D_HEAD), F32)
        n_ref[...] = jnp.zeros(n_ref.shape, F32)
        m_ref[...] = jnp.zeros(m_ref.shape, F32)
    c_ref[...] = geo.scan_chunks(step, c_init)

    def epilogue(g):
        sl = geo.rows_of(g)
        geo.store(y_ref, g, _rms(_sigmoid(gate_scr[sl]) * v_scr[sl], nw_ref[...]), relay)

    geo.for_groups(epilogue)


def _ret_kernel(*refs, layer, geo, has_state, fresh):
    ((h_ref,), w_refs, (cos_ref, sin_ref, nw_ref), r0, _, (y_ref,), r_out,
     (q_scr, o_scr, u_scr, gate_scr), relay) = _split_refs(
        refs, (1, 4, 3, int(has_state), int(not fresh), 1, 1, 4, len(refs)))
    (r_ref,) = _state_views(r_out, ['mat'], layer, fresh)
    rows, width, valid = geo.rows, geo.width, geo.valid
    head = pl.program_id(1)
    log_gamma = jnp.zeros((1, 1, 1), F32)
    for h in range(N_HEAD):
        log_gamma = jnp.where(head == h, math.log(1.0 - 2.0 ** (-5.0 - h)), log_gamma)
    scale = D_HEAD ** -0.5
    t_i = lax.broadcasted_iota(jnp.int32, (1, rows, width), 1)
    s_i = lax.broadcasted_iota(jnp.int32, (1, rows, width), 2)
    decay = jnp.where(s_i <= t_i, jnp.exp((t_i - s_i).astype(F32) * log_gamma), 0.0)
    t_col = lax.broadcasted_iota(jnp.int32, (1, rows, D_HEAD), 1).astype(F32)
    w_in = jnp.exp((t_col + 1.0) * log_gamma)
    w_st = jnp.exp((valid - 1.0 - t_col) * log_gamma)
    g_all = jnp.exp(valid * log_gamma)
    w = _join_columns(w_refs)

    def rotate(x, cos, sin):
        flat = x.reshape(x.shape[0] * rows, D_HEAD)
        return x * cos + pltpu.roll(flat, D_HEAD // 2, axis=1).reshape(x.shape) * sin

    def phase_a(g):
        q, k, v, rg = geo.split(geo.project(h_ref, w, g), relay)
        if geo.chunks > 1:
            cos, sin = cos_ref[geo.rows_of(g)], sin_ref[geo.rows_of(g)]
        else:
            cos, sin = cos_ref[...], sin_ref[...]
        qr = rotate(q, cos, sin).astype(BF16)
        kr = rotate(k, cos, sin) * scale
        vb = _pad_rows(v, width).astype(BF16)
        a = _bdot_nt(qr, _pad_rows(kr, width).astype(BF16)) * decay
        sl = geo.rows_of(g)
        gate_scr[sl] = rg
        q_scr[sl] = qr
        o_scr[sl] = _bdot(a.astype(BF16), vb)
        u_scr[sl] = _bdot_tn(_pad_rows(kr * w_st, width).astype(BF16), vb)

    geo.for_groups_static(phase_a)

    def step(sl, state):
        o_scr[sl] = o_scr[sl] + _bdot(q_scr[sl], state.astype(BF16)) * w_in
        return g_all * state + u_scr[sl]

    init = r0[0][...] if has_state else jnp.zeros((geo.seqs, D_HEAD, D_HEAD), F32)
    r_ref[...] = geo.scan_chunks(step, init)

    def epilogue(g):
        sl = geo.rows_of(g)
        geo.store(y_ref, g, _rms(o_scr[sl], nw_ref[...]) * _silu(gate_scr[sl]), relay)

    geo.for_groups(epilogue)


def _mixer_spec(kernel, name, geo, layer, h, weights, extra, states_in, states_prev, scratch):
    seqs, n_cols = geo.seqs, len(weights) * D_HEAD
    fresh = isinstance(states_prev[0], jax.ShapeDtypeStruct)

    def weight_spec(first, per_head):
        return pl.BlockSpec((None, D_MODEL, D_HEAD), lambda bi, j: (layer, 0, first + per_head * j))

    def state_spec(arr, layered=False):
        lead, first = (arr.shape[0], 0) if layered else (None, layer)
        if len(arr.shape) == 5:
            return pl.BlockSpec((lead, seqs, None, D_HEAD, D_HEAD), lambda bi, j: (first, bi, j, 0, 0))
        return pl.BlockSpec((lead, seqs, N_HEAD, D_HEAD), lambda bi, j: (first, bi, 0, 0))

    has_state = states_in is not None
    args = [h] + [a for a, _, _ in weights] + [a for a, _ in extra]
    in_specs = [pl.BlockSpec((geo.tokens, D_MODEL), lambda bi, j: (bi, 0))] + [
        weight_spec(first, per_head) for _, first, per_head in weights] + [s for _, s in extra]
    if has_state:
        args += list(states_in)
        in_specs += [state_spec(a) for a in states_in]
    n_in = len(args)
    if not fresh:
        args += list(states_prev)
        in_specs += [pl.BlockSpec(memory_space=pl.ANY) for _ in states_prev]
    out_specs = [pl.BlockSpec((geo.tokens, D_HEAD), lambda bi, j: (bi, j))] + [
        state_spec(a, fresh) for a in states_prev]
    out_shape = [jax.ShapeDtypeStruct((h.shape[0], MIX_W), F32 if geo.padded else BF16)] + [
        jax.ShapeDtypeStruct(a.shape, F32) for a in states_prev]
    scratch_shapes = list(scratch) + [pltpu.VMEM((geo.problems, geo.rows, D_HEAD), F32)]
    if geo.padded:
        scratch_shapes += [pltpu.VMEM((geo.group_tokens, D_HEAD), F32),
                           pltpu.VMEM((seqs * geo.rows, D_HEAD), F32)] * (n_cols // D_HEAD)
    return dict(
        kernel=functools.partial(kernel, layer=layer, geo=geo, has_state=has_state, fresh=fresh), name=name,
        args=args, in_specs=in_specs, out_specs=out_specs, out_shape=out_shape, scratch_shapes=scratch_shapes,
        aliases={} if fresh else {n_in + i: 1 + i for i in range(len(states_prev))})


def _run_mixers(geo, specs):
    n_ins = [len(sp["args"]) - 1 for sp in specs]
    n_outs = [len(sp["out_shape"]) for sp in specs]
    n_scr = [len(sp["scratch_shapes"]) for sp in specs]
    in_off, out_off, scr_off = (np.concatenate([[0], np.cumsum(c)]).tolist() for c in (n_ins, n_outs, n_scr))
    total_in, total_out = 1 + in_off[-1], out_off[-1]

    def body(*refs):
        ins, outs, scr = refs[1:total_in], refs[total_in:total_in + total_out], refs[total_in + total_out:]
        for m, sp in enumerate(specs):
            sp["kernel"](refs[0], *ins[in_off[m]:in_off[m + 1]], *outs[out_off[m]:out_off[m + 1]],
                         *scr[scr_off[m]:scr_off[m + 1]])

    aliases = {}
    for m, sp in enumerate(specs):
        for i, o in sp["aliases"].items():
            aliases[i + in_off[m]] = o + out_off[m]
    flat = pl.pallas_call(
        body,
        grid=(geo.batch // geo.seqs, N_HEAD),
        in_specs=specs[0]["in_specs"][:1] + [s for sp in specs for s in sp["in_specs"][1:]],
        out_specs=[s for sp in specs for s in sp["out_specs"]],
        out_shape=[s for sp in specs for s in sp["out_shape"]],
        scratch_shapes=[s for sp in specs for s in sp["scratch_shapes"]],
        input_output_aliases=aliases,
        compiler_params=pltpu.CompilerParams(
            dimension_semantics=("arbitrary", "arbitrary"), vmem_limit_bytes=VMEM_LIMIT),
        name="+".join(sp["name"] for sp in specs),
    )(specs[0]["args"][0], *[a for sp in specs for a in sp["args"][1:]])
    return [flat[out_off[m]:out_off[m + 1]] for m in range(len(specs))]


def _head_block(rows):
    return pl.BlockSpec((rows, D_HEAD), lambda bi, j: (0, j))


def _part_blocks(w_main, mixer):
    return [(w_main, (mixer * 4 + p) * N_HEAD, 1) for p in range(4)]


def _hgrn(geo, layer, h, w_main, hgrn_lb, nw, state_in, state_prev):
    n = geo.problems
    scratch = [pltpu.VMEM((n, geo.rows, D_HEAD), BF16), pltpu.VMEM((n, geo.rows, D_HEAD), F32),
               pltpu.VMEM((n, D_HEAD, D_HEAD), F32), pltpu.VMEM((n, 1, D_HEAD), F32)]
    place = _hgrn_placement(geo)
    return _mixer_spec(
        _hgrn_kernel, "hgrn2", geo, layer, h, _part_blocks(w_main, 0),
        [(hgrn_lb, _head_block(hgrn_lb.shape[0])), (nw, _head_block(1)), (place, _resident(place.shape))],
        None if state_in is None else [state_in], [state_prev], scratch)


def _mlstm(geo, layer, h, w_main, gates, gate_b, nw, states_in, states_prev):
    n = geo.problems
    col = lambda r: pltpu.VMEM((n, r, LANES), F32)
    scratch = [pltpu.VMEM((n, geo.rows, D_HEAD), BF16), pltpu.VMEM((n, geo.rows, D_HEAD), F32),
               pltpu.VMEM((n, D_HEAD, D_HEAD), F32), pltpu.VMEM((n, 1, D_HEAD), F32),
               col(geo.rows), col(geo.rows), col(geo.rows), col(1), col(1)]
    return _mixer_spec(
        _mlstm_kernel, "mlstm", geo, layer, h, _part_blocks(w_main, 1),
        [(gates, pl.BlockSpec((geo.tokens, LANES), lambda bi, j: (bi, 0))),
         (gate_b, pl.BlockSpec((1, LANES), lambda bi, j: (0, 0))), (nw, _head_block(1))],
        states_in, states_prev, scratch)


def _retention(geo, layer, h, w_main, cos, sin, nw, state_in, state_prev):
    n = geo.problems
    scratch = [pltpu.VMEM((n, geo.rows, D_HEAD), BF16), pltpu.VMEM((n, geo.rows, D_HEAD), F32),
               pltpu.VMEM((n, D_HEAD, D_HEAD), F32)]
    table = pl.BlockSpec(cos.shape, lambda bi, j: (0, 0, 0))
    return _mixer_spec(
        _ret_kernel, "retention", geo, layer, h, _part_blocks(w_main, 2),
        [(cos, table), (sin, table), (nw, _head_block(1))],
        None if state_in is None else [state_in], [state_prev], scratch)


def _merge_kernel(x_ref, h_ref, yh_ref, ym_ref, yr_ref, wg_ref, bg_ref, wb_ref, wo_ref, n2_ref, x1_ref, h2_ref):
    hb = h_ref[...]
    merged = None
    for j, y_ref in enumerate((yh_ref, ym_ref, yr_ref)):
        cols = slice(j * D_MODEL, (j + 1) * D_MODEL)
        gate = _sigmoid(_dot(hb, wg_ref[:, cols]) + bg_ref[:, cols])
        term = gate * _dot(y_ref[...].astype(BF16), wb_ref[j])
        merged = term if merged is None else merged + term
    x1 = x_ref[...] + _dot(merged.astype(BF16), wo_ref[...])
    x1_ref[...] = x1
    h2_ref[...] = _rms(x1, n2_ref[...]).astype(BF16)


def _merge(x, h, yh, ym, yr, w_bgate, gate_b, w_branch, w_out, norm2_w):
    n = x.shape[0]
    tm = min(TOKEN_TILE, n)
    gate_w = N_BRANCH * D_MODEL
    row = lambda width: pl.BlockSpec((tm, width), lambda i: (i, 0))
    vec = lambda width: pl.BlockSpec((1, width), lambda i: (0, 0))
    return pl.pallas_call(
        _merge_kernel,
        grid=(n // tm,),
        in_specs=[
            row(D_MODEL), row(D_MODEL), row(MIX_W), row(MIX_W), row(MIX_W),
            _resident((D_MODEL, gate_w)), vec(gate_w),
            _resident((N_BRANCH, MIX_W, D_MODEL)),
            _resident((D_MODEL, D_MODEL)),
            vec(D_MODEL),
        ],
        out_specs=[row(D_MODEL), row(D_MODEL)],
        out_shape=[jax.ShapeDtypeStruct((n, D_MODEL), F32), jax.ShapeDtypeStruct((n, D_MODEL), BF16)],
        compiler_params=pltpu.CompilerParams(
            dimension_semantics=("arbitrary",), vmem_limit_bytes=VMEM_LIMIT),
        name="merge",
    )(x, h, yh, ym, yr, w_bgate, gate_b, w_branch, w_out, norm2_w)


def _ffn_kernel(*refs, seq_len, has_state, final_norm, tail_rows):
    refs = list(refs)
    x1_ref, h2_ref, wg_ref, wu_ref, wd_ref, cw_ref, cb_ref, nw_ref = refs[:8]
    refs = refs[8:]
    p1_ref = p2_ref = None
    if has_state:
        p1_ref, p2_ref = refs[:2]
        refs = refs[2:]
    if final_norm:
        out_ref, tail_ref, carry_scr, a_scr = refs
    else:
        wgn_ref, out_ref, hn_ref, gn_ref, tail_ref, carry_scr, a_scr = refs
    tm = x1_ref.shape[0]
    i = pl.program_id(0)

    @pl.when(i == 0)
    def _():
        carry_scr[...] = jnp.zeros(carry_scr.shape, F32)

    t = (i * tm + lax.broadcasted_iota(jnp.int32, (tm, FF_TILE), 0)) & (seq_len - 1)
    h2 = h2_ref[...]
    acc = x1_ref[...]
    for j in range(D_FF // FF_TILE):
        cols = slice(j * FF_TILE, (j + 1) * FF_TILE)
        a = _dot(h2, wg_ref[:, cols])
        u = _dot(h2, wu_ref[:, cols])
        a_scr[0:SUBLANES, :] = carry_scr[:, cols]
        a_scr[SUBLANES:, :] = a
        carry_scr[:, cols] = a[tm - SUBLANES:, :]
        prev1 = jnp.where(t >= 1, a_scr[SUBLANES - 1:SUBLANES - 1 + tm, :], 0.0)
        prev2 = jnp.where(t >= 2, a_scr[SUBLANES - 2:SUBLANES - 2 + tm, :], 0.0)
        if has_state:
            prev1 = prev1 + p1_ref[:, cols]
            prev2 = prev2 + p2_ref[:, cols]
        conv = prev2 * cw_ref[0:1, cols] + prev1 * cw_ref[1:2, cols] + a * cw_ref[2:3, cols] + cb_ref[:, cols]
        acc = acc + _dot((_silu(conv) * u).astype(BF16), wd_ref[cols, :])
        if tail_rows == tm:
            tail_ref[:, cols] = a
        else:
            tail_ref[0, :, cols] = a[tm - tail_rows:, :]
    if final_norm:
        out_ref[...] = _rms(acc, nw_ref[...])
    else:
        out_ref[...] = acc
        hn = _rms(acc, nw_ref[...]).astype(BF16)
        hn_ref[...] = hn
        gn_ref[...] = _dot(hn, wgn_ref[...])


def _ffn(x1, h2, wg, wu, wd, conv_w, conv_b, seq_len, conv_state, norm_w, next_w_gate):
    final_norm = next_w_gate is None
    n = x1.shape[0]
    has_state = conv_state is not None
    tm = min(TOKEN_TILE // 2 if has_state else TOKEN_TILE, n)
    assert n % tm == 0 and seq_len & (seq_len - 1) == 0
    batch = n // seq_len
    row = lambda width: pl.BlockSpec((tm, width), lambda i: (i, 0))
    full = lambda shape: pl.BlockSpec(shape, lambda i: (0,) * len(shape))
    in_specs = [row(D_MODEL), row(D_MODEL), _resident((D_MODEL, D_FF)), _resident((D_MODEL, D_FF)),
                _resident((D_FF, D_MODEL)), full((CONV_W, D_FF)), full((1, D_FF)), full((1, D_MODEL))]
    args = [x1, h2, wg, wu, wd, conv_w, conv_b, norm_w]
    if has_state:
        p1 = jnp.pad(conv_state[:, 1:2], ((0, 0), (0, seq_len - 1), (0, 0))).reshape(n, D_FF)
        p2 = jnp.pad(conv_state, ((0, 0), (0, seq_len - (CONV_W - 1)), (0, 0))).reshape(n, D_FF)
        in_specs += [row(D_FF), row(D_FF)]
        args += [p1, p2]
    if seq_len % tm == 0:
        tail_rows = CONV_W - 1
        tail_spec = pl.BlockSpec((1, tail_rows, D_FF), lambda i: ((i * tm) // seq_len, 0, 0))
        tail_shape = jax.ShapeDtypeStruct((batch, tail_rows, D_FF), F32)
    else:
        assert tm % seq_len == 0 and seq_len >= CONV_W - 1
        tail_rows = tm
        tail_spec = row(D_FF)
        tail_shape = jax.ShapeDtypeStruct((n, D_FF), F32)
    out_specs, out_shape = [row(D_MODEL)], [jax.ShapeDtypeStruct((n, D_MODEL), F32)]
    if not final_norm:
        in_specs.append(_resident((D_MODEL, LANES)))
        args.append(next_w_gate)
        out_specs += [row(D_MODEL), row(LANES)]
        out_shape += [jax.ShapeDtypeStruct((n, D_MODEL), BF16), jax.ShapeDtypeStruct((n, LANES), F32)]
    *outs, tail = pl.pallas_call(
        functools.partial(_ffn_kernel, seq_len=seq_len, has_state=has_state,
                          final_norm=final_norm, tail_rows=tail_rows),
        grid=(n // tm,),
        in_specs=in_specs,
        out_specs=out_specs + [tail_spec],
        out_shape=out_shape + [tail_shape],
        scratch_shapes=[pltpu.VMEM((SUBLANES, D_FF), F32), pltpu.VMEM((tm + SUBLANES, FF_TILE), F32)],
        compiler_params=pltpu.CompilerParams(
            dimension_semantics=("arbitrary",), vmem_limit_bytes=VMEM_LIMIT),
        name="conv_ffn",
    )(*args)
    if tail_rows == tm:
        tail = tail.reshape(batch, seq_len, D_FF)[:, seq_len - (CONV_W - 1):, :]
    return outs, tail


def _prep_weights(w_in, w_branch, w_out, ffn_w_gate, ffn_w_up, ffn_w_down):
    w_in = w_in.astype(BF16)
    w_main = jnp.concatenate([w_in[:, :, HG_OFF:ML_GATE_OFF], w_in[:, :, RET_OFF:GATE_OFF]], axis=2)
    w_gate = jnp.pad(w_in[:, :, ML_GATE_OFF:RET_OFF], ((0, 0), (0, 0), (0, LANES - 2 * N_HEAD)))
    return (w_main, w_gate,
            w_in[:, :, GATE_OFF:], w_branch.astype(BF16), w_out.astype(BF16),
            ffn_w_gate.astype(BF16), ffn_w_up.astype(BF16), ffn_w_down.astype(BF16))


def _trunk(x3, pos0, states, params, prepped):
    (norm1_w, mlstm_gate_b, branch_gate_b, hgrn_lb, hgrn_norm_w, mlstm_norm_w, ret_norm_w,
     norm2_w, ffn_conv_w, ffn_conv_b, final_norm_w) = params
    w_main, w_gate, w_bgate, w_branch, w_out, w_ffg, w_ffu, w_ffd = prepped
    batch, seq_len, _ = x3.shape
    n = batch * seq_len
    depth = norm1_w.shape[0]
    geo = _Geo(batch, seq_len)
    cos, sin = (t.reshape(geo.chunks, geo.rows, D_HEAD) for t in _rope_tables(geo.chunks * geo.rows, pos0))

    mat = jax.ShapeDtypeStruct((depth, batch, N_HEAD, D_HEAD, D_HEAD), F32)
    vec = jax.ShapeDtypeStruct((depth, batch, N_HEAD, D_HEAD), F32)
    if states is None:
        hg_in = ml_in = ret_in = conv_in = None
    else:
        hg_in, ml_c, ml_n, ml_m, ret_in, conv_in = states
        ml_in = [ml_c, ml_n, jnp.broadcast_to(ml_m[..., None], vec.shape)]
    hg_out, c_out, r_out, n_out, m_out = mat, mat, mat, vec, vec

    x = x3.reshape(n, D_MODEL)
    h, gates = _norm(x, norm1_w[0:1], w_gate[0])
    conv_tails = []
    for l in range(depth):
        last = l == depth - 1
        gate_b = jnp.pad(mlstm_gate_b[l:l + 1], ((0, 0), (0, LANES - 2 * N_HEAD)))
        specs = [_hgrn(geo, l, h, w_main, hgrn_lb, hgrn_norm_w[l:l + 1], hg_in, hg_out),
                 _mlstm(geo, l, h, w_main, gates, gate_b, mlstm_norm_w[l:l + 1], ml_in, [c_out, n_out, m_out]),
                 _retention(geo, l, h, w_main, cos, sin, ret_norm_w[l:l + 1], ret_in, r_out)]
        runs = [specs] if not geo.padded else [[sp] for sp in specs]
        (y_hg, hg_out), (y_ml, c_out, n_out, m_out), (y_ret, r_out) = [
            out for run in runs for out in _run_mixers(geo, run)]
        x1, h2 = _merge(x, h, y_hg, y_ml, y_ret, w_bgate[l], branch_gate_b[l:l + 1], w_branch[l], w_out[l],
                        norm2_w[l:l + 1])
        outs, conv_tail = _ffn(x1, h2, w_ffg[l], w_ffu[l], w_ffd[l], ffn_conv_w[l], ffn_conv_b[l:l + 1],
                               seq_len, None if conv_in is None else conv_in[l],
                               final_norm_w[None, :] if last else norm1_w[l + 1:l + 2],
                               None if last else w_gate[l + 1])
        x = outs[0]
        h, gates = (None, None) if last else outs[1:]
        conv_tails.append(conv_tail)
    return x.reshape(batch, seq_len, D_MODEL), (
        hg_out, c_out, n_out, m_out[:, :, :, 0], r_out,
        jnp.stack(conv_tails))


def kernel(x_prompt, x_sample, state_hgrn, state_mlstm_C, state_mlstm_n, state_mlstm_m, state_ret,
           state_ffn_conv, norm1_w, w_in, mlstm_gate_b, branch_gate_b, hgrn_lb, hgrn_norm_w,
           mlstm_norm_w, ret_norm_w, w_branch, w_out, norm2_w, ffn_w_gate, ffn_w_up, ffn_conv_w,
           ffn_conv_b, ffn_w_down, final_norm_w):
    params = (norm1_w, mlstm_gate_b, branch_gate_b, hgrn_lb, hgrn_norm_w, mlstm_norm_w, ret_norm_w,
              norm2_w, ffn_conv_w, ffn_conv_b, final_norm_w)
    prepped = _prep_weights(w_in, w_branch, w_out, ffn_w_gate, ffn_w_up, ffn_w_down)
    y_p, (hg_p, c_p, n_p, m_p, r_p, cv_p) = _trunk(x_prompt, 0, None, params, prepped)
    y_s, (hg_s, c_s, n_s, m_s, r_s, cv_s) = _trunk(
        x_sample, PAST_LEN,
        (state_hgrn, state_mlstm_C, state_mlstm_n, state_mlstm_m, state_ret, state_ffn_conv),
        params, prepped)
    return (y_p, y_s, hg_p, hg_s, c_p, c_s, n_p, n_s, m_p, m_s, r_p, r_s, cv_p, cv_s)
```

```python
import functools
import math

import numpy as np
import jax
import jax.numpy as jnp
from jax import lax
from jax.experimental import pallas as pl
from jax.experimental.pallas import tpu as pltpu

F32 = jnp.float32
BF16 = jnp.bfloat16

D_MODEL = 1024
MIX_W = 512
N_HEAD = 4
D_HEAD = 128
N_BRANCH = 3
D_FF = 2816
CONV_W = 3
CHUNK = 64
EPS = 1e-6
ROPE_BASE = 10000.0
PAST_LEN = 16384

HG_OFF = 0
ML_OFF = 4 * MIX_W
ML_GATE_OFF = ML_OFF + 4 * MIX_W
RET_OFF = ML_GATE_OFF + 2 * N_HEAD
GATE_OFF = RET_OFF + 4 * MIX_W
LOG2E = 1.4426950408889634

LANES = 128
SUBLANES = 8
VMEM_LIMIT = 56 * 1024 * 1024

TOKEN_TILE = 512
NORM_TOKEN_TILE = 1024
FF_TILE = 2816
HG_SUB = 16
GROUP = 32
SEQ_BLOCK = 64
SCAN_UNROLL = 32


def _resident(shape):
    return pl.BlockSpec(shape, lambda *_: (0,) * len(shape), pipeline_mode=pl.Buffered(1))


def _dot(a, b):
    return jnp.dot(a, b, preferred_element_type=F32)


def _dot_tn(a, b):
    return lax.dot_general(a, b, (((0,), (0,)), ((), ())), preferred_element_type=F32)


def _bdot(a, b):
    return jnp.einsum('nlk,nkv->nlv', a, b, preferred_element_type=F32)


def _bdot_nt(a, b):
    return jnp.einsum('nqd,nkd->nqk', a, b, preferred_element_type=F32)


def _bdot_tn(a, b):
    return jnp.stack([_dot_tn(a[i], b[i]) for i in range(a.shape[0])])


def _sigmoid(x):
    return 1.0 / (1.0 + jnp.exp(-x))


def _silu(x):
    return x * _sigmoid(x)


def _log_sigmoid(x):
    return jnp.minimum(x, 0.0) - jnp.log(1.0 + jnp.exp(-jnp.abs(x)))


def _rms(x, w):
    return x * lax.rsqrt(jnp.mean(x * x, axis=-1, keepdims=True) + EPS) * w


def _pad_rows(x, n):
    if x.shape[1] >= n:
        return x
    return jnp.concatenate([x, jnp.zeros((x.shape[0], n - x.shape[1], x.shape[2]), x.dtype)], axis=1)


def _seg_cumsum(x, seg):
    shape = x.shape
    flat = x.reshape(shape[0] * seg, shape[2])
    row = lax.broadcasted_iota(jnp.int32, flat.shape, 0) & (seg - 1)
    s = 1
    while s < seg:
        flat = flat + jnp.where(row >= s, pltpu.roll(flat, s, axis=0), 0.0)
        s *= 2
    return flat.reshape(shape)


def _through_vmem(x):
    def body(ref):
        ref[...] = x
        return ref[...]
    return pl.run_scoped(body, pltpu.VMEM(x.shape, x.dtype))


def _row_bcast(col, width):
    n, rows, _ = col.shape
    vals = _through_vmem(_pad_rows(col, width))
    return jnp.stack([vals[i].T[0:rows] for i in range(n)])


def _chunk_geometry(seq_len):
    if seq_len % CHUNK == 0:
        return CHUNK, CHUNK, CHUNK
    valid = math.gcd(seq_len, CHUNK)
    assert valid == seq_len and valid <= SUBLANES, "unsupported sequence length"
    return SUBLANES, valid, LANES


def _rope_kernel(inv_ref, cos_ref, sin_ref, *, pos0):
    shape = cos_ref.shape
    pos = lax.broadcasted_iota(jnp.int32, shape, 0).astype(F32) + pos0
    ang = pos * inv_ref[...]
    lane = lax.broadcasted_iota(jnp.int32, shape, 1)
    cos_ref[...] = jnp.cos(ang)
    sin_ref[...] = jnp.where(lane < D_HEAD // 2, -jnp.sin(ang), jnp.sin(ang))


def _rope_tables(rows, pos0):
    inv = ROPE_BASE ** (-jnp.linspace(0.0, 1.0, D_HEAD // 2, dtype=F32))
    inv2 = jnp.concatenate([inv, inv])[None, :]
    return pl.pallas_call(
        functools.partial(_rope_kernel, pos0=float(pos0)),
        out_shape=(jax.ShapeDtypeStruct((rows, D_HEAD), F32),) * 2,
        name="rope_tables",
    )(inv2)


def _norm_kernel(x_ref, w_ref, wg_ref, h_ref, g_ref):
    hb = _rms(x_ref[...], w_ref[...]).astype(BF16)
    h_ref[...] = hb
    g_ref[...] = _dot(hb, wg_ref[...])


def _norm(x, w, w_gate):
    n = x.shape[0]
    tm = min(NORM_TOKEN_TILE, n)
    assert n % tm == 0
    row = lambda width: pl.BlockSpec((tm, width), lambda i: (i, 0))
    return pl.pallas_call(
        _norm_kernel,
        grid=(n // tm,),
        in_specs=[row(D_MODEL), pl.BlockSpec((1, D_MODEL), lambda i: (0, 0)), _resident((D_MODEL, LANES))],
        out_specs=[row(D_MODEL), row(LANES)],
        out_shape=[jax.ShapeDtypeStruct((n, D_MODEL), BF16), jax.ShapeDtypeStruct((n, LANES), F32)],
        compiler_params=pltpu.CompilerParams(dimension_semantics=("arbitrary",), vmem_limit_bytes=VMEM_LIMIT),
        name="norm_in",
    )(x, w, w_gate)


class _Geo:
    def __init__(self, batch, seq_len):
        self.rows, self.valid, self.width = _chunk_geometry(seq_len)
        if self.valid == self.rows:
            self.seqs, self.chunks = 1, seq_len // self.rows
            self.group = min(GROUP, self.chunks)
        else:
            self.seqs, self.chunks = min(SEQ_BLOCK, batch), 1
            self.group = self.seqs
        assert batch % self.seqs == 0 and self.chunks % min(self.group, self.chunks) == 0
        self.batch = batch
        self.problems = self.seqs * self.chunks
        self.n_groups = self.problems // self.group
        self.group_tokens = self.group * self.valid
        self.tokens = self.problems * self.valid
        self.padded = self.valid < self.rows

    def project(self, h_ref, w, g):
        return _dot(h_ref[pl.ds(g * self.group_tokens, self.group_tokens), :], w)

    def split(self, z, relay):
        n_parts = z.shape[1] // D_HEAD
        cols = lambda p: slice(p * D_HEAD, (p + 1) * D_HEAD)
        if not self.padded:
            return [z[:, cols(p)].reshape(self.group, self.rows, D_HEAD) for p in range(n_parts)]
        out = []
        for p in range(n_parts):
            flat_ref, pad_ref = relay[2 * p], relay[2 * p + 1]
            flat_ref[...] = z[:, cols(p)]
            pad_ref[...] = jnp.zeros(pad_ref.shape, F32)
            for t in range(self.valid):
                pad_ref[pl.ds(t, self.seqs, stride=self.rows), :] = flat_ref[
                    pl.ds(t, self.seqs, stride=self.valid), :]
            out.append(pad_ref[...].reshape(self.seqs, self.rows, D_HEAD))
        return out

    def store(self, y_ref, g, y, relay):
        if not self.padded:
            n = self.group * self.rows
            y_ref[pl.ds(g * n, n), :] = y.reshape(n, D_HEAD).astype(y_ref.dtype)
            return
        pad = relay[1]
        pad[...] = y.reshape(self.seqs * self.rows, D_HEAD)
        for t in range(self.valid):
            y_ref[pl.ds(t, self.seqs, stride=self.valid), :] = pad[pl.ds(t, self.seqs, stride=self.rows), :].astype(
                y_ref.dtype)

    def rows_of(self, g):
        return pl.ds(g * self.group, self.group)

    def row_mask(self, shape):
        if not self.padded:
            return None
        return lax.broadcasted_iota(jnp.int32, shape, 1) < self.valid

    def for_groups(self, body):
        if self.n_groups == 1:
            body(0)
        else:
            lax.fori_loop(0, self.n_groups, lambda g, c: (body(g), c)[1], 0)

    def for_groups_static(self, body):
        for g in range(self.n_groups):
            body(g)

    def scan_chunks(self, step, init):
        if self.chunks == 1:
            return step(pl.ds(0, self.seqs), init)
        return lax.fori_loop(0, self.chunks, lambda c, carry: step(pl.ds(c * self.seqs, self.seqs), carry), init,
                             unroll=min(SCAN_UNROLL, self.chunks))


def _join_columns(w_refs):
    return jnp.concatenate([r[...] for r in w_refs], axis=1)


def _transpose_minor(x):
    return jnp.stack([x[i].T for i in range(x.shape[0])])


def _state_views(refs, kinds, layer, layered):
    head = pl.program_id(1)
    views = []
    for ref, kind in zip(refs, kinds):
        for l in (range(ref.shape[0]) if layered else [layer]):
            view = ref.at[l] if layered else ref
            if kind == 'vec':
                view = view.at[:, pl.ds(head, 1), :]
            if l == layer:
                views.append(view)
            else:
                view[...] = jnp.zeros(view.shape, F32)
    return views


def _split_refs(refs, counts):
    out, k = [], 0
    for c in counts:
        out.append(refs[k:k + c])
        k += c
    return out


def _hgrn_lower_bound(lb_ref, layer):
    x = lb_ref[...]
    e = jnp.exp(x - jnp.max(x, axis=0, keepdims=True))
    soft = e / jnp.sum(e, axis=0, keepdims=True)
    cum = soft[0:1]
    for j in range(1, layer + 1):
        cum = cum + soft[j:j + 1]
    return cum - soft[0:1]


def _hgrn_placement(geo):
    sub = min(HG_SUB, geo.rows)
    r = np.arange(sub * D_HEAD)[:, None] // D_HEAD
    c = np.arange(geo.width)[None, :]
    return jnp.asarray(np.stack([r + i * sub == c for i in range(geo.rows // sub)]), BF16)


def _hgrn_intra(q, k, b2, place_ref, geo):
    n = q.shape[0]
    sub = min(HG_SUB, geo.rows)
    sub_row = lax.broadcasted_iota(jnp.int32, (n, sub, geo.width), 1)
    lane = lax.broadcasted_iota(jnp.int32, (n, sub, geo.width), 2)
    blocks = []
    for i in range(geo.rows // sub):
        r0 = i * sub
        b_i, q_i, k_i = (a[:, r0:r0 + sub] for a in (b2, q, k))
        terms = []
        for s in range(sub):
            lo = s // SUBLANES * SUBLANES
            e = jnp.exp2(jnp.minimum(b_i[:, lo:] - b_i[:, s:s + 1], 0.0))
            term = q_i[:, lo:] * e * k_i[:, s:s + 1]
            if lo:
                term = jnp.concatenate([jnp.zeros((n, lo, D_HEAD), F32), term], axis=1)
            terms.append(term)
        cat = jnp.concatenate(terms, axis=2).reshape(n * sub, sub * D_HEAD)
        a_i = _dot(cat.astype(BF16), place_ref[i]).reshape(n, sub, geo.width)
        a_i = jnp.where(sub_row >= lane - r0, a_i, 0.0)
        if i > 0:
            b_ref = b_i[:, 0:1]
            q_dec = q_i * jnp.exp2(b_i - b_ref)
            k_dec = _pad_rows(k[:, 0:r0] * jnp.exp2(b_ref - b2[:, 0:r0]), geo.width)
            a_i = jnp.where(lane < r0, _bdot_nt(q_dec.astype(BF16), k_dec.astype(BF16)), a_i)
        blocks.append(a_i)
    return blocks[0] if len(blocks) == 1 else jnp.concatenate(blocks, axis=1)


def _hgrn_kernel(*refs, layer, geo, has_state, fresh):
    ((h_ref,), w_refs, (lb_ref, nw_ref, place), s0, _, (y_ref,), s_out,
     (qt_scr, o_scr, u_scr, dc_scr, gate_scr), relay) = _split_refs(
        refs, (1, 4, 3, int(has_state), int(not fresh), 1, 1, 5, len(refs)))
    (s_ref,) = _state_views(s_out, ['mat'], layer, fresh)
    lb = _hgrn_lower_bound(lb_ref, layer)
    rows, width = geo.rows, geo.width
    w = _join_columns(w_refs)

    def phase_a(g):
        hq, hf, hi, hg = geo.split(geo.project(h_ref, w, g), relay)
        q = _silu(hq)
        f = lb + (1.0 - lb) * _sigmoid(hf)
        lf = jnp.log(f)
        k = 1.0 - f
        mask = geo.row_mask(lf.shape)
        if mask is not None:
            lf = jnp.where(mask, lf, 0.0)
            k = jnp.where(mask, k, 0.0)
        b2 = _seg_cumsum(lf, rows) * LOG2E
        sl = geo.rows_of(g)
        gate_scr[sl] = hg
        qt_scr[sl] = (q * jnp.exp2(b2)).astype(BF16)
        a = _hgrn_intra(q, k, b2, place, geo)
        vb = _pad_rows(hi, width).astype(BF16)
        o_scr[sl] = _bdot(a.astype(BF16), vb)
        b_last = b2[:, rows - 1:rows]
        k_w = k * jnp.exp2(b_last - b2)
        u_scr[sl] = _bdot_tn(vb, _pad_rows(k_w, width).astype(BF16))
        dc_scr[sl] = jnp.exp2(b_last)

    geo.for_groups_static(phase_a)

    def step(sl, state_t):
        o_scr[sl] = o_scr[sl] + _bdot_nt(qt_scr[sl], state_t.astype(BF16))
        return dc_scr[sl] * state_t + u_scr[sl]

    init = _transpose_minor(s0[0][...]) if has_state else jnp.zeros((geo.seqs, D_HEAD, D_HEAD), F32)
    s_ref[...] = _transpose_minor(geo.scan_chunks(step, init))

    def epilogue(g):
        sl = geo.rows_of(g)
        geo.store(y_ref, g, _rms(o_scr[sl], nw_ref[...]) * _silu(gate_scr[sl]), relay)

    geo.for_groups(epilogue)


def _mlstm_kernel(*refs, layer, geo, has_state, fresh):
    ((h_ref,), w_refs, (g_ref, gb_ref, nw_ref), st0, _, (y_ref,), st_out,
     (q_scr, v_scr, u_scr, nu_scr, bc_scr, ml_scr, rs_scr, bl_scr, mu_scr, gate_scr), relay) = _split_refs(
        refs, (1, 4, 3, 3 * int(has_state), 3 * int(not fresh), 1, 3, 10, len(refs)))
    kinds = ['mat', 'vec', 'vec']
    c_ref, n_ref, m_ref = _state_views(st_out, kinds, layer, fresh)
    st0 = _state_views(st0, kinds, layer, False) if has_state else st0
    rows, width = geo.rows, geo.width
    head = pl.program_id(1)
    scale = D_HEAD ** -0.5
    t_i = lax.broadcasted_iota(jnp.int32, (1, rows, width), 1)
    s_i = lax.broadcasted_iota(jnp.int32, (1, rows, width), 2)
    ok = s_i <= t_i
    if geo.padded:
        ok = ok & (s_i < geo.valid)
    w = _join_columns(w_refs)

    def phase_a(g):
        q, k, v, og = geo.split(geo.project(h_ref, w, g), relay)
        (gates,) = geo.split(g_ref[pl.ds(g * geo.group_tokens, geo.group_tokens), :], relay)
        gates = gates + gb_ref[...]
        lf_all = _log_sigmoid(gates)
        mask = geo.row_mask(gates.shape)
        if mask is not None:
            lf_all = jnp.where(mask, lf_all, 0.0)
        bc_all = _seg_cumsum(lf_all, rows)
        lane = lax.broadcasted_iota(jnp.int32, gates.shape, 2)
        wide = lambda x: jnp.broadcast_to(x, gates.shape)
        ig = wide(jnp.sum(jnp.where(lane == head, gates, 0.0), axis=2, keepdims=True))
        bc_col = jnp.sum(jnp.where(lane == head + N_HEAD, bc_all, 0.0), axis=2, keepdims=True)
        bc = wide(bc_col)
        log_d = jnp.where(ok, bc_col + _row_bcast(ig - bc, width), -jnp.inf)
        m_col = jnp.max(log_d, axis=2, keepdims=True)
        m_loc = wide(m_col)
        k = k * scale
        qb = q.astype(BF16)
        vb = _pad_rows(v, width).astype(BF16)
        s_m = _bdot_nt(qb, _pad_rows(k, width).astype(BF16)) * jnp.exp(log_d - m_col)
        sl = geo.rows_of(g)
        gate_scr[sl] = og
        q_scr[sl] = qb
        v_scr[sl] = _bdot(s_m.astype(BF16), vb)
        rs_scr[sl] = wide(jnp.sum(s_m, axis=2, keepdims=True))
        bc_scr[sl] = bc
        ml_scr[sl] = m_loc
        b_last = bc[:, rows - 1:rows]
        m_upd = m_loc[:, rows - 1:rows]
        w_exp = b_last - bc + ig - m_upd
        if mask is not None:
            w_exp = jnp.where(mask, w_exp, -jnp.inf)
        k_w = k * jnp.exp(w_exp)
        u_scr[sl] = _bdot_tn(_pad_rows(k_w, width).astype(BF16), vb)
        nu_scr[sl] = jnp.sum(k_w, axis=1, keepdims=True)
        bl_scr[sl] = b_last
        mu_scr[sl] = m_upd

    geo.for_groups_static(phase_a)

    def step(sl, c_state):
        n_state, m_prev = n_ref[...], m_ref[...]
        inter_log = bc_scr[sl] + m_prev
        m_t = jnp.maximum(inter_log, ml_scr[sl])
        w_inter = jnp.exp(inter_log - m_t)
        w_loc = jnp.exp(ml_scr[sl] - m_t)
        qb = q_scr[sl]
        num = w_inter * _bdot(qb, c_state.astype(BF16)) + w_loc * v_scr[sl]
        q_n = jnp.broadcast_to(jnp.sum(qb.astype(F32) * n_state, axis=2, keepdims=True), m_t.shape)
        nq = w_inter * q_n + w_loc * rs_scr[sl]
        v_scr[sl] = num / jnp.maximum(jnp.abs(nq), jnp.exp(-m_t))
        m_new = m_t[:, rows - 1:rows]
        w_prev = jnp.exp(bl_scr[sl] + m_prev - m_new)
        w_upd = jnp.exp(mu_scr[sl] - m_new)
        n_ref[...] = w_prev * n_state + w_upd * nu_scr[sl]
        m_ref[...] = m_new
        return w_prev * c_state + w_upd * u_scr[sl]

    if has_state:
        c_init = st0[0][...]
        n_ref[...] = st0[1][...]
        m_ref[...] = st0[2][...]
    else:
        c_init = jnp.zeros((geo.seqs, D_HEAD, D_HEAD), F32)
        n_ref[...] = jnp.zeros(n_ref.shape, F32)
        m_ref[...] = jnp.zeros(m_ref.shape, F32)
    c_ref[...] = geo.scan_chunks(step, c_init)

    def epilogue(g):
        sl = geo.rows_of(g)
        geo.store(y_ref, g, _rms(_sigmoid(gate_scr[sl]) * v_scr[sl], nw_ref[...]), relay)

    geo.for_groups(epilogue)


def _ret_kernel(*refs, layer, geo, has_state, fresh):
    ((h_ref,), w_refs, (cos_ref, sin_ref, nw_ref), r0, _, (y_ref,), r_out,
     (q_scr, o_scr, u_scr, gate_scr), relay) = _split_refs(
        refs, (1, 4, 3, int(has_state), int(not fresh), 1, 1, 4, len(refs)))
    (r_ref,) = _state_views(r_out, ['mat'], layer, fresh)
    rows, width, valid = geo.rows, geo.width, geo.valid
    head = pl.program_id(1)
    log_gamma = jnp.zeros((1, 1, 1), F32)
    for h in range(N_HEAD):
        log_gamma = jnp.where(head == h, math.log(1.0 - 2.0 ** (-5.0 - h)), log_gamma)
    scale = D_HEAD ** -0.5
    t_i = lax.broadcasted_iota(jnp.int32, (1, rows, width), 1)
    s_i = lax.broadcasted_iota(jnp.int32, (1, rows, width), 2)
    decay = jnp.where(s_i <= t_i, jnp.exp((t_i - s_i).astype(F32) * log_gamma), 0.0)
    t_col = lax.broadcasted_iota(jnp.int32, (1, rows, D_HEAD), 1).astype(F32)
    w_in = jnp.exp((t_col + 1.0) * log_gamma)
    w_st = jnp.exp((valid - 1.0 - t_col) * log_gamma)
    g_all = jnp.exp(valid * log_gamma)
    w = _join_columns(w_refs)

    def rotate(x, cos, sin):
        flat = x.reshape(x.shape[0] * rows, D_HEAD)
        return x * cos + pltpu.roll(flat, D_HEAD // 2, axis=1).reshape(x.shape) * sin

    def phase_a(g):
        q, k, v, rg = geo.split(geo.project(h_ref, w, g), relay)
        if geo.chunks > 1:
            cos, sin = cos_ref[geo.rows_of(g)], sin_ref[geo.rows_of(g)]
        else:
            cos, sin = cos_ref[...], sin_ref[...]
        qr = rotate(q, cos, sin).astype(BF16)
        kr = rotate(k, cos, sin) * scale
        vb = _pad_rows(v, width).astype(BF16)
        a = _bdot_nt(qr, _pad_rows(kr, width).astype(BF16)) * decay
        sl = geo.rows_of(g)
        gate_scr[sl] = rg
        q_scr[sl] = qr
        o_scr[sl] = _bdot(a.astype(BF16), vb)
        u_scr[sl] = _bdot_tn(_pad_rows(kr * w_st, width).astype(BF16), vb)

    geo.for_groups_static(phase_a)

    def step(sl, state):
        o_scr[sl] = o_scr[sl] + _bdot(q_scr[sl], state.astype(BF16)) * w_in
        return g_all * state + u_scr[sl]

    init = r0[0][...] if has_state else jnp.zeros((geo.seqs, D_HEAD, D_HEAD), F32)
    r_ref[...] = geo.scan_chunks(step, init)

    def epilogue(g):
        sl = geo.rows_of(g)
        geo.store(y_ref, g, _rms(o_scr[sl], nw_ref[...]) * _silu(gate_scr[sl]), relay)

    geo.for_groups(epilogue)


def _mixer_spec(kernel, name, geo, layer, h, weights, extra, states_in, states_prev, scratch):
    seqs, n_cols = geo.seqs, len(weights) * D_HEAD
    fresh = isinstance(states_prev[0], jax.ShapeDtypeStruct)

    def weight_spec(first, per_head):
        return pl.BlockSpec((None, D_MODEL, D_HEAD), lambda bi, j: (layer, 0, first + per_head * j))

    def state_spec(arr, layered=False):
        lead, first = (arr.shape[0], 0) if layered else (None, layer)
        if len(arr.shape) == 5:
            return pl.BlockSpec((lead, seqs, None, D_HEAD, D_HEAD), lambda bi, j: (first, bi, j, 0, 0))
        return pl.BlockSpec((lead, seqs, N_HEAD, D_HEAD), lambda bi, j: (first, bi, 0, 0))

    has_state = states_in is not None
    args = [h] + [a for a, _, _ in weights] + [a for a, _ in extra]
    in_specs = [pl.BlockSpec((geo.tokens, D_MODEL), lambda bi, j: (bi, 0))] + [
        weight_spec(first, per_head) for _, first, per_head in weights] + [s for _, s in extra]
    if has_state:
        args += list(states_in)
        in_specs += [state_spec(a) for a in states_in]
    n_in = len(args)
    if not fresh:
        args += list(states_prev)
        in_specs += [pl.BlockSpec(memory_space=pl.ANY) for _ in states_prev]
    out_specs = [pl.BlockSpec((geo.tokens, D_HEAD), lambda bi, j: (bi, j))] + [
        state_spec(a, fresh) for a in states_prev]
    out_shape = [jax.ShapeDtypeStruct((h.shape[0], MIX_W), F32 if geo.padded else BF16)] + [
        jax.ShapeDtypeStruct(a.shape, F32) for a in states_prev]
    scratch_shapes = list(scratch) + [pltpu.VMEM((geo.problems, geo.rows, D_HEAD), F32)]
    if geo.padded:
        scratch_shapes += [pltpu.VMEM((geo.group_tokens, D_HEAD), F32),
                           pltpu.VMEM((seqs * geo.rows, D_HEAD), F32)] * (n_cols // D_HEAD)
    return dict(
        kernel=functools.partial(kernel, layer=layer, geo=geo, has_state=has_state, fresh=fresh), name=name,
        args=args, in_specs=in_specs, out_specs=out_specs, out_shape=out_shape, scratch_shapes=scratch_shapes,
        aliases={} if fresh else {n_in + i: 1 + i for i in range(len(states_prev))})


def _run_mixers(geo, specs):
    n_ins = [len(sp["args"]) - 1 for sp in specs]
    n_outs = [len(sp["out_shape"]) for sp in specs]
    n_scr = [len(sp["scratch_shapes"]) for sp in specs]
    in_off, out_off, scr_off = (np.concatenate([[0], np.cumsum(c)]).tolist() for c in (n_ins, n_outs, n_scr))
    total_in, total_out = 1 + in_off[-1], out_off[-1]

    def body(*refs):
        ins, outs, scr = refs[1:total_in], refs[total_in:total_in + total_out], refs[total_in + total_out:]
        for m, sp in enumerate(specs):
            sp["kernel"](refs[0], *ins[in_off[m]:in_off[m + 1]], *outs[out_off[m]:out_off[m + 1]],
                         *scr[scr_off[m]:scr_off[m + 1]])

    aliases = {}
    for m, sp in enumerate(specs):
        for i, o in sp["aliases"].items():
            aliases[i + in_off[m]] = o + out_off[m]
    flat = pl.pallas_call(
        body,
        grid=(geo.batch // geo.seqs, N_HEAD),
        in_specs=specs[0]["in_specs"][:1] + [s for sp in specs for s in sp["in_specs"][1:]],
        out_specs=[s for sp in specs for s in sp["out_specs"]],
        out_shape=[s for sp in specs for s in sp["out_shape"]],
        scratch_shapes=[s for sp in specs for s in sp["scratch_shapes"]],
        input_output_aliases=aliases,
        compiler_params=pltpu.CompilerParams(
            dimension_semantics=("arbitrary", "arbitrary"), vmem_limit_bytes=VMEM_LIMIT),
        name="+".join(sp["name"] for sp in specs),
    )(specs[0]["args"][0], *[a for sp in specs for a in sp["args"][1:]])
    return [flat[out_off[m]:out_off[m + 1]] for m in range(len(specs))]


def _head_block(rows):
    return pl.BlockSpec((rows, D_HEAD), lambda bi, j: (0, j))


def _part_blocks(w_main, mixer):
    return [(w_main, (mixer * 4 + p) * N_HEAD, 1) for p in range(4)]


def _hgrn(geo, layer, h, w_main, hgrn_lb, nw, state_in, state_prev):
    n = geo.problems
    scratch = [pltpu.VMEM((n, geo.rows, D_HEAD), BF16), pltpu.VMEM((n, geo.rows, D_HEAD), F32),
               pltpu.VMEM((n, D_HEAD, D_HEAD), F32), pltpu.VMEM((n, 1, D_HEAD), F32)]
    place = _hgrn_placement(geo)
    return _mixer_spec(
        _hgrn_kernel, "hgrn2", geo, layer, h, _part_blocks(w_main, 0),
        [(hgrn_lb, _head_block(hgrn_lb.shape[0])), (nw, _head_block(1)), (place, _resident(place.shape))],
        None if state_in is None else [state_in], [state_prev], scratch)


def _mlstm(geo, layer, h, w_main, gates, gate_b, nw, states_in, states_prev):
    n = geo.problems
    col = lambda r: pltpu.VMEM((n, r, LANES), F32)
    scratch = [pltpu.VMEM((n, geo.rows, D_HEAD), BF16), pltpu.VMEM((n, geo.rows, D_HEAD), F32),
               pltpu.VMEM((n, D_HEAD, D_HEAD), F32), pltpu.VMEM((n, 1, D_HEAD), F32),
               col(geo.rows), col(geo.rows), col(geo.rows), col(1), col(1)]
    return _mixer_spec(
        _mlstm_kernel, "mlstm", geo, layer, h, _part_blocks(w_main, 1),
        [(gates, pl.BlockSpec((geo.tokens, LANES), lambda bi, j: (bi, 0))),
         (gate_b, pl.BlockSpec((1, LANES), lambda bi, j: (0, 0))), (nw, _head_block(1))],
        states_in, states_prev, scratch)


def _retention(geo, layer, h, w_main, cos, sin, nw, state_in, state_prev):
    n = geo.problems
    scratch = [pltpu.VMEM((n, geo.rows, D_HEAD), BF16), pltpu.VMEM((n, geo.rows, D_HEAD), F32),
               pltpu.VMEM((n, D_HEAD, D_HEAD), F32)]
    table = pl.BlockSpec(cos.shape, lambda bi, j: (0, 0, 0))
    return _mixer_spec(
        _ret_kernel, "retention", geo, layer, h, _part_blocks(w_main, 2),
        [(cos, table), (sin, table), (nw, _head_block(1))],
        None if state_in is None else [state_in], [state_prev], scratch)


def _merge_kernel(x_ref, h_ref, yh_ref, ym_ref, yr_ref, wg_ref, bg_ref, wb_ref, wo_ref, n2_ref, x1_ref, h2_ref):
    hb = h_ref[...]
    merged = None
    for j, y_ref in enumerate((yh_ref, ym_ref, yr_ref)):
        cols = slice(j * D_MODEL, (j + 1) * D_MODEL)
        gate = _sigmoid(_dot(hb, wg_ref[:, cols]) + bg_ref[:, cols])
        term = gate * _dot(y_ref[...].astype(BF16), wb_ref[j])
        merged = term if merged is None else merged + term
    x1 = x_ref[...] + _dot(merged.astype(BF16), wo_ref[...])
    x1_ref[...] = x1
    h2_ref[...] = _rms(x1, n2_ref[...]).astype(BF16)


def _merge(x, h, yh, ym, yr, w_bgate, gate_b, w_branch, w_out, norm2_w):
    n = x.shape[0]
    tm = min(TOKEN_TILE, n)
    gate_w = N_BRANCH * D_MODEL
    row = lambda width: pl.BlockSpec((tm, width), lambda i: (i, 0))
    vec = lambda width: pl.BlockSpec((1, width), lambda i: (0, 0))
    return pl.pallas_call(
        _merge_kernel,
        grid=(n // tm,),
        in_specs=[
            row(D_MODEL), row(D_MODEL), row(MIX_W), row(MIX_W), row(MIX_W),
            _resident((D_MODEL, gate_w)), vec(gate_w),
            _resident((N_BRANCH, MIX_W, D_MODEL)),
            _resident((D_MODEL, D_MODEL)),
            vec(D_MODEL),
        ],
        out_specs=[row(D_MODEL), row(D_MODEL)],
        out_shape=[jax.ShapeDtypeStruct((n, D_MODEL), F32), jax.ShapeDtypeStruct((n, D_MODEL), BF16)],
        compiler_params=pltpu.CompilerParams(
            dimension_semantics=("arbitrary",), vmem_limit_bytes=VMEM_LIMIT),
        name="merge",
    )(x, h, yh, ym, yr, w_bgate, gate_b, w_branch, w_out, norm2_w)


def _ffn_kernel(*refs, seq_len, has_state, final_norm, tail_rows):
    refs = list(refs)
    x1_ref, h2_ref, wg_ref, wu_ref, wd_ref, cw_ref, cb_ref, nw_ref = refs[:8]
    refs = refs[8:]
    p1_ref = p2_ref = None
    if has_state:
        p1_ref, p2_ref = refs[:2]
        refs = refs[2:]
    if final_norm:
        out_ref, tail_ref, carry_scr, a_scr = refs
    else:
        wgn_ref, out_ref, hn_ref, gn_ref, tail_ref, carry_scr, a_scr = refs
    tm = x1_ref.shape[0]
    i = pl.program_id(0)

    @pl.when(i == 0)
    def _():
        carry_scr[...] = jnp.zeros(carry_scr.shape, F32)

    t = (i * tm + lax.broadcasted_iota(jnp.int32, (tm, FF_TILE), 0)) & (seq_len - 1)
    h2 = h2_ref[...]
    acc = x1_ref[...]
    for j in range(D_FF // FF_TILE):
        cols = slice(j * FF_TILE, (j + 1) * FF_TILE)
        a = _dot(h2, wg_ref[:, cols])
        u = _dot(h2, wu_ref[:, cols])
        a_scr[0:SUBLANES, :] = carry_scr[:, cols]
        a_scr[SUBLANES:, :] = a
        carry_scr[:, cols] = a[tm - SUBLANES:, :]
        prev1 = jnp.where(t >= 1, a_scr[SUBLANES - 1:SUBLANES - 1 + tm, :], 0.0)
        prev2 = jnp.where(t >= 2, a_scr[SUBLANES - 2:SUBLANES - 2 + tm, :], 0.0)
        if has_state:
            prev1 = prev1 + p1_ref[:, cols]
            prev2 = prev2 + p2_ref[:, cols]
        conv = prev2 * cw_ref[0:1, cols] + prev1 * cw_ref[1:2, cols] + a * cw_ref[2:3, cols] + cb_ref[:, cols]
        acc = acc + _dot((_silu(conv) * u).astype(BF16), wd_ref[cols, :])
        if tail_rows == tm:
            tail_ref[:, cols] = a
        else:
            tail_ref[0, :, cols] = a[tm - tail_rows:, :]
    if final_norm:
        out_ref[...] = _rms(acc, nw_ref[...])
    else:
        out_ref[...] = acc
        hn = _rms(acc, nw_ref[...]).astype(BF16)
        hn_ref[...] = hn
        gn_ref[...] = _dot(hn, wgn_ref[...])


def _ffn(x1, h2, wg, wu, wd, conv_w, conv_b, seq_len, conv_state, norm_w, next_w_gate):
    final_norm = next_w_gate is None
    n = x1.shape[0]
    has_state = conv_state is not None
    tm = min(TOKEN_TILE // 2 if has_state else TOKEN_TILE, n)
    assert n % tm == 0 and seq_len & (seq_len - 1) == 0
    batch = n // seq_len
    row = lambda width: pl.BlockSpec((tm, width), lambda i: (i, 0))
    full = lambda shape: pl.BlockSpec(shape, lambda i: (0,) * len(shape))
    in_specs = [row(D_MODEL), row(D_MODEL), _resident((D_MODEL, D_FF)), _resident((D_MODEL, D_FF)),
                _resident((D_FF, D_MODEL)), full((CONV_W, D_FF)), full((1, D_FF)), full((1, D_MODEL))]
    args = [x1, h2, wg, wu, wd, conv_w, conv_b, norm_w]
    if has_state:
        p1 = jnp.pad(conv_state[:, 1:2], ((0, 0), (0, seq_len - 1), (0, 0))).reshape(n, D_FF)
        p2 = jnp.pad(conv_state, ((0, 0), (0, seq_len - (CONV_W - 1)), (0, 0))).reshape(n, D_FF)
        in_specs += [row(D_FF), row(D_FF)]
        args += [p1, p2]
    if seq_len % tm == 0:
        tail_rows = CONV_W - 1
        tail_spec = pl.BlockSpec((1, tail_rows, D_FF), lambda i: ((i * tm) // seq_len, 0, 0))
        tail_shape = jax.ShapeDtypeStruct((batch, tail_rows, D_FF), F32)
    else:
        assert tm % seq_len == 0 and seq_len >= CONV_W - 1
        tail_rows = tm
        tail_spec = row(D_FF)
        tail_shape = jax.ShapeDtypeStruct((n, D_FF), F32)
    out_specs, out_shape = [row(D_MODEL)], [jax.ShapeDtypeStruct((n, D_MODEL), F32)]
    if not final_norm:
        in_specs.append(_resident((D_MODEL, LANES)))
        args.append(next_w_gate)
        out_specs += [row(D_MODEL), row(LANES)]
        out_shape += [jax.ShapeDtypeStruct((n, D_MODEL), BF16), jax.ShapeDtypeStruct((n, LANES), F32)]
    *outs, tail = pl.pallas_call(
        functools.partial(_ffn_kernel, seq_len=seq_len, has_state=has_state,
                          final_norm=final_norm, tail_rows=tail_rows),
        grid=(n // tm,),
        in_specs=in_specs,
        out_specs=out_specs + [tail_spec],
        out_shape=out_shape + [tail_shape],
        scratch_shapes=[pltpu.VMEM((SUBLANES, D_FF), F32), pltpu.VMEM((tm + SUBLANES, FF_TILE), F32)],
        compiler_params=pltpu.CompilerParams(
            dimension_semantics=("arbitrary",), vmem_limit_bytes=VMEM_LIMIT),
        name="conv_ffn",
    )(*args)
    if tail_rows == tm:
        tail = tail.reshape(batch, seq_len, D_FF)[:, seq_len - (CONV_W - 1):, :]
    return outs, tail


def _prep_weights(w_in, w_branch, w_out, ffn_w_gate, ffn_w_up, ffn_w_down):
    w_in = w_in.astype(BF16)
    w_main = jnp.concatenate([w_in[:, :, HG_OFF:ML_GATE_OFF], w_in[:, :, RET_OFF:GATE_OFF]], axis=2)
    w_gate = jnp.pad(w_in[:, :, ML_GATE_OFF:RET_OFF], ((0, 0), (0, 0), (0, LANES - 2 * N_HEAD)))
    return (w_main, w_gate,
            w_in[:, :, GATE_OFF:], w_branch.astype(BF16), w_out.astype(BF16),
            ffn_w_gate.astype(BF16), ffn_w_up.astype(BF16), ffn_w_down.astype(BF16))


def _trunk(x3, pos0, states, params, prepped):
    (norm1_w, mlstm_gate_b, branch_gate_b, hgrn_lb, hgrn_norm_w, mlstm_norm_w, ret_norm_w,
     norm2_w, ffn_conv_w, ffn_conv_b, final_norm_w) = params
    w_main, w_gate, w_bgate, w_branch, w_out, w_ffg, w_ffu, w_ffd = prepped
    batch, seq_len, _ = x3.shape
    n = batch * seq_len
    depth = norm1_w.shape[0]
    geo = _Geo(batch, seq_len)
    cos, sin = (t.reshape(geo.chunks, geo.rows, D_HEAD) for t in _rope_tables(geo.chunks * geo.rows, pos0))

    mat = jax.ShapeDtypeStruct((depth, batch, N_HEAD, D_HEAD, D_HEAD), F32)
    vec = jax.ShapeDtypeStruct((depth, batch, N_HEAD, D_HEAD), F32)
    if states is None:
        hg_in = ml_in = ret_in = conv_in = None
    else:
        hg_in, ml_c, ml_n, ml_m, ret_in, conv_in = states
        ml_in = [ml_c, ml_n, jnp.broadcast_to(ml_m[..., None], vec.shape)]
    hg_out, c_out, r_out, n_out, m_out = mat, mat, mat, vec, vec

    x = x3.reshape(n, D_MODEL)
    h, gates = _norm(x, norm1_w[0:1], w_gate[0])
    conv_tails = []
    for l in range(depth):
        last = l == depth - 1
        gate_b = jnp.pad(mlstm_gate_b[l:l + 1], ((0, 0), (0, LANES - 2 * N_HEAD)))
        specs = [_hgrn(geo, l, h, w_main, hgrn_lb, hgrn_norm_w[l:l + 1], hg_in, hg_out),
                 _mlstm(geo, l, h, w_main, gates, gate_b, mlstm_norm_w[l:l + 1], ml_in, [c_out, n_out, m_out]),
                 _retention(geo, l, h, w_main, cos, sin, ret_norm_w[l:l + 1], ret_in, r_out)]
        runs = [specs] if not geo.padded else [[sp] for sp in specs]
        (y_hg, hg_out), (y_ml, c_out, n_out, m_out), (y_ret, r_out) = [
            out for run in runs for out in _run_mixers(geo, run)]
        x1, h2 = _merge(x, h, y_hg, y_ml, y_ret, w_bgate[l], branch_gate_b[l:l + 1], w_branch[l], w_out[l],
                        norm2_w[l:l + 1])
        outs, conv_tail = _ffn(x1, h2, w_ffg[l], w_ffu[l], w_ffd[l], ffn_conv_w[l], ffn_conv_b[l:l + 1],
                               seq_len, None if conv_in is None else conv_in[l],
                               final_norm_w[None, :] if last else norm1_w[l + 1:l + 2],
                               None if last else w_gate[l + 1])
        x = outs[0]
        h, gates = (None, None) if last else outs[1:]
        conv_tails.append(conv_tail)
    return x.reshape(batch, seq_len, D_MODEL), (
        hg_out, c_out, n_out, m_out[:, :, :, 0], r_out,
        jnp.stack(conv_tails))


def kernel(x_prompt, x_sample, state_hgrn, state_mlstm_C, state_mlstm_n, state_mlstm_m, state_ret,
           state_ffn_conv, norm1_w, w_in, mlstm_gate_b, branch_gate_b, hgrn_lb, hgrn_norm_w,
           mlstm_norm_w, ret_norm_w, w_branch, w_out, norm2_w, ffn_w_gate, ffn_w_up, ffn_conv_w,
           ffn_conv_b, ffn_w_down, final_norm_w):
    params = (norm1_w, mlstm_gate_b, branch_gate_b, hgrn_lb, hgrn_norm_w, mlstm_norm_w, ret_norm_w,
              norm2_w, ffn_conv_w, ffn_conv_b, final_norm_w)
    prepped = _prep_weights(w_in, w_branch, w_out, ffn_w_gate, ffn_w_up, ffn_w_down)
    y_p, (hg_p, c_p, n_p, m_p, r_p, cv_p) = _trunk(x_prompt, 0, None, params, prepped)
    y_s, (hg_s, c_s, n_s, m_s, r_s, cv_s) = _trunk(
        x_sample, PAST_LEN,
        (state_hgrn, state_mlstm_C, state_mlstm_n, state_mlstm_m, state_ret, state_ffn_conv),
        params, prepped)
    return (y_p, y_s, hg_p, hg_s, c_p, c_s, n_p, n_s, m_p, m_s, r_p, r_s, cv_p, cv_s)
```

```python
import functools
import math

import numpy as np
import jax
import jax.numpy as jnp
from jax import lax
from jax.experimental import pallas as pl
from jax.experimental.pallas import tpu as pltpu

F32 = jnp.float32
BF16 = jnp.bfloat16

D_MODEL = 1024
MIX_W = 512
N_HEAD = 4
D_HEAD = 128
N_BRANCH = 3
D_FF = 2816
CONV_W = 3
CHUNK = 64
EPS = 1e-6
ROPE_BASE = 10000.0
PAST_LEN = 16384

HG_OFF = 0
ML_OFF = 4 * MIX_W
ML_GATE_OFF = ML_OFF + 4 * MIX_W
RET_OFF = ML_GATE_OFF + 2 * N_HEAD
GATE_OFF = RET_OFF + 4 * MIX_W
LOG2E = 1.4426950408889634

LANES = 128
SUBLANES = 8
VMEM_LIMIT = 56 * 1024 * 1024

TOKEN_TILE = 512
NORM_TOKEN_TILE = 1024
FF_TILE = 2816
HG_SUB = 16
GROUP = 32
SEQ_BLOCK = 64
SCAN_UNROLL = 32


def _resident(shape):
    return pl.BlockSpec(shape, lambda *_: (0,) * len(shape), pipeline_mode=pl.Buffered(1))


def _dot(a, b):
    return jnp.dot(a, b, preferred_element_type=F32)


def _dot_tn(a, b):
    return lax.dot_general(a, b, (((0,), (0,)), ((), ())), preferred_element_type=F32)


def _bdot(a, b):
    return jnp.einsum('nlk,nkv->nlv', a, b, preferred_element_type=F32)


def _bdot_nt(a, b):
    return jnp.einsum('nqd,nkd->nqk', a, b, preferred_element_type=F32)


def _bdot_tn(a, b):
    return jnp.stack([_dot_tn(a[i], b[i]) for i in range(a.shape[0])])


def _sigmoid(x):
    return 1.0 / (1.0 + jnp.exp(-x))


def _silu(x):
    return x * _sigmoid(x)


def _log_sigmoid(x):
    return jnp.minimum(x, 0.0) - jnp.log(1.0 + jnp.exp(-jnp.abs(x)))


def _rms(x, w):
    return x * lax.rsqrt(jnp.mean(x * x, axis=-1, keepdims=True) + EPS) * w


def _pad_rows(x, n):
    if x.shape[1] >= n:
        return x
    return jnp.concatenate([x, jnp.zeros((x.shape[0], n - x.shape[1], x.shape[2]), x.dtype)], axis=1)


def _seg_cumsum(x, seg):
    shape = x.shape
    flat = x.reshape(shape[0] * seg, shape[2])
    row = lax.broadcasted_iota(jnp.int32, flat.shape, 0) & (seg - 1)
    s = 1
    while s < seg:
        flat = flat + jnp.where(row >= s, pltpu.roll(flat, s, axis=0), 0.0)
        s *= 2
    return flat.reshape(shape)


def _through_vmem(x):
    def body(ref):
        ref[...] = x
        return ref[...]
    return pl.run_scoped(body, pltpu.VMEM(x.shape, x.dtype))


def _row_bcast(col, width):
    n, rows, _ = col.shape
    vals = _pad_rows(col, width)
    return jnp.stack([vals[i].T[0:rows] for i in range(n)])


def _chunk_geometry(seq_len):
    if seq_len % CHUNK == 0:
        return CHUNK, CHUNK, CHUNK
    valid = math.gcd(seq_len, CHUNK)
    assert valid == seq_len and valid <= SUBLANES, "unsupported sequence length"
    return SUBLANES, valid, LANES


def _rope_kernel(inv_ref, cos_ref, sin_ref, *, pos0):
    shape = cos_ref.shape
    pos = lax.broadcasted_iota(jnp.int32, shape, 0).astype(F32) + pos0
    ang = pos * inv_ref[...]
    lane = lax.broadcasted_iota(jnp.int32, shape, 1)
    cos_ref[...] = jnp.cos(ang)
    sin_ref[...] = jnp.where(lane < D_HEAD // 2, -jnp.sin(ang), jnp.sin(ang))


def _rope_tables(rows, pos0):
    inv = ROPE_BASE ** (-jnp.linspace(0.0, 1.0, D_HEAD // 2, dtype=F32))
    inv2 = jnp.concatenate([inv, inv])[None, :]
    return pl.pallas_call(
        functools.partial(_rope_kernel, pos0=float(pos0)),
        out_shape=(jax.ShapeDtypeStruct((rows, D_HEAD), F32),) * 2,
        name="rope_tables",
    )(inv2)


def _norm_kernel(x_ref, w_ref, wg_ref, h_ref, g_ref):
    hb = _rms(x_ref[...], w_ref[...]).astype(BF16)
    h_ref[...] = hb
    g_ref[...] = _dot(hb, wg_ref[...])


def _norm(x, w, w_gate):
    n = x.shape[0]
    tm = min(NORM_TOKEN_TILE, n)
    assert n % tm == 0
    row = lambda width: pl.BlockSpec((tm, width), lambda i: (i, 0))
    return pl.pallas_call(
        _norm_kernel,
        grid=(n // tm,),
        in_specs=[row(D_MODEL), pl.BlockSpec((1, D_MODEL), lambda i: (0, 0)), _resident((D_MODEL, LANES))],
        out_specs=[row(D_MODEL), row(LANES)],
        out_shape=[jax.ShapeDtypeStruct((n, D_MODEL), BF16), jax.ShapeDtypeStruct((n, LANES), F32)],
        compiler_params=pltpu.CompilerParams(dimension_semantics=("arbitrary",), vmem_limit_bytes=VMEM_LIMIT),
        name="norm_in",
    )(x, w, w_gate)


class _Geo:
    def __init__(self, batch, seq_len):
        self.rows, self.valid, self.width = _chunk_geometry(seq_len)
        if self.valid == self.rows:
            self.seqs, self.chunks = 1, seq_len // self.rows
            self.group = min(GROUP, self.chunks)
        else:
            self.seqs, self.chunks = min(SEQ_BLOCK, batch), 1
            self.group = self.seqs
        assert batch % self.seqs == 0 and self.chunks % min(self.group, self.chunks) == 0
        self.batch = batch
        self.problems = self.seqs * self.chunks
        self.n_groups = self.problems // self.group
        self.group_tokens = self.group * self.valid
        self.tokens = self.problems * self.valid
        self.padded = self.valid < self.rows

    def project(self, h_ref, w, g):
        return _dot(h_ref[pl.ds(g * self.group_tokens, self.group_tokens), :], w)

    def split(self, z, relay):
        n_parts = z.shape[1] // D_HEAD
        cols = lambda p: slice(p * D_HEAD, (p + 1) * D_HEAD)
        if not self.padded:
            return [z[:, cols(p)].reshape(self.group, self.rows, D_HEAD) for p in range(n_parts)]
        out = []
        for p in range(n_parts):
            flat_ref, pad_ref = relay[2 * p], relay[2 * p + 1]
            flat_ref[...] = z[:, cols(p)]
            pad_ref[...] = jnp.zeros(pad_ref.shape, F32)
            for t in range(self.valid):
                pad_ref[pl.ds(t, self.seqs, stride=self.rows), :] = flat_ref[
                    pl.ds(t, self.seqs, stride=self.valid), :]
            out.append(pad_ref[...].reshape(self.seqs, self.rows, D_HEAD))
        return out

    def store(self, y_ref, g, y, relay):
        if not self.padded:
            n = self.group * self.rows
            y_ref[pl.ds(g * n, n), :] = y.reshape(n, D_HEAD).astype(y_ref.dtype)
            return
        pad = relay[1]
        pad[...] = y.reshape(self.seqs * self.rows, D_HEAD)
        for t in range(self.valid):
            y_ref[pl.ds(t, self.seqs, stride=self.valid), :] = pad[pl.ds(t, self.seqs, stride=self.rows), :].astype(
                y_ref.dtype)

    def rows_of(self, g):
        return pl.ds(g * self.group, self.group)

    def row_mask(self, shape):
        if not self.padded:
            return None
        return lax.broadcasted_iota(jnp.int32, shape, 1) < self.valid

    def for_groups(self, body):
        if self.n_groups == 1:
            body(0)
        else:
            lax.fori_loop(0, self.n_groups, lambda g, c: (body(g), c)[1], 0)

    def for_groups_static(self, body):
        for g in range(self.n_groups):
            body(g)

    def scan_chunks(self, step, init):
        if self.chunks == 1:
            return step(pl.ds(0, self.seqs), init)
        return lax.fori_loop(0, self.chunks, lambda c, carry: step(pl.ds(c * self.seqs, self.seqs), carry), init,
                             unroll=min(SCAN_UNROLL, self.chunks))


def _join_columns(w_refs):
    return jnp.concatenate([r[...] for r in w_refs], axis=1)


def _transpose_minor(x):
    return jnp.stack([x[i].T for i in range(x.shape[0])])


def _state_views(refs, kinds, layer, layered):
    head = pl.program_id(1)
    views = []
    for ref, kind in zip(refs, kinds):
        for l in (range(ref.shape[0]) if layered else [layer]):
            view = ref.at[l] if layered else ref
            if kind == 'vec':
                view = view.at[:, pl.ds(head, 1), :]
            if l == layer:
                views.append(view)
            else:
                view[...] = jnp.zeros(view.shape, F32)
    return views


def _split_refs(refs, counts):
    out, k = [], 0
    for c in counts:
        out.append(refs[k:k + c])
        k += c
    return out


def _hgrn_lower_bound(lb_ref, layer):
    x = lb_ref[...]
    e = jnp.exp(x - jnp.max(x, axis=0, keepdims=True))
    soft = e / jnp.sum(e, axis=0, keepdims=True)
    cum = soft[0:1]
    for j in range(1, layer + 1):
        cum = cum + soft[j:j + 1]
    return cum - soft[0:1]


def _hgrn_placement(geo):
    sub = min(HG_SUB, geo.rows)
    r = np.arange(sub * D_HEAD)[:, None] // D_HEAD
    c = np.arange(geo.width)[None, :]
    return jnp.asarray(np.stack([r + i * sub == c for i in range(geo.rows // sub)]), BF16)


def _hgrn_intra(q, k, b2, place_ref, geo):
    n = q.shape[0]
    sub = min(HG_SUB, geo.rows)
    sub_row = lax.broadcasted_iota(jnp.int32, (n, sub, geo.width), 1)
    lane = lax.broadcasted_iota(jnp.int32, (n, sub, geo.width), 2)
    blocks = []
    for i in range(geo.rows // sub):
        r0 = i * sub
        b_i, q_i, k_i = (a[:, r0:r0 + sub] for a in (b2, q, k))
        terms = []
        for s in range(sub):
            lo = s // SUBLANES * SUBLANES
            e = jnp.exp2(jnp.minimum(b_i[:, lo:] - b_i[:, s:s + 1], 0.0))
            term = q_i[:, lo:] * e * k_i[:, s:s + 1]
            if lo:
                term = jnp.concatenate([jnp.zeros((n, lo, D_HEAD), F32), term], axis=1)
            terms.append(term)
        cat = jnp.concatenate(terms, axis=2).reshape(n * sub, sub * D_HEAD)
        a_i = _dot(cat.astype(BF16), place_ref[i]).reshape(n, sub, geo.width)
        a_i = jnp.where(sub_row >= lane - r0, a_i, 0.0)
        if i > 0:
            b_ref = b_i[:, 0:1]
            q_dec = q_i * jnp.exp2(b_i - b_ref)
            k_dec = _pad_rows(k[:, 0:r0] * jnp.exp2(b_ref - b2[:, 0:r0]), geo.width)
            a_i = jnp.where(lane < r0, _bdot_nt(q_dec.astype(BF16), k_dec.astype(BF16)), a_i)
        blocks.append(a_i)
    return blocks[0] if len(blocks) == 1 else jnp.concatenate(blocks, axis=1)


def _hgrn_kernel(*refs, layer, geo, has_state, fresh):
    ((h_ref,), w_refs, (lb_ref, nw_ref, place), s0, _, (y_ref,), s_out,
     (qt_scr, o_scr, u_scr, dc_scr, gate_scr), relay) = _split_refs(
        refs, (1, 4, 3, int(has_state), int(not fresh), 1, 1, 5, len(refs)))
    (s_ref,) = _state_views(s_out, ['mat'], layer, fresh)
    lb = _hgrn_lower_bound(lb_ref, layer)
    rows, width = geo.rows, geo.width
    w = _join_columns(w_refs)

    def phase_a(g):
        hq, hf, hi, hg = geo.split(geo.project(h_ref, w, g), relay)
        q = _silu(hq)
        f = lb + (1.0 - lb) * _sigmoid(hf)
        lf = jnp.log(f)
        k = 1.0 - f
        mask = geo.row_mask(lf.shape)
        if mask is not None:
            lf = jnp.where(mask, lf, 0.0)
            k = jnp.where(mask, k, 0.0)
        b2 = _seg_cumsum(lf, rows) * LOG2E
        sl = geo.rows_of(g)
        gate_scr[sl] = hg
        qt_scr[sl] = (q * jnp.exp2(b2)).astype(BF16)
        a = _hgrn_intra(q, k, b2, place, geo)
        vb = _pad_rows(hi, width).astype(BF16)
        o_scr[sl] = _bdot(a.astype(BF16), vb)
        b_last = b2[:, rows - 1:rows]
        k_w = k * jnp.exp2(b_last - b2)
        u_scr[sl] = _bdot_tn(vb, _pad_rows(k_w, width).astype(BF16))
        dc_scr[sl] = jnp.exp2(b_last)

    geo.for_groups_static(phase_a)

    def step(sl, state_t):
        o_scr[sl] = o_scr[sl] + _bdot_nt(qt_scr[sl], state_t.astype(BF16))
        return dc_scr[sl] * state_t + u_scr[sl]

    init = _transpose_minor(s0[0][...]) if has_state else jnp.zeros((geo.seqs, D_HEAD, D_HEAD), F32)
    s_ref[...] = _transpose_minor(geo.scan_chunks(step, init))

    def epilogue(g):
        sl = geo.rows_of(g)
        geo.store(y_ref, g, _rms(o_scr[sl], nw_ref[...]) * _silu(gate_scr[sl]), relay)

    geo.for_groups(epilogue)


def _mlstm_kernel(*refs, layer, geo, has_state, fresh):
    ((h_ref,), w_refs, (g_ref, gb_ref, nw_ref), st0, _, (y_ref,), st_out,
     (q_scr, v_scr, u_scr, nu_scr, bc_scr, ml_scr, rs_scr, bl_scr, mu_scr, gate_scr), relay) = _split_refs(
        refs, (1, 4, 3, 3 * int(has_state), 3 * int(not fresh), 1, 3, 10, len(refs)))
    kinds = ['mat', 'vec', 'vec']
    c_ref, n_ref, m_ref = _state_views(st_out, kinds, layer, fresh)
    st0 = _state_views(st0, kinds, layer, False) if has_state else st0
    rows, width = geo.rows, geo.width
    head = pl.program_id(1)
    scale = D_HEAD ** -0.5
    t_i = lax.broadcasted_iota(jnp.int32, (1, rows, width), 1)
    s_i = lax.broadcasted_iota(jnp.int32, (1, rows, width), 2)
    ok = s_i <= t_i
    if geo.padded:
        ok = ok & (s_i < geo.valid)
    w = _join_columns(w_refs)

    def phase_a(g):
        q, k, v, og = geo.split(geo.project(h_ref, w, g), relay)
        (gates,) = geo.split(g_ref[pl.ds(g * geo.group_tokens, geo.group_tokens), :], relay)
        gates = gates + gb_ref[...]
        lf_all = _log_sigmoid(gates)
        mask = geo.row_mask(gates.shape)
        if mask is not None:
            lf_all = jnp.where(mask, lf_all, 0.0)
        bc_all = _seg_cumsum(lf_all, rows)
        lane = lax.broadcasted_iota(jnp.int32, gates.shape, 2)
        wide = lambda x: jnp.broadcast_to(x, gates.shape)
        ig = wide(jnp.sum(jnp.where(lane == head, gates, 0.0), axis=2, keepdims=True))
        bc_col = jnp.sum(jnp.where(lane == head + N_HEAD, bc_all, 0.0), axis=2, keepdims=True)
        bc = wide(bc_col)
        log_d = jnp.where(ok, bc_col + _row_bcast(ig - bc, width), -jnp.inf)
        m_col = jnp.max(log_d, axis=2, keepdims=True)
        m_loc = wide(m_col)
        k = k * scale
        qb = q.astype(BF16)
        vb = _pad_rows(v, width).astype(BF16)
        s_m = _bdot_nt(qb, _pad_rows(k, width).astype(BF16)) * jnp.exp(log_d - m_col)
        sl = geo.rows_of(g)
        gate_scr[sl] = og
        q_scr[sl] = qb
        v_scr[sl] = _bdot(s_m.astype(BF16), vb)
        rs_scr[sl] = wide(jnp.sum(s_m, axis=2, keepdims=True))
        bc_scr[sl] = bc
        ml_scr[sl] = m_loc
        b_last = bc[:, rows - 1:rows]
        m_upd = m_loc[:, rows - 1:rows]
        w_exp = b_last - bc + ig - m_upd
        if mask is not None:
            w_exp = jnp.where(mask, w_exp, -jnp.inf)
        k_w = k * jnp.exp(w_exp)
        u_scr[sl] = _bdot_tn(_pad_rows(k_w, width).astype(BF16), vb)
        nu_scr[sl] = jnp.sum(k_w, axis=1, keepdims=True)
        bl_scr[sl] = b_last
        mu_scr[sl] = m_upd

    geo.for_groups_static(phase_a)

    def step(sl, c_state):
        n_state, m_prev = n_ref[...], m_ref[...]
        inter_log = bc_scr[sl] + m_prev
        m_t = jnp.maximum(inter_log, ml_scr[sl])
        w_inter = jnp.exp(inter_log - m_t)
        w_loc = jnp.exp(ml_scr[sl] - m_t)
        qb = q_scr[sl]
        num = w_inter * _bdot(qb, c_state.astype(BF16)) + w_loc * v_scr[sl]
        q_n = jnp.broadcast_to(jnp.sum(qb.astype(F32) * n_state, axis=2, keepdims=True), m_t.shape)
        nq = w_inter * q_n + w_loc * rs_scr[sl]
        v_scr[sl] = num / jnp.maximum(jnp.abs(nq), jnp.exp(-m_t))
        m_new = m_t[:, rows - 1:rows]
        w_prev = jnp.exp(bl_scr[sl] + m_prev - m_new)
        w_upd = jnp.exp(mu_scr[sl] - m_new)
        n_ref[...] = w_prev * n_state + w_upd * nu_scr[sl]
        m_ref[...] = m_new
        return w_prev * c_state + w_upd * u_scr[sl]

    if has_state:
        c_init = st0[0][...]
        n_ref[...] = st0[1][...]
        m_ref[...] = st0[2][...]
    else:
        c_init = jnp.zeros((geo.seqs, D_HEAD, D_HEAD), F32)
        n_ref[...] = jnp.zeros(n_ref.shape, F32)
        m_ref[...] = jnp.zeros(m_ref.shape, F32)
    c_ref[...] = geo.scan_chunks(step, c_init)

    def epilogue(g):
        sl = geo.rows_of(g)
        geo.store(y_ref, g, _rms(_sigmoid(gate_scr[sl]) * v_scr[sl], nw_ref[...]), relay)

    geo.for_groups(epilogue)


def _ret_kernel(*refs, layer, geo, has_state, fresh):
    ((h_ref,), w_refs, (cos_ref, sin_ref, nw_ref), r0, _, (y_ref,), r_out,
     (q_scr, o_scr, u_scr, gate_scr), relay) = _split_refs(
        refs, (1, 4, 3, int(has_state), int(not fresh), 1, 1, 4, len(refs)))
    (r_ref,) = _state_views(r_out, ['mat'], layer, fresh)
    rows, width, valid = geo.rows, geo.width, geo.valid
    head = pl.program_id(1)
    log_gamma = jnp.zeros((1, 1, 1), F32)
    for h in range(N_HEAD):
        log_gamma = jnp.where(head == h, math.log(1.0 - 2.0 ** (-5.0 - h)), log_gamma)
    scale = D_HEAD ** -0.5
    t_i = lax.broadcasted_iota(jnp.int32, (1, rows, width), 1)
    s_i = lax.broadcasted_iota(jnp.int32, (1, rows, width), 2)
    decay = jnp.where(s_i <= t_i, jnp.exp((t_i - s_i).astype(F32) * log_gamma), 0.0)
    t_col = lax.broadcasted_iota(jnp.int32, (1, rows, D_HEAD), 1).astype(F32)
    w_in = jnp.exp((t_col + 1.0) * log_gamma)
    w_st = jnp.exp((valid - 1.0 - t_col) * log_gamma)
    g_all = jnp.exp(valid * log_gamma)
    w = _join_columns(w_refs)

    def rotate(x, cos, sin):
        flat = x.reshape(x.shape[0] * rows, D_HEAD)
        return x * cos + pltpu.roll(flat, D_HEAD // 2, axis=1).reshape(x.shape) * sin

    def phase_a(g):
        q, k, v, rg = geo.split(geo.project(h_ref, w, g), relay)
        if geo.chunks > 1:
            cos, sin = cos_ref[geo.rows_of(g)], sin_ref[geo.rows_of(g)]
        else:
            cos, sin = cos_ref[...], sin_ref[...]
        qr = rotate(q, cos, sin).astype(BF16)
        kr = rotate(k, cos, sin) * scale
        vb = _pad_rows(v, width).astype(BF16)
        a = _bdot_nt(qr, _pad_rows(kr, width).astype(BF16)) * decay
        sl = geo.rows_of(g)
        gate_scr[sl] = rg
        q_scr[sl] = qr
        o_scr[sl] = _bdot(a.astype(BF16), vb)
        u_scr[sl] = _bdot_tn(_pad_rows(kr * w_st, width).astype(BF16), vb)

    geo.for_groups_static(phase_a)

    def step(sl, state):
        o_scr[sl] = o_scr[sl] + _bdot(q_scr[sl], state.astype(BF16)) * w_in
        return g_all * state + u_scr[sl]

    init = r0[0][...] if has_state else jnp.zeros((geo.seqs, D_HEAD, D_HEAD), F32)
    r_ref[...] = geo.scan_chunks(step, init)

    def epilogue(g):
        sl = geo.rows_of(g)
        geo.store(y_ref, g, _rms(o_scr[sl], nw_ref[...]) * _silu(gate_scr[sl]), relay)

    geo.for_groups(epilogue)


def _mixer_spec(kernel, name, geo, layer, h, weights, extra, states_in, states_prev, scratch):
    seqs, n_cols = geo.seqs, len(weights) * D_HEAD
    fresh = isinstance(states_prev[0], jax.ShapeDtypeStruct)

    def weight_spec(first, per_head):
        return pl.BlockSpec((None, D_MODEL, D_HEAD), lambda bi, j: (layer, 0, first + per_head * j))

    def state_spec(arr, layered=False):
        lead, first = (arr.shape[0], 0) if layered else (None, layer)
        if len(arr.shape) == 5:
            return pl.BlockSpec((lead, seqs, None, D_HEAD, D_HEAD), lambda bi, j: (first, bi, j, 0, 0))
        return pl.BlockSpec((lead, seqs, N_HEAD, D_HEAD), lambda bi, j: (first, bi, 0, 0))

    has_state = states_in is not None
    args = [h] + [a for a, _, _ in weights] + [a for a, _ in extra]
    in_specs = [pl.BlockSpec((geo.tokens, D_MODEL), lambda bi, j: (bi, 0))] + [
        weight_spec(first, per_head) for _, first, per_head in weights] + [s for _, s in extra]
    if has_state:
        args += list(states_in)
        in_specs += [state_spec(a) for a in states_in]
    n_in = len(args)
    if not fresh:
        args += list(states_prev)
        in_specs += [pl.BlockSpec(memory_space=pl.ANY) for _ in states_prev]
    out_specs = [pl.BlockSpec((geo.tokens, D_HEAD), lambda bi, j: (bi, j))] + [
        state_spec(a, fresh) for a in states_prev]
    out_shape = [jax.ShapeDtypeStruct((h.shape[0], MIX_W), F32 if geo.padded else BF16)] + [
        jax.ShapeDtypeStruct(a.shape, F32) for a in states_prev]
    scratch_shapes = list(scratch) + [pltpu.VMEM((geo.problems, geo.rows, D_HEAD), F32)]
    if geo.padded:
        scratch_shapes += [pltpu.VMEM((geo.group_tokens, D_HEAD), F32),
                           pltpu.VMEM((seqs * geo.rows, D_HEAD), F32)] * (n_cols // D_HEAD)
    return dict(
        kernel=functools.partial(kernel, layer=layer, geo=geo, has_state=has_state, fresh=fresh), name=name,
        args=args, in_specs=in_specs, out_specs=out_specs, out_shape=out_shape, scratch_shapes=scratch_shapes,
        aliases={} if fresh else {n_in + i: 1 + i for i in range(len(states_prev))})


def _run_mixers(geo, specs):
    n_ins = [len(sp["args"]) - 1 for sp in specs]
    n_outs = [len(sp["out_shape"]) for sp in specs]
    n_scr = [len(sp["scratch_shapes"]) for sp in specs]
    in_off, out_off, scr_off = (np.concatenate([[0], np.cumsum(c)]).tolist() for c in (n_ins, n_outs, n_scr))
    total_in, total_out = 1 + in_off[-1], out_off[-1]

    def body(*refs):
        ins, outs, scr = refs[1:total_in], refs[total_in:total_in + total_out], refs[total_in + total_out:]
        for m, sp in enumerate(specs):
            sp["kernel"](refs[0], *ins[in_off[m]:in_off[m + 1]], *outs[out_off[m]:out_off[m + 1]],
                         *scr[scr_off[m]:scr_off[m + 1]])

    aliases = {}
    for m, sp in enumerate(specs):
        for i, o in sp["aliases"].items():
            aliases[i + in_off[m]] = o + out_off[m]
    flat = pl.pallas_call(
        body,
        grid=(geo.batch // geo.seqs, N_HEAD),
        in_specs=specs[0]["in_specs"][:1] + [s for sp in specs for s in sp["in_specs"][1:]],
        out_specs=[s for sp in specs for s in sp["out_specs"]],
        out_shape=[s for sp in specs for s in sp["out_shape"]],
        scratch_shapes=[s for sp in specs for s in sp["scratch_shapes"]],
        input_output_aliases=aliases,
        compiler_params=pltpu.CompilerParams(
            dimension_semantics=("arbitrary", "arbitrary"), vmem_limit_bytes=VMEM_LIMIT),
        name="+".join(sp["name"] for sp in specs),
    )(specs[0]["args"][0], *[a for sp in specs for a in sp["args"][1:]])
    return [flat[out_off[m]:out_off[m + 1]] for m in range(len(specs))]


def _head_block(rows):
    return pl.BlockSpec((rows, D_HEAD), lambda bi, j: (0, j))


def _part_blocks(w_main, mixer):
    return [(w_main, (mixer * 4 + p) * N_HEAD, 1) for p in range(4)]


def _hgrn(geo, layer, h, w_main, hgrn_lb, nw, state_in, state_prev):
    n = geo.problems
    scratch = [pltpu.VMEM((n, geo.rows, D_HEAD), BF16), pltpu.VMEM((n, geo.rows, D_HEAD), F32),
               pltpu.VMEM((n, D_HEAD, D_HEAD), F32), pltpu.VMEM((n, 1, D_HEAD), F32)]
    place = _hgrn_placement(geo)
    return _mixer_spec(
        _hgrn_kernel, "hgrn2", geo, layer, h, _part_blocks(w_main, 0),
        [(hgrn_lb, _head_block(hgrn_lb.shape[0])), (nw, _head_block(1)), (place, _resident(place.shape))],
        None if state_in is None else [state_in], [state_prev], scratch)


def _mlstm(geo, layer, h, w_main, gates, gate_b, nw, states_in, states_prev):
    n = geo.problems
    col = lambda r: pltpu.VMEM((n, r, LANES), F32)
    scratch = [pltpu.VMEM((n, geo.rows, D_HEAD), BF16), pltpu.VMEM((n, geo.rows, D_HEAD), F32),
               pltpu.VMEM((n, D_HEAD, D_HEAD), F32), pltpu.VMEM((n, 1, D_HEAD), F32),
               col(geo.rows), col(geo.rows), col(geo.rows), col(1), col(1)]
    return _mixer_spec(
        _mlstm_kernel, "mlstm", geo, layer, h, _part_blocks(w_main, 1),
        [(gates, pl.BlockSpec((geo.tokens, LANES), lambda bi, j: (bi, 0))),
         (gate_b, pl.BlockSpec((1, LANES), lambda bi, j: (0, 0))), (nw, _head_block(1))],
        states_in, states_prev, scratch)


def _retention(geo, layer, h, w_main, cos, sin, nw, state_in, state_prev):
    n = geo.problems
    scratch = [pltpu.VMEM((n, geo.rows, D_HEAD), BF16), pltpu.VMEM((n, geo.rows, D_HEAD), F32),
               pltpu.VMEM((n, D_HEAD, D_HEAD), F32)]
    table = pl.BlockSpec(cos.shape, lambda bi, j: (0, 0, 0))
    return _mixer_spec(
        _ret_kernel, "retention", geo, layer, h, _part_blocks(w_main, 2),
        [(cos, table), (sin, table), (nw, _head_block(1))],
        None if state_in is None else [state_in], [state_prev], scratch)


def _merge_kernel(x_ref, h_ref, yh_ref, ym_ref, yr_ref, wg_ref, bg_ref, wb_ref, wo_ref, n2_ref, x1_ref, h2_ref):
    hb = h_ref[...]
    merged = None
    for j, y_ref in enumerate((yh_ref, ym_ref, yr_ref)):
        cols = slice(j * D_MODEL, (j + 1) * D_MODEL)
        gate = _sigmoid(_dot(hb, wg_ref[:, cols]) + bg_ref[:, cols])
        term = gate * _dot(y_ref[...].astype(BF16), wb_ref[j])
        merged = term if merged is None else merged + term
    x1 = x_ref[...] + _dot(merged.astype(BF16), wo_ref[...])
    x1_ref[...] = x1
    h2_ref[...] = _rms(x1, n2_ref[...]).astype(BF16)


def _merge(x, h, yh, ym, yr, w_bgate, gate_b, w_branch, w_out, norm2_w):
    n = x.shape[0]
    tm = min(TOKEN_TILE, n)
    gate_w = N_BRANCH * D_MODEL
    row = lambda width: pl.BlockSpec((tm, width), lambda i: (i, 0))
    vec = lambda width: pl.BlockSpec((1, width), lambda i: (0, 0))
    return pl.pallas_call(
        _merge_kernel,
        grid=(n // tm,),
        in_specs=[
            row(D_MODEL), row(D_MODEL), row(MIX_W), row(MIX_W), row(MIX_W),
            _resident((D_MODEL, gate_w)), vec(gate_w),
            _resident((N_BRANCH, MIX_W, D_MODEL)),
            _resident((D_MODEL, D_MODEL)),
            vec(D_MODEL),
        ],
        out_specs=[row(D_MODEL), row(D_MODEL)],
        out_shape=[jax.ShapeDtypeStruct((n, D_MODEL), F32), jax.ShapeDtypeStruct((n, D_MODEL), BF16)],
        compiler_params=pltpu.CompilerParams(
            dimension_semantics=("arbitrary",), vmem_limit_bytes=VMEM_LIMIT),
        name="merge",
    )(x, h, yh, ym, yr, w_bgate, gate_b, w_branch, w_out, norm2_w)


def _ffn_kernel(*refs, seq_len, has_state, final_norm, tail_rows):
    refs = list(refs)
    x1_ref, h2_ref, wg_ref, wu_ref, wd_ref, cw_ref, cb_ref, nw_ref = refs[:8]
    refs = refs[8:]
    p1_ref = p2_ref = None
    if has_state:
        p1_ref, p2_ref = refs[:2]
        refs = refs[2:]
    if final_norm:
        out_ref, tail_ref, carry_scr, a_scr = refs
    else:
        wgn_ref, out_ref, hn_ref, gn_ref, tail_ref, carry_scr, a_scr = refs
    tm = x1_ref.shape[0]
    i = pl.program_id(0)

    @pl.when(i == 0)
    def _():
        carry_scr[...] = jnp.zeros(carry_scr.shape, F32)

    t = (i * tm + lax.broadcasted_iota(jnp.int32, (tm, FF_TILE), 0)) & (seq_len - 1)
    h2 = h2_ref[...]
    acc = x1_ref[...]
    for j in range(D_FF // FF_TILE):
        cols = slice(j * FF_TILE, (j + 1) * FF_TILE)
        a = _dot(h2, wg_ref[:, cols])
        u = _dot(h2, wu_ref[:, cols])
        a_scr[0:SUBLANES, :] = carry_scr[:, cols]
        a_scr[SUBLANES:, :] = a
        carry_scr[:, cols] = a[tm - SUBLANES:, :]
        prev1 = jnp.where(t >= 1, a_scr[SUBLANES - 1:SUBLANES - 1 + tm, :], 0.0)
        prev2 = jnp.where(t >= 2, a_scr[SUBLANES - 2:SUBLANES - 2 + tm, :], 0.0)
        if has_state:
            prev1 = prev1 + p1_ref[:, cols]
            prev2 = prev2 + p2_ref[:, cols]
        conv = prev2 * cw_ref[0:1, cols] + prev1 * cw_ref[1:2, cols] + a * cw_ref[2:3, cols] + cb_ref[:, cols]
        acc = acc + _dot((_silu(conv) * u).astype(BF16), wd_ref[cols, :])
        if tail_rows == tm:
            tail_ref[:, cols] = a
        else:
            tail_ref[0, :, cols] = a[tm - tail_rows:, :]
    if final_norm:
        out_ref[...] = _rms(acc, nw_ref[...])
    else:
        out_ref[...] = acc
        hn = _rms(acc, nw_ref[...]).astype(BF16)
        hn_ref[...] = hn
        gn_ref[...] = _dot(hn, wgn_ref[...])


def _ffn(x1, h2, wg, wu, wd, conv_w, conv_b, seq_len, conv_state, norm_w, next_w_gate):
    final_norm = next_w_gate is None
    n = x1.shape[0]
    has_state = conv_state is not None
    tm = min(TOKEN_TILE // 2 if has_state else TOKEN_TILE, n)
    assert n % tm == 0 and seq_len & (seq_len - 1) == 0
    batch = n // seq_len
    row = lambda width: pl.BlockSpec((tm, width), lambda i: (i, 0))
    full = lambda shape: pl.BlockSpec(shape, lambda i: (0,) * len(shape))
    in_specs = [row(D_MODEL), row(D_MODEL), _resident((D_MODEL, D_FF)), _resident((D_MODEL, D_FF)),
                _resident((D_FF, D_MODEL)), full((CONV_W, D_FF)), full((1, D_FF)), full((1, D_MODEL))]
    args = [x1, h2, wg, wu, wd, conv_w, conv_b, norm_w]
    if has_state:
        p1 = jnp.pad(conv_state[:, 1:2], ((0, 0), (0, seq_len - 1), (0, 0))).reshape(n, D_FF)
        p2 = jnp.pad(conv_state, ((0, 0), (0, seq_len - (CONV_W - 1)), (0, 0))).reshape(n, D_FF)
        in_specs += [row(D_FF), row(D_FF)]
        args += [p1, p2]
    if seq_len % tm == 0:
        tail_rows = CONV_W - 1
        tail_spec = pl.BlockSpec((1, tail_rows, D_FF), lambda i: ((i * tm) // seq_len, 0, 0))
        tail_shape = jax.ShapeDtypeStruct((batch, tail_rows, D_FF), F32)
    else:
        assert tm % seq_len == 0 and seq_len >= CONV_W - 1
        tail_rows = tm
        tail_spec = row(D_FF)
        tail_shape = jax.ShapeDtypeStruct((n, D_FF), F32)
    out_specs, out_shape = [row(D_MODEL)], [jax.ShapeDtypeStruct((n, D_MODEL), F32)]
    if not final_norm:
        in_specs.append(_resident((D_MODEL, LANES)))
        args.append(next_w_gate)
        out_specs += [row(D_MODEL), row(LANES)]
        out_shape += [jax.ShapeDtypeStruct((n, D_MODEL), BF16), jax.ShapeDtypeStruct((n, LANES), F32)]
    *outs, tail = pl.pallas_call(
        functools.partial(_ffn_kernel, seq_len=seq_len, has_state=has_state,
                          final_norm=final_norm, tail_rows=tail_rows),
        grid=(n // tm,),
        in_specs=in_specs,
        out_specs=out_specs + [tail_spec],
        out_shape=out_shape + [tail_shape],
        scratch_shapes=[pltpu.VMEM((SUBLANES, D_FF), F32), pltpu.VMEM((tm + SUBLANES, FF_TILE), F32)],
        compiler_params=pltpu.CompilerParams(
            dimension_semantics=("arbitrary",), vmem_limit_bytes=VMEM_LIMIT),
        name="conv_ffn",
    )(*args)
    if tail_rows == tm:
        tail = tail.reshape(batch, seq_len, D_FF)[:, seq_len - (CONV_W - 1):, :]
    return outs, tail


def _prep_weights(w_in, w_branch, w_out, ffn_w_gate, ffn_w_up, ffn_w_down):
    w_in = w_in.astype(BF16)
    w_main = jnp.concatenate([w_in[:, :, HG_OFF:ML_GATE_OFF], w_in[:, :, RET_OFF:GATE_OFF]], axis=2)
    w_gate = jnp.pad(w_in[:, :, ML_GATE_OFF:RET_OFF], ((0, 0), (0, 0), (0, LANES - 2 * N_HEAD)))
    return (w_main, w_gate,
            w_in[:, :, GATE_OFF:], w_branch.astype(BF16), w_out.astype(BF16),
            ffn_w_gate.astype(BF16), ffn_w_up.astype(BF16), ffn_w_down.astype(BF16))


def _trunk(x3, pos0, states, params, prepped):
    (norm1_w, mlstm_gate_b, branch_gate_b, hgrn_lb, hgrn_norm_w, mlstm_norm_w, ret_norm_w,
     norm2_w, ffn_conv_w, ffn_conv_b, final_norm_w) = params
    w_main, w_gate, w_bgate, w_branch, w_out, w_ffg, w_ffu, w_ffd = prepped
    batch, seq_len, _ = x3.shape
    n = batch * seq_len
    depth = norm1_w.shape[0]
    geo = _Geo(batch, seq_len)
    cos, sin = (t.reshape(geo.chunks, geo.rows, D_HEAD) for t in _rope_tables(geo.chunks * geo.rows, pos0))

    mat = jax.ShapeDtypeStruct((depth, batch, N_HEAD, D_HEAD, D_HEAD), F32)
    vec = jax.ShapeDtypeStruct((depth, batch, N_HEAD, D_HEAD), F32)
    if states is None:
        hg_in = ml_in = ret_in = conv_in = None
    else:
        hg_in, ml_c, ml_n, ml_m, ret_in, conv_in = states
        ml_in = [ml_c, ml_n, jnp.broadcast_to(ml_m[..., None], vec.shape)]
    hg_out, c_out, r_out, n_out, m_out = mat, mat, mat, vec, vec

    x = x3.reshape(n, D_MODEL)
    h, gates = _norm(x, norm1_w[0:1], w_gate[0])
    conv_tails = []
    for l in range(depth):
        last = l == depth - 1
        gate_b = jnp.pad(mlstm_gate_b[l:l + 1], ((0, 0), (0, LANES - 2 * N_HEAD)))
        specs = [_hgrn(geo, l, h, w_main, hgrn_lb, hgrn_norm_w[l:l + 1], hg_in, hg_out),
                 _mlstm(geo, l, h, w_main, gates, gate_b, mlstm_norm_w[l:l + 1], ml_in, [c_out, n_out, m_out]),
                 _retention(geo, l, h, w_main, cos, sin, ret_norm_w[l:l + 1], ret_in, r_out)]
        runs = [specs] if not geo.padded else [[sp] for sp in specs]
        (y_hg, hg_out), (y_ml, c_out, n_out, m_out), (y_ret, r_out) = [
            out for run in runs for out in _run_mixers(geo, run)]
        x1, h2 = _merge(x, h, y_hg, y_ml, y_ret, w_bgate[l], branch_gate_b[l:l + 1], w_branch[l], w_out[l],
                        norm2_w[l:l + 1])
        outs, conv_tail = _ffn(x1, h2, w_ffg[l], w_ffu[l], w_ffd[l], ffn_conv_w[l], ffn_conv_b[l:l + 1],
                               seq_len, None if conv_in is None else conv_in[l],
                               final_norm_w[None, :] if last else norm1_w[l + 1:l + 2],
                               None if last else w_gate[l + 1])
        x = outs[0]
        h, gates = (None, None) if last else outs[1:]
        conv_tails.append(conv_tail)
    return x.reshape(batch, seq_len, D_MODEL), (
        hg_out, c_out, n_out, m_out[:, :, :, 0], r_out,
        jnp.stack(conv_tails))


def kernel(x_prompt, x_sample, state_hgrn, state_mlstm_C, state_mlstm_n, state_mlstm_m, state_ret,
           state_ffn_conv, norm1_w, w_in, mlstm_gate_b, branch_gate_b, hgrn_lb, hgrn_norm_w,
           mlstm_norm_w, ret_norm_w, w_branch, w_out, norm2_w, ffn_w_gate, ffn_w_up, ffn_conv_w,
           ffn_conv_b, ffn_w_down, final_norm_w):
    params = (norm1_w, mlstm_gate_b, branch_gate_b, hgrn_lb, hgrn_norm_w, mlstm_norm_w, ret_norm_w,
              norm2_w, ffn_conv_w, ffn_conv_b, final_norm_w)
    prepped = _prep_weights(w_in, w_branch, w_out, ffn_w_gate, ffn_w_up, ffn_w_down)
    y_p, (hg_p, c_p, n_p, m_p, r_p, cv_p) = _trunk(x_prompt, 0, None, params, prepped)
    y_s, (hg_s, c_s, n_s, m_s, r_s, cv_s) = _trunk(
        x_sample, PAST_LEN,
        (state_hgrn, state_mlstm_C, state_mlstm_n, state_mlstm_m, state_ret, state_ffn_conv),
        params, prepped)
    return (y_p, y_s, hg_p, hg_s, c_p, c_s, n_p, n_s, m_p, m_s, r_p, r_s, cv_p, cv_s)
```
